```python
import math
import jax, jax.numpy as jnp
from jax import lax
import numpy as np

D_MODEL = 2048
BATCH = 4
SEQ = 2048
DEPTH = 1
DEC_BATCH = 32
DEC_SEQ = 16
PAST_LEN = 2048

CHUNK = 64
Q_BLOCK = 128
D_MIX = D_MODEL
D_POOL = D_MIX // 2
POOL_WINDOWS = (2, 4, 8, 16)
N_POOL_GROUPS = len(POOL_WINDOWS)
POOL_GROUP = D_POOL // N_POOL_GROUPS
POOL_HIST = max(POOL_WINDOWS) - 1
N_HEADS = 8
D_NOPE = 128
D_ROPE = 64
D_V = 128
D_MLA = N_HEADS * D_V
Q_RANK = D_MODEL // 4
KV_RANK = D_MODEL // 8
D_PLE = 256
ROPE_THETA = 10000.0
EPS = 1e-6
ATTN_SCALE = (D_NOPE + D_ROPE) ** -0.5
NEG_INF = -1e30
D_IN = 2 * D_POOL + Q_RANK + KV_RANK + D_ROPE + D_MLA
SPLIT_IDX = (D_POOL, 2 * D_POOL, 2 * D_POOL + Q_RANK, 2 * D_POOL + Q_RANK + KV_RANK,
             2 * D_POOL + Q_RANK + KV_RANK + D_ROPE)

kernel_name = 'hymba_pool_mla_streaming_step'


def rms_norm(x, g):
    xf = x.astype(jnp.float32)
    y = xf * lax.rsqrt(jnp.mean(xf * xf, axis=-1, keepdims=True) + EPS)
    return (y * g.astype(jnp.float32)).astype(x.dtype)


def rope_angles(pos):
    inv = ROPE_THETA ** (-(jnp.arange(0, D_ROPE, 2, dtype=jnp.float32) / D_ROPE))
    ang = pos.astype(jnp.float32)[:, None] * inv[None, :]
    return jnp.cos(ang), jnp.sin(ang)


def apply_rope(x, cos, sin):
    xf = x.astype(jnp.float32)
    x1, x2 = xf[..., :D_ROPE // 2], xf[..., D_ROPE // 2:]
    return jnp.concatenate([x1 * cos - x2 * sin, x2 * cos + x1 * sin], axis=-1).astype(x.dtype)


def multiscale_pool(u_hist, u, pos0, w_pool, pool_scale):
    B, T, C = u.shape
    ucat = jnp.concatenate([u_hist, u], axis=1).astype(jnp.float32)
    cs = jnp.concatenate([jnp.zeros((B, 1, C), jnp.float32), jnp.cumsum(ucat, axis=1)], axis=1)
    end = cs[:, POOL_HIST + 1:]
    pos = pos0 + jnp.arange(T)
    means = []
    for gi, w in enumerate(POOL_WINDOWS):
        sl = slice(gi * POOL_GROUP, (gi + 1) * POOL_GROUP)
        start = cs[:, POOL_HIST + 1 - w:POOL_HIST + 1 - w + T, sl]
        cnt = jnp.minimum(pos + 1, w).astype(jnp.float32)[None, :, None]
        means.append((end[..., sl] - start) / cnt)
    d = (jnp.concatenate(means, axis=-1) - u.astype(jnp.float32)).astype(u.dtype)
    d = d.reshape(B, T, N_POOL_GROUPS, POOL_GROUP)
    out = jnp.einsum('btgc,gcd->btgd', d, w_pool).reshape(B, T, C)
    return out * pool_scale


def chunk_causal_attention(q_nope, q_rope, k_nope, k_rope, v, q_pos, k_pos):
    B, T, H, _ = q_nope.shape
    blk = Q_BLOCK if T % Q_BLOCK == 0 else T
    nb = T // blk
    k_chunk = k_pos // CHUNK

    def to_blocks(a):
        return jnp.moveaxis(a.reshape((B, nb, blk) + a.shape[2:]), 1, 0)

    def one_block(args):
        qn, qr, qp = args
        s = (jnp.einsum('bqhd,bkhd->bhqk', qn, k_nope).astype(jnp.float32)
             + jnp.einsum('bqhr,bkr->bhqk', qr, k_rope).astype(jnp.float32)) * ATTN_SCALE
        mask = k_chunk[None, :] <= (qp // CHUNK)[:, None]
        s = jnp.where(mask[None, None], s, NEG_INF)
        pr = jax.nn.softmax(s, axis=-1).astype(v.dtype)
        return jnp.einsum('bhqk,bkhd->bqhd', pr, v)

    out = lax.map(one_block, (to_blocks(q_nope), to_blocks(q_rope), q_pos.reshape(nb, blk)))
    return jnp.moveaxis(out, 0, 1).reshape(B, T, H, D_V)


def hybrid_layer(x, p, pool_hist, ckv_hist, krope_hist, norm_g, w_in, q_norm_g, w_uq, kv_norm_g,
                 w_ukv, q_nope_g, q_rope_g, k_nope_g, k_rope_g, w_pool, pool_scale, w_out,
                 ple_norm_g, w_ple_gate, b_ple_gate, w_ple):
    B, T, _ = x.shape
    pos0 = ckv_hist.shape[1]
    xn = rms_norm(x, norm_g)
    u, g_pool, c_q, c_kv, k_r, g_mla = jnp.split(xn @ w_in, SPLIT_IDX, axis=-1)
    pool_out = multiscale_pool(pool_hist, u, pos0, w_pool, pool_scale) * jax.nn.silu(g_pool)
    new_pool = jnp.concatenate([pool_hist, u], axis=1)[:, -POOL_HIST:]
    q_pos = pos0 + jnp.arange(T)
    cos, sin = rope_angles(q_pos)
    q = (rms_norm(c_q, q_norm_g) @ w_uq).reshape(B, T, N_HEADS, D_NOPE + D_ROPE)
    q_nope = rms_norm(q[..., :D_NOPE], q_nope_g)
    q_rope = apply_rope(rms_norm(q[..., D_NOPE:], q_rope_g), cos[:, None], sin[:, None])
    ckv_new = rms_norm(c_kv, kv_norm_g)
    krope_new = apply_rope(rms_norm(k_r, k_rope_g), cos, sin)
    ckv_all = jnp.concatenate([ckv_hist, ckv_new], axis=1)
    krope_all = jnp.concatenate([krope_hist, krope_new], axis=1)
    S = ckv_all.shape[1]
    kv = (ckv_all @ w_ukv).reshape(B, S, N_HEADS, D_NOPE + D_V)
    k_nope = rms_norm(kv[..., :D_NOPE], k_nope_g)
    v = kv[..., D_NOPE:]
    attn = chunk_causal_attention(q_nope, q_rope, k_nope, krope_all, v, q_pos, jnp.arange(S))
    mla_out = attn.reshape(B, T, D_MLA) * jax.nn.silu(g_mla)
    h = x + jnp.concatenate([pool_out, mla_out], axis=-1) @ w_out
    gate = jax.nn.sigmoid(rms_norm(h, ple_norm_g) @ w_ple_gate + b_ple_gate)
    y = h + gate * (p @ w_ple)
    return y, ckv_new, krope_new, new_pool


def setup_inputs(seed: int = 0) -> dict:
    key = jax.random.key(seed)
    ks = jax.random.split(key, 32)
    f32 = jnp.float32

    def nrm(k, shape, scale=1.0):
        return jax.random.normal(k, shape, f32) * scale

    def gain(k, n):
        return 1.0 + 0.1 * jax.random.normal(k, (DEPTH, n), f32)

    return {
        'x_prompt': nrm(ks[0], (BATCH, SEQ, D_MODEL)),
        'x_sample': nrm(ks[1], (DEC_BATCH, DEC_SEQ, D_MODEL)),
        'cache_ckv': nrm(ks[2], (DEPTH, DEC_BATCH, PAST_LEN, KV_RANK)),
        'cache_krope': nrm(ks[3], (DEPTH, DEC_BATCH, PAST_LEN, D_ROPE)),
        'state_pool': nrm(ks[4], (DEPTH, DEC_BATCH, POOL_HIST, D_POOL)),
        'p_prompt': nrm(ks[5], (DEPTH, BATCH, SEQ, D_PLE)),
        'p_sample': nrm(ks[6], (DEPTH, DEC_BATCH, DEC_SEQ, D_PLE)),
        'norm_g': gain(ks[7], D_MODEL),
        'w_in': nrm(ks[8], (DEPTH, D_MODEL, D_IN), D_MODEL ** -0.5),
        'q_norm_g': gain(ks[9], Q_RANK),
        'w_uq': nrm(ks[10], (DEPTH, Q_RANK, N_HEADS * (D_NOPE + D_ROPE)), Q_RANK ** -0.5),
        'kv_norm_g': gain(ks[11], KV_RANK),
        'w_ukv': nrm(ks[12], (DEPTH, KV_RANK, N_HEADS * (D_NOPE + D_V)), KV_RANK ** -0.5),
        'q_nope_g': gain(ks[13], D_NOPE),
        'q_rope_g': gain(ks[14], D_ROPE),
        'k_nope_g': gain(ks[15], D_NOPE),
        'k_rope_g': gain(ks[16], D_ROPE),
        'w_pool': nrm(ks[17], (DEPTH, N_POOL_GROUPS, POOL_GROUP, POOL_GROUP), POOL_GROUP ** -0.5),
        'pool_scale': gain(ks[18], D_POOL),
        'w_out': nrm(ks[19], (DEPTH, D_MIX, D_MODEL), D_MIX ** -0.5),
        'ple_norm_g': gain(ks[20], D_MODEL),
        'w_ple_gate': nrm(ks[21], (DEPTH, D_MODEL, D_MODEL), D_MODEL ** -0.5),
        'b_ple_gate': nrm(ks[22], (DEPTH, D_MODEL), 0.01),
        'w_ple': nrm(ks[23], (DEPTH, D_PLE, D_MODEL), D_PLE ** -0.5),
    }


def reference(x_prompt, x_sample, cache_ckv, cache_krope, state_pool, p_prompt, p_sample,
              norm_g, w_in, q_norm_g, w_uq, kv_norm_g, w_ukv, q_nope_g, q_rope_g, k_nope_g,
              k_rope_g, w_pool, pool_scale, w_out, ple_norm_g, w_ple_gate, b_ple_gate, w_ple):
    yp, ys = x_prompt, x_sample
    B, dt = x_prompt.shape[0], x_prompt.dtype
    ckv_p, kr_p, pool_p, ckv_s, kr_s, pool_s = [], [], [], [], [], []
    for i in range(DEPTH):
        w = (norm_g[i], w_in[i], q_norm_g[i], w_uq[i], kv_norm_g[i], w_ukv[i], q_nope_g[i],
             q_rope_g[i], k_nope_g[i], k_rope_g[i], w_pool[i], pool_scale[i], w_out[i],
             ple_norm_g[i], w_ple_gate[i], b_ple_gate[i], w_ple[i])
        yp, c1, k1, s1 = hybrid_layer(
            yp, p_prompt[i], jnp.zeros((B, POOL_HIST, D_POOL), dt),
            jnp.zeros((B, 0, KV_RANK), dt), jnp.zeros((B, 0, D_ROPE), dt), *w)
        ys, c2, k2, s2 = hybrid_layer(ys, p_sample[i], state_pool[i], cache_ckv[i], cache_krope[i], *w)
        ckv_p.append(c1); kr_p.append(k1); pool_p.append(s1)
        ckv_s.append(c2); kr_s.append(k2); pool_s.append(s2)
    return (yp, ys, jnp.stack(ckv_p), jnp.stack(kr_p), jnp.stack(pool_p),
            jnp.stack(ckv_s), jnp.stack(kr_s), jnp.stack(pool_s))
```

```python
import functools

import jax
import jax.numpy as jnp
import numpy as np
from jax import lax
from jax.experimental import pallas as pl
from jax.experimental.pallas import tpu as pltpu

D_MODEL = 2048
CHUNK = 64
D_POOL = 1024
POOL_WINDOWS = (2, 4, 8, 16)
POOL_GROUP = 256
POOL_HIST = 15
HALO = 16
N_HEADS = 8
D_NOPE = 128
D_ROPE = 64
D_V = 128
D_MLA = N_HEADS * D_V
Q_RANK = 512
KV_RANK = 256
D_PLE = 256
D_QK = 256
ROPE_THETA = 10000.0
EPS = 1e-6
ATTN_SCALE = (D_NOPE + D_ROPE) ** -0.5
NEG_INF = -1e30
LANES = 128

C_U, C_GP, C_GM, C_Q, C_KV, C_KR = 0, 1024, 2048, 3072, 3584, 3840
D_IN_PAD = 3968

VMEM_LIMIT = 56 * 1024 * 1024

f32 = jnp.float32
bf16 = jnp.bfloat16


def _cparams(n_axes):
    return pltpu.CompilerParams(dimension_semantics=("arbitrary",) * n_axes,
                                vmem_limit_bytes=VMEM_LIMIT)


def _resident(shape):
    nd = len(shape)
    return pl.BlockSpec(shape, lambda *_: (0,) * nd, pipeline_mode=pl.Buffered(1))


def _rms(x, n):
    return lax.rsqrt(jnp.sum(x * x, axis=-1, keepdims=True) * (1.0 / n) + EPS)


def _dot(a, b):
    return jnp.dot(a, b, preferred_element_type=f32)


def _dot_nt(a, b):
    return lax.dot_general(a, b, (((1,), (1,)), ((), ())), preferred_element_type=f32)


def _rope128(x, cos2, sin2):
    lane = lax.broadcasted_iota(jnp.int32, x.shape, 1)
    swapped = jnp.where(lane < D_ROPE // 2, pltpu.roll(x, LANES - D_ROPE // 2, 1),
                        pltpu.roll(x, D_ROPE // 2, 1))
    return x * cos2 + swapped * sin2


def _in_proj_body(x_ref, hist_ref, cos_ref, sin_ref, pos_ref, ng_ref, w_ref, qg_ref, kvg_ref, krg_ref,
                  wp_ref, ps_ref,
                  pact_ref, gm_ref, ql_ref, ckv_ref, kr_ref, npool_ref,
                  ucat_ref, *, nseg, seg_len, tiles_per_seq, carry):
    i = pl.program_id(0)
    tm = nseg * seg_len
    x = x_ref[...]
    xn = (x * _rms(x, D_MODEL) * ng_ref[...]).astype(bf16)

    u = _dot(xn, w_ref[:, C_U:C_U + D_POOL])
    if carry:
        @pl.when(i % tiles_per_seq == 0)
        def _():
            ucat_ref[:, 0:HALO, :] = jnp.zeros((nseg, HALO, D_POOL), f32)
    else:
        ucat_ref[:, 0:HALO, :] = hist_ref[...]
    ucat_ref[:, HALO:HALO + seg_len, :] = u.reshape(nseg, seg_len, D_POOL)

    pos = pos_ref[...]
    gp = _dot(xn, w_ref[:, C_GP:C_GP + D_POOL])
    gate = gp * jax.nn.sigmoid(gp) * ps_ref[...]
    for g, w in enumerate(POOL_WINDOWS):
        sl = slice(g * POOL_GROUP, (g + 1) * POOL_GROUP)
        acc = ucat_ref[:, HALO:HALO + seg_len, sl]
        for k in range(1, w):
            acc = acc + ucat_ref[:, HALO - k:HALO - k + seg_len, sl]
        acc = acc.reshape(tm, POOL_GROUP)
        inv_cnt = 1.0 / jnp.minimum(pos + 1.0, float(w))
        d = (acc * inv_cnt - u[:, sl]).astype(bf16)
        mixed = _dot(d, wp_ref[g])
        pact_ref[:, sl] = (mixed * gate[:, sl]).astype(bf16)

    npool_ref[...] = ucat_ref[:, seg_len + 1:seg_len + HALO, :]
    if carry:
        ucat_ref[:, 0:HALO, :] = ucat_ref[:, seg_len:seg_len + HALO, :]

    gm = _dot(xn, w_ref[:, C_GM:C_GM + D_MLA])
    gm_ref[...] = (gm * jax.nn.sigmoid(gm)).astype(bf16)

    cq = _dot(xn, w_ref[:, C_Q:C_Q + Q_RANK])
    ql_ref[...] = (cq * _rms(cq, Q_RANK) * qg_ref[...]).astype(bf16)

    ckr = _dot(xn, w_ref[:, C_KV:D_IN_PAD])
    ckv = ckr[:, :KV_RANK]
    ckv_ref[...] = ckv * _rms(ckv, KV_RANK) * kvg_ref[...]
    kr = ckr[:, KV_RANK:]
    kr = kr * _rms(kr, D_ROPE) * krg_ref[...]
    kr_ref[...] = _rope128(kr, cos_ref[...], sin_ref[...])[:, :D_ROPE]


def _in_proj(x2d, hist, cos2, sin2, pos, wts, *, nseg, seg_len, n_seq, carry):
    n_tok = x2d.shape[0]
    tm = nseg * seg_len
    n_tiles = n_tok // tm
    tiles_per_seq = n_tiles // n_seq if carry else 1
    tab_tiles = cos2.shape[0] // tm
    row = lambda i: (i, 0)
    tab = lambda i: (i % tab_tiles, 0)
    if carry:
        hist_spec = pl.BlockSpec((1, HALO, D_POOL), lambda i: (0, 0, 0))
        npool_spec = pl.BlockSpec((1, POOL_HIST, D_POOL), lambda i: (i // tiles_per_seq, 0, 0))
    else:
        hist_spec = pl.BlockSpec((nseg, HALO, D_POOL), lambda i: (i, 0, 0))
        npool_spec = pl.BlockSpec((nseg, POOL_HIST, D_POOL), lambda i: (i, 0, 0))
    body = functools.partial(_in_proj_body, nseg=nseg, seg_len=seg_len,
                             tiles_per_seq=tiles_per_seq, carry=carry)
    return pl.pallas_call(
        body,
        grid=(n_tiles,),
        in_specs=[
            pl.BlockSpec((tm, D_MODEL), row),
            hist_spec,
            pl.BlockSpec((tm, LANES), tab),
            pl.BlockSpec((tm, LANES), tab),
            pl.BlockSpec((tm, 1), tab),
            _resident((1, D_MODEL)),
            _resident((D_MODEL, D_IN_PAD)),
            _resident((1, Q_RANK)),
            _resident((1, KV_RANK)),
            _resident((1, LANES)),
            _resident((4, POOL_GROUP, POOL_GROUP)),
            _resident((1, D_POOL)),
        ],
        out_specs=[
            pl.BlockSpec((tm, D_POOL), row),
            pl.BlockSpec((tm, D_MLA), row),
            pl.BlockSpec((tm, Q_RANK), row),
            pl.BlockSpec((tm, KV_RANK), row),
            pl.BlockSpec((tm, D_ROPE), row),
            npool_spec,
        ],
        out_shape=[
            jax.ShapeDtypeStruct((n_tok, D_POOL), bf16),
            jax.ShapeDtypeStruct((n_tok, D_MLA), bf16),
            jax.ShapeDtypeStruct((n_tok, Q_RANK), bf16),
            jax.ShapeDtypeStruct((n_tok, KV_RANK), f32),
            jax.ShapeDtypeStruct((n_tok, D_ROPE), f32),
            jax.ShapeDtypeStruct((n_seq, POOL_HIST, D_POOL), f32),
        ],
        scratch_shapes=[pltpu.VMEM((nseg, HALO + seg_len, D_POOL), f32)],
        compiler_params=_cparams(1),
        name="in_proj",
    )(x2d, hist, cos2, sin2, pos, wts["norm_g"], wts["w_in"], wts["q_norm_g"], wts["kv_norm_g"],
      wts["k_rope_g"], wts["w_pool"], wts["pool_scale"])


def _q_heads(ql_ref, wq_ref, cos_ref, sin_ref, qng_ref, qrg_ref):
    q = _dot(ql_ref[...], wq_ref[...])
    cos2, sin2 = cos_ref[...], sin_ref[...]
    for h in range(N_HEADS):
        qn = q[:, h * D_QK:h * D_QK + D_NOPE]
        qn = qn * _rms(qn, D_NOPE) * qng_ref[...]
        qr = q[:, h * D_QK + D_NOPE:(h + 1) * D_QK]
        qr = _rope128(qr * _rms(qr, D_ROPE) * qrg_ref[...], cos2, sin2)
        yield h, qn, qr


def _q_proj_prompt_body(ql_ref, wq_ref, cos_ref, sin_ref, qng_ref, qrg_ref, q_ref):
    for h, qn, qr in _q_heads(ql_ref, wq_ref, cos_ref, sin_ref, qng_ref, qrg_ref):
        q_ref[:, h * D_QK:h * D_QK + D_NOPE] = qn.astype(bf16)
        q_ref[:, h * D_QK + D_NOPE:(h + 1) * D_QK] = qr.astype(bf16)


def _q_proj_prompt(qlat, cos2, sin2, wts, *, tm):
    n_tok = qlat.shape[0]
    tab_tiles = cos2.shape[0] // tm
    row = lambda i: (i, 0)
    tab = lambda i: (i % tab_tiles, 0)
    return pl.pallas_call(
        _q_proj_prompt_body,
        grid=(n_tok // tm,),
        in_specs=[pl.BlockSpec((tm, Q_RANK), row), _resident((Q_RANK, N_HEADS * D_QK)),
                  pl.BlockSpec((tm, LANES), tab), pl.BlockSpec((tm, LANES), tab),
                  _resident((1, D_NOPE)), _resident((1, LANES))],
        out_specs=pl.BlockSpec((tm, N_HEADS * D_QK), row),
        out_shape=jax.ShapeDtypeStruct((n_tok, N_HEADS * D_QK), bf16),
        compiler_params=_cparams(1),
        name="q_proj_prompt",
    )(qlat, wts["w_uq"], cos2, sin2, wts["q_nope_g"], wts["q_rope_g"])


def _q_proj_sample_body(ql_ref, wq_ref, cos_ref, sin_ref, qng_ref, qrg_ref, kng_ref, wukt_ref,
                        qa_ref, qr_ref):
    for h, qn, qr in _q_heads(ql_ref, wq_ref, cos_ref, sin_ref, qng_ref, qrg_ref):
        qa_ref[h] = _dot((qn * kng_ref[...]).astype(bf16), wukt_ref[h]).astype(bf16)
        qr_ref[h] = qr[:, :D_ROPE].astype(bf16)


def _q_proj_sample(qlat, cos2, sin2, wts):
    n_tok = qlat.shape[0]
    return pl.pallas_call(
        _q_proj_sample_body,
        grid=(1,),
        in_specs=[_resident((n_tok, Q_RANK)), _resident((Q_RANK, N_HEADS * D_QK)),
                  _resident((n_tok, LANES)), _resident((n_tok, LANES)),
                  _resident((1, D_NOPE)), _resident((1, LANES)), _resident((1, D_NOPE)),
                  _resident((N_HEADS, D_NOPE, KV_RANK))],
        out_specs=[_resident((N_HEADS, n_tok, KV_RANK)), _resident((N_HEADS, n_tok, D_ROPE))],
        out_shape=[jax.ShapeDtypeStruct((N_HEADS, n_tok, KV_RANK), bf16),
                   jax.ShapeDtypeStruct((N_HEADS, n_tok, D_ROPE), bf16)],
        compiler_params=_cparams(1),
        name="q_proj_sample",
    )(qlat, wts["w_uq"], cos2, sin2, wts["q_nope_g"], wts["q_rope_g"], wts["k_nope_g"],
      wts["w_uk_t"])


def _kv_proj_body(ckv_ref, kr_ref, wk_ref, wv_ref, kng_ref, k_ref, v_ref):
    c = ckv_ref[...].astype(bf16)
    k = _dot(c, wk_ref[...])
    v_ref[...] = _dot(c, wv_ref[...]).astype(bf16)
    tm = c.shape[0]
    kr = jnp.concatenate([kr_ref[...], jnp.zeros((tm, LANES - D_ROPE), f32)], axis=-1).astype(bf16)
    for h in range(N_HEADS):
        kn = k[:, h * D_NOPE:(h + 1) * D_NOPE]
        k_ref[:, h * D_QK:h * D_QK + D_NOPE] = (kn * _rms(kn, D_NOPE) * kng_ref[...]).astype(bf16)
        k_ref[:, h * D_QK + D_NOPE:(h + 1) * D_QK] = kr


def _kv_proj(ckv, krope, wts, *, tm):
    n_tok = ckv.shape[0]
    row = lambda i: (i, 0)
    return pl.pallas_call(
        _kv_proj_body,
        grid=(n_tok // tm,),
        in_specs=[pl.BlockSpec((tm, KV_RANK), row), pl.BlockSpec((tm, D_ROPE), row),
                  _resident((KV_RANK, N_HEADS * D_NOPE)), _resident((KV_RANK, D_MLA)),
                  _resident((1, D_NOPE))],
        out_specs=[pl.BlockSpec((tm, N_HEADS * D_QK), row), pl.BlockSpec((tm, D_MLA), row)],
        out_shape=[jax.ShapeDtypeStruct((n_tok, N_HEADS * D_QK), bf16),
                   jax.ShapeDtypeStruct((n_tok, D_MLA), bf16)],
        compiler_params=_cparams(1),
        name="kv_proj",
    )(ckv, krope, wts["w_uk"], wts["w_uv"], wts["k_nope_g"])


def _attn_prompt_body(q_ref, k_ref, v_ref, gm_ref, o_ref, m_ref, l_ref, acc_ref, *, tq, tk):
    qi, ki = pl.program_id(1), pl.program_id(2)
    last_k = (qi * tq + tq - 1) // tk

    @pl.when(ki == 0)
    def _():
        m_ref[...] = jnp.full(m_ref.shape, NEG_INF, f32)
        l_ref[...] = jnp.zeros(l_ref.shape, f32)
        acc_ref[...] = jnp.zeros(acc_ref.shape, f32)

    @pl.when(ki <= last_k)
    def _():
        q_chunk = (qi * tq + lax.broadcasted_iota(jnp.int32, (tq, tk), 0)) // CHUNK
        k_chunk = (ki * tk + lax.broadcasted_iota(jnp.int32, (tq, tk), 1)) // CHUNK
        visible = k_chunk <= q_chunk
        for h in range(N_HEADS):
            s = _dot_nt(q_ref[:, h * D_QK:(h + 1) * D_QK], k_ref[:, h * D_QK:(h + 1) * D_QK])
            s = jnp.where(visible, s * ATTN_SCALE, NEG_INF)
            m_old = m_ref[h]
            m_new = jnp.maximum(m_old, jnp.max(s, axis=-1, keepdims=True))
            alpha = jnp.exp(m_old - m_new)
            p = jnp.exp(s - m_new)
            l_ref[h] = alpha * l_ref[h] + jnp.sum(p, axis=-1, keepdims=True)
            m_ref[h] = m_new
            hs = slice(h * D_V, (h + 1) * D_V)
            acc_ref[:, hs] = alpha * acc_ref[:, hs] + _dot(p.astype(bf16), v_ref[:, hs])

    @pl.when(ki == last_k)
    def _():
        for h in range(N_HEADS):
            hs = slice(h * D_V, (h + 1) * D_V)
            o = acc_ref[:, hs] / l_ref[h]
            o_ref[:, hs] = (o * gm_ref[:, hs].astype(f32)).astype(bf16)


def _attn_prompt(q, k, v, gm, *, n_seq, seq_len, tq, tk):
    nq, nk = seq_len // tq, seq_len // tk
    qmap = lambda b, i, j: (b * nq + i, 0)
    kmap = lambda b, i, j: (b * nk + jnp.minimum(j, (i * tq + tq - 1) // tk), 0)
    body = functools.partial(_attn_prompt_body, tq=tq, tk=tk)
    return pl.pallas_call(
        body,
        grid=(n_seq, nq, nk),
        in_specs=[pl.BlockSpec((tq, N_HEADS * D_QK), qmap), pl.BlockSpec((tk, N_HEADS * D_QK), kmap),
                  pl.BlockSpec((tk, D_MLA), kmap), pl.BlockSpec((tq, D_MLA), qmap)],
        out_specs=pl.BlockSpec((tq, D_MLA), qmap),
        out_shape=jax.ShapeDtypeStruct((n_seq * seq_len, D_MLA), bf16),
        scratch_shapes=[pltpu.VMEM((N_HEADS, tq, 1), f32), pltpu.VMEM((N_HEADS, tq, 1), f32),
                        pltpu.VMEM((tq, D_MLA), f32)],
        compiler_params=_cparams(3),
        name="attn_prompt",
    )(q, k, v, gm)


def _attn_sample_body(cckv_ref, ckr_ref, nckv_ref, nkr_ref, qa_ref, qr_ref, gm_ref, wukt_ref, wuv_ref,
                      o_ref, call_ref, kall_ref, s_ref, *, past, t_new, chunks):
    s_pad = call_ref.shape[0]
    rows = N_HEADS * t_new
    call_ref[0:past, :] = cckv_ref[0].astype(bf16)
    call_ref[past:past + t_new, :] = nckv_ref[0].astype(bf16)
    call_ref[past + t_new:s_pad, :] = jnp.zeros((s_pad - past - t_new, KV_RANK), bf16)
    kall_ref[0:past, :] = ckr_ref[0].astype(bf16)
    kall_ref[past:past + t_new, :] = nkr_ref[0].astype(bf16)
    kall_ref[past + t_new:s_pad, :] = jnp.zeros((s_pad - past - t_new, D_ROPE), bf16)

    qa = qa_ref[...].reshape(rows, KV_RANK)
    qr = qr_ref[...].reshape(rows, D_ROPE)
    for start, size in chunks:
        c = call_ref[start:start + size, :]
        k_t = _dot_nt(wukt_ref[...], c)
        ssq = jnp.sum((k_t * k_t).reshape(N_HEADS, D_NOPE, size), axis=1)
        r = lax.rsqrt(ssq * (1.0 / D_NOPE) + EPS)
        s_nope = _dot_nt(qa, c)
        s_rope = _dot_nt(qr, kall_ref[start:start + size, :])
        for h in range(N_HEADS):
            hs = slice(h * t_new, (h + 1) * t_new)
            s = (s_nope[hs] * r[h:h + 1, :] + s_rope[hs]) * ATTN_SCALE
            if start + size > past + t_new:
                key = start + lax.broadcasted_iota(jnp.int32, (t_new, size), 1)
                s = jnp.where(key < past + t_new, s, NEG_INF)
            s_ref[hs, start:start + size] = s

    s = s_ref[...]
    p = jnp.exp(s - jnp.max(s, axis=-1, keepdims=True))
    l = jnp.sum(p, axis=-1, keepdims=True)
    o_lat = (_dot(p.astype(bf16), call_ref[...]) / l).astype(bf16)
    for h in range(N_HEADS):
        o = _dot(o_lat[h * t_new:(h + 1) * t_new], wuv_ref[h])
        hs = slice(h * D_V, (h + 1) * D_V)
        o_ref[:, hs] = (o * gm_ref[:, hs].astype(f32)).astype(bf16)


def _attn_sample(cache_ckv, cache_krope, ckv_new, krope_new, q_abs, q_rope, gm, wts, *, t_new):
    n_seq, past, _ = cache_ckv.shape
    chunk = 512
    chunks = tuple((s, min(chunk, past - s)) for s in range(0, past, chunk)) + ((past, LANES),)
    s_pad = past + LANES
    body = functools.partial(_attn_sample_body, past=past, t_new=t_new, chunks=chunks)
    return pl.pallas_call(
        body,
        grid=(n_seq,),
        in_specs=[pl.BlockSpec((1, past, KV_RANK), lambda b: (b, 0, 0)),
                  pl.BlockSpec((1, past, D_ROPE), lambda b: (b, 0, 0)),
                  pl.BlockSpec((1, t_new, KV_RANK), lambda b: (b, 0, 0)),
                  pl.BlockSpec((1, t_new, D_ROPE), lambda b: (b, 0, 0)),
                  pl.BlockSpec((N_HEADS, t_new, KV_RANK), lambda b: (0, b, 0)),
                  pl.BlockSpec((N_HEADS, t_new, D_ROPE), lambda b: (0, b, 0)),
                  pl.BlockSpec((t_new, D_MLA), lambda b: (b, 0)),
                  _resident((N_HEADS * D_NOPE, KV_RANK)),
                  _resident((N_HEADS, KV_RANK, D_V))],
        out_specs=pl.BlockSpec((t_new, D_MLA), lambda b: (b, 0)),
        out_shape=jax.ShapeDtypeStruct((n_seq * t_new, D_MLA), bf16),
        scratch_shapes=[pltpu.VMEM((s_pad, KV_RANK), bf16), pltpu.VMEM((s_pad, D_ROPE), bf16),
                        pltpu.VMEM((N_HEADS * t_new, s_pad), f32)],
        compiler_params=_cparams(1),
        name="attn_sample",
    )(cache_ckv, cache_krope, ckv_new, krope_new, q_abs, q_rope, gm,
      wts["w_uk_t"].reshape(N_HEADS * D_NOPE, KV_RANK), wts["w_uv3"])


def _out_proj_body(x_ref, pa_ref, ma_ref, p_ref, wo_ref, png_ref, wg_ref, bg_ref, wple_ref, y_ref):
    h = x_ref[...] + _dot(pa_ref[...], wo_ref[0:D_POOL, :]) + _dot(ma_ref[...], wo_ref[D_POOL:, :])
    hn = (h * _rms(h, D_MODEL) * png_ref[...]).astype(bf16)
    gate = jax.nn.sigmoid(_dot(hn, wg_ref[...]) + bg_ref[...])
    y_ref[...] = h + gate * _dot(p_ref[...].astype(bf16), wple_ref[...])


def _out_proj(x2d, pool_act, mla_act, p2d, wts, *, tm):
    n_tok = x2d.shape[0]
    row = lambda i: (i, 0)
    return pl.pallas_call(
        _out_proj_body,
        grid=(n_tok // tm,),
        in_specs=[pl.BlockSpec((tm, D_MODEL), row), pl.BlockSpec((tm, D_POOL), row),
                  pl.BlockSpec((tm, D_MLA), row), pl.BlockSpec((tm, D_PLE), row),
                  _resident((D_MODEL, D_MODEL)), _resident((1, D_MODEL)),
                  _resident((D_MODEL, D_MODEL)), _resident((1, D_MODEL)),
                  _resident((D_PLE, D_MODEL))],
        out_specs=pl.BlockSpec((tm, D_MODEL), row),
        out_shape=jax.ShapeDtypeStruct((n_tok, D_MODEL), f32),
        compiler_params=_cparams(1),
        name="out_proj",
    )(x2d, pool_act, mla_act, p2d, wts["w_out"], wts["ple_norm_g"], wts["w_ple_gate"],
      wts["b_ple_gate"], wts["w_ple"])


def _rope_tables(pos0, t):
    pos = pos0 + jnp.arange(t)
    inv = ROPE_THETA ** (-(jnp.arange(0, D_ROPE, 2, dtype=f32) / D_ROPE))
    ang = pos.astype(f32)[:, None] * inv[None, :]
    cos, sin = jnp.cos(ang), jnp.sin(ang)
    zero = jnp.zeros((t, LANES - D_ROPE), f32)
    return (jnp.concatenate([cos, cos, zero], axis=-1), jnp.concatenate([-sin, sin, zero], axis=-1),
            pos.astype(f32)[:, None])


def _prep_weights(norm_g, w_in, q_norm_g, w_uq, kv_norm_g, w_ukv, q_nope_g, q_rope_g, k_nope_g,
                  k_rope_g, w_pool, pool_scale, w_out, ple_norm_g, w_ple_gate, b_ple_gate, w_ple):
    s_u, s_gp, s_cq, s_ckv, s_kr = 1024, 2048, 2560, 2816, 2880
    w_in_r = jnp.concatenate(
        [w_in[:, :s_gp], w_in[:, s_kr:], w_in[:, s_gp:s_kr],
         jnp.zeros((D_MODEL, D_IN_PAD - C_KR - D_ROPE), w_in.dtype)], axis=-1).astype(bf16)
    w_uq_r = jnp.pad(w_uq.reshape(Q_RANK, N_HEADS, D_NOPE + D_ROPE),
                     ((0, 0), (0, 0), (0, D_QK - D_NOPE - D_ROPE))).reshape(Q_RANK, N_HEADS * D_QK)
    w_ukv3 = w_ukv.reshape(KV_RANK, N_HEADS, D_NOPE + D_V)
    w_uk3, w_uv3 = w_ukv3[..., :D_NOPE], w_ukv3[..., D_NOPE:]
    pad_rope = lambda g: jnp.pad(g, (0, LANES - D_ROPE))[None, :]
    return {
        "norm_g": norm_g[None, :], "w_in": w_in_r, "q_norm_g": q_norm_g[None, :],
        "kv_norm_g": kv_norm_g[None, :], "k_rope_g": pad_rope(k_rope_g),
        "w_pool": w_pool.astype(bf16), "pool_scale": pool_scale[None, :],
        "w_uq": w_uq_r.astype(bf16), "q_nope_g": q_nope_g[None, :], "q_rope_g": pad_rope(q_rope_g),
        "k_nope_g": k_nope_g[None, :],
        "w_uk": w_uk3.reshape(KV_RANK, N_HEADS * D_NOPE).astype(bf16),
        "w_uv": w_uv3.reshape(KV_RANK, D_MLA).astype(bf16),
        "w_uk_t": jnp.transpose(w_uk3, (1, 2, 0)).astype(bf16),
        "w_uv3": jnp.transpose(w_uv3, (1, 0, 2)).astype(bf16),
        "w_out": w_out.astype(bf16), "ple_norm_g": ple_norm_g[None, :],
        "w_ple_gate": w_ple_gate.astype(bf16), "b_ple_gate": b_ple_gate[None, :],
        "w_ple": w_ple.astype(bf16),
    }


def _layer_prompt(x, p, wts):
    n_seq, seq_len, _ = x.shape
    x2d = x.reshape(n_seq * seq_len, D_MODEL)
    cos2, sin2, pos = _rope_tables(0, seq_len)
    tm = 256
    hist = jnp.zeros((1, HALO, D_POOL), f32)
    pact, gm, qlat, ckv, krope, npool = _in_proj(
        x2d, hist, cos2, sin2, pos, wts, nseg=1, seg_len=tm, n_seq=n_seq, carry=True)
    q = _q_proj_prompt(qlat, cos2, sin2, wts, tm=512)
    k, v = _kv_proj(ckv, krope, wts, tm=512)
    mact = _attn_prompt(q, k, v, gm, n_seq=n_seq, seq_len=seq_len, tq=256, tk=256)
    y = _out_proj(x2d, pact, mact, p.reshape(n_seq * seq_len, D_PLE), wts, tm=256)
    return (y.reshape(x.shape), ckv.reshape(n_seq, seq_len, KV_RANK),
            krope.reshape(n_seq, seq_len, D_ROPE), npool)


def _layer_sample(x, p, state_pool, cache_ckv, cache_krope, wts):
    n_seq, t_new, _ = x.shape
    past = cache_ckv.shape[1]
    n_tok = n_seq * t_new
    x2d = x.reshape(n_tok, D_MODEL)
    cos1, sin1, pos1 = _rope_tables(past, t_new)
    cos2, sin2, pos = (jnp.tile(a, (n_seq, 1)) for a in (cos1, sin1, pos1))
    hist = jnp.pad(state_pool, ((0, 0), (HALO - POOL_HIST, 0), (0, 0)))
    pact, gm, qlat, ckv, krope, npool = _in_proj(
        x2d, hist, cos2, sin2, pos, wts, nseg=n_seq, seg_len=t_new, n_seq=n_seq, carry=False)
    q_abs, q_rope = _q_proj_sample(qlat, cos2, sin2, wts)
    ckv3 = ckv.reshape(n_seq, t_new, KV_RANK)
    krope3 = krope.reshape(n_seq, t_new, D_ROPE)
    mact = _attn_sample(cache_ckv, cache_krope, ckv3, krope3, q_abs, q_rope, gm, wts, t_new=t_new)
    y = _out_proj(x2d, pact, mact, p.reshape(n_tok, D_PLE), wts, tm=256)
    return y.reshape(x.shape), ckv3, krope3, npool


def kernel(x_prompt, x_sample, cache_ckv, cache_krope, state_pool, p_prompt, p_sample, norm_g, w_in,
           q_norm_g, w_uq, kv_norm_g, w_ukv, q_nope_g, q_rope_g, k_nope_g, k_rope_g, w_pool,
           pool_scale, w_out, ple_norm_g, w_ple_gate, b_ple_gate, w_ple):
    depth = norm_g.shape[0]
    layer_w = (norm_g, w_in, q_norm_g, w_uq, kv_norm_g, w_ukv, q_nope_g, q_rope_g, k_nope_g, k_rope_g,
               w_pool, pool_scale, w_out, ple_norm_g, w_ple_gate, b_ple_gate, w_ple)
    yp, ys = x_prompt, x_sample
    outs = [[] for _ in range(6)]
    for i in range(depth):
        wts = _prep_weights(*(w[i] for w in layer_w))
        yp, c1, k1, s1 = _layer_prompt(yp, p_prompt[i], wts)
        ys, c2, k2, s2 = _layer_sample(ys, p_sample[i], state_pool[i], cache_ckv[i], cache_krope[i], wts)
        for lst, val in zip(outs, (c1, k1, s1, c2, k2, s2)):
            lst.append(val)
    return (yp, ys) + tuple(jnp.stack(o) for o in outs)
```

```python
import functools

import jax
import jax.numpy as jnp
import numpy as np
from jax import lax
from jax.experimental import pallas as pl
from jax.experimental.pallas import tpu as pltpu

D_MODEL = 2048
CHUNK = 64
D_POOL = 1024
POOL_WINDOWS = (2, 4, 8, 16)
POOL_GROUP = 256
POOL_HIST = 15
HALO = 16
N_HEADS = 8
D_NOPE = 128
D_ROPE = 64
D_V = 128
D_MLA = N_HEADS * D_V
Q_RANK = 512
KV_RANK = 256
D_PLE = 256
D_QK = 256
ROPE_THETA = 10000.0
EPS = 1e-6
ATTN_SCALE = (D_NOPE + D_ROPE) ** -0.5
NEG_INF = -1e30
EXP2_SCALE = ATTN_SCALE * float(np.log2(np.e))
LANES = 128

C_U, C_GP, C_Q, C_KV, C_KR, C_GM = 0, 1024, 2048, 2560, 2816, 2880
D_IN = 3904

VMEM_LIMIT = 56 * 1024 * 1024

f32 = jnp.float32
bf16 = jnp.bfloat16


def _cparams(n_axes):
    return pltpu.CompilerParams(dimension_semantics=("arbitrary",) * n_axes,
                                vmem_limit_bytes=VMEM_LIMIT)


def _resident(shape):
    nd = len(shape)
    return pl.BlockSpec(shape, lambda *_: (0,) * nd, pipeline_mode=pl.Buffered(1))


def _rms(x, n):
    return lax.rsqrt(jnp.sum(x * x, axis=-1, keepdims=True) * (1.0 / n) + EPS)


def _dot(a, b):
    return jnp.dot(a, b, preferred_element_type=f32)


def _dot_nt(a, b):
    return lax.dot_general(a, b, (((1,), (1,)), ((), ())), preferred_element_type=f32)


def _rope128(x, cos2, sin2):
    lane = lax.broadcasted_iota(jnp.int32, x.shape, 1)
    swapped = jnp.where(lane < D_ROPE // 2, pltpu.roll(x, LANES - D_ROPE // 2, 1),
                        pltpu.roll(x, D_ROPE // 2, 1))
    return x * cos2 + swapped * sin2


def _in_proj_body(x_ref, hist_ref, cos_ref, sin_ref, pos_ref, ng_ref, w_ref, qg_ref, kvg_ref, krg_ref,
                  wp_ref, ps_ref,
                  pact_ref, gm_ref, ql_ref, ckv_ref, kr_ref, npool_ref,
                  ucat_ref, *, nseg, seg_len, tiles_per_seq, carry):
    i = pl.program_id(0)
    tm = nseg * seg_len
    x = x_ref[...]
    xn = (x * _rms(x, D_MODEL) * ng_ref[...]).astype(bf16)

    u = _dot(xn, w_ref[:, C_U:C_U + D_POOL])
    if carry:
        @pl.when(i % tiles_per_seq == 0)
        def _():
            ucat_ref[:, 0:HALO, :] = jnp.zeros((nseg, HALO, D_POOL), f32)
    else:
        ucat_ref[:, 0:HALO, :] = hist_ref[...]
    ucat_ref[:, HALO:HALO + seg_len, :] = u.reshape(nseg, seg_len, D_POOL)

    pos = pos_ref[...]
    gp = _dot(xn, w_ref[:, C_GP:C_GP + D_POOL])
    gate = gp * jax.nn.sigmoid(gp) * ps_ref[...]
    for g, w in enumerate(POOL_WINDOWS):
        sl = slice(g * POOL_GROUP, (g + 1) * POOL_GROUP)
        acc = ucat_ref[:, HALO:HALO + seg_len, sl]
        for k in range(1, w):
            acc = acc + ucat_ref[:, HALO - k:HALO - k + seg_len, sl]
        acc = acc.reshape(tm, POOL_GROUP)
        inv_cnt = 1.0 / jnp.minimum(pos + 1.0, float(w))
        d = (acc * inv_cnt - u[:, sl]).astype(bf16)
        mixed = _dot(d, wp_ref[g])
        pact_ref[:, sl] = (mixed * gate[:, sl]).astype(bf16)

    npool_ref[...] = ucat_ref[:, seg_len + 1:seg_len + HALO, :]
    if carry:
        ucat_ref[:, 0:HALO, :] = ucat_ref[:, seg_len:seg_len + HALO, :]

    cq = _dot(xn, w_ref[:, C_Q:C_Q + Q_RANK])
    ql_ref[...] = (cq * _rms(cq, Q_RANK) * qg_ref[...]).astype(bf16)

    ckv = _dot(xn, w_ref[:, C_KV:C_KV + KV_RANK])
    ckv_ref[...] = ckv * _rms(ckv, KV_RANK) * kvg_ref[...]

    tail = _dot(xn, w_ref[:, C_KR:D_IN])
    lane = lax.broadcasted_iota(jnp.int32, (tm, LANES), 1)
    kr = jnp.where(lane < D_ROPE, tail[:, :LANES], 0.0)
    kr = kr * _rms(kr, D_ROPE) * krg_ref[...]
    kr_ref[...] = _rope128(kr, cos_ref[...], sin_ref[...])[:, :D_ROPE]
    gm = tail[:, D_ROPE:]
    gm_ref[...] = (gm * jax.nn.sigmoid(gm)).astype(bf16)


def _in_proj(x2d, hist, cos2, sin2, pos, wts, *, nseg, seg_len, n_seq, carry):
    n_tok = x2d.shape[0]
    tm = nseg * seg_len
    n_tiles = n_tok // tm
    tiles_per_seq = n_tiles // n_seq if carry else 1
    tab_tiles = cos2.shape[0] // tm
    row = lambda i: (i, 0)
    tab = lambda i: (i % tab_tiles, 0)
    if carry:
        hist_spec = pl.BlockSpec((1, HALO, D_POOL), lambda i: (0, 0, 0))
        npool_spec = pl.BlockSpec((1, POOL_HIST, D_POOL), lambda i: (i // tiles_per_seq, 0, 0))
    else:
        hist_spec = pl.BlockSpec((nseg, HALO, D_POOL), lambda i: (i, 0, 0))
        npool_spec = pl.BlockSpec((nseg, POOL_HIST, D_POOL), lambda i: (i, 0, 0))
    body = functools.partial(_in_proj_body, nseg=nseg, seg_len=seg_len,
                             tiles_per_seq=tiles_per_seq, carry=carry)
    return pl.pallas_call(
        body,
        grid=(n_tiles,),
        in_specs=[
            pl.BlockSpec((tm, D_MODEL), row),
            hist_spec,
            pl.BlockSpec((tm, LANES), tab),
            pl.BlockSpec((tm, LANES), tab),
            pl.BlockSpec((tm, 1), tab),
            _resident((1, D_MODEL)),
            _resident((D_MODEL, D_IN)),
            _resident((1, Q_RANK)),
            _resident((1, KV_RANK)),
            _resident((1, LANES)),
            _resident((4, POOL_GROUP, POOL_GROUP)),
            _resident((1, D_POOL)),
        ],
        out_specs=[
            pl.BlockSpec((tm, D_POOL), row),
            pl.BlockSpec((tm, D_MLA), row),
            pl.BlockSpec((tm, Q_RANK), row),
            pl.BlockSpec((tm, KV_RANK), row),
            pl.BlockSpec((tm, D_ROPE), row),
            npool_spec,
        ],
        out_shape=[
            jax.ShapeDtypeStruct((n_tok, D_POOL), bf16),
            jax.ShapeDtypeStruct((n_tok, D_MLA), bf16),
            jax.ShapeDtypeStruct((n_tok, Q_RANK), bf16),
            jax.ShapeDtypeStruct((n_tok, KV_RANK), f32),
            jax.ShapeDtypeStruct((n_tok, D_ROPE), f32),
            jax.ShapeDtypeStruct((n_seq, POOL_HIST, D_POOL), f32),
        ],
        scratch_shapes=[pltpu.VMEM((nseg, HALO + seg_len, D_POOL), f32)],
        compiler_params=_cparams(1),
        name="in_proj",
    )(x2d, hist, cos2, sin2, pos, wts["norm_g"], wts["w_in"], wts["q_norm_g"], wts["kv_norm_g"],
      wts["k_rope_g"], wts["w_pool"], wts["pool_scale"])


def _q_heads(ql_ref, wq_ref, cos_ref, sin_ref, qng_ref, qrg_ref):
    q = _dot(ql_ref[...], wq_ref[...])
    cos2, sin2 = cos_ref[...], sin_ref[...]
    for h in range(N_HEADS):
        qn = q[:, h * D_QK:h * D_QK + D_NOPE]
        qn = qn * _rms(qn, D_NOPE) * qng_ref[...]
        qr = q[:, h * D_QK + D_NOPE:(h + 1) * D_QK]
        qr = _rope128(qr * _rms(qr, D_ROPE) * qrg_ref[...], cos2, sin2)
        yield h, qn, qr


def _q_proj_prompt_body(ql_ref, wqt_ref, cos_ref, sin_ref, qng_ref, qrg_ref, qt_ref):
    q_t = _dot_nt(wqt_ref[...], ql_ref[...])
    tm = q_t.shape[1]
    cos, sin = cos_ref[...], sin_ref[...]
    half = D_ROPE // 2
    for h in range(N_HEADS):
        r0 = h * D_QK
        qn = q_t[r0:r0 + D_NOPE]
        rn = lax.rsqrt(jnp.sum(qn * qn, axis=0, keepdims=True) * (1.0 / D_NOPE) + EPS)
        qt_ref[0, r0:r0 + D_NOPE, :] = (qn * rn * qng_ref[...]).astype(bf16)
        qr = q_t[r0 + D_NOPE:r0 + D_NOPE + D_ROPE]
        rr = lax.rsqrt(jnp.sum(qr * qr, axis=0, keepdims=True) * (1.0 / D_ROPE) + EPS)
        qr = qr * rr * qrg_ref[...]
        x1, x2 = qr[:half], qr[half:]
        qt_ref[0, r0 + D_NOPE:r0 + D_NOPE + half, :] = (x1 * cos - x2 * sin).astype(bf16)
        qt_ref[0, r0 + D_NOPE + half:r0 + D_NOPE + D_ROPE, :] = (x2 * cos + x1 * sin).astype(bf16)
        qt_ref[0, r0 + D_NOPE + D_ROPE:r0 + D_QK, :] = jnp.zeros((D_QK - D_NOPE - D_ROPE, tm), bf16)


def _q_proj_prompt(qlat, cos_t, sin_t, wts, *, n_seq, seq_len, tm):
    tiles = seq_len // tm
    return pl.pallas_call(
        _q_proj_prompt_body,
        grid=(n_seq * tiles,),
        in_specs=[pl.BlockSpec((tm, Q_RANK), lambda i: (i, 0)),
                  _resident((N_HEADS * D_QK, Q_RANK)),
                  pl.BlockSpec((D_ROPE // 2, tm), lambda i: (0, i % tiles)),
                  pl.BlockSpec((D_ROPE // 2, tm), lambda i: (0, i % tiles)),
                  _resident((D_NOPE, 1)), _resident((D_ROPE, 1))],
        out_specs=pl.BlockSpec((1, N_HEADS * D_QK, tm), lambda i: (i // tiles, 0, i % tiles)),
        out_shape=jax.ShapeDtypeStruct((n_seq, N_HEADS * D_QK, seq_len), bf16),
        compiler_params=_cparams(1),
        name="q_proj_prompt",
    )(qlat, wts["w_uq_t"], cos_t, sin_t, wts["q_nope_g_col"], wts["q_rope_g_col"])


def _q_proj_sample_body(ql_ref, wq_ref, cos_ref, sin_ref, qng_ref, qrg_ref, kng_ref, wukt_ref,
                        qa_ref, qr_ref):
    for h, qn, qr in _q_heads(ql_ref, wq_ref, cos_ref, sin_ref, qng_ref, qrg_ref):
        qa_ref[h] = _dot((qn * kng_ref[...]).astype(bf16), wukt_ref[h]).astype(bf16)
        qr_ref[h] = qr[:, :D_ROPE].astype(bf16)


def _q_proj_sample(qlat, cos2, sin2, wts):
    n_tok = qlat.shape[0]
    return pl.pallas_call(
        _q_proj_sample_body,
        grid=(1,),
        in_specs=[_resident((n_tok, Q_RANK)), _resident((Q_RANK, N_HEADS * D_QK)),
                  _resident((n_tok, LANES)), _resident((n_tok, LANES)),
                  _resident((1, D_NOPE)), _resident((1, LANES)), _resident((1, D_NOPE)),
                  _resident((N_HEADS, D_NOPE, KV_RANK))],
        out_specs=[_resident((N_HEADS, n_tok, KV_RANK)), _resident((N_HEADS, n_tok, D_ROPE))],
        out_shape=[jax.ShapeDtypeStruct((N_HEADS, n_tok, KV_RANK), bf16),
                   jax.ShapeDtypeStruct((N_HEADS, n_tok, D_ROPE), bf16)],
        compiler_params=_cparams(1),
        name="q_proj_sample",
    )(qlat, wts["w_uq"], cos2, sin2, wts["q_nope_g"], wts["q_rope_g"], wts["k_nope_g"],
      wts["w_uk_t"])


def _kv_proj_body(ckv_ref, kr_ref, wk_ref, wvt_ref, kng_ref, k_ref, vt_ref):
    c = ckv_ref[...].astype(bf16)
    k = _dot(c, wk_ref[...])
    vt_ref[0, 0] = _dot_nt(wvt_ref[...], c).astype(bf16)
    tm = c.shape[0]
    kr = jnp.concatenate([kr_ref[...], jnp.zeros((tm, LANES - D_ROPE), f32)], axis=-1).astype(bf16)
    for h in range(N_HEADS):
        kn = k[:, h * D_NOPE:(h + 1) * D_NOPE]
        k_ref[:, h * D_QK:h * D_QK + D_NOPE] = (kn * _rms(kn, D_NOPE) * kng_ref[...]).astype(bf16)
        k_ref[:, h * D_QK + D_NOPE:(h + 1) * D_QK] = kr


def _kv_proj(ckv, krope, wts, *, n_seq, seq_len, tm):
    n_tok = ckv.shape[0]
    tiles = seq_len // tm
    row = lambda i: (i, 0)
    return pl.pallas_call(
        _kv_proj_body,
        grid=(n_tok // tm,),
        in_specs=[pl.BlockSpec((tm, KV_RANK), row), pl.BlockSpec((tm, D_ROPE), row),
                  _resident((KV_RANK, N_HEADS * D_NOPE)), _resident((D_MLA, KV_RANK)),
                  _resident((1, D_NOPE))],
        out_specs=[pl.BlockSpec((tm, N_HEADS * D_QK), row),
                   pl.BlockSpec((1, 1, D_MLA, tm), lambda i: (i // tiles, i % tiles, 0, 0))],
        out_shape=[jax.ShapeDtypeStruct((n_tok, N_HEADS * D_QK), bf16),
                   jax.ShapeDtypeStruct((n_seq, tiles, D_MLA, tm), bf16)],
        compiler_params=_cparams(1),
        name="kv_proj",
    )(ckv, krope, wts["w_uk"], wts["w_uv_t"], wts["k_nope_g"])


def _attn_prompt_body(qt_ref, k_ref, vt_ref, gm_ref, o_ref, m_ref, l_ref, acc_ref, s_ref, *, tq, tk):
    qi = pl.program_id(1)
    m_ref[...] = jnp.full(m_ref.shape, NEG_INF, f32)
    l_ref[...] = jnp.zeros(l_ref.shape, f32)
    acc_ref[...] = jnp.zeros(acc_ref.shape, f32)

    def kv_block(j, masked):
        row0 = pl.multiple_of(j * tk, tk)
        if masked:
            k_chunk = lax.broadcasted_iota(jnp.int32, (tk, tq), 0) // CHUNK
            q_chunk = lax.broadcasted_iota(jnp.int32, (tk, tq), 1) // CHUNK
            visible = k_chunk <= q_chunk
        blk_max = []
        for h in range(N_HEADS):
            kb = k_ref[pl.ds(row0, tk), h * D_QK:(h + 1) * D_QK]
            s = _dot(kb, qt_ref[0, h * D_QK:(h + 1) * D_QK, :])
            if masked:
                s = jnp.where(visible, s, NEG_INF)
            s_ref[h] = s
            blk_max.append(jnp.max(s, axis=0, keepdims=True))
        for h in range(N_HEADS):
            m_old = m_ref[h]
            m_new = jnp.maximum(m_old, blk_max[h])
            alpha = jnp.exp2((m_old - m_new) * EXP2_SCALE)
            p = jnp.exp2((s_ref[h] - m_new) * EXP2_SCALE)
            l_ref[h] = alpha * l_ref[h] + jnp.sum(p, axis=0, keepdims=True)
            m_ref[h] = m_new
            vt = vt_ref[0, j, h * D_V:(h + 1) * D_V, :]
            acc_ref[h] = alpha * acc_ref[h] + _dot(vt, p.astype(bf16))

    def full_block(j, carry):
        kv_block(j, False)
        return carry

    lax.fori_loop(0, qi, full_block, 0)
    kv_block(qi, True)

    for h in range(N_HEADS):
        hs = slice(h * D_V, (h + 1) * D_V)
        o = (acc_ref[h] / l_ref[h]).T
        o_ref[:, hs] = (o * gm_ref[:, hs].astype(f32)).astype(bf16)


def _attn_prompt(qt, k, vt, gm, *, n_seq, seq_len, tq):
    tk = tq
    nq = seq_len // tq
    body = functools.partial(_attn_prompt_body, tq=tq, tk=tk)
    return pl.pallas_call(
        body,
        grid=(n_seq, nq),
        in_specs=[pl.BlockSpec((1, N_HEADS * D_QK, tq), lambda b, i: (b, 0, i)),
                  pl.BlockSpec((seq_len, N_HEADS * D_QK), lambda b, i: (b, 0)),
                  pl.BlockSpec((1, seq_len // tk, D_MLA, tk), lambda b, i: (b, 0, 0, 0)),
                  pl.BlockSpec((tq, D_MLA), lambda b, i: (b * nq + i, 0))],
        out_specs=pl.BlockSpec((tq, D_MLA), lambda b, i: (b * nq + i, 0)),
        out_shape=jax.ShapeDtypeStruct((n_seq * seq_len, D_MLA), bf16),
        scratch_shapes=[pltpu.VMEM((N_HEADS, 1, tq), f32), pltpu.VMEM((N_HEADS, 1, tq), f32),
                        pltpu.VMEM((N_HEADS, D_V, tq), f32), pltpu.VMEM((N_HEADS, tk, tq), f32)],
        compiler_params=_cparams(2),
        name="attn_prompt",
    )(qt, k, vt, gm)


def _attn_sample_body(cckv_ref, ckr_t_ref, nckv_ref, nkr_ref, qa_ref, qr_ref, gm_ref, wukt_ref, wuv_ref,
                      o_ref, call_ref, ktail_ref, s_ref, *, past, t_new, chunks):
    s_pad = call_ref.shape[0]
    rows = N_HEADS * t_new
    call_ref[0:past, :] = cckv_ref[0].astype(bf16)
    call_ref[past:past + t_new, :] = nckv_ref[0].astype(bf16)
    call_ref[past + t_new:s_pad, :] = jnp.zeros((s_pad - past - t_new, KV_RANK), bf16)
    ktail_ref[0:t_new, :] = nkr_ref[0].astype(bf16)
    ktail_ref[t_new:, :] = jnp.zeros((s_pad - past - t_new, D_ROPE), bf16)

    qa = qa_ref[...].reshape(rows, KV_RANK)
    qr = qr_ref[...].reshape(rows, D_ROPE)
    for start, size in chunks:
        c = call_ref[start:start + size, :]
        k_t = _dot_nt(wukt_ref[...], c)
        ssq = jnp.sum((k_t * k_t).reshape(N_HEADS, D_NOPE, size), axis=1)
        r = lax.rsqrt(ssq * (1.0 / D_NOPE) + EPS)
        s_nope = _dot_nt(qa, c)
        if start < past:
            s_rope = _dot(qr, ckr_t_ref[0, :, start:start + size].astype(bf16))
        else:
            s_rope = _dot_nt(qr, ktail_ref[...])
        for h in range(N_HEADS):
            hs = slice(h * t_new, (h + 1) * t_new)
            s = (s_nope[hs] * r[h:h + 1, :] + s_rope[hs]) * ATTN_SCALE
            if start + size > past + t_new:
                key = start + lax.broadcasted_iota(jnp.int32, (t_new, size), 1)
                s = jnp.where(key < past + t_new, s, NEG_INF)
            s_ref[hs, start:start + size] = s

    s = s_ref[...]
    p = jnp.exp(s - jnp.max(s, axis=-1, keepdims=True))
    l = jnp.sum(p, axis=-1, keepdims=True)
    o_lat = (_dot(p.astype(bf16), call_ref[...]) / l).astype(bf16)
    for h in range(N_HEADS):
        o = _dot(o_lat[h * t_new:(h + 1) * t_new], wuv_ref[h])
        hs = slice(h * D_V, (h + 1) * D_V)
        o_ref[:, hs] = (o * gm_ref[:, hs].astype(f32)).astype(bf16)


def _attn_sample(cache_ckv, cache_krope_t, ckv_new, krope_new, q_abs, q_rope, gm, wts, *, t_new):
    n_seq, past, _ = cache_ckv.shape
    chunk = 512
    chunks = tuple((s, min(chunk, past - s)) for s in range(0, past, chunk)) + ((past, LANES),)
    s_pad = past + LANES
    body = functools.partial(_attn_sample_body, past=past, t_new=t_new, chunks=chunks)
    return pl.pallas_call(
        body,
        grid=(n_seq,),
        in_specs=[pl.BlockSpec((1, past, KV_RANK), lambda b: (b, 0, 0)),
                  pl.BlockSpec((1, D_ROPE, past), lambda b: (b, 0, 0)),
                  pl.BlockSpec((1, t_new, KV_RANK), lambda b: (b, 0, 0)),
                  pl.BlockSpec((1, t_new, D_ROPE), lambda b: (b, 0, 0)),
                  pl.BlockSpec((N_HEADS, t_new, KV_RANK), lambda b: (0, b, 0)),
                  pl.BlockSpec((N_HEADS, t_new, D_ROPE), lambda b: (0, b, 0)),
                  pl.BlockSpec((t_new, D_MLA), lambda b: (b, 0)),
                  _resident((N_HEADS * D_NOPE, KV_RANK)),
                  _resident((N_HEADS, KV_RANK, D_V))],
        out_specs=pl.BlockSpec((t_new, D_MLA), lambda b: (b, 0)),
        out_shape=jax.ShapeDtypeStruct((n_seq * t_new, D_MLA), bf16),
        scratch_shapes=[pltpu.VMEM((s_pad, KV_RANK), bf16), pltpu.VMEM((s_pad - past, D_ROPE), bf16),
                        pltpu.VMEM((N_HEADS * t_new, s_pad), f32)],
        compiler_params=_cparams(1),
        name="attn_sample",
    )(cache_ckv, cache_krope_t, ckv_new, krope_new, q_abs, q_rope, gm,
      wts["w_uk_t"].reshape(N_HEADS * D_NOPE, KV_RANK), wts["w_uv3"])


def _out_proj_body(x_ref, pa_ref, ma_ref, p_ref, wo_ref, png_ref, wg_ref, bg_ref, wple_ref, y_ref):
    h = x_ref[...] + _dot(pa_ref[...], wo_ref[0:D_POOL, :]) + _dot(ma_ref[...], wo_ref[D_POOL:, :])
    hn = (h * _rms(h, D_MODEL) * png_ref[...]).astype(bf16)
    gate = jax.nn.sigmoid(_dot(hn, wg_ref[...]) + bg_ref[...])
    y_ref[...] = h + gate * _dot(p_ref[...].astype(bf16), wple_ref[...])


def _out_proj(x2d, pool_act, mla_act, p2d, wts, *, tm):
    n_tok = x2d.shape[0]
    row = lambda i: (i, 0)
    return pl.pallas_call(
        _out_proj_body,
        grid=(n_tok // tm,),
        in_specs=[pl.BlockSpec((tm, D_MODEL), row), pl.BlockSpec((tm, D_POOL), row),
                  pl.BlockSpec((tm, D_MLA), row), pl.BlockSpec((tm, D_PLE), row),
                  _resident((D_MODEL, D_MODEL)), _resident((1, D_MODEL)),
                  _resident((D_MODEL, D_MODEL)), _resident((1, D_MODEL)),
                  _resident((D_PLE, D_MODEL))],
        out_specs=pl.BlockSpec((tm, D_MODEL), row),
        out_shape=jax.ShapeDtypeStruct((n_tok, D_MODEL), f32),
        compiler_params=_cparams(1),
        name="out_proj",
    )(x2d, pool_act, mla_act, p2d, wts["w_out"], wts["ple_norm_g"], wts["w_ple_gate"],
      wts["b_ple_gate"], wts["w_ple"])


def _rope_angles(pos0, t):
    pos = pos0 + jnp.arange(t)
    inv = ROPE_THETA ** (-(jnp.arange(0, D_ROPE, 2, dtype=f32) / D_ROPE))
    ang = pos.astype(f32)[:, None] * inv[None, :]
    return pos, jnp.cos(ang), jnp.sin(ang)


def _rope_tables(pos0, t):
    pos, cos, sin = _rope_angles(pos0, t)
    zero = jnp.zeros((t, LANES - D_ROPE), f32)
    return (jnp.concatenate([cos, cos, zero], axis=-1), jnp.concatenate([-sin, sin, zero], axis=-1),
            pos.astype(f32)[:, None])


def _prep_weights(norm_g, w_in, q_norm_g, w_uq, kv_norm_g, w_ukv, q_nope_g, q_rope_g, k_nope_g,
                  k_rope_g, w_pool, pool_scale, w_out, ple_norm_g, w_ple_gate, b_ple_gate, w_ple):
    w_uq_r = jnp.pad(w_uq.reshape(Q_RANK, N_HEADS, D_NOPE + D_ROPE),
                     ((0, 0), (0, 0), (0, D_QK - D_NOPE - D_ROPE))).reshape(Q_RANK, N_HEADS * D_QK)
    w_ukv3 = w_ukv.reshape(KV_RANK, N_HEADS, D_NOPE + D_V)
    w_uk3, w_uv3 = w_ukv3[..., :D_NOPE], w_ukv3[..., D_NOPE:]
    pad_rope = lambda g: jnp.pad(g, (0, LANES - D_ROPE))[None, :]
    return {
        "norm_g": norm_g[None, :], "w_in": w_in.astype(bf16), "q_norm_g": q_norm_g[None, :],
        "kv_norm_g": kv_norm_g[None, :], "k_rope_g": pad_rope(k_rope_g),
        "w_pool": w_pool.astype(bf16), "pool_scale": pool_scale[None, :],
        "w_uq": w_uq_r.astype(bf16), "q_nope_g": q_nope_g[None, :], "q_rope_g": pad_rope(q_rope_g),
        "w_uq_t": w_uq_r.T.astype(bf16),
        "q_nope_g_col": q_nope_g[:, None], "q_rope_g_col": q_rope_g[:, None],
        "k_nope_g": k_nope_g[None, :],
        "w_uk": w_uk3.reshape(KV_RANK, N_HEADS * D_NOPE).astype(bf16),
        "w_uv_t": w_uv3.reshape(KV_RANK, D_MLA).T.astype(bf16),
        "w_uk_t": jnp.transpose(w_uk3, (1, 2, 0)).astype(bf16),
        "w_uv3": jnp.transpose(w_uv3, (1, 0, 2)).astype(bf16),
        "w_out": w_out.astype(bf16), "ple_norm_g": ple_norm_g[None, :],
        "w_ple_gate": w_ple_gate.astype(bf16), "b_ple_gate": b_ple_gate[None, :],
        "w_ple": w_ple.astype(bf16),
    }


def _layer_prompt(x, p, wts):
    n_seq, seq_len, _ = x.shape
    x2d = x.reshape(n_seq * seq_len, D_MODEL)
    cos2, sin2, pos = _rope_tables(0, seq_len)
    tm = 256
    hist = jnp.zeros((1, HALO, D_POOL), f32)
    pact, gm, qlat, ckv, krope, npool = _in_proj(
        x2d, hist, cos2, sin2, pos, wts, nseg=1, seg_len=tm, n_seq=n_seq, carry=True)
    _, cos, sin = _rope_angles(0, seq_len)
    tq = 256
    qt = _q_proj_prompt(qlat, cos.T, sin.T, wts, n_seq=n_seq, seq_len=seq_len, tm=512)
    k, vt = _kv_proj(ckv, krope, wts, n_seq=n_seq, seq_len=seq_len, tm=tq)
    mact = _attn_prompt(qt, k, vt, gm, n_seq=n_seq, seq_len=seq_len, tq=tq)
    y = _out_proj(x2d, pact, mact, p.reshape(n_seq * seq_len, D_PLE), wts, tm=256)
    return (y.reshape(x.shape), ckv.reshape(n_seq, seq_len, KV_RANK),
            krope.reshape(n_seq, seq_len, D_ROPE), npool)


def _layer_sample(x, p, state_pool, cache_ckv, cache_krope, wts):
    n_seq, t_new, _ = x.shape
    past = cache_ckv.shape[1]
    n_tok = n_seq * t_new
    x2d = x.reshape(n_tok, D_MODEL)
    cos1, sin1, pos1 = _rope_tables(past, t_new)
    cos2, sin2, pos = (jnp.tile(a, (n_seq, 1)) for a in (cos1, sin1, pos1))
    hist = jnp.pad(state_pool, ((0, 0), (HALO - POOL_HIST, 0), (0, 0)))
    pact, gm, qlat, ckv, krope, npool = _in_proj(
        x2d, hist, cos2, sin2, pos, wts, nseg=n_seq, seg_len=t_new, n_seq=n_seq, carry=False)
    q_abs, q_rope = _q_proj_sample(qlat, cos2, sin2, wts)
    ckv3 = ckv.reshape(n_seq, t_new, KV_RANK)
    krope3 = krope.reshape(n_seq, t_new, D_ROPE)
    cache_krope_t = jnp.transpose(cache_krope, (0, 2, 1))
    mact = _attn_sample(cache_ckv, cache_krope_t, ckv3, krope3, q_abs, q_rope, gm, wts, t_new=t_new)
    y = _out_proj(x2d, pact, mact, p.reshape(n_tok, D_PLE), wts, tm=256)
    return y.reshape(x.shape), ckv3, krope3, npool


def kernel(x_prompt, x_sample, cache_ckv, cache_krope, state_pool, p_prompt, p_sample, norm_g, w_in,
           q_norm_g, w_uq, kv_norm_g, w_ukv, q_nope_g, q_rope_g, k_nope_g, k_rope_g, w_pool,
           pool_scale, w_out, ple_norm_g, w_ple_gate, b_ple_gate, w_ple):
    depth = norm_g.shape[0]
    layer_w = (norm_g, w_in, q_norm_g, w_uq, kv_norm_g, w_ukv, q_nope_g, q_rope_g, k_nope_g, k_rope_g,
               w_pool, pool_scale, w_out, ple_norm_g, w_ple_gate, b_ple_gate, w_ple)
    yp, ys = x_prompt, x_sample
    outs = [[] for _ in range(6)]
    for i in range(depth):
        wts = _prep_weights(*(w[i] for w in layer_w))
        yp, c1, k1, s1 = _layer_prompt(yp, p_prompt[i], wts)
        ys, c2, k2, s2 = _layer_sample(ys, p_sample[i], state_pool[i], cache_ckv[i], cache_krope[i], wts)
        for lst, val in zip(outs, (c1, k1, s1, c2, k2, s2)):
            lst.append(val)
    return (yp, ys) + tuple(jnp.stack(o) for o in outs)
```

```python
import functools

import jax
import jax.numpy as jnp
import numpy as np
from jax import lax
from jax.experimental import pallas as pl
from jax.experimental.pallas import tpu as pltpu

D_MODEL = 2048
CHUNK = 64
D_POOL = 1024
POOL_WINDOWS = (2, 4, 8, 16)
POOL_GROUP = 256
POOL_HIST = 15
HALO = 16
N_HEADS = 8
D_NOPE = 128
D_ROPE = 64
D_V = 128
D_VX = D_V + 16
D_MLA = N_HEADS * D_V
Q_RANK = 512
KV_RANK = 256
D_PLE = 256
D_QK = 256
ROPE_THETA = 10000.0
EPS = 1e-6
ATTN_SCALE = (D_NOPE + D_ROPE) ** -0.5
NEG_INF = -1e30
EXP2_SCALE = ATTN_SCALE * float(np.log2(np.e))
LANES = 128

C_U, C_GP, C_Q, C_KV, C_KR, C_GM = 0, 1024, 2048, 2560, 2816, 2880
D_IN = 3904

VMEM_LIMIT = 56 * 1024 * 1024

f32 = jnp.float32
bf16 = jnp.bfloat16


def _cparams(n_axes):
    return pltpu.CompilerParams(dimension_semantics=("arbitrary",) * n_axes,
                                vmem_limit_bytes=VMEM_LIMIT)


def _resident(shape):
    nd = len(shape)
    return pl.BlockSpec(shape, lambda *_: (0,) * nd, pipeline_mode=pl.Buffered(1))


def _rms(x, n):
    return lax.rsqrt(jnp.sum(x * x, axis=-1, keepdims=True) * (1.0 / n) + EPS)


def _dot(a, b):
    return jnp.dot(a, b, preferred_element_type=f32)


def _dot_nt(a, b):
    return lax.dot_general(a, b, (((1,), (1,)), ((), ())), preferred_element_type=f32)


def _rope128(x, cos2, sin2):
    lane = lax.broadcasted_iota(jnp.int32, x.shape, 1)
    swapped = jnp.where(lane < D_ROPE // 2, pltpu.roll(x, LANES - D_ROPE // 2, 1),
                        pltpu.roll(x, D_ROPE // 2, 1))
    return x * cos2 + swapped * sin2


def _in_proj_body(x_ref, hist_ref, cos_ref, sin_ref, pos_ref, ng_ref, w_ref, qg_ref, kvg_ref, krg_ref,
                  wp_ref, ps_ref,
                  pact_ref, gm_ref, ql_ref, ckv_ref, kr_ref, npool_ref,
                  ucat_ref, *, nseg, seg_len, tiles_per_seq, carry):
    i = pl.program_id(0)
    tm = nseg * seg_len
    x = x_ref[...]
    xn = (x * _rms(x, D_MODEL) * ng_ref[...]).astype(bf16)

    u = _dot(xn, w_ref[:, C_U:C_U + D_POOL])
    if carry:
        @pl.when(i % tiles_per_seq == 0)
        def _():
            ucat_ref[:, 0:HALO, :] = jnp.zeros((nseg, HALO, D_POOL), f32)
    else:
        ucat_ref[:, 0:HALO, :] = hist_ref[...]
    ucat_ref[:, HALO:HALO + seg_len, :] = u.reshape(nseg, seg_len, D_POOL)

    pos = pos_ref[...]
    gp = _dot(xn, w_ref[:, C_GP:C_GP + D_POOL])
    gate = gp * jax.nn.sigmoid(gp) * ps_ref[...]

    def pool_group(g):
        w = POOL_WINDOWS[g]
        sl = slice(g * POOL_GROUP, (g + 1) * POOL_GROUP)
        acc = ucat_ref[:, HALO:HALO + seg_len, sl]
        for k in range(1, w):
            acc = acc + ucat_ref[:, HALO - k:HALO - k + seg_len, sl]
        acc = acc.reshape(tm, POOL_GROUP)
        inv_cnt = 1.0 / jnp.minimum(pos + 1.0, float(w))
        d = (acc * inv_cnt - u[:, sl]).astype(bf16)
        mixed = _dot(d, wp_ref[g])
        pact_ref[:, sl] = (mixed * gate[:, sl]).astype(bf16)

    cq = _dot(xn, w_ref[:, C_Q:C_Q + Q_RANK])
    pool_group(0)
    pool_group(1)
    ql_ref[...] = (cq * _rms(cq, Q_RANK) * qg_ref[...]).astype(bf16)

    ckv = _dot(xn, w_ref[:, C_KV:C_KV + KV_RANK])
    pool_group(2)
    ckv_ref[...] = ckv * _rms(ckv, KV_RANK) * kvg_ref[...]

    tail = _dot(xn, w_ref[:, C_KR:D_IN])
    pool_group(3)
    npool_ref[...] = ucat_ref[:, seg_len + 1:seg_len + HALO, :]
    if carry:
        ucat_ref[:, 0:HALO, :] = ucat_ref[:, seg_len:seg_len + HALO, :]
    lane = lax.broadcasted_iota(jnp.int32, (tm, LANES), 1)
    kr = jnp.where(lane < D_ROPE, tail[:, :LANES], 0.0)
    kr = kr * _rms(kr, D_ROPE) * krg_ref[...]
    kr_ref[...] = _rope128(kr, cos_ref[...], sin_ref[...])[:, :D_ROPE]
    gm = tail[:, D_ROPE:]
    gm_ref[...] = (gm * jax.nn.sigmoid(gm)).astype(bf16)


def _in_proj(x2d, hist, cos2, sin2, pos, wts, *, nseg, seg_len, n_seq, carry):
    n_tok = x2d.shape[0]
    tm = nseg * seg_len
    n_tiles = n_tok // tm
    tiles_per_seq = n_tiles // n_seq if carry else 1
    tab_tiles = cos2.shape[0] // tm
    row = lambda i: (i, 0)
    tab = lambda i: (i % tab_tiles, 0)
    if carry:
        hist_spec = pl.BlockSpec((1, HALO, D_POOL), lambda i: (0, 0, 0))
        npool_spec = pl.BlockSpec((1, POOL_HIST, D_POOL), lambda i: (i // tiles_per_seq, 0, 0))
    else:
        hist_spec = pl.BlockSpec((nseg, HALO, D_POOL), lambda i: (i, 0, 0))
        npool_spec = pl.BlockSpec((nseg, POOL_HIST, D_POOL), lambda i: (i, 0, 0))
    body = functools.partial(_in_proj_body, nseg=nseg, seg_len=seg_len,
                             tiles_per_seq=tiles_per_seq, carry=carry)
    return pl.pallas_call(
        body,
        grid=(n_tiles,),
        in_specs=[
            pl.BlockSpec((tm, D_MODEL), row),
            hist_spec,
            pl.BlockSpec((tm, LANES), tab),
            pl.BlockSpec((tm, LANES), tab),
            pl.BlockSpec((tm, 1), tab),
            _resident((1, D_MODEL)),
            _resident((D_MODEL, D_IN)),
            _resident((1, Q_RANK)),
            _resident((1, KV_RANK)),
            _resident((1, LANES)),
            _resident((4, POOL_GROUP, POOL_GROUP)),
            _resident((1, D_POOL)),
        ],
        out_specs=[
            pl.BlockSpec((tm, D_POOL), row),
            pl.BlockSpec((tm, D_MLA), row),
            pl.BlockSpec((tm, Q_RANK), row),
            pl.BlockSpec((tm, KV_RANK), row),
            pl.BlockSpec((tm, D_ROPE), row),
            npool_spec,
        ],
        out_shape=[
            jax.ShapeDtypeStruct((n_tok, D_POOL), bf16),
            jax.ShapeDtypeStruct((n_tok, D_MLA), bf16),
            jax.ShapeDtypeStruct((n_tok, Q_RANK), bf16),
            jax.ShapeDtypeStruct((n_tok, KV_RANK), f32),
            jax.ShapeDtypeStruct((n_tok, D_ROPE), f32),
            jax.ShapeDtypeStruct((n_seq, POOL_HIST, D_POOL), f32),
        ],
        scratch_shapes=[pltpu.VMEM((nseg, HALO + seg_len, D_POOL), f32)],
        compiler_params=_cparams(1),
        name="in_proj",
    )(x2d, hist, cos2, sin2, pos, wts["norm_g"], wts["w_in"], wts["q_norm_g"], wts["kv_norm_g"],
      wts["k_rope_g"], wts["w_pool"], wts["pool_scale"])


def _q_heads(ql_ref, wq_ref, cos_ref, sin_ref, qng_ref, qrg_ref):
    q = _dot(ql_ref[...], wq_ref[...])
    cos2, sin2 = cos_ref[...], sin_ref[...]
    for h in range(N_HEADS):
        qn = q[:, h * D_QK:h * D_QK + D_NOPE]
        qn = qn * _rms(qn, D_NOPE) * qng_ref[...]
        qr = q[:, h * D_QK + D_NOPE:(h + 1) * D_QK]
        qr = _rope128(qr * _rms(qr, D_ROPE) * qrg_ref[...], cos2, sin2)
        yield h, qn, qr


def _q_proj_prompt_body(ql_ref, wqt_ref, cos_ref, sin_ref, qng_ref, qrg_ref, qt_ref):
    q_t = _dot_nt(wqt_ref[...], ql_ref[...])
    tm = q_t.shape[1]
    cos, sin = cos_ref[...], sin_ref[...]
    half = D_ROPE // 2
    for h in range(N_HEADS):
        r0 = h * D_QK
        qn = q_t[r0:r0 + D_NOPE]
        rn = lax.rsqrt(jnp.sum(qn * qn, axis=0, keepdims=True) * (1.0 / D_NOPE) + EPS)
        qt_ref[0, r0:r0 + D_NOPE, :] = (qn * rn * qng_ref[...]).astype(bf16)
        qr = q_t[r0 + D_NOPE:r0 + D_NOPE + D_ROPE]
        rr = lax.rsqrt(jnp.sum(qr * qr, axis=0, keepdims=True) * (1.0 / D_ROPE) + EPS)
        qr = qr * rr * qrg_ref[...]
        x1, x2 = qr[:half], qr[half:]
        qt_ref[0, r0 + D_NOPE:r0 + D_NOPE + half, :] = (x1 * cos - x2 * sin).astype(bf16)
        qt_ref[0, r0 + D_NOPE + half:r0 + D_NOPE + D_ROPE, :] = (x2 * cos + x1 * sin).astype(bf16)
        qt_ref[0, r0 + D_NOPE + D_ROPE:r0 + D_QK, :] = jnp.zeros((D_QK - D_NOPE - D_ROPE, tm), bf16)


def _q_proj_prompt(qlat, cos_t, sin_t, wts, *, n_seq, seq_len, tm):
    tiles = seq_len // tm
    return pl.pallas_call(
        _q_proj_prompt_body,
        grid=(n_seq * tiles,),
        in_specs=[pl.BlockSpec((tm, Q_RANK), lambda i: (i, 0)),
                  _resident((N_HEADS * D_QK, Q_RANK)),
                  pl.BlockSpec((D_ROPE // 2, tm), lambda i: (0, i % tiles)),
                  pl.BlockSpec((D_ROPE // 2, tm), lambda i: (0, i % tiles)),
                  _resident((D_NOPE, 1)), _resident((D_ROPE, 1))],
        out_specs=pl.BlockSpec((1, N_HEADS * D_QK, tm), lambda i: (i // tiles, 0, i % tiles)),
        out_shape=jax.ShapeDtypeStruct((n_seq, N_HEADS * D_QK, seq_len), bf16),
        compiler_params=_cparams(1),
        name="q_proj_prompt",
    )(qlat, wts["w_uq_t"], cos_t, sin_t, wts["q_nope_g_col"], wts["q_rope_g_col"])


def _q_proj_sample_body(ql_ref, wq_ref, cos_ref, sin_ref, qng_ref, qrg_ref, kng_ref, wukt_ref,
                        qa_ref, qr_ref):
    for h, qn, qr in _q_heads(ql_ref, wq_ref, cos_ref, sin_ref, qng_ref, qrg_ref):
        qa_ref[h] = _dot((qn * kng_ref[...]).astype(bf16), wukt_ref[h]).astype(bf16)
        qr_ref[h] = qr[:, :D_ROPE].astype(bf16)


def _q_proj_sample(qlat, cos2, sin2, wts):
    n_tok = qlat.shape[0]
    return pl.pallas_call(
        _q_proj_sample_body,
        grid=(1,),
        in_specs=[_resident((n_tok, Q_RANK)), _resident((Q_RANK, N_HEADS * D_QK)),
                  _resident((n_tok, LANES)), _resident((n_tok, LANES)),
                  _resident((1, D_NOPE)), _resident((1, LANES)), _resident((1, D_NOPE)),
                  _resident((N_HEADS, D_NOPE, KV_RANK))],
        out_specs=[_resident((N_HEADS, n_tok, KV_RANK)), _resident((N_HEADS, n_tok, D_ROPE))],
        out_shape=[jax.ShapeDtypeStruct((N_HEADS, n_tok, KV_RANK), bf16),
                   jax.ShapeDtypeStruct((N_HEADS, n_tok, D_ROPE), bf16)],
        compiler_params=_cparams(1),
        name="q_proj_sample",
    )(qlat, wts["w_uq"], cos2, sin2, wts["q_nope_g"], wts["q_rope_g"], wts["k_nope_g"],
      wts["w_uk_t"])


def _kv_proj_body(ckv_ref, kr_ref, wk_ref, wvt_ref, kng_ref, k_ref, vt_ref):
    c = ckv_ref[...].astype(bf16)
    k = _dot(c, wk_ref[...])
    tm = c.shape[0]
    v_t = _dot_nt(wvt_ref[...], c).astype(bf16)
    row = lax.broadcasted_iota(jnp.int32, (D_VX - D_V, tm), 0)
    ones_row = jnp.where(row == 0, 1.0, 0.0).astype(bf16)
    for h in range(N_HEADS):
        vt_ref[0, 0, h, 0:D_V, :] = v_t[h * D_V:(h + 1) * D_V]
        vt_ref[0, 0, h, D_V:D_VX, :] = ones_row
    kr = jnp.concatenate([kr_ref[...], jnp.zeros((tm, LANES - D_ROPE), f32)], axis=-1).astype(bf16)
    for h in range(N_HEADS):
        kn = k[:, h * D_NOPE:(h + 1) * D_NOPE]
        k_ref[:, h * D_QK:h * D_QK + D_NOPE] = (kn * _rms(kn, D_NOPE) * kng_ref[...]).astype(bf16)
        k_ref[:, h * D_QK + D_NOPE:(h + 1) * D_QK] = kr


def _kv_proj(ckv, krope, wts, *, n_seq, seq_len, tm):
    n_tok = ckv.shape[0]
    tiles = seq_len // tm
    row = lambda i: (i, 0)
    return pl.pallas_call(
        _kv_proj_body,
        grid=(n_tok // tm,),
        in_specs=[pl.BlockSpec((tm, KV_RANK), row), pl.BlockSpec((tm, D_ROPE), row),
                  _resident((KV_RANK, N_HEADS * D_NOPE)), _resident((D_MLA, KV_RANK)),
                  _resident((1, D_NOPE))],
        out_specs=[pl.BlockSpec((tm, N_HEADS * D_QK), row),
                   pl.BlockSpec((1, 1, N_HEADS, D_VX, tm), lambda i: (i // tiles, i % tiles, 0, 0, 0))],
        out_shape=[jax.ShapeDtypeStruct((n_tok, N_HEADS * D_QK), bf16),
                   jax.ShapeDtypeStruct((n_seq, tiles, N_HEADS, D_VX, tm), bf16)],
        compiler_params=_cparams(1),
        name="kv_proj",
    )(ckv, krope, wts["w_uk"], wts["w_uv_t"], wts["k_nope_g"])


def _attn_prompt_body(qt_ref, k_ref, vt_ref, gm_ref, o_ref, m_ref, acc_ref, s_ref, mx_ref, *, tq, tk):
    qi = pl.program_id(1)
    m_ref[...] = jnp.full(m_ref.shape, NEG_INF, f32)
    acc_ref[...] = jnp.zeros(acc_ref.shape, f32)

    def scores(h, j, buf, masked):
        row0 = pl.multiple_of(j * tk, tk)
        kb = k_ref[pl.ds(row0, tk), h * D_QK:(h + 1) * D_QK]
        s = _dot(kb, qt_ref[0, h * D_QK:(h + 1) * D_QK, :])
        if masked:
            k_chunk = lax.broadcasted_iota(jnp.int32, (tk, tq), 0) // CHUNK
            q_chunk = lax.broadcasted_iota(jnp.int32, (tk, tq), 1) // CHUNK
            s = jnp.where(k_chunk <= q_chunk, s, NEG_INF)
        s_ref[buf, h] = s
        mx_ref[buf, h] = jnp.max(s, axis=0, keepdims=True)

    def values(h, j, buf):
        m_old = m_ref[h]
        m_new = jnp.maximum(m_old, mx_ref[buf, h])
        alpha = jnp.exp2((m_old - m_new) * EXP2_SCALE)
        p = jnp.exp2((s_ref[buf, h] - m_new) * EXP2_SCALE).astype(bf16)
        m_ref[h] = m_new
        acc_ref[h] = alpha * acc_ref[h] + _dot(vt_ref[0, j, h], p)

    for h in range(N_HEADS):
        scores(h, qi, 0, True)

    def step(t, buf):
        prev = jnp.where(t == 1, qi, t - 2)
        for h in range(N_HEADS):
            scores(h, t - 1, buf, False)
            values(h, prev, 1 - buf)

    def step_pair(u, carry):
        step(2 * u + 1, 1)
        step(2 * u + 2, 0)
        return carry

    lax.fori_loop(0, qi // 2, step_pair, 0)

    @pl.when(qi % 2 == 1)
    def _():
        step(qi, 1)
        for h in range(N_HEADS):
            values(h, qi - 1, 1)

    @pl.when(qi % 2 == 0)
    def _():
        last = jnp.where(qi == 0, qi, qi - 1)
        for h in range(N_HEADS):
            values(h, last, 0)

    for h in range(N_HEADS):
        hs = slice(h * D_V, (h + 1) * D_V)
        o = (acc_ref[h, 0:D_V, :] / acc_ref[h, D_V:D_V + 1, :]).T
        o_ref[:, hs] = (o * gm_ref[:, hs].astype(f32)).astype(bf16)


def _attn_prompt(qt, k, vt, gm, *, n_seq, seq_len, tq):
    tk = tq
    nq = seq_len // tq
    body = functools.partial(_attn_prompt_body, tq=tq, tk=tk)
    return pl.pallas_call(
        body,
        grid=(n_seq, nq),
        in_specs=[pl.BlockSpec((1, N_HEADS * D_QK, tq), lambda b, i: (b, 0, i)),
                  pl.BlockSpec((seq_len, N_HEADS * D_QK), lambda b, i: (b, 0)),
                  pl.BlockSpec((1, seq_len // tk, N_HEADS, D_VX, tk), lambda b, i: (b, 0, 0, 0, 0)),
                  pl.BlockSpec((tq, D_MLA), lambda b, i: (b * nq + i, 0))],
        out_specs=pl.BlockSpec((tq, D_MLA), lambda b, i: (b * nq + i, 0)),
        out_shape=jax.ShapeDtypeStruct((n_seq * seq_len, D_MLA), bf16),
        scratch_shapes=[pltpu.VMEM((N_HEADS, 1, tq), f32), pltpu.VMEM((N_HEADS, D_VX, tq), f32),
                        pltpu.VMEM((2, N_HEADS, tk, tq), f32), pltpu.VMEM((2, N_HEADS, 1, tq), f32)],
        compiler_params=_cparams(2),
        name="attn_prompt",
    )(qt, k, vt, gm)


def _attn_sample_body(cckv_ref, ckr_t_ref, nckv_ref, nkr_ref, qa_ref, qr_ref, gm_ref, wukt_ref, wuv_ref,
                      o_ref, call_ref, ktail_ref, s_ref, *, past, t_new, chunks):
    s_pad = call_ref.shape[0]
    rows = N_HEADS * t_new
    call_ref[0:past, :] = cckv_ref[0].astype(bf16)
    call_ref[past:past + t_new, :] = nckv_ref[0].astype(bf16)
    call_ref[past + t_new:s_pad, :] = jnp.zeros((s_pad - past - t_new, KV_RANK), bf16)
    ktail_ref[0:t_new, :] = nkr_ref[0].astype(bf16)
    ktail_ref[t_new:, :] = jnp.zeros((s_pad - past - t_new, D_ROPE), bf16)

    qa = qa_ref[...].reshape(rows, KV_RANK)
    qr = qr_ref[...].reshape(rows, D_ROPE)
    for start, size in chunks:
        c = call_ref[start:start + size, :]
        k_t = _dot_nt(wukt_ref[...], c)
        ssq = jnp.sum((k_t * k_t).reshape(N_HEADS, D_NOPE, size), axis=1)
        r = lax.rsqrt(ssq * (1.0 / D_NOPE) + EPS)
        s_nope = _dot_nt(qa, c)
        if start < past:
            s_rope = _dot(qr, ckr_t_ref[0, :, start:start + size].astype(bf16))
        else:
            s_rope = _dot_nt(qr, ktail_ref[...])
        for h in range(N_HEADS):
            hs = slice(h * t_new, (h + 1) * t_new)
            s = (s_nope[hs] * r[h:h + 1, :] + s_rope[hs]) * ATTN_SCALE
            if start + size > past + t_new:
                key = start + lax.broadcasted_iota(jnp.int32, (t_new, size), 1)
                s = jnp.where(key < past + t_new, s, NEG_INF)
            s_ref[hs, start:start + size] = s

    s = s_ref[...]
    p = jnp.exp(s - jnp.max(s, axis=-1, keepdims=True))
    l = jnp.sum(p, axis=-1, keepdims=True)
    o_lat = (_dot(p.astype(bf16), call_ref[...]) / l).astype(bf16)
    for h in range(N_HEADS):
        o = _dot(o_lat[h * t_new:(h + 1) * t_new], wuv_ref[h])
        hs = slice(h * D_V, (h + 1) * D_V)
        o_ref[:, hs] = (o * gm_ref[:, hs].astype(f32)).astype(bf16)


def _attn_sample(cache_ckv, cache_krope_t, ckv_new, krope_new, q_abs, q_rope, gm, wts, *, t_new):
    n_seq, past, _ = cache_ckv.shape
    chunk = 512
    chunks = tuple((s, min(chunk, past - s)) for s in range(0, past, chunk)) + ((past, LANES),)
    s_pad = past + LANES
    body = functools.partial(_attn_sample_body, past=past, t_new=t_new, chunks=chunks)
    return pl.pallas_call(
        body,
        grid=(n_seq,),
        in_specs=[pl.BlockSpec((1, past, KV_RANK), lambda b: (b, 0, 0)),
                  pl.BlockSpec((1, D_ROPE, past), lambda b: (b, 0, 0)),
                  pl.BlockSpec((1, t_new, KV_RANK), lambda b: (b, 0, 0)),
                  pl.BlockSpec((1, t_new, D_ROPE), lambda b: (b, 0, 0)),
                  pl.BlockSpec((N_HEADS, t_new, KV_RANK), lambda b: (0, b, 0)),
                  pl.BlockSpec((N_HEADS, t_new, D_ROPE), lambda b: (0, b, 0)),
                  pl.BlockSpec((t_new, D_MLA), lambda b: (b, 0)),
                  _resident((N_HEADS * D_NOPE, KV_RANK)),
                  _resident((N_HEADS, KV_RANK, D_V))],
        out_specs=pl.BlockSpec((t_new, D_MLA), lambda b: (b, 0)),
        out_shape=jax.ShapeDtypeStruct((n_seq * t_new, D_MLA), bf16),
        scratch_shapes=[pltpu.VMEM((s_pad, KV_RANK), bf16), pltpu.VMEM((s_pad - past, D_ROPE), bf16),
                        pltpu.VMEM((N_HEADS * t_new, s_pad), f32)],
        compiler_params=_cparams(1),
        name="attn_sample",
    )(cache_ckv, cache_krope_t, ckv_new, krope_new, q_abs, q_rope, gm,
      wts["w_uk_t"].reshape(N_HEADS * D_NOPE, KV_RANK), wts["w_uv3"])


def _out_proj_body(x_ref, pa_ref, ma_ref, p_ref, wo_ref, png_ref, wg_ref, bg_ref, wple_ref, y_ref):
    h = x_ref[...] + _dot(pa_ref[...], wo_ref[0:D_POOL, :]) + _dot(ma_ref[...], wo_ref[D_POOL:, :])
    hn = (h * _rms(h, D_MODEL) * png_ref[...]).astype(bf16)
    gate = jax.nn.sigmoid(_dot(hn, wg_ref[...]) + bg_ref[...])
    y_ref[...] = h + gate * _dot(p_ref[...].astype(bf16), wple_ref[...])


def _out_proj(x2d, pool_act, mla_act, p2d, wts, *, tm):
    n_tok = x2d.shape[0]
    row = lambda i: (i, 0)
    return pl.pallas_call(
        _out_proj_body,
        grid=(n_tok // tm,),
        in_specs=[pl.BlockSpec((tm, D_MODEL), row), pl.BlockSpec((tm, D_POOL), row),
                  pl.BlockSpec((tm, D_MLA), row), pl.BlockSpec((tm, D_PLE), row),
                  _resident((D_MODEL, D_MODEL)), _resident((1, D_MODEL)),
                  _resident((D_MODEL, D_MODEL)), _resident((1, D_MODEL)),
                  _resident((D_PLE, D_MODEL))],
        out_specs=pl.BlockSpec((tm, D_MODEL), row),
        out_shape=jax.ShapeDtypeStruct((n_tok, D_MODEL), f32),
        compiler_params=_cparams(1),
        name="out_proj",
    )(x2d, pool_act, mla_act, p2d, wts["w_out"], wts["ple_norm_g"], wts["w_ple_gate"],
      wts["b_ple_gate"], wts["w_ple"])


def _rope_angles(pos0, t):
    pos = pos0 + jnp.arange(t)
    inv = ROPE_THETA ** (-(jnp.arange(0, D_ROPE, 2, dtype=f32) / D_ROPE))
    ang = pos.astype(f32)[:, None] * inv[None, :]
    return pos, jnp.cos(ang), jnp.sin(ang)


def _rope_tables(pos0, t):
    pos, cos, sin = _rope_angles(pos0, t)
    zero = jnp.zeros((t, LANES - D_ROPE), f32)
    return (jnp.concatenate([cos, cos, zero], axis=-1), jnp.concatenate([-sin, sin, zero], axis=-1),
            pos.astype(f32)[:, None])


def _prep_weights(norm_g, w_in, q_norm_g, w_uq, kv_norm_g, w_ukv, q_nope_g, q_rope_g, k_nope_g,
                  k_rope_g, w_pool, pool_scale, w_out, ple_norm_g, w_ple_gate, b_ple_gate, w_ple):
    w_uq_r = jnp.pad(w_uq.reshape(Q_RANK, N_HEADS, D_NOPE + D_ROPE),
                     ((0, 0), (0, 0), (0, D_QK - D_NOPE - D_ROPE))).reshape(Q_RANK, N_HEADS * D_QK)
    w_ukv3 = w_ukv.reshape(KV_RANK, N_HEADS, D_NOPE + D_V)
    w_uk3, w_uv3 = w_ukv3[..., :D_NOPE], w_ukv3[..., D_NOPE:]
    pad_rope = lambda g: jnp.pad(g, (0, LANES - D_ROPE))[None, :]
    return {
        "norm_g": norm_g[None, :], "w_in": w_in.astype(bf16), "q_norm_g": q_norm_g[None, :],
        "kv_norm_g": kv_norm_g[None, :], "k_rope_g": pad_rope(k_rope_g),
        "w_pool": w_pool.astype(bf16), "pool_scale": pool_scale[None, :],
        "w_uq": w_uq_r.astype(bf16), "q_nope_g": q_nope_g[None, :], "q_rope_g": pad_rope(q_rope_g),
        "w_uq_t": w_uq_r.T.astype(bf16),
        "q_nope_g_col": q_nope_g[:, None], "q_rope_g_col": q_rope_g[:, None],
        "k_nope_g": k_nope_g[None, :],
        "w_uk": w_uk3.reshape(KV_RANK, N_HEADS * D_NOPE).astype(bf16),
        "w_uv_t": w_uv3.reshape(KV_RANK, D_MLA).T.astype(bf16),
        "w_uk_t": jnp.transpose(w_uk3, (1, 2, 0)).astype(bf16),
        "w_uv3": jnp.transpose(w_uv3, (1, 0, 2)).astype(bf16),
        "w_out": w_out.astype(bf16), "ple_norm_g": ple_norm_g[None, :],
        "w_ple_gate": w_ple_gate.astype(bf16), "b_ple_gate": b_ple_gate[None, :],
        "w_ple": w_ple.astype(bf16),
    }


def _layer_prompt(x, p, wts):
    n_seq, seq_len, _ = x.shape
    x2d = x.reshape(n_seq * seq_len, D_MODEL)
    cos2, sin2, pos = _rope_tables(0, seq_len)
    tm = 256
    hist = jnp.zeros((1, HALO, D_POOL), f32)
    pact, gm, qlat, ckv, krope, npool = _in_proj(
        x2d, hist, cos2, sin2, pos, wts, nseg=1, seg_len=tm, n_seq=n_seq, carry=True)
    _, cos, sin = _rope_angles(0, seq_len)
    tq = 256
    qt = _q_proj_prompt(qlat, cos.T, sin.T, wts, n_seq=n_seq, seq_len=seq_len, tm=512)
    k, vt = _kv_proj(ckv, krope, wts, n_seq=n_seq, seq_len=seq_len, tm=tq)
    mact = _attn_prompt(qt, k, vt, gm, n_seq=n_seq, seq_len=seq_len, tq=tq)
    y = _out_proj(x2d, pact, mact, p.reshape(n_seq * seq_len, D_PLE), wts, tm=256)
    return (y.reshape(x.shape), ckv.reshape(n_seq, seq_len, KV_RANK),
            krope.reshape(n_seq, seq_len, D_ROPE), npool)


def _layer_sample(x, p, state_pool, cache_ckv, cache_krope, wts):
    n_seq, t_new, _ = x.shape
    past = cache_ckv.shape[1]
    n_tok = n_seq * t_new
    x2d = x.reshape(n_tok, D_MODEL)
    cos1, sin1, pos1 = _rope_tables(past, t_new)
    cos2, sin2, pos = (jnp.tile(a, (n_seq, 1)) for a in (cos1, sin1, pos1))
    hist = jnp.pad(state_pool, ((0, 0), (HALO - POOL_HIST, 0), (0, 0)))
    pact, gm, qlat, ckv, krope, npool = _in_proj(
        x2d, hist, cos2, sin2, pos, wts, nseg=n_seq, seg_len=t_new, n_seq=n_seq, carry=False)
    q_abs, q_rope = _q_proj_sample(qlat, cos2, sin2, wts)
    ckv3 = ckv.reshape(n_seq, t_new, KV_RANK)
    krope3 = krope.reshape(n_seq, t_new, D_ROPE)
    cache_krope_t = jnp.transpose(cache_krope, (0, 2, 1))
    mact = _attn_sample(cache_ckv, cache_krope_t, ckv3, krope3, q_abs, q_rope, gm, wts, t_new=t_new)
    y = _out_proj(x2d, pact, mact, p.reshape(n_tok, D_PLE), wts, tm=256)
    return y.reshape(x.shape), ckv3, krope3, npool


def kernel(x_prompt, x_sample, cache_ckv, cache_krope, state_pool, p_prompt, p_sample, norm_g, w_in,
           q_norm_g, w_uq, kv_norm_g, w_ukv, q_nope_g, q_rope_g, k_nope_g, k_rope_g, w_pool,
           pool_scale, w_out, ple_norm_g, w_ple_gate, b_ple_gate, w_ple):
    depth = norm_g.shape[0]
    layer_w = (norm_g, w_in, q_norm_g, w_uq, kv_norm_g, w_ukv, q_nope_g, q_rope_g, k_nope_g, k_rope_g,
               w_pool, pool_scale, w_out, ple_norm_g, w_ple_gate, b_ple_gate, w_ple)
    yp, ys = x_prompt, x_sample
    outs = [[] for _ in range(6)]
    for i in range(depth):
        wts = _prep_weights(*(w[i] for w in layer_w))
        yp, c1, k1, s1 = _layer_prompt(yp, p_prompt[i], wts)
        ys, c2, k2, s2 = _layer_sample(ys, p_sample[i], state_pool[i], cache_ckv[i], cache_krope[i], wts)
        for lst, val in zip(outs, (c1, k1, s1, c2, k2, s2)):
            lst.append(val)
    return (yp, ys) + tuple(jnp.stack(o) for o in outs)
```

```python
import functools

import jax
import jax.numpy as jnp
import numpy as np
from jax import lax
from jax.experimental import pallas as pl
from jax.experimental.pallas import tpu as pltpu

D_MODEL = 2048
CHUNK = 64
D_POOL = 1024
POOL_WINDOWS = (2, 4, 8, 16)
POOL_GROUP = 256
POOL_HIST = 15
HALO = 16
N_HEADS = 8
D_NOPE = 128
D_ROPE = 64
D_V = 128
D_VX = D_V + 16
D_MLA = N_HEADS * D_V
Q_RANK = 512
KV_RANK = 256
D_PLE = 256
D_QK = 256
ROPE_THETA = 10000.0
EPS = 1e-6
ATTN_SCALE = (D_NOPE + D_ROPE) ** -0.5
NEG_INF = -1e30
EXP2_SCALE = ATTN_SCALE * float(np.log2(np.e))
LANES = 128

C_U, C_GP, C_Q, C_KV, C_KR, C_GM = 0, 1024, 2048, 2560, 2816, 2880
D_IN = 3904

VMEM_LIMIT = 56 * 1024 * 1024
PROMPT_TILE = 256
OUT_TILE = 256

f32 = jnp.float32
bf16 = jnp.bfloat16


def _cparams(n_axes):
    return pltpu.CompilerParams(dimension_semantics=("arbitrary",) * n_axes,
                                vmem_limit_bytes=VMEM_LIMIT)


def _resident(shape):
    nd = len(shape)
    return pl.BlockSpec(shape, lambda *_: (0,) * nd, pipeline_mode=pl.Buffered(1))


def _rms(x, n):
    return lax.rsqrt(jnp.sum(x * x, axis=-1, keepdims=True) * (1.0 / n) + EPS)


def _rms_cols(x_t, n):
    return lax.rsqrt(jnp.sum(x_t * x_t, axis=0, keepdims=True) * (1.0 / n) + EPS)


def _dot(a, b):
    return jnp.dot(a, b, preferred_element_type=f32)


def _dot_nt(a, b):
    return lax.dot_general(a, b, (((1,), (1,)), ((), ())), preferred_element_type=f32)


def _rope128(x, cos2, sin2):
    lane = lax.broadcasted_iota(jnp.int32, x.shape, 1)
    swapped = jnp.where(lane < D_ROPE // 2, pltpu.roll(x, LANES - D_ROPE // 2, 1),
                        pltpu.roll(x, D_ROPE // 2, 1))
    return x * cos2 + swapped * sin2


def _queries_t(ql, wqt_ref, cos_ref, sin_ref, qng_ref, qrg_ref, qt_ref):
    q_t = _dot_nt(wqt_ref[...], ql)
    tm = q_t.shape[1]
    cos, sin = cos_ref[...], sin_ref[...]
    half = D_ROPE // 2
    for h in range(N_HEADS):
        r0 = h * D_QK
        qn = q_t[r0:r0 + D_NOPE]
        qt_ref[0, r0:r0 + D_NOPE, :] = (qn * _rms_cols(qn, D_NOPE) * qng_ref[...]).astype(bf16)
        qr = q_t[r0 + D_NOPE:r0 + D_NOPE + D_ROPE]
        qr = qr * _rms_cols(qr, D_ROPE) * qrg_ref[...]
        x1, x2 = qr[:half], qr[half:]
        qt_ref[0, r0 + D_NOPE:r0 + D_NOPE + half, :] = (x1 * cos - x2 * sin).astype(bf16)
        qt_ref[0, r0 + D_NOPE + half:r0 + D_NOPE + D_ROPE, :] = (x2 * cos + x1 * sin).astype(bf16)
        qt_ref[0, r0 + D_NOPE + D_ROPE:r0 + D_QK, :] = jnp.zeros((D_QK - D_NOPE - D_ROPE, tm), bf16)


def _keys_values(c, kr128, wk_ref, wvt_ref, kng_ref, k_ref, vt_ref):
    tm = c.shape[0]
    k = _dot(c, wk_ref[...])
    kr = kr128.astype(bf16)
    for h in range(N_HEADS):
        kn = k[:, h * D_NOPE:(h + 1) * D_NOPE]
        k_ref[:, h * D_QK:h * D_QK + D_NOPE] = (kn * _rms(kn, D_NOPE) * kng_ref[...]).astype(bf16)
        k_ref[:, h * D_QK + D_NOPE:(h + 1) * D_QK] = kr
    v_t = _dot_nt(wvt_ref[...], c).astype(bf16)
    row = lax.broadcasted_iota(jnp.int32, (D_VX - D_V, tm), 0)
    ones_row = jnp.where(row == 0, 1.0, 0.0).astype(bf16)
    for h in range(N_HEADS):
        vt_ref[0, 0, h, 0:D_V, :] = v_t[h * D_V:(h + 1) * D_V]
        vt_ref[0, 0, h, D_V:D_VX, :] = ones_row


def _in_proj_body(*refs, nseg, seg_len, tiles_per_seq, carry, fuse_qkv):
    (x_ref, hist_ref, cos_ref, sin_ref, pos_ref, ng_ref, w_ref, qg_ref, kvg_ref, krg_ref, wp_ref,
     ps_ref) = refs[:12]
    if fuse_qkv:
        cost_ref, sint_ref, wqt_ref, qng_ref, qrg_ref, wk_ref, wvt_ref, kng_ref = refs[12:20]
        pact_ref, gm_ref, ckv_ref, kr_ref, npool_ref, qt_ref, k_ref, vt_ref, ucat_ref = refs[20:]
    else:
        pact_ref, gm_ref, ckv_ref, kr_ref, npool_ref, ql_ref, ucat_ref = refs[12:]
    i = pl.program_id(0)
    tm = nseg * seg_len
    x = x_ref[...]
    xn = (x * _rms(x, D_MODEL) * ng_ref[...]).astype(bf16)

    u = _dot(xn, w_ref[:, C_U:C_U + D_POOL])
    if carry:
        @pl.when(i % tiles_per_seq == 0)
        def _():
            ucat_ref[:, 0:HALO, :] = jnp.zeros((nseg, HALO, D_POOL), f32)
    else:
        ucat_ref[:, 0:HALO, :] = hist_ref[...]
    ucat_ref[:, HALO:HALO + seg_len, :] = u.reshape(nseg, seg_len, D_POOL)

    pos = pos_ref[...]
    gp = _dot(xn, w_ref[:, C_GP:C_GP + D_POOL])
    gate = gp * jax.nn.sigmoid(gp) * ps_ref[...]

    def pool_group(g):
        w = POOL_WINDOWS[g]
        sl = slice(g * POOL_GROUP, (g + 1) * POOL_GROUP)
        acc = ucat_ref[:, HALO:HALO + seg_len, sl]
        for k in range(1, w):
            acc = acc + ucat_ref[:, HALO - k:HALO - k + seg_len, sl]
        acc = acc.reshape(tm, POOL_GROUP)
        inv_cnt = 1.0 / jnp.minimum(pos + 1.0, float(w))
        d = (acc * inv_cnt - u[:, sl]).astype(bf16)
        mixed = _dot(d, wp_ref[g])
        pact_ref[:, sl] = (mixed * gate[:, sl]).astype(bf16)

    cq = _dot(xn, w_ref[:, C_Q:C_Q + Q_RANK])
    pool_group(0)
    pool_group(1)
    ql = (cq * _rms(cq, Q_RANK) * qg_ref[...]).astype(bf16)
    if fuse_qkv:
        _queries_t(ql, wqt_ref, cost_ref, sint_ref, qng_ref, qrg_ref, qt_ref)
    else:
        ql_ref[...] = ql

    ckv = _dot(xn, w_ref[:, C_KV:C_KV + KV_RANK])
    pool_group(2)
    ckv = ckv * _rms(ckv, KV_RANK) * kvg_ref[...]
    ckv_ref[...] = ckv

    tail = _dot(xn, w_ref[:, C_KR:D_IN])
    pool_group(3)
    npool_ref[...] = ucat_ref[:, seg_len + 1:seg_len + HALO, :]
    if carry:
        ucat_ref[:, 0:HALO, :] = ucat_ref[:, seg_len:seg_len + HALO, :]
    lane = lax.broadcasted_iota(jnp.int32, (tm, LANES), 1)
    kr = jnp.where(lane < D_ROPE, tail[:, :LANES], 0.0)
    kr = _rope128(kr * _rms(kr, D_ROPE) * krg_ref[...], cos_ref[...], sin_ref[...])
    kr_ref[...] = kr[:, :D_ROPE]
    gm = tail[:, D_ROPE:]
    gm_ref[...] = (gm * jax.nn.sigmoid(gm)).astype(bf16)
    if fuse_qkv:
        _keys_values(ckv.astype(bf16), kr, wk_ref, wvt_ref, kng_ref, k_ref, vt_ref)


def _in_proj(x2d, hist, cos2, sin2, pos, wts, *, nseg, seg_len, n_seq, carry, rope_t=None):
    n_tok = x2d.shape[0]
    tm = nseg * seg_len
    n_tiles = n_tok // tm
    tiles_per_seq = n_tiles // n_seq if carry else 1
    tab_tiles = cos2.shape[0] // tm
    fuse_qkv = rope_t is not None
    row = lambda i: (i, 0)
    tab = lambda i: (i % tab_tiles, 0)
    if carry:
        hist_spec = pl.BlockSpec((1, HALO, D_POOL), lambda i: (0, 0, 0))
        npool_spec = pl.BlockSpec((1, POOL_HIST, D_POOL), lambda i: (i // tiles_per_seq, 0, 0))
    else:
        hist_spec = pl.BlockSpec((nseg, HALO, D_POOL), lambda i: (i, 0, 0))
        npool_spec = pl.BlockSpec((nseg, POOL_HIST, D_POOL), lambda i: (i, 0, 0))
    in_specs = [
        pl.BlockSpec((tm, D_MODEL), row), hist_spec, pl.BlockSpec((tm, LANES), tab),
        pl.BlockSpec((tm, LANES), tab), pl.BlockSpec((tm, 1), tab), _resident((1, D_MODEL)),
        _resident((D_MODEL, D_IN)), _resident((1, Q_RANK)), _resident((1, KV_RANK)),
        _resident((1, LANES)), _resident((4, POOL_GROUP, POOL_GROUP)), _resident((1, D_POOL)),
    ]
    args = [x2d, hist, cos2, sin2, pos, wts["norm_g"], wts["w_in"], wts["q_norm_g"],
            wts["kv_norm_g"], wts["k_rope_g"], wts["w_pool"], wts["pool_scale"]]
    out_specs = [pl.BlockSpec((tm, D_POOL), row), pl.BlockSpec((tm, D_MLA), row),
                 pl.BlockSpec((tm, KV_RANK), row), pl.BlockSpec((tm, D_ROPE), row), npool_spec]
    out_shape = [jax.ShapeDtypeStruct((n_tok, D_POOL), bf16), jax.ShapeDtypeStruct((n_tok, D_MLA), bf16),
                 jax.ShapeDtypeStruct((n_tok, KV_RANK), f32), jax.ShapeDtypeStruct((n_tok, D_ROPE), f32),
                 jax.ShapeDtypeStruct((n_seq, POOL_HIST, D_POOL), f32)]
    if fuse_qkv:
        seq_tab = lambda i: (0, i % tiles_per_seq)
        in_specs += [pl.BlockSpec((D_ROPE // 2, tm), seq_tab), pl.BlockSpec((D_ROPE // 2, tm), seq_tab),
                     _resident((N_HEADS * D_QK, Q_RANK)), _resident((D_NOPE, 1)), _resident((D_ROPE, 1)),
                     _resident((KV_RANK, N_HEADS * D_NOPE)), _resident((D_MLA, KV_RANK)),
                     _resident((1, D_NOPE))]
        args += [rope_t[0], rope_t[1], wts["w_uq_t"], wts["q_nope_g_col"], wts["q_rope_g_col"],
                 wts["w_uk"], wts["w_uv_t"], wts["k_nope_g"]]
        out_specs += [
            pl.BlockSpec((1, N_HEADS * D_QK, tm), lambda i: (i // tiles_per_seq, 0, i % tiles_per_seq)),
            pl.BlockSpec((tm, N_HEADS * D_QK), row),
            pl.BlockSpec((1, 1, N_HEADS, D_VX, tm),
                         lambda i: (i // tiles_per_seq, i % tiles_per_seq, 0, 0, 0))]
        out_shape += [jax.ShapeDtypeStruct((n_seq, N_HEADS * D_QK, tiles_per_seq * tm), bf16),
                      jax.ShapeDtypeStruct((n_tok, N_HEADS * D_QK), bf16),
                      jax.ShapeDtypeStruct((n_seq, tiles_per_seq, N_HEADS, D_VX, tm), bf16)]
    else:
        out_specs.append(pl.BlockSpec((tm, Q_RANK), row))
        out_shape.append(jax.ShapeDtypeStruct((n_tok, Q_RANK), bf16))
    body = functools.partial(_in_proj_body, nseg=nseg, seg_len=seg_len,
                             tiles_per_seq=tiles_per_seq, carry=carry, fuse_qkv=fuse_qkv)
    return pl.pallas_call(
        body,
        grid=(n_tiles,),
        in_specs=in_specs,
        out_specs=out_specs,
        out_shape=out_shape,
        scratch_shapes=[pltpu.VMEM((nseg, HALO + seg_len, D_POOL), f32)],
        compiler_params=_cparams(1),
        name="in_proj",
    )(*args)


def _q_proj_sample_body(ql_ref, wq_ref, cos_ref, sin_ref, qng_ref, qrg_ref, kng_ref, wukt_ref,
                        qa_ref, qr_ref):
    q = _dot(ql_ref[...], wq_ref[...])
    cos2, sin2 = cos_ref[...], sin_ref[...]
    for h in range(N_HEADS):
        qn = q[:, h * D_QK:h * D_QK + D_NOPE]
        qn = qn * _rms(qn, D_NOPE) * qng_ref[...]
        qa_ref[h] = _dot((qn * kng_ref[...]).astype(bf16), wukt_ref[h]).astype(bf16)
        qr = q[:, h * D_QK + D_NOPE:(h + 1) * D_QK]
        qr = _rope128(qr * _rms(qr, D_ROPE) * qrg_ref[...], cos2, sin2)
        qr_ref[h] = qr[:, :D_ROPE].astype(bf16)


def _q_proj_sample(qlat, cos2, sin2, wts):
    n_tok = qlat.shape[0]
    return pl.pallas_call(
        _q_proj_sample_body,
        grid=(1,),
        in_specs=[_resident((n_tok, Q_RANK)), _resident((Q_RANK, N_HEADS * D_QK)),
                  _resident((n_tok, LANES)), _resident((n_tok, LANES)),
                  _resident((1, D_NOPE)), _resident((1, LANES)), _resident((1, D_NOPE)),
                  _resident((N_HEADS, D_NOPE, KV_RANK))],
        out_specs=[_resident((N_HEADS, n_tok, KV_RANK)), _resident((N_HEADS, n_tok, D_ROPE))],
        out_shape=[jax.ShapeDtypeStruct((N_HEADS, n_tok, KV_RANK), bf16),
                   jax.ShapeDtypeStruct((N_HEADS, n_tok, D_ROPE), bf16)],
        compiler_params=_cparams(1),
        name="q_proj_sample",
    )(qlat, wts["w_uq"], cos2, sin2, wts["q_nope_g"], wts["q_rope_g"], wts["k_nope_g"],
      wts["w_uk_t"])


def _attn_prompt_body(qt_ref, k_ref, vt_ref, gm_ref, o_ref, m_ref, acc_ref, s_ref, mx_ref, *, tq, tk):
    qi = pl.program_id(1)
    m_ref[...] = jnp.full(m_ref.shape, NEG_INF, f32)
    acc_ref[...] = jnp.zeros(acc_ref.shape, f32)

    def scores(h, j, buf, masked):
        row0 = pl.multiple_of(j * tk, tk)
        kb = k_ref[pl.ds(row0, tk), h * D_QK:(h + 1) * D_QK]
        s = _dot(kb, qt_ref[0, h * D_QK:(h + 1) * D_QK, :])
        if masked:
            k_chunk = lax.broadcasted_iota(jnp.int32, (tk, tq), 0) // CHUNK
            q_chunk = lax.broadcasted_iota(jnp.int32, (tk, tq), 1) // CHUNK
            s = jnp.where(k_chunk <= q_chunk, s, NEG_INF)
        s_ref[buf, h] = s
        mx_ref[buf, h] = jnp.max(s, axis=0, keepdims=True)

    def values(h, j, buf):
        m_old = m_ref[h]
        m_new = jnp.maximum(m_old, mx_ref[buf, h])
        alpha = jnp.exp2((m_old - m_new) * EXP2_SCALE)
        p = jnp.exp2((s_ref[buf, h] - m_new) * EXP2_SCALE).astype(bf16)
        m_ref[h] = m_new
        acc_ref[h] = alpha * acc_ref[h] + _dot(vt_ref[0, j, h], p)

    for h in range(N_HEADS):
        scores(h, qi, 0, True)

    def step(t, buf):
        prev = jnp.where(t == 1, qi, t - 2)
        for h in range(N_HEADS):
            scores(h, t - 1, buf, False)
            values(h, prev, 1 - buf)

    def step_pair(u, carry):
        step(2 * u + 1, 1)
        step(2 * u + 2, 0)
        return carry

    lax.fori_loop(0, qi // 2, step_pair, 0)

    @pl.when(qi % 2 == 1)
    def _():
        step(qi, 1)
        for h in range(N_HEADS):
            values(h, qi - 1, 1)

    @pl.when(qi % 2 == 0)
    def _():
        last = jnp.where(qi == 0, qi, qi - 1)
        for h in range(N_HEADS):
            values(h, last, 0)

    for h in range(N_HEADS):
        hs = slice(h * D_V, (h + 1) * D_V)
        o = (acc_ref[h, 0:D_V, :] / acc_ref[h, D_V:D_V + 1, :]).T
        o_ref[:, hs] = (o * gm_ref[:, hs].astype(f32)).astype(bf16)


def _attn_prompt(qt, k, vt, gm, *, n_seq, seq_len, tq):
    tk = tq
    nq = seq_len // tq
    body = functools.partial(_attn_prompt_body, tq=tq, tk=tk)
    return pl.pallas_call(
        body,
        grid=(n_seq, nq),
        in_specs=[pl.BlockSpec((1, N_HEADS * D_QK, tq), lambda b, i: (b, 0, i)),
                  pl.BlockSpec((seq_len, N_HEADS * D_QK), lambda b, i: (b, 0)),
                  pl.BlockSpec((1, seq_len // tk, N_HEADS, D_VX, tk), lambda b, i: (b, 0, 0, 0, 0)),
                  pl.BlockSpec((tq, D_MLA), lambda b, i: (b * nq + i, 0))],
        out_specs=pl.BlockSpec((tq, D_MLA), lambda b, i: (b * nq + i, 0)),
        out_shape=jax.ShapeDtypeStruct((n_seq * seq_len, D_MLA), bf16),
        scratch_shapes=[pltpu.VMEM((N_HEADS, 1, tq), f32), pltpu.VMEM((N_HEADS, D_VX, tq), f32),
                        pltpu.VMEM((2, N_HEADS, tk, tq), f32), pltpu.VMEM((2, N_HEADS, 1, tq), f32)],
        compiler_params=_cparams(2),
        name="attn_prompt",
    )(qt, k, vt, gm)


def _attn_sample_body(cckv_ref, ckr_t_ref, nckv_ref, nkr_ref, qa_ref, qr_ref, gm_ref, wukt_ref, wuv_ref,
                      o_ref, call_ref, ktail_ref, s_ref, *, past, t_new, chunks):
    s_pad = call_ref.shape[0]
    rows = N_HEADS * t_new
    call_ref[0:past, :] = cckv_ref[0].astype(bf16)
    call_ref[past:past + t_new, :] = nckv_ref[0].astype(bf16)
    call_ref[past + t_new:s_pad, :] = jnp.zeros((s_pad - past - t_new, KV_RANK), bf16)
    ktail_ref[0:t_new, :] = nkr_ref[0].astype(bf16)
    ktail_ref[t_new:, :] = jnp.zeros((s_pad - past - t_new, D_ROPE), bf16)

    qa = qa_ref[...].reshape(rows, KV_RANK)
    qr = qr_ref[...].reshape(rows, D_ROPE)
    for start, size in chunks:
        c = call_ref[start:start + size, :]
        k_t = _dot_nt(wukt_ref[...], c)
        ssq = jnp.sum((k_t * k_t).reshape(N_HEADS, D_NOPE, size), axis=1)
        r = lax.rsqrt(ssq * (1.0 / D_NOPE) + EPS)
        s_nope = _dot_nt(qa, c)
        if start < past:
            s_rope = _dot(qr, ckr_t_ref[0, :, start:start + size].astype(bf16))
        else:
            s_rope = _dot_nt(qr, ktail_ref[...])
        for h in range(N_HEADS):
            hs = slice(h * t_new, (h + 1) * t_new)
            s = (s_nope[hs] * r[h:h + 1, :] + s_rope[hs]) * ATTN_SCALE
            if start + size > past + t_new:
                key = start + lax.broadcasted_iota(jnp.int32, (t_new, size), 1)
                s = jnp.where(key < past + t_new, s, NEG_INF)
            s_ref[hs, start:start + size] = s

    s = s_ref[...]
    p = jnp.exp(s - jnp.max(s, axis=-1, keepdims=True))
    l = jnp.sum(p, axis=-1, keepdims=True)
    o_lat = (_dot(p.astype(bf16), call_ref[...]) / l).astype(bf16)
    for h in range(N_HEADS):
        o = _dot(o_lat[h * t_new:(h + 1) * t_new], wuv_ref[h])
        hs = slice(h * D_V, (h + 1) * D_V)
        o_ref[:, hs] = (o * gm_ref[:, hs].astype(f32)).astype(bf16)


def _attn_sample(cache_ckv, cache_krope_t, ckv_new, krope_new, q_abs, q_rope, gm, wts, *, t_new):
    n_seq, past, _ = cache_ckv.shape
    chunk = 512
    chunks = tuple((s, min(chunk, past - s)) for s in range(0, past, chunk)) + ((past, LANES),)
    s_pad = past + LANES
    body = functools.partial(_attn_sample_body, past=past, t_new=t_new, chunks=chunks)
    return pl.pallas_call(
        body,
        grid=(n_seq,),
        in_specs=[pl.BlockSpec((1, past, KV_RANK), lambda b: (b, 0, 0)),
                  pl.BlockSpec((1, D_ROPE, past), lambda b: (b, 0, 0)),
                  pl.BlockSpec((1, t_new, KV_RANK), lambda b: (b, 0, 0)),
                  pl.BlockSpec((1, t_new, D_ROPE), lambda b: (b, 0, 0)),
                  pl.BlockSpec((N_HEADS, t_new, KV_RANK), lambda b: (0, b, 0)),
                  pl.BlockSpec((N_HEADS, t_new, D_ROPE), lambda b: (0, b, 0)),
                  pl.BlockSpec((t_new, D_MLA), lambda b: (b, 0)),
                  _resident((N_HEADS * D_NOPE, KV_RANK)),
                  _resident((N_HEADS, KV_RANK, D_V))],
        out_specs=pl.BlockSpec((t_new, D_MLA), lambda b: (b, 0)),
        out_shape=jax.ShapeDtypeStruct((n_seq * t_new, D_MLA), bf16),
        scratch_shapes=[pltpu.VMEM((s_pad, KV_RANK), bf16), pltpu.VMEM((s_pad - past, D_ROPE), bf16),
                        pltpu.VMEM((N_HEADS * t_new, s_pad), f32)],
        compiler_params=_cparams(1),
        name="attn_sample",
    )(cache_ckv, cache_krope_t, ckv_new, krope_new, q_abs, q_rope, gm,
      wts["w_uk_t"].reshape(N_HEADS * D_NOPE, KV_RANK), wts["w_uv3"])


def _out_proj_body(x_ref, pa_ref, ma_ref, p_ref, wo_ref, png_ref, wg_ref, bg_ref, wple_ref, y_ref):
    h = x_ref[...] + _dot(pa_ref[...], wo_ref[0:D_POOL, :]) + _dot(ma_ref[...], wo_ref[D_POOL:, :])
    hn = (h * _rms(h, D_MODEL) * png_ref[...]).astype(bf16)
    gate = jax.nn.sigmoid(_dot(hn, wg_ref[...]) + bg_ref[...])
    y_ref[...] = h + gate * _dot(p_ref[...].astype(bf16), wple_ref[...])


def _out_proj(x2d, pool_act, mla_act, p2d, wts, *, tm):
    n_tok = x2d.shape[0]
    row = lambda i: (i, 0)
    return pl.pallas_call(
        _out_proj_body,
        grid=(n_tok // tm,),
        in_specs=[pl.BlockSpec((tm, D_MODEL), row), pl.BlockSpec((tm, D_POOL), row),
                  pl.BlockSpec((tm, D_MLA), row), pl.BlockSpec((tm, D_PLE), row),
                  _resident((D_MODEL, D_MODEL)), _resident((1, D_MODEL)),
                  _resident((D_MODEL, D_MODEL)), _resident((1, D_MODEL)),
                  _resident((D_PLE, D_MODEL))],
        out_specs=pl.BlockSpec((tm, D_MODEL), row),
        out_shape=jax.ShapeDtypeStruct((n_tok, D_MODEL), f32),
        compiler_params=_cparams(1),
        name="out_proj",
    )(x2d, pool_act, mla_act, p2d, wts["w_out"], wts["ple_norm_g"], wts["w_ple_gate"],
      wts["b_ple_gate"], wts["w_ple"])


def _rope_angles(pos0, t):
    pos = pos0 + jnp.arange(t)
    inv = ROPE_THETA ** (-(jnp.arange(0, D_ROPE, 2, dtype=f32) / D_ROPE))
    ang = pos.astype(f32)[:, None] * inv[None, :]
    return pos, jnp.cos(ang), jnp.sin(ang)


def _rope_tables(pos0, t):
    pos, cos, sin = _rope_angles(pos0, t)
    zero = jnp.zeros((t, LANES - D_ROPE), f32)
    return (jnp.concatenate([cos, cos, zero], axis=-1), jnp.concatenate([-sin, sin, zero], axis=-1),
            pos.astype(f32)[:, None], (cos.T, sin.T))


def _prep_weights(norm_g, w_in, q_norm_g, w_uq, kv_norm_g, w_ukv, q_nope_g, q_rope_g, k_nope_g,
                  k_rope_g, w_pool, pool_scale, w_out, ple_norm_g, w_ple_gate, b_ple_gate, w_ple):
    w_uq_r = jnp.pad(w_uq.reshape(Q_RANK, N_HEADS, D_NOPE + D_ROPE),
                     ((0, 0), (0, 0), (0, D_QK - D_NOPE - D_ROPE))).reshape(Q_RANK, N_HEADS * D_QK)
    w_ukv3 = w_ukv.reshape(KV_RANK, N_HEADS, D_NOPE + D_V)
    w_uk3, w_uv3 = w_ukv3[..., :D_NOPE], w_ukv3[..., D_NOPE:]
    pad_rope = lambda g: jnp.pad(g, (0, LANES - D_ROPE))[None, :]
    return {
        "norm_g": norm_g[None, :], "w_in": w_in.astype(bf16), "q_norm_g": q_norm_g[None, :],
        "kv_norm_g": kv_norm_g[None, :], "k_rope_g": pad_rope(k_rope_g),
        "w_pool": w_pool.astype(bf16), "pool_scale": pool_scale[None, :],
        "w_uq": w_uq_r.astype(bf16), "q_nope_g": q_nope_g[None, :], "q_rope_g": pad_rope(q_rope_g),
        "w_uq_t": w_uq_r.T.astype(bf16),
        "q_nope_g_col": q_nope_g[:, None], "q_rope_g_col": q_rope_g[:, None],
        "k_nope_g": k_nope_g[None, :],
        "w_uk": w_uk3.reshape(KV_RANK, N_HEADS * D_NOPE).astype(bf16),
        "w_uv_t": w_uv3.reshape(KV_RANK, D_MLA).T.astype(bf16),
        "w_uk_t": jnp.transpose(w_uk3, (1, 2, 0)).astype(bf16),
        "w_uv3": jnp.transpose(w_uv3, (1, 0, 2)).astype(bf16),
        "w_out": w_out.astype(bf16), "ple_norm_g": ple_norm_g[None, :],
        "w_ple_gate": w_ple_gate.astype(bf16), "b_ple_gate": b_ple_gate[None, :],
        "w_ple": w_ple.astype(bf16),
    }


def _layer_prompt(x, p, wts):
    n_seq, seq_len, _ = x.shape
    x2d = x.reshape(n_seq * seq_len, D_MODEL)
    cos2, sin2, pos, rope_t = _rope_tables(0, seq_len)
    hist = jnp.zeros((1, HALO, D_POOL), f32)
    pact, gm, ckv, krope, npool, qt, k, vt = _in_proj(
        x2d, hist, cos2, sin2, pos, wts, nseg=1, seg_len=PROMPT_TILE, n_seq=n_seq, carry=True,
        rope_t=rope_t)
    mact = _attn_prompt(qt, k, vt, gm, n_seq=n_seq, seq_len=seq_len, tq=PROMPT_TILE)
    y = _out_proj(x2d, pact, mact, p.reshape(n_seq * seq_len, D_PLE), wts, tm=OUT_TILE)
    return (y.reshape(x.shape), ckv.reshape(n_seq, seq_len, KV_RANK),
            krope.reshape(n_seq, seq_len, D_ROPE), npool)


def _layer_sample(x, p, state_pool, cache_ckv, cache_krope, wts):
    n_seq, t_new, _ = x.shape
    past = cache_ckv.shape[1]
    n_tok = n_seq * t_new
    x2d = x.reshape(n_tok, D_MODEL)
    cos1, sin1, pos1, _ = _rope_tables(past, t_new)
    cos2, sin2, pos = (jnp.tile(a, (n_seq, 1)) for a in (cos1, sin1, pos1))
    hist = jnp.pad(state_pool, ((0, 0), (HALO - POOL_HIST, 0), (0, 0)))
    pact, gm, ckv, krope, npool, qlat = _in_proj(
        x2d, hist, cos2, sin2, pos, wts, nseg=n_seq, seg_len=t_new, n_seq=n_seq, carry=False)
    q_abs, q_rope = _q_proj_sample(qlat, cos2, sin2, wts)
    ckv3 = ckv.reshape(n_seq, t_new, KV_RANK)
    krope3 = krope.reshape(n_seq, t_new, D_ROPE)
    cache_krope_t = jnp.transpose(cache_krope, (0, 2, 1))
    mact = _attn_sample(cache_ckv, cache_krope_t, ckv3, krope3, q_abs, q_rope, gm, wts, t_new=t_new)
    y = _out_proj(x2d, pact, mact, p.reshape(n_tok, D_PLE), wts, tm=OUT_TILE)
    return y.reshape(x.shape), ckv3, krope3, npool


def kernel(x_prompt, x_sample, cache_ckv, cache_krope, state_pool, p_prompt, p_sample, norm_g, w_in,
           q_norm_g, w_uq, kv_norm_g, w_ukv, q_nope_g, q_rope_g, k_nope_g, k_rope_g, w_pool,
           pool_scale, w_out, ple_norm_g, w_ple_gate, b_ple_gate, w_ple):
    depth = norm_g.shape[0]
    layer_w = (norm_g, w_in, q_norm_g, w_uq, kv_norm_g, w_ukv, q_nope_g, q_rope_g, k_nope_g, k_rope_g,
               w_pool, pool_scale, w_out, ple_norm_g, w_ple_gate, b_ple_gate, w_ple)
    yp, ys = x_prompt, x_sample
    outs = [[] for _ in range(6)]
    for i in range(depth):
        wts = _prep_weights(*(w[i] for w in layer_w))
        yp, c1, k1, s1 = _layer_prompt(yp, p_prompt[i], wts)
        ys, c2, k2, s2 = _layer_sample(ys, p_sample[i], state_pool[i], cache_ckv[i], cache_krope[i], wts)
        for lst, val in zip(outs, (c1, k1, s1, c2, k2, s2)):
            lst.append(val)
    return (yp, ys) + tuple(jnp.stack(o) for o in outs)
```

```python
import functools

import jax
import jax.numpy as jnp
import numpy as np
from jax import lax
from jax.experimental import pallas as pl
from jax.experimental.pallas import tpu as pltpu

D_MODEL = 2048
CHUNK = 64
D_POOL = 1024
POOL_WINDOWS = (2, 4, 8, 16)
POOL_GROUP = 256
POOL_HIST = 15
HALO = 16
N_HEADS = 8
D_NOPE = 128
D_ROPE = 64
D_V = 128
D_VX = D_V + 16
D_MLA = N_HEADS * D_V
Q_RANK = 512
KV_RANK = 256
D_PLE = 256
D_QK = 256
ROPE_THETA = 10000.0
EPS = 1e-6
ATTN_SCALE = (D_NOPE + D_ROPE) ** -0.5
NEG_INF = -1e30
EXP2_SCALE = ATTN_SCALE * float(np.log2(np.e))
LANES = 128

C_U, C_GP, C_Q, C_KV, C_KR, C_GM = 0, 1024, 2048, 2560, 2816, 2880
D_IN = 3904

VMEM_LIMIT = 56 * 1024 * 1024
PROMPT_TILE = 256
OUT_TILE = 256
SAMPLE_BATCH_PER_STEP = 4

f32 = jnp.float32
bf16 = jnp.bfloat16


def _cparams(n_axes):
    return pltpu.CompilerParams(dimension_semantics=("arbitrary",) * n_axes,
                                vmem_limit_bytes=VMEM_LIMIT)


def _resident(shape):
    nd = len(shape)
    return pl.BlockSpec(shape, lambda *_: (0,) * nd, pipeline_mode=pl.Buffered(1))


def _rms(x, n):
    return lax.rsqrt(jnp.sum(x * x, axis=-1, keepdims=True) * (1.0 / n) + EPS)


def _rms_cols(x_t, n):
    return lax.rsqrt(jnp.sum(x_t * x_t, axis=0, keepdims=True) * (1.0 / n) + EPS)


def _dot(a, b):
    return jnp.dot(a, b, preferred_element_type=f32)


def _dot_nt(a, b):
    return lax.dot_general(a, b, (((1,), (1,)), ((), ())), preferred_element_type=f32)


def _rope128(x, cos2, sin2):
    lane = lax.broadcasted_iota(jnp.int32, x.shape, 1)
    swapped = jnp.where(lane < D_ROPE // 2, pltpu.roll(x, LANES - D_ROPE // 2, 1),
                        pltpu.roll(x, D_ROPE // 2, 1))
    return x * cos2 + swapped * sin2


def _queries_t(ql, wqt_ref, cos_ref, sin_ref, qng_ref, qrg_ref, qt_ref):
    q_t = _dot_nt(wqt_ref[...], ql)
    tm = q_t.shape[1]
    cos, sin = cos_ref[...], sin_ref[...]
    half = D_ROPE // 2
    for h in range(N_HEADS):
        r0 = h * D_QK
        s0 = h * (D_NOPE + D_ROPE)
        qn = q_t[s0:s0 + D_NOPE]
        qt_ref[0, r0:r0 + D_NOPE, :] = (qn * _rms_cols(qn, D_NOPE) * qng_ref[...]).astype(bf16)
        qr = q_t[s0 + D_NOPE:s0 + D_NOPE + D_ROPE]
        qr = qr * _rms_cols(qr, D_ROPE) * qrg_ref[...]
        x1, x2 = qr[:half], qr[half:]
        qt_ref[0, r0 + D_NOPE:r0 + D_NOPE + half, :] = (x1 * cos - x2 * sin).astype(bf16)
        qt_ref[0, r0 + D_NOPE + half:r0 + D_NOPE + D_ROPE, :] = (x2 * cos + x1 * sin).astype(bf16)
        qt_ref[0, r0 + D_NOPE + D_ROPE:r0 + D_QK, :] = jnp.zeros((D_QK - D_NOPE - D_ROPE, tm), bf16)


def _keys_values(c, kr128, wk_ref, wvt_ref, kng_ref, k_ref, vt_ref):
    tm = c.shape[0]
    k = _dot(c, wk_ref[...])
    kr = kr128.astype(bf16)
    for h in range(N_HEADS):
        kn = k[:, h * D_NOPE:(h + 1) * D_NOPE]
        k_ref[:, h * D_QK:h * D_QK + D_NOPE] = (kn * _rms(kn, D_NOPE) * kng_ref[...]).astype(bf16)
        k_ref[:, h * D_QK + D_NOPE:(h + 1) * D_QK] = kr
    v_t = _dot_nt(wvt_ref[...], c).astype(bf16)
    row = lax.broadcasted_iota(jnp.int32, (D_VX - D_V, tm), 0)
    ones_row = jnp.where(row == 0, 1.0, 0.0).astype(bf16)
    for h in range(N_HEADS):
        vt_ref[0, 0, h, 0:D_V, :] = v_t[h * D_V:(h + 1) * D_V]
        vt_ref[0, 0, h, D_V:D_VX, :] = ones_row


def _in_proj_body(*refs, nseg, seg_len, tiles_per_seq, carry, fuse_qkv):
    (x_ref, hist_ref, cos_ref, sin_ref, pos_ref, ng_ref, w_ref, qg_ref, kvg_ref, krg_ref, wp_ref,
     ps_ref) = refs[:12]
    if fuse_qkv:
        cost_ref, sint_ref, wqt_ref, qng_ref, qrg_ref, wk_ref, wvt_ref, kng_ref = refs[12:20]
        pact_ref, gm_ref, ckv_ref, kr_ref, npool_ref, qt_ref, k_ref, vt_ref, ucat_ref = refs[20:]
    else:
        pact_ref, gm_ref, ckv_ref, kr_ref, npool_ref, ql_ref, ucat_ref = refs[12:]
    i = pl.program_id(0)
    tm = nseg * seg_len
    x = x_ref[...]
    xn = (x * _rms(x, D_MODEL) * ng_ref[...]).astype(bf16)

    u = _dot(xn, w_ref[:, C_U:C_U + D_POOL])
    if carry:
        @pl.when(i % tiles_per_seq == 0)
        def _():
            ucat_ref[:, 0:HALO, :] = jnp.zeros((nseg, HALO, D_POOL), f32)
    else:
        ucat_ref[:, 0:HALO, :] = hist_ref[...]
    ucat_ref[:, HALO:HALO + seg_len, :] = u.reshape(nseg, seg_len, D_POOL)

    pos = pos_ref[...]
    gp = _dot(xn, w_ref[:, C_GP:C_GP + D_POOL])
    gate = gp * jax.nn.sigmoid(gp) * ps_ref[...]

    def pool_group(g):
        w = POOL_WINDOWS[g]
        sl = slice(g * POOL_GROUP, (g + 1) * POOL_GROUP)
        acc = ucat_ref[:, HALO:HALO + seg_len, sl]
        for k in range(1, w):
            acc = acc + ucat_ref[:, HALO - k:HALO - k + seg_len, sl]
        acc = acc.reshape(tm, POOL_GROUP)
        inv_cnt = 1.0 / jnp.minimum(pos + 1.0, float(w))
        d = (acc * inv_cnt - u[:, sl]).astype(bf16)
        mixed = _dot(d, wp_ref[g])
        pact_ref[:, sl] = (mixed * gate[:, sl]).astype(bf16)

    cq = _dot(xn, w_ref[:, C_Q:C_Q + Q_RANK])
    pool_group(0)
    pool_group(1)
    ql = (cq * _rms(cq, Q_RANK) * qg_ref[...]).astype(bf16)
    if fuse_qkv:
        _queries_t(ql, wqt_ref, cost_ref, sint_ref, qng_ref, qrg_ref, qt_ref)
    else:
        ql_ref[...] = ql

    ckv = _dot(xn, w_ref[:, C_KV:C_KV + KV_RANK])
    pool_group(2)
    ckv = ckv * _rms(ckv, KV_RANK) * kvg_ref[...]
    ckv_ref[...] = ckv

    tail = _dot(xn, w_ref[:, C_KR:D_IN])
    pool_group(3)
    npool_ref[...] = ucat_ref[:, seg_len + 1:seg_len + HALO, :]
    if carry:
        ucat_ref[:, 0:HALO, :] = ucat_ref[:, seg_len:seg_len + HALO, :]
    lane = lax.broadcasted_iota(jnp.int32, (tm, LANES), 1)
    kr = jnp.where(lane < D_ROPE, tail[:, :LANES], 0.0)
    kr = _rope128(kr * _rms(kr, D_ROPE) * krg_ref[...], cos_ref[...], sin_ref[...])
    kr_ref[...] = kr[:, :D_ROPE]
    gm = tail[:, D_ROPE:]
    gm_ref[...] = (gm * jax.nn.sigmoid(gm)).astype(bf16)
    if fuse_qkv:
        _keys_values(ckv.astype(bf16), kr, wk_ref, wvt_ref, kng_ref, k_ref, vt_ref)


def _in_proj(x2d, hist, cos2, sin2, pos, wts, *, nseg, seg_len, n_seq, carry, rope_t=None):
    n_tok = x2d.shape[0]
    tm = nseg * seg_len
    n_tiles = n_tok // tm
    tiles_per_seq = n_tiles // n_seq if carry else 1
    tab_tiles = cos2.shape[0] // tm
    fuse_qkv = rope_t is not None
    row = lambda i: (i, 0)
    tab = lambda i: (i % tab_tiles, 0)
    if carry:
        hist_spec = pl.BlockSpec((1, HALO, D_POOL), lambda i: (0, 0, 0))
        npool_spec = pl.BlockSpec((1, POOL_HIST, D_POOL), lambda i: (i // tiles_per_seq, 0, 0))
    else:
        hist_spec = pl.BlockSpec((nseg, HALO, D_POOL), lambda i: (i, 0, 0))
        npool_spec = pl.BlockSpec((nseg, POOL_HIST, D_POOL), lambda i: (i, 0, 0))
    in_specs = [
        pl.BlockSpec((tm, D_MODEL), row), hist_spec, pl.BlockSpec((tm, LANES), tab),
        pl.BlockSpec((tm, LANES), tab), pl.BlockSpec((tm, 1), tab), _resident((1, D_MODEL)),
        _resident((D_MODEL, D_IN)), _resident((1, Q_RANK)), _resident((1, KV_RANK)),
        _resident((1, LANES)), _resident((4, POOL_GROUP, POOL_GROUP)), _resident((1, D_POOL)),
    ]
    args = [x2d, hist, cos2, sin2, pos, wts["norm_g"], wts["w_in"], wts["q_norm_g"],
            wts["kv_norm_g"], wts["k_rope_g"], wts["w_pool"], wts["pool_scale"]]
    out_specs = [pl.BlockSpec((tm, D_POOL), row), pl.BlockSpec((tm, D_MLA), row),
                 pl.BlockSpec((tm, KV_RANK), row), pl.BlockSpec((tm, D_ROPE), row), npool_spec]
    out_shape = [jax.ShapeDtypeStruct((n_tok, D_POOL), bf16), jax.ShapeDtypeStruct((n_tok, D_MLA), bf16),
                 jax.ShapeDtypeStruct((n_tok, KV_RANK), f32), jax.ShapeDtypeStruct((n_tok, D_ROPE), f32),
                 jax.ShapeDtypeStruct((n_seq, POOL_HIST, D_POOL), f32)]
    if fuse_qkv:
        seq_tab = lambda i: (0, i % tiles_per_seq)
        in_specs += [pl.BlockSpec((D_ROPE // 2, tm), seq_tab), pl.BlockSpec((D_ROPE // 2, tm), seq_tab),
                     _resident((N_HEADS * (D_NOPE + D_ROPE), Q_RANK)), _resident((D_NOPE, 1)),
                     _resident((D_ROPE, 1)),
                     _resident((KV_RANK, N_HEADS * D_NOPE)), _resident((D_MLA, KV_RANK)),
                     _resident((1, D_NOPE))]
        args += [rope_t[0], rope_t[1], wts["w_uq_t"], wts["q_nope_g_col"], wts["q_rope_g_col"],
                 wts["w_uk"], wts["w_uv_t"], wts["k_nope_g"]]
        out_specs += [
            pl.BlockSpec((1, N_HEADS * D_QK, tm), lambda i: (i // tiles_per_seq, 0, i % tiles_per_seq)),
            pl.BlockSpec((tm, N_HEADS * D_QK), row),
            pl.BlockSpec((1, 1, N_HEADS, D_VX, tm),
                         lambda i: (i // tiles_per_seq, i % tiles_per_seq, 0, 0, 0))]
        out_shape += [jax.ShapeDtypeStruct((n_seq, N_HEADS * D_QK, tiles_per_seq * tm), bf16),
                      jax.ShapeDtypeStruct((n_tok, N_HEADS * D_QK), bf16),
                      jax.ShapeDtypeStruct((n_seq, tiles_per_seq, N_HEADS, D_VX, tm), bf16)]
    else:
        out_specs.append(pl.BlockSpec((tm, Q_RANK), row))
        out_shape.append(jax.ShapeDtypeStruct((n_tok, Q_RANK), bf16))
    body = functools.partial(_in_proj_body, nseg=nseg, seg_len=seg_len,
                             tiles_per_seq=tiles_per_seq, carry=carry, fuse_qkv=fuse_qkv)
    return pl.pallas_call(
        body,
        grid=(n_tiles,),
        in_specs=in_specs,
        out_specs=out_specs,
        out_shape=out_shape,
        scratch_shapes=[pltpu.VMEM((nseg, HALO + seg_len, D_POOL), f32)],
        compiler_params=_cparams(1),
        name="in_proj",
    )(*args)


def _q_proj_sample_body(ql_ref, wq_ref, cos_ref, sin_ref, qng_ref, qrg_ref, kng_ref, wukt_ref,
                        qa_ref, qr_ref):
    q = _dot(ql_ref[...], wq_ref[...])
    cos2, sin2 = cos_ref[...], sin_ref[...]
    for h in range(N_HEADS):
        qn = q[:, h * D_QK:h * D_QK + D_NOPE]
        qn = qn * _rms(qn, D_NOPE) * qng_ref[...]
        qa_ref[h] = _dot((qn * kng_ref[...]).astype(bf16), wukt_ref[h]).astype(bf16)
        qr = q[:, h * D_QK + D_NOPE:(h + 1) * D_QK]
        qr = _rope128(qr * _rms(qr, D_ROPE) * qrg_ref[...], cos2, sin2)
        qr_ref[h] = qr[:, :D_ROPE].astype(bf16)


def _q_proj_sample(qlat, cos2, sin2, wts):
    n_tok = qlat.shape[0]
    return pl.pallas_call(
        _q_proj_sample_body,
        grid=(1,),
        in_specs=[_resident((n_tok, Q_RANK)), _resident((Q_RANK, N_HEADS * D_QK)),
                  _resident((n_tok, LANES)), _resident((n_tok, LANES)),
                  _resident((1, D_NOPE)), _resident((1, LANES)), _resident((1, D_NOPE)),
                  _resident((N_HEADS, D_NOPE, KV_RANK))],
        out_specs=[_resident((N_HEADS, n_tok, KV_RANK)), _resident((N_HEADS, n_tok, D_ROPE))],
        out_shape=[jax.ShapeDtypeStruct((N_HEADS, n_tok, KV_RANK), bf16),
                   jax.ShapeDtypeStruct((N_HEADS, n_tok, D_ROPE), bf16)],
        compiler_params=_cparams(1),
        name="q_proj_sample",
    )(qlat, wts["w_uq"], cos2, sin2, wts["q_nope_g"], wts["q_rope_g"], wts["k_nope_g"],
      wts["w_uk_t"])


def _attn_prompt_body(qt_ref, k_ref, vt_ref, gm_ref, o_ref, m_ref, acc_ref, s_ref, mx_ref, *, tq, tk):
    qi = pl.program_id(1)
    m_ref[...] = jnp.full(m_ref.shape, NEG_INF, f32)
    acc_ref[...] = jnp.zeros(acc_ref.shape, f32)

    def scores(h, j, buf, masked):
        row0 = pl.multiple_of(j * tk, tk)
        kb = k_ref[pl.ds(row0, tk), h * D_QK:(h + 1) * D_QK]
        s = _dot(kb, qt_ref[0, h * D_QK:(h + 1) * D_QK, :])
        if masked:
            k_chunk = lax.broadcasted_iota(jnp.int32, (tk, tq), 0) // CHUNK
            q_chunk = lax.broadcasted_iota(jnp.int32, (tk, tq), 1) // CHUNK
            s = jnp.where(k_chunk <= q_chunk, s, NEG_INF)
        s_ref[buf, h] = s
        mx_ref[buf, h] = jnp.max(s, axis=0, keepdims=True)

    def values(h, j, buf):
        m_old = m_ref[h]
        m_new = jnp.maximum(m_old, mx_ref[buf, h])
        alpha = jnp.exp2((m_old - m_new) * EXP2_SCALE)
        p = jnp.exp2((s_ref[buf, h] - m_new) * EXP2_SCALE).astype(bf16)
        m_ref[h] = m_new
        acc_ref[h] = alpha * acc_ref[h] + _dot(vt_ref[0, j, h], p)

    for h in range(N_HEADS):
        scores(h, qi, 0, True)

    def step(t, buf):
        prev = jnp.where(t == 1, qi, t - 2)
        for h in range(N_HEADS):
            scores(h, t - 1, buf, False)
            values(h, prev, 1 - buf)

    def step_pair(u, carry):
        step(2 * u + 1, 1)
        step(2 * u + 2, 0)
        return carry

    lax.fori_loop(0, qi // 2, step_pair, 0)

    @pl.when(qi % 2 == 1)
    def _():
        step(qi, 1)
        for h in range(N_HEADS):
            values(h, qi - 1, 1)

    @pl.when(qi % 2 == 0)
    def _():
        last = jnp.where(qi == 0, qi, qi - 1)
        for h in range(N_HEADS):
            values(h, last, 0)

    for h in range(N_HEADS):
        hs = slice(h * D_V, (h + 1) * D_V)
        o = (acc_ref[h, 0:D_V, :] / acc_ref[h, D_V:D_V + 1, :]).T
        o_ref[:, hs] = (o * gm_ref[:, hs].astype(f32)).astype(bf16)


def _attn_prompt(qt, k, vt, gm, *, n_seq, seq_len, tq):
    tk = tq
    nq = seq_len // tq
    body = functools.partial(_attn_prompt_body, tq=tq, tk=tk)
    return pl.pallas_call(
        body,
        grid=(n_seq, nq),
        in_specs=[pl.BlockSpec((1, N_HEADS * D_QK, tq), lambda b, i: (b, 0, i)),
                  pl.BlockSpec((seq_len, N_HEADS * D_QK), lambda b, i: (b, 0)),
                  pl.BlockSpec((1, seq_len // tk, N_HEADS, D_VX, tk), lambda b, i: (b, 0, 0, 0, 0)),
                  pl.BlockSpec((tq, D_MLA), lambda b, i: (b * nq + i, 0))],
        out_specs=pl.BlockSpec((tq, D_MLA), lambda b, i: (b * nq + i, 0)),
        out_shape=jax.ShapeDtypeStruct((n_seq * seq_len, D_MLA), bf16),
        scratch_shapes=[pltpu.VMEM((N_HEADS, 1, tq), f32), pltpu.VMEM((N_HEADS, D_VX, tq), f32),
                        pltpu.VMEM((2, N_HEADS, tk, tq), f32), pltpu.VMEM((2, N_HEADS, 1, tq), f32)],
        compiler_params=_cparams(2),
        name="attn_prompt",
    )(qt, k, vt, gm)


def _attn_sample_body(cckv_ref, ckr_t_ref, nckv_ref, nkr_ref, qa_ref, qr_ref, gm_ref, wukt_ref, wuv_ref,
                      o_ref, c_ref, ktail_ref, s_ref, *, past, t_new, chunks, n_b):
    n_tail = c_ref.shape[1] - past
    rows = N_HEADS * t_new
    c_ref[:, past:past + t_new, :] = nckv_ref[...].astype(bf16)
    c_ref[:, past + t_new:, :] = jnp.zeros((n_b, n_tail - t_new, KV_RANK), bf16)
    ktail_ref[:, 0:t_new, :] = nkr_ref[...].astype(bf16)
    ktail_ref[:, t_new:, :] = jnp.zeros((n_b, n_tail - t_new, D_ROPE), bf16)

    def chunk_scores(b, start, size):
        cached = start < past
        if cached:
            c_ref[b, start:start + size, :] = cckv_ref[b, start:start + size, :].astype(bf16)
        c = c_ref[b, start:start + size, :]
        k_t = _dot_nt(wukt_ref[...], c)
        ssq = jnp.sum((k_t * k_t).reshape(N_HEADS, D_NOPE, size), axis=1)
        r = lax.rsqrt(ssq * (1.0 / D_NOPE) + EPS)
        qa = qa_ref[:, b * t_new:(b + 1) * t_new, :].reshape(rows, KV_RANK)
        qr = qr_ref[:, b * t_new:(b + 1) * t_new, :].reshape(rows, D_ROPE)
        s_nope = _dot_nt(qa, c)
        if cached:
            s_rope = _dot(qr, ckr_t_ref[b, :, start:start + size].astype(bf16))
        else:
            s_rope = _dot_nt(qr, ktail_ref[b])
        for h in range(N_HEADS):
            hs = slice(h * t_new, (h + 1) * t_new)
            s = (s_nope[hs] * r[h:h + 1, :] + s_rope[hs]) * ATTN_SCALE
            if not cached:
                key = lax.broadcasted_iota(jnp.int32, (t_new, size), 1)
                s = jnp.where(key < t_new, s, NEG_INF)
            s_ref[b, hs, start:start + size] = s

    def chunk_values(b, start, size, state):
        m, l, acc = state
        s = s_ref[b, :, start:start + size]
        m_new = jnp.maximum(m, jnp.max(s, axis=-1, keepdims=True))
        alpha = jnp.exp(m - m_new)
        p = jnp.exp(s - m_new)
        l = alpha * l + jnp.sum(p, axis=-1, keepdims=True)
        acc = alpha * acc + _dot(p.astype(bf16), c_ref[b, start:start + size, :])
        return m_new, l, acc

    def finish(b, state):
        _, l, acc = state
        o_lat = (acc / l).astype(bf16)
        for h in range(N_HEADS):
            o = _dot(o_lat[h * t_new:(h + 1) * t_new], wuv_ref[h])
            hs = slice(h * D_V, (h + 1) * D_V)
            bs = slice(b * t_new, (b + 1) * t_new)
            o_ref[bs, hs] = (o * gm_ref[bs, hs].astype(f32)).astype(bf16)

    items = [(b,) + ch for b in range(n_b) for ch in chunks]
    init = (jnp.full((rows, 1), NEG_INF, f32), jnp.zeros((rows, 1), f32),
            jnp.zeros((rows, KV_RANK), f32))
    state = init
    chunk_scores(*items[0])
    for nxt, cur in zip(items[1:] + [None], items):
        if nxt is not None:
            chunk_scores(*nxt)
        state = chunk_values(*cur, state)
        if nxt is None or nxt[0] != cur[0]:
            finish(cur[0], state)
            state = init


def _attn_sample(cache_ckv, cache_krope_t, ckv_new, krope_new, q_abs, q_rope, gm, wts, *, t_new, n_b):
    n_seq, past, _ = cache_ckv.shape
    chunk = 512
    chunks = tuple((s, min(chunk, past - s)) for s in range(0, past, chunk)) + ((past, LANES),)
    s_pad = past + LANES
    body = functools.partial(_attn_sample_body, past=past, t_new=t_new, chunks=chunks, n_b=n_b)
    return pl.pallas_call(
        body,
        grid=(n_seq // n_b,),
        in_specs=[pl.BlockSpec((n_b, past, KV_RANK), lambda b: (b, 0, 0)),
                  pl.BlockSpec((n_b, D_ROPE, past), lambda b: (b, 0, 0)),
                  pl.BlockSpec((n_b, t_new, KV_RANK), lambda b: (b, 0, 0)),
                  pl.BlockSpec((n_b, t_new, D_ROPE), lambda b: (b, 0, 0)),
                  pl.BlockSpec((N_HEADS, n_b * t_new, KV_RANK), lambda b: (0, b, 0)),
                  pl.BlockSpec((N_HEADS, n_b * t_new, D_ROPE), lambda b: (0, b, 0)),
                  pl.BlockSpec((n_b * t_new, D_MLA), lambda b: (b, 0)),
                  _resident((N_HEADS * D_NOPE, KV_RANK)),
                  _resident((N_HEADS, KV_RANK, D_V))],
        out_specs=pl.BlockSpec((n_b * t_new, D_MLA), lambda b: (b, 0)),
        out_shape=jax.ShapeDtypeStruct((n_seq * t_new, D_MLA), bf16),
        scratch_shapes=[pltpu.VMEM((n_b, s_pad, KV_RANK), bf16),
                        pltpu.VMEM((n_b, s_pad - past, D_ROPE), bf16),
                        pltpu.VMEM((n_b, N_HEADS * t_new, s_pad), f32)],
        compiler_params=_cparams(1),
        name="attn_sample",
    )(cache_ckv, cache_krope_t, ckv_new, krope_new, q_abs, q_rope, gm,
      wts["w_uk_t"].reshape(N_HEADS * D_NOPE, KV_RANK), wts["w_uv3"])


def _out_proj_body(x_ref, pa_ref, ma_ref, p_ref, wo_ref, png_ref, wg_ref, bg_ref, wple_ref, y_ref):
    h = x_ref[...] + _dot(pa_ref[...], wo_ref[0:D_POOL, :]) + _dot(ma_ref[...], wo_ref[D_POOL:, :])
    hn = (h * _rms(h, D_MODEL) * png_ref[...]).astype(bf16)
    gate = jax.nn.sigmoid(_dot(hn, wg_ref[...]) + bg_ref[...])
    y_ref[...] = h + gate * _dot(p_ref[...].astype(bf16), wple_ref[...])


def _out_proj(x2d, pool_act, mla_act, p2d, wts, *, tm):
    n_tok = x2d.shape[0]
    row = lambda i: (i, 0)
    return pl.pallas_call(
        _out_proj_body,
        grid=(n_tok // tm,),
        in_specs=[pl.BlockSpec((tm, D_MODEL), row), pl.BlockSpec((tm, D_POOL), row),
                  pl.BlockSpec((tm, D_MLA), row), pl.BlockSpec((tm, D_PLE), row),
                  _resident((D_MODEL, D_MODEL)), _resident((1, D_MODEL)),
                  _resident((D_MODEL, D_MODEL)), _resident((1, D_MODEL)),
                  _resident((D_PLE, D_MODEL))],
        out_specs=pl.BlockSpec((tm, D_MODEL), row),
        out_shape=jax.ShapeDtypeStruct((n_tok, D_MODEL), f32),
        compiler_params=_cparams(1),
        name="out_proj",
    )(x2d, pool_act, mla_act, p2d, wts["w_out"], wts["ple_norm_g"], wts["w_ple_gate"],
      wts["b_ple_gate"], wts["w_ple"])


def _rope_angles(pos0, t):
    pos = pos0 + jnp.arange(t)
    inv = ROPE_THETA ** (-(jnp.arange(0, D_ROPE, 2, dtype=f32) / D_ROPE))
    ang = pos.astype(f32)[:, None] * inv[None, :]
    return pos, jnp.cos(ang), jnp.sin(ang)


def _rope_tables(pos0, t):
    pos, cos, sin = _rope_angles(pos0, t)
    zero = jnp.zeros((t, LANES - D_ROPE), f32)
    return (jnp.concatenate([cos, cos, zero], axis=-1), jnp.concatenate([-sin, sin, zero], axis=-1),
            pos.astype(f32)[:, None], (cos.T, sin.T))


def _prep_weights(norm_g, w_in, q_norm_g, w_uq, kv_norm_g, w_ukv, q_nope_g, q_rope_g, k_nope_g,
                  k_rope_g, w_pool, pool_scale, w_out, ple_norm_g, w_ple_gate, b_ple_gate, w_ple):
    w_uq_r = jnp.pad(w_uq.reshape(Q_RANK, N_HEADS, D_NOPE + D_ROPE),
                     ((0, 0), (0, 0), (0, D_QK - D_NOPE - D_ROPE))).reshape(Q_RANK, N_HEADS * D_QK)
    w_ukv3 = w_ukv.reshape(KV_RANK, N_HEADS, D_NOPE + D_V)
    w_uk3, w_uv3 = w_ukv3[..., :D_NOPE], w_ukv3[..., D_NOPE:]
    pad_rope = lambda g: jnp.pad(g, (0, LANES - D_ROPE))[None, :]
    return {
        "norm_g": norm_g[None, :], "w_in": w_in.astype(bf16), "q_norm_g": q_norm_g[None, :],
        "kv_norm_g": kv_norm_g[None, :], "k_rope_g": pad_rope(k_rope_g),
        "w_pool": w_pool.astype(bf16), "pool_scale": pool_scale[None, :],
        "w_uq": w_uq_r.astype(bf16), "q_nope_g": q_nope_g[None, :], "q_rope_g": pad_rope(q_rope_g),
        "w_uq_t": w_uq.T.astype(bf16),
        "q_nope_g_col": q_nope_g[:, None], "q_rope_g_col": q_rope_g[:, None],
        "k_nope_g": k_nope_g[None, :],
        "w_uk": w_uk3.reshape(KV_RANK, N_HEADS * D_NOPE).astype(bf16),
        "w_uv_t": w_uv3.reshape(KV_RANK, D_MLA).T.astype(bf16),
        "w_uk_t": jnp.transpose(w_uk3, (1, 2, 0)).astype(bf16),
        "w_uv3": jnp.transpose(w_uv3, (1, 0, 2)).astype(bf16),
        "w_out": w_out.astype(bf16), "ple_norm_g": ple_norm_g[None, :],
        "w_ple_gate": w_ple_gate.astype(bf16), "b_ple_gate": b_ple_gate[None, :],
        "w_ple": w_ple.astype(bf16),
    }


def _layer_prompt(x, p, wts):
    n_seq, seq_len, _ = x.shape
    x2d = x.reshape(n_seq * seq_len, D_MODEL)
    cos2, sin2, pos, rope_t = _rope_tables(0, seq_len)
    hist = jnp.zeros((1, HALO, D_POOL), f32)
    pact, gm, ckv, krope, npool, qt, k, vt = _in_proj(
        x2d, hist, cos2, sin2, pos, wts, nseg=1, seg_len=PROMPT_TILE, n_seq=n_seq, carry=True,
        rope_t=rope_t)
    mact = _attn_prompt(qt, k, vt, gm, n_seq=n_seq, seq_len=seq_len, tq=PROMPT_TILE)
    y = _out_proj(x2d, pact, mact, p.reshape(n_seq * seq_len, D_PLE), wts, tm=OUT_TILE)
    return (y.reshape(x.shape), ckv.reshape(n_seq, seq_len, KV_RANK),
            krope.reshape(n_seq, seq_len, D_ROPE), npool)


def _layer_sample(x, p, state_pool, cache_ckv, cache_krope, wts):
    n_seq, t_new, _ = x.shape
    past = cache_ckv.shape[1]
    n_tok = n_seq * t_new
    x2d = x.reshape(n_tok, D_MODEL)
    cos1, sin1, pos1, _ = _rope_tables(past, t_new)
    cos2, sin2, pos = (jnp.tile(a, (n_seq, 1)) for a in (cos1, sin1, pos1))
    hist = jnp.pad(state_pool, ((0, 0), (HALO - POOL_HIST, 0), (0, 0)))
    pact, gm, ckv, krope, npool, qlat = _in_proj(
        x2d, hist, cos2, sin2, pos, wts, nseg=n_seq, seg_len=t_new, n_seq=n_seq, carry=False)
    q_abs, q_rope = _q_proj_sample(qlat, cos2, sin2, wts)
    ckv3 = ckv.reshape(n_seq, t_new, KV_RANK)
    krope3 = krope.reshape(n_seq, t_new, D_ROPE)
    cache_krope_t = jnp.transpose(cache_krope, (0, 2, 1))
    mact = _attn_sample(cache_ckv, cache_krope_t, ckv3, krope3, q_abs, q_rope, gm, wts, t_new=t_new,
                        n_b=SAMPLE_BATCH_PER_STEP)
    y = _out_proj(x2d, pact, mact, p.reshape(n_tok, D_PLE), wts, tm=OUT_TILE)
    return y.reshape(x.shape), ckv3, krope3, npool


def kernel(x_prompt, x_sample, cache_ckv, cache_krope, state_pool, p_prompt, p_sample, norm_g, w_in,
           q_norm_g, w_uq, kv_norm_g, w_ukv, q_nope_g, q_rope_g, k_nope_g, k_rope_g, w_pool,
           pool_scale, w_out, ple_norm_g, w_ple_gate, b_ple_gate, w_ple):
    depth = norm_g.shape[0]
    layer_w = (norm_g, w_in, q_norm_g, w_uq, kv_norm_g, w_ukv, q_nope_g, q_rope_g, k_nope_g, k_rope_g,
               w_pool, pool_scale, w_out, ple_norm_g, w_ple_gate, b_ple_gate, w_ple)
    yp, ys = x_prompt, x_sample
    outs = [[] for _ in range(6)]
    for i in range(depth):
        wts = _prep_weights(*(w[i] for w in layer_w))
        yp, c1, k1, s1 = _layer_prompt(yp, p_prompt[i], wts)
        ys, c2, k2, s2 = _layer_sample(ys, p_sample[i], state_pool[i], cache_ckv[i], cache_krope[i], wts)
        for lst, val in zip(outs, (c1, k1, s1, c2, k2, s2)):
            lst.append(val)
    return (yp, ys) + tuple(jnp.stack(o) for o in outs)
```

```python
import functools

import jax
import jax.numpy as jnp
import numpy as np
from jax import lax
from jax.experimental import pallas as pl
from jax.experimental.pallas import tpu as pltpu

D_MODEL = 2048
CHUNK = 64
D_POOL = 1024
POOL_WINDOWS = (2, 4, 8, 16)
POOL_GROUP = 256
POOL_HIST = 15
HALO = 16
N_HEADS = 8
D_NOPE = 128
D_ROPE = 64
D_V = 128
D_VX = D_V + 16
D_MLA = N_HEADS * D_V
Q_RANK = 512
KV_RANK = 256
D_PLE = 256
D_QK = 256
ROPE_THETA = 10000.0
EPS = 1e-6
ATTN_SCALE = (D_NOPE + D_ROPE) ** -0.5
NEG_INF = -1e30
EXP2_SCALE = ATTN_SCALE * float(np.log2(np.e))
LANES = 128

C_U, C_GP, C_Q, C_KV, C_KR, C_GM = 0, 1024, 2048, 2560, 2816, 2880
D_IN = 3904

VMEM_LIMIT = 56 * 1024 * 1024
PROMPT_TILE = 256
OUT_TILE = 256
SAMPLE_BATCH_PER_STEP = 4

f32 = jnp.float32
bf16 = jnp.bfloat16


def _cparams(n_axes):
    return pltpu.CompilerParams(dimension_semantics=("arbitrary",) * n_axes,
                                vmem_limit_bytes=VMEM_LIMIT)


def _resident(shape):
    nd = len(shape)
    return pl.BlockSpec(shape, lambda *_: (0,) * nd, pipeline_mode=pl.Buffered(1))


def _rms(x, n):
    return lax.rsqrt(jnp.sum(x * x, axis=-1, keepdims=True) * (1.0 / n) + EPS)


def _rms_cols(x_t, n):
    return lax.rsqrt(jnp.sum(x_t * x_t, axis=0, keepdims=True) * (1.0 / n) + EPS)


def _dot(a, b):
    return jnp.dot(a, b, preferred_element_type=f32)


def _dot_nt(a, b):
    return lax.dot_general(a, b, (((1,), (1,)), ((), ())), preferred_element_type=f32)


def _rope128(x, cos2, sin2):
    lane = lax.broadcasted_iota(jnp.int32, x.shape, 1)
    swapped = jnp.where(lane < D_ROPE // 2, pltpu.roll(x, LANES - D_ROPE // 2, 1),
                        pltpu.roll(x, D_ROPE // 2, 1))
    return x * cos2 + swapped * sin2


def _queries_t(ql, wqt_ref, cos_ref, sin_ref, qng_ref, qrg_ref, qt_ref):
    q_t = _dot_nt(wqt_ref[...], ql)
    tm = q_t.shape[1]
    cos, sin = cos_ref[...], sin_ref[...]
    half = D_ROPE // 2
    for h in range(N_HEADS):
        r0 = h * D_QK
        s0 = h * (D_NOPE + D_ROPE)
        qn = q_t[s0:s0 + D_NOPE]
        qn = qn * (_rms_cols(qn, D_NOPE) * EXP2_SCALE) * qng_ref[...]
        qt_ref[0, r0:r0 + D_NOPE, :] = qn.astype(bf16)
        qr = q_t[s0 + D_NOPE:s0 + D_NOPE + D_ROPE]
        qr = qr * (_rms_cols(qr, D_ROPE) * EXP2_SCALE) * qrg_ref[...]
        x1, x2 = qr[:half], qr[half:]
        qt_ref[0, r0 + D_NOPE:r0 + D_NOPE + half, :] = (x1 * cos - x2 * sin).astype(bf16)
        qt_ref[0, r0 + D_NOPE + half:r0 + D_NOPE + D_ROPE, :] = (x2 * cos + x1 * sin).astype(bf16)
        qt_ref[0, r0 + D_NOPE + D_ROPE:r0 + D_QK, :] = jnp.zeros((D_QK - D_NOPE - D_ROPE, tm), bf16)


def _keys_values(c, kr128, wk_ref, wvt_ref, kng_ref, k_ref, vt_ref):
    tm = c.shape[0]
    k = _dot(c, wk_ref[...])
    kr = kr128.astype(bf16)
    for h in range(N_HEADS):
        kn = k[:, h * D_NOPE:(h + 1) * D_NOPE]
        k_ref[:, h * D_QK:h * D_QK + D_NOPE] = (kn * _rms(kn, D_NOPE) * kng_ref[...]).astype(bf16)
        k_ref[:, h * D_QK + D_NOPE:(h + 1) * D_QK] = kr
    v_t = _dot_nt(wvt_ref[...], c).astype(bf16)
    row = lax.broadcasted_iota(jnp.int32, (D_VX - D_V, tm), 0)
    ones_row = jnp.where(row == 0, 1.0, 0.0).astype(bf16)
    for h in range(N_HEADS):
        vt_ref[0, 0, h, 0:D_V, :] = v_t[h * D_V:(h + 1) * D_V]
        vt_ref[0, 0, h, D_V:D_VX, :] = ones_row


def _in_proj_body(*refs, nseg, seg_len, tiles_per_seq, carry, fuse_qkv):
    (x_ref, hist_ref, cos_ref, sin_ref, pos_ref, ng_ref, w_ref, qg_ref, kvg_ref, krg_ref, wp_ref,
     ps_ref) = refs[:12]
    if fuse_qkv:
        cost_ref, sint_ref, wqt_ref, qng_ref, qrg_ref, wk_ref, wvt_ref, kng_ref = refs[12:20]
        pact_ref, gm_ref, ckv_ref, kr_ref, npool_ref, qt_ref, k_ref, vt_ref, ucat_ref = refs[20:]
    else:
        pact_ref, gm_ref, ckv_ref, kr_ref, npool_ref, ql_ref, ucat_ref = refs[12:]
    i = pl.program_id(0)
    tm = nseg * seg_len
    x = x_ref[...]
    xn = (x * _rms(x, D_MODEL) * ng_ref[...]).astype(bf16)

    u = _dot(xn, w_ref[:, C_U:C_U + D_POOL])
    if carry:
        @pl.when(i % tiles_per_seq == 0)
        def _():
            ucat_ref[:, 0:HALO, :] = jnp.zeros((nseg, HALO, D_POOL), f32)
    else:
        ucat_ref[:, 0:HALO, :] = hist_ref[...]
    ucat_ref[:, HALO:HALO + seg_len, :] = u.reshape(nseg, seg_len, D_POOL)

    pos = pos_ref[...]
    gp = _dot(xn, w_ref[:, C_GP:C_GP + D_POOL])
    gate = gp * jax.nn.sigmoid(gp) * ps_ref[...]

    def pool_group(g):
        w = POOL_WINDOWS[g]
        sl = slice(g * POOL_GROUP, (g + 1) * POOL_GROUP)
        acc = ucat_ref[:, HALO:HALO + seg_len, sl]
        for k in range(1, w):
            acc = acc + ucat_ref[:, HALO - k:HALO - k + seg_len, sl]
        acc = acc.reshape(tm, POOL_GROUP)
        inv_cnt = 1.0 / jnp.minimum(pos + 1.0, float(w))
        d = (acc * inv_cnt - u[:, sl]).astype(bf16)
        mixed = _dot(d, wp_ref[g])
        pact_ref[:, sl] = (mixed * gate[:, sl]).astype(bf16)

    cq = _dot(xn, w_ref[:, C_Q:C_Q + Q_RANK])
    pool_group(0)
    pool_group(1)
    ql = (cq * _rms(cq, Q_RANK) * qg_ref[...]).astype(bf16)
    if fuse_qkv:
        _queries_t(ql, wqt_ref, cost_ref, sint_ref, qng_ref, qrg_ref, qt_ref)
    else:
        ql_ref[...] = ql

    ckv = _dot(xn, w_ref[:, C_KV:C_KV + KV_RANK])
    pool_group(2)
    ckv = ckv * _rms(ckv, KV_RANK) * kvg_ref[...]
    ckv_ref[...] = ckv

    tail = _dot(xn, w_ref[:, C_KR:D_IN])
    pool_group(3)
    npool_ref[...] = ucat_ref[:, seg_len + 1:seg_len + HALO, :]
    if carry:
        ucat_ref[:, 0:HALO, :] = ucat_ref[:, seg_len:seg_len + HALO, :]
    lane = lax.broadcasted_iota(jnp.int32, (tm, LANES), 1)
    kr = jnp.where(lane < D_ROPE, tail[:, :LANES], 0.0)
    kr = _rope128(kr * _rms(kr, D_ROPE) * krg_ref[...], cos_ref[...], sin_ref[...])
    if fuse_qkv:
        kr_ref[0] = kr.T[:D_ROPE]
    else:
        kr_ref[...] = kr[:, :D_ROPE]
    gm = tail[:, D_ROPE:]
    gm_ref[...] = (gm * jax.nn.sigmoid(gm)).astype(bf16)
    if fuse_qkv:
        _keys_values(ckv.astype(bf16), kr, wk_ref, wvt_ref, kng_ref, k_ref, vt_ref)


def _in_proj(x2d, hist, cos2, sin2, pos, wts, *, nseg, seg_len, n_seq, carry, rope_t=None):
    n_tok = x2d.shape[0]
    tm = nseg * seg_len
    n_tiles = n_tok // tm
    tiles_per_seq = n_tiles // n_seq if carry else 1
    tab_tiles = cos2.shape[0] // tm
    fuse_qkv = rope_t is not None
    row = lambda i: (i, 0)
    tab = lambda i: (i % tab_tiles, 0)
    if carry:
        hist_spec = pl.BlockSpec((1, HALO, D_POOL), lambda i: (0, 0, 0))
        npool_spec = pl.BlockSpec((1, POOL_HIST, D_POOL), lambda i: (i // tiles_per_seq, 0, 0))
    else:
        hist_spec = pl.BlockSpec((nseg, HALO, D_POOL), lambda i: (i, 0, 0))
        npool_spec = pl.BlockSpec((nseg, POOL_HIST, D_POOL), lambda i: (i, 0, 0))
    in_specs = [
        pl.BlockSpec((tm, D_MODEL), row), hist_spec, pl.BlockSpec((tm, LANES), tab),
        pl.BlockSpec((tm, LANES), tab), pl.BlockSpec((tm, 1), tab), _resident((1, D_MODEL)),
        _resident((D_MODEL, D_IN)), _resident((1, Q_RANK)), _resident((1, KV_RANK)),
        _resident((1, LANES)), _resident((4, POOL_GROUP, POOL_GROUP)), _resident((1, D_POOL)),
    ]
    args = [x2d, hist, cos2, sin2, pos, wts["norm_g"], wts["w_in"], wts["q_norm_g"],
            wts["kv_norm_g"], wts["k_rope_g"], wts["w_pool"], wts["pool_scale"]]
    out_specs = [pl.BlockSpec((tm, D_POOL), row), pl.BlockSpec((tm, D_MLA), row),
                 pl.BlockSpec((tm, KV_RANK), row), pl.BlockSpec((tm, D_ROPE), row), npool_spec]
    out_shape = [jax.ShapeDtypeStruct((n_tok, D_POOL), bf16), jax.ShapeDtypeStruct((n_tok, D_MLA), bf16),
                 jax.ShapeDtypeStruct((n_tok, KV_RANK), f32), jax.ShapeDtypeStruct((n_tok, D_ROPE), f32),
                 jax.ShapeDtypeStruct((n_seq, POOL_HIST, D_POOL), f32)]
    if fuse_qkv:
        seq_tab = lambda i: (0, i % tiles_per_seq)
        out_specs[3] = pl.BlockSpec((1, D_ROPE, tm), lambda i: (i // tiles_per_seq, 0, i % tiles_per_seq))
        out_shape[3] = jax.ShapeDtypeStruct((n_seq, D_ROPE, tiles_per_seq * tm), f32)
        in_specs += [pl.BlockSpec((D_ROPE // 2, tm), seq_tab), pl.BlockSpec((D_ROPE // 2, tm), seq_tab),
                     _resident((N_HEADS * (D_NOPE + D_ROPE), Q_RANK)), _resident((D_NOPE, 1)),
                     _resident((D_ROPE, 1)),
                     _resident((KV_RANK, N_HEADS * D_NOPE)), _resident((D_MLA, KV_RANK)),
                     _resident((1, D_NOPE))]
        args += [rope_t[0], rope_t[1], wts["w_uq_t"], wts["q_nope_g_col"], wts["q_rope_g_col"],
                 wts["w_uk"], wts["w_uv_t"], wts["k_nope_g"]]
        out_specs += [
            pl.BlockSpec((1, N_HEADS * D_QK, tm), lambda i: (i // tiles_per_seq, 0, i % tiles_per_seq)),
            pl.BlockSpec((tm, N_HEADS * D_QK), row),
            pl.BlockSpec((1, 1, N_HEADS, D_VX, tm),
                         lambda i: (i // tiles_per_seq, i % tiles_per_seq, 0, 0, 0))]
        out_shape += [jax.ShapeDtypeStruct((n_seq, N_HEADS * D_QK, tiles_per_seq * tm), bf16),
                      jax.ShapeDtypeStruct((n_tok, N_HEADS * D_QK), bf16),
                      jax.ShapeDtypeStruct((n_seq, tiles_per_seq, N_HEADS, D_VX, tm), bf16)]
    else:
        out_specs.append(pl.BlockSpec((tm, Q_RANK), row))
        out_shape.append(jax.ShapeDtypeStruct((n_tok, Q_RANK), bf16))
    body = functools.partial(_in_proj_body, nseg=nseg, seg_len=seg_len,
                             tiles_per_seq=tiles_per_seq, carry=carry, fuse_qkv=fuse_qkv)
    return pl.pallas_call(
        body,
        grid=(n_tiles,),
        in_specs=in_specs,
        out_specs=out_specs,
        out_shape=out_shape,
        scratch_shapes=[pltpu.VMEM((nseg, HALO + seg_len, D_POOL), f32)],
        compiler_params=_cparams(1),
        name="in_proj",
    )(*args)


def _q_proj_sample_body(ql_ref, wq_ref, cos_ref, sin_ref, qng_ref, qrg_ref, kng_ref, wukt_ref,
                        qa_ref, qr_ref):
    q = _dot(ql_ref[...], wq_ref[...])
    cos2, sin2 = cos_ref[...], sin_ref[...]
    for h in range(N_HEADS):
        qn = q[:, h * D_QK:h * D_QK + D_NOPE]
        qn = qn * _rms(qn, D_NOPE) * qng_ref[...]
        qa_ref[h] = _dot((qn * kng_ref[...]).astype(bf16), wukt_ref[h]).astype(bf16)
        qr = q[:, h * D_QK + D_NOPE:(h + 1) * D_QK]
        qr = _rope128(qr * _rms(qr, D_ROPE) * qrg_ref[...], cos2, sin2)
        qr_ref[h] = qr[:, :D_ROPE].astype(bf16)


def _q_proj_sample(qlat, cos2, sin2, wts):
    n_tok = qlat.shape[0]
    return pl.pallas_call(
        _q_proj_sample_body,
        grid=(1,),
        in_specs=[_resident((n_tok, Q_RANK)), _resident((Q_RANK, N_HEADS * D_QK)),
                  _resident((n_tok, LANES)), _resident((n_tok, LANES)),
                  _resident((1, D_NOPE)), _resident((1, LANES)), _resident((1, D_NOPE)),
                  _resident((N_HEADS, D_NOPE, KV_RANK))],
        out_specs=[_resident((N_HEADS, n_tok, KV_RANK)), _resident((N_HEADS, n_tok, D_ROPE))],
        out_shape=[jax.ShapeDtypeStruct((N_HEADS, n_tok, KV_RANK), bf16),
                   jax.ShapeDtypeStruct((N_HEADS, n_tok, D_ROPE), bf16)],
        compiler_params=_cparams(1),
        name="q_proj_sample",
    )(qlat, wts["w_uq"], cos2, sin2, wts["q_nope_g"], wts["q_rope_g"], wts["k_nope_g"],
      wts["w_uk_t"])


def _attn_prompt_body(qt_ref, k_ref, vt_ref, gm_ref, o_ref, m_ref, acc_ref, s_ref, mx_ref, *, tq, tk):
    qi = pl.program_id(1)
    m_ref[...] = jnp.full(m_ref.shape, NEG_INF, f32)
    acc_ref[...] = jnp.zeros(acc_ref.shape, f32)

    def all_sublanes_max(x):
        for shift in (4, 2, 1):
            x = jnp.maximum(x, pltpu.roll(x, shift, 0))
        return x

    def scores(h, j, buf, masked):
        row0 = pl.multiple_of(j * tk, tk)
        kb = k_ref[pl.ds(row0, tk), h * D_QK:(h + 1) * D_QK]
        s = _dot(kb, qt_ref[0, h * D_QK:(h + 1) * D_QK, :])
        if masked:
            k_chunk = lax.broadcasted_iota(jnp.int32, (tk, tq), 0) // CHUNK
            q_chunk = lax.broadcasted_iota(jnp.int32, (tk, tq), 1) // CHUNK
            s = jnp.where(k_chunk <= q_chunk, s, NEG_INF)
        s_ref[buf, h] = s
        mx_ref[buf, h] = all_sublanes_max(jnp.max(s.reshape(tk // 8, 8, tq), axis=0))

    def values(h, j, buf):
        m_old = m_ref[h]
        m_new = jnp.maximum(m_old, mx_ref[buf, h])
        alpha = jnp.exp2(m_old - m_new)
        p = jnp.exp2(s_ref[buf, h].reshape(tk // 8, 8, tq) - m_new[None]).reshape(tk, tq)
        m_ref[h] = m_new
        acc = acc_ref[h].reshape(D_VX // 8, 8, tq) * alpha[None]
        acc_ref[h] = acc.reshape(D_VX, tq) + _dot(vt_ref[0, j, h], p.astype(bf16))

    for h in range(N_HEADS):
        scores(h, qi, 0, True)

    def step(t, buf):
        prev = jnp.where(t == 1, qi, t - 2)
        for h in range(N_HEADS):
            scores(h, t - 1, buf, False)
            values(h, prev, 1 - buf)

    def step_pair(u, carry):
        step(2 * u + 1, 1)
        step(2 * u + 2, 0)
        return carry

    lax.fori_loop(0, qi // 2, step_pair, 0)

    @pl.when(qi % 2 == 1)
    def _():
        step(qi, 1)
        for h in range(N_HEADS):
            values(h, qi - 1, 1)

    @pl.when(qi % 2 == 0)
    def _():
        last = jnp.where(qi == 0, qi, qi - 1)
        for h in range(N_HEADS):
            values(h, last, 0)

    for h in range(N_HEADS):
        hs = slice(h * D_V, (h + 1) * D_V)
        o = (acc_ref[h, 0:D_V, :] / acc_ref[h, D_V:D_V + 1, :]).T
        o_ref[:, hs] = (o * gm_ref[:, hs].astype(f32)).astype(bf16)


def _attn_prompt(qt, k, vt, gm, *, n_seq, seq_len, tq):
    tk = tq
    nq = seq_len // tq
    body = functools.partial(_attn_prompt_body, tq=tq, tk=tk)
    return pl.pallas_call(
        body,
        grid=(n_seq, nq),
        in_specs=[pl.BlockSpec((1, N_HEADS * D_QK, tq), lambda b, i: (b, 0, i)),
                  pl.BlockSpec((seq_len, N_HEADS * D_QK), lambda b, i: (b, 0)),
                  pl.BlockSpec((1, seq_len // tk, N_HEADS, D_VX, tk), lambda b, i: (b, 0, 0, 0, 0)),
                  pl.BlockSpec((tq, D_MLA), lambda b, i: (b * nq + i, 0))],
        out_specs=pl.BlockSpec((tq, D_MLA), lambda b, i: (b * nq + i, 0)),
        out_shape=jax.ShapeDtypeStruct((n_seq * seq_len, D_MLA), bf16),
        scratch_shapes=[pltpu.VMEM((N_HEADS, 8, tq), f32), pltpu.VMEM((N_HEADS, D_VX, tq), f32),
                        pltpu.VMEM((2, N_HEADS, tk, tq), f32), pltpu.VMEM((2, N_HEADS, 8, tq), f32)],
        compiler_params=_cparams(2),
        name="attn_prompt",
    )(qt, k, vt, gm)


def _attn_sample_body(cckv_ref, ckr_t_ref, nckv_ref, nkr_ref, qa_ref, qr_ref, gm_ref, wukt_ref, wuv_ref,
                      o_ref, c_ref, ktail_ref, s_ref, *, past, t_new, chunks, n_b):
    n_tail = c_ref.shape[1] - past
    rows = N_HEADS * t_new
    c_ref[:, past:past + t_new, :] = nckv_ref[...].astype(bf16)
    c_ref[:, past + t_new:, :] = jnp.zeros((n_b, n_tail - t_new, KV_RANK), bf16)
    ktail_ref[:, 0:t_new, :] = nkr_ref[...].astype(bf16)
    ktail_ref[:, t_new:, :] = jnp.zeros((n_b, n_tail - t_new, D_ROPE), bf16)

    def chunk_scores(b, start, size):
        cached = start < past
        if cached:
            c_ref[b, start:start + size, :] = cckv_ref[b, start:start + size, :].astype(bf16)
        c = c_ref[b, start:start + size, :]
        k_t = _dot_nt(wukt_ref[...], c)
        ssq = jnp.sum((k_t * k_t).reshape(N_HEADS, D_NOPE, size), axis=1)
        r = lax.rsqrt(ssq * (1.0 / D_NOPE) + EPS)
        qa = qa_ref[:, b * t_new:(b + 1) * t_new, :].reshape(rows, KV_RANK)
        qr = qr_ref[:, b * t_new:(b + 1) * t_new, :].reshape(rows, D_ROPE)
        s_nope = _dot_nt(qa, c)
        if cached:
            s_rope = _dot(qr, ckr_t_ref[b, :, start:start + size].astype(bf16))
        else:
            s_rope = _dot_nt(qr, ktail_ref[b])
        for h in range(N_HEADS):
            hs = slice(h * t_new, (h + 1) * t_new)
            s = (s_nope[hs] * r[h:h + 1, :] + s_rope[hs]) * ATTN_SCALE
            if not cached:
                key = lax.broadcasted_iota(jnp.int32, (t_new, size), 1)
                s = jnp.where(key < t_new, s, NEG_INF)
            s_ref[b, hs, start:start + size] = s

    def chunk_values(b, start, size, state):
        m, l, acc = state
        s = s_ref[b, :, start:start + size]
        m_new = jnp.maximum(m, jnp.max(s, axis=-1, keepdims=True))
        alpha = jnp.exp(m - m_new)
        p = jnp.exp(s - m_new)
        l = alpha * l + jnp.sum(p, axis=-1, keepdims=True)
        acc = alpha * acc + _dot(p.astype(bf16), c_ref[b, start:start + size, :])
        return m_new, l, acc

    def finish(b, state):
        _, l, acc = state
        o_lat = (acc / l).astype(bf16)
        for h in range(N_HEADS):
            o = _dot(o_lat[h * t_new:(h + 1) * t_new], wuv_ref[h])
            hs = slice(h * D_V, (h + 1) * D_V)
            bs = slice(b * t_new, (b + 1) * t_new)
            o_ref[bs, hs] = (o * gm_ref[bs, hs].astype(f32)).astype(bf16)

    items = [(b,) + ch for b in range(n_b) for ch in chunks]
    init = (jnp.full((rows, 1), NEG_INF, f32), jnp.zeros((rows, 1), f32),
            jnp.zeros((rows, KV_RANK), f32))
    state = init
    chunk_scores(*items[0])
    for nxt, cur in zip(items[1:] + [None], items):
        if nxt is not None:
            chunk_scores(*nxt)
        state = chunk_values(*cur, state)
        if nxt is None or nxt[0] != cur[0]:
            finish(cur[0], state)
            state = init


def _attn_sample(cache_ckv, cache_krope_t, ckv_new, krope_new, q_abs, q_rope, gm, wts, *, t_new, n_b):
    n_seq, past, _ = cache_ckv.shape
    chunk = 512
    chunks = tuple((s, min(chunk, past - s)) for s in range(0, past, chunk)) + ((past, LANES),)
    s_pad = past + LANES
    body = functools.partial(_attn_sample_body, past=past, t_new=t_new, chunks=chunks, n_b=n_b)
    return pl.pallas_call(
        body,
        grid=(n_seq // n_b,),
        in_specs=[pl.BlockSpec((n_b, past, KV_RANK), lambda b: (b, 0, 0)),
                  pl.BlockSpec((n_b, D_ROPE, past), lambda b: (b, 0, 0)),
                  pl.BlockSpec((n_b, t_new, KV_RANK), lambda b: (b, 0, 0)),
                  pl.BlockSpec((n_b, t_new, D_ROPE), lambda b: (b, 0, 0)),
                  pl.BlockSpec((N_HEADS, n_b * t_new, KV_RANK), lambda b: (0, b, 0)),
                  pl.BlockSpec((N_HEADS, n_b * t_new, D_ROPE), lambda b: (0, b, 0)),
                  pl.BlockSpec((n_b * t_new, D_MLA), lambda b: (b, 0)),
                  _resident((N_HEADS * D_NOPE, KV_RANK)),
                  _resident((N_HEADS, KV_RANK, D_V))],
        out_specs=pl.BlockSpec((n_b * t_new, D_MLA), lambda b: (b, 0)),
        out_shape=jax.ShapeDtypeStruct((n_seq * t_new, D_MLA), bf16),
        scratch_shapes=[pltpu.VMEM((n_b, s_pad, KV_RANK), bf16),
                        pltpu.VMEM((n_b, s_pad - past, D_ROPE), bf16),
                        pltpu.VMEM((n_b, N_HEADS * t_new, s_pad), f32)],
        compiler_params=_cparams(1),
        name="attn_sample",
    )(cache_ckv, cache_krope_t, ckv_new, krope_new, q_abs, q_rope, gm,
      wts["w_uk_t"].reshape(N_HEADS * D_NOPE, KV_RANK), wts["w_uv3"])


def _out_proj_body(x_ref, pa_ref, ma_ref, p_ref, wo_ref, png_ref, wg_ref, bg_ref, wple_ref, y_ref):
    h = x_ref[...] + _dot(pa_ref[...], wo_ref[0:D_POOL, :]) + _dot(ma_ref[...], wo_ref[D_POOL:, :])
    hn = (h * _rms(h, D_MODEL) * png_ref[...]).astype(bf16)
    gate = jax.nn.sigmoid(_dot(hn, wg_ref[...]) + bg_ref[...])
    y_ref[...] = h + gate * _dot(p_ref[...].astype(bf16), wple_ref[...])


def _out_proj(x2d, pool_act, mla_act, p2d, wts, *, tm):
    n_tok = x2d.shape[0]
    row = lambda i: (i, 0)
    return pl.pallas_call(
        _out_proj_body,
        grid=(n_tok // tm,),
        in_specs=[pl.BlockSpec((tm, D_MODEL), row), pl.BlockSpec((tm, D_POOL), row),
                  pl.BlockSpec((tm, D_MLA), row), pl.BlockSpec((tm, D_PLE), row),
                  _resident((D_MODEL, D_MODEL)), _resident((1, D_MODEL)),
                  _resident((D_MODEL, D_MODEL)), _resident((1, D_MODEL)),
                  _resident((D_PLE, D_MODEL))],
        out_specs=pl.BlockSpec((tm, D_MODEL), row),
        out_shape=jax.ShapeDtypeStruct((n_tok, D_MODEL), f32),
        compiler_params=_cparams(1),
        name="out_proj",
    )(x2d, pool_act, mla_act, p2d, wts["w_out"], wts["ple_norm_g"], wts["w_ple_gate"],
      wts["b_ple_gate"], wts["w_ple"])


def _rope_tables(pos0, t, reps=1):
    pos = (pos0 + np.arange(t)).astype(np.float64)
    inv = ROPE_THETA ** (-(np.arange(0, D_ROPE, 2, dtype=np.float64) / D_ROPE))
    ang = pos[:, None] * inv[None, :]
    cos, sin = np.cos(ang), np.sin(ang)
    zero = np.zeros((t, LANES - D_ROPE))
    rows = (np.concatenate([cos, cos, zero], axis=-1), np.concatenate([-sin, sin, zero], axis=-1),
            pos[:, None])
    as_f32 = lambda a: jnp.asarray(a.astype(np.float32))
    return tuple(as_f32(np.tile(a, (reps, 1))) for a in rows) + ((as_f32(cos.T), as_f32(sin.T)),)


def _prep_weights(norm_g, w_in, q_norm_g, w_uq, kv_norm_g, w_ukv, q_nope_g, q_rope_g, k_nope_g,
                  k_rope_g, w_pool, pool_scale, w_out, ple_norm_g, w_ple_gate, b_ple_gate, w_ple):
    w_uq_r = jnp.pad(w_uq.reshape(Q_RANK, N_HEADS, D_NOPE + D_ROPE),
                     ((0, 0), (0, 0), (0, D_QK - D_NOPE - D_ROPE))).reshape(Q_RANK, N_HEADS * D_QK)
    w_ukv3 = w_ukv.reshape(KV_RANK, N_HEADS, D_NOPE + D_V)
    w_uk3, w_uv3 = w_ukv3[..., :D_NOPE], w_ukv3[..., D_NOPE:]
    pad_rope = lambda g: jnp.pad(g, (0, LANES - D_ROPE))[None, :]
    return {
        "norm_g": norm_g[None, :], "w_in": w_in.astype(bf16), "q_norm_g": q_norm_g[None, :],
        "kv_norm_g": kv_norm_g[None, :], "k_rope_g": pad_rope(k_rope_g),
        "w_pool": w_pool.astype(bf16), "pool_scale": pool_scale[None, :],
        "w_uq": w_uq_r.astype(bf16), "q_nope_g": q_nope_g[None, :], "q_rope_g": pad_rope(q_rope_g),
        "w_uq_t": w_uq.T.astype(bf16),
        "q_nope_g_col": q_nope_g[:, None], "q_rope_g_col": q_rope_g[:, None],
        "k_nope_g": k_nope_g[None, :],
        "w_uk": w_uk3.reshape(KV_RANK, N_HEADS * D_NOPE).astype(bf16),
        "w_uv_t": w_uv3.reshape(KV_RANK, D_MLA).T.astype(bf16),
        "w_uk_t": jnp.transpose(w_uk3, (1, 2, 0)).astype(bf16),
        "w_uv3": jnp.transpose(w_uv3, (1, 0, 2)).astype(bf16),
        "w_out": w_out.astype(bf16), "ple_norm_g": ple_norm_g[None, :],
        "w_ple_gate": w_ple_gate.astype(bf16), "b_ple_gate": b_ple_gate[None, :],
        "w_ple": w_ple.astype(bf16),
    }


def _layer_prompt(x, p, wts):
    n_seq, seq_len, _ = x.shape
    x2d = x.reshape(n_seq * seq_len, D_MODEL)
    cos2, sin2, pos, rope_t = _rope_tables(0, seq_len)
    hist = jnp.zeros((1, HALO, D_POOL), f32)
    pact, gm, ckv, krope, npool, qt, k, vt = _in_proj(
        x2d, hist, cos2, sin2, pos, wts, nseg=1, seg_len=PROMPT_TILE, n_seq=n_seq, carry=True,
        rope_t=rope_t)
    mact = _attn_prompt(qt, k, vt, gm, n_seq=n_seq, seq_len=seq_len, tq=PROMPT_TILE)
    y = _out_proj(x2d, pact, mact, p.reshape(n_seq * seq_len, D_PLE), wts, tm=OUT_TILE)
    return (y.reshape(x.shape), ckv.reshape(n_seq, seq_len, KV_RANK),
            jnp.transpose(krope, (0, 2, 1)), npool)


def _layer_sample(x, p, state_pool, cache_ckv, cache_krope, wts):
    n_seq, t_new, _ = x.shape
    past = cache_ckv.shape[1]
    n_tok = n_seq * t_new
    x2d = x.reshape(n_tok, D_MODEL)
    cos2, sin2, pos, _ = _rope_tables(past, t_new, reps=n_seq)
    hist = jnp.pad(state_pool, ((0, 0), (HALO - POOL_HIST, 0), (0, 0)))
    pact, gm, ckv, krope, npool, qlat = _in_proj(
        x2d, hist, cos2, sin2, pos, wts, nseg=n_seq, seg_len=t_new, n_seq=n_seq, carry=False)
    q_abs, q_rope = _q_proj_sample(qlat, cos2, sin2, wts)
    ckv3 = ckv.reshape(n_seq, t_new, KV_RANK)
    krope3 = krope.reshape(n_seq, t_new, D_ROPE)
    cache_krope_t = jnp.transpose(cache_krope, (0, 2, 1))
    mact = _attn_sample(cache_ckv, cache_krope_t, ckv3, krope3, q_abs, q_rope, gm, wts, t_new=t_new,
                        n_b=SAMPLE_BATCH_PER_STEP)
    y = _out_proj(x2d, pact, mact, p.reshape(n_tok, D_PLE), wts, tm=OUT_TILE)
    return y.reshape(x.shape), ckv3, krope3, npool


def kernel(x_prompt, x_sample, cache_ckv, cache_krope, state_pool, p_prompt, p_sample, norm_g, w_in,
           q_norm_g, w_uq, kv_norm_g, w_ukv, q_nope_g, q_rope_g, k_nope_g, k_rope_g, w_pool,
           pool_scale, w_out, ple_norm_g, w_ple_gate, b_ple_gate, w_ple):
    depth = norm_g.shape[0]
    layer_w = (norm_g, w_in, q_norm_g, w_uq, kv_norm_g, w_ukv, q_nope_g, q_rope_g, k_nope_g, k_rope_g,
               w_pool, pool_scale, w_out, ple_norm_g, w_ple_gate, b_ple_gate, w_ple)
    yp, ys = x_prompt, x_sample
    outs = [[] for _ in range(6)]
    for i in range(depth):
        wts = _prep_weights(*(w[i] for w in layer_w))
        yp, c1, k1, s1 = _layer_prompt(yp, p_prompt[i], wts)
        ys, c2, k2, s2 = _layer_sample(ys, p_sample[i], state_pool[i], cache_ckv[i], cache_krope[i], wts)
        for lst, val in zip(outs, (c1, k1, s1, c2, k2, s2)):
            lst.append(val)
    return (yp, ys) + tuple(jnp.stack(o) for o in outs)
```

```python
import functools

import jax
import jax.numpy as jnp
import numpy as np
from jax import lax
from jax.experimental import pallas as pl
from jax.experimental.pallas import tpu as pltpu

D_MODEL = 2048
CHUNK = 64
D_POOL = 1024
POOL_WINDOWS = (2, 4, 8, 16)
POOL_GROUP = 256
POOL_HIST = 15
HALO = 16
N_HEADS = 8
D_NOPE = 128
D_ROPE = 64
D_V = 128
D_VX = D_V + 16
D_MLA = N_HEADS * D_V
Q_RANK = 512
KV_RANK = 256
D_PLE = 256
D_QK = 256
ROPE_THETA = 10000.0
EPS = 1e-6
ATTN_SCALE = (D_NOPE + D_ROPE) ** -0.5
NEG_INF = -1e30
EXP2_SCALE = ATTN_SCALE * float(np.log2(np.e))
LANES = 128

C_U, C_GP, C_Q, C_KV, C_KR, C_GM = 0, 1024, 2048, 2560, 2816, 2880
D_IN = 3904

VMEM_LIMIT = 56 * 1024 * 1024
PROMPT_TILE = 256
ATTN_BLOCK = 256
OUT_TILE = 512
SAMPLE_BATCH_PER_STEP = 4

f32 = jnp.float32
bf16 = jnp.bfloat16


def _cparams(n_axes):
    return pltpu.CompilerParams(dimension_semantics=("arbitrary",) * n_axes,
                                vmem_limit_bytes=VMEM_LIMIT)


def _resident(shape):
    nd = len(shape)
    return pl.BlockSpec(shape, lambda *_: (0,) * nd, pipeline_mode=pl.Buffered(1))


def _rms(x, n):
    return lax.rsqrt(jnp.sum(x * x, axis=-1, keepdims=True) * (1.0 / n) + EPS)


def _rms_cols(x_t, n):
    return lax.rsqrt(jnp.sum(x_t * x_t, axis=0, keepdims=True) * (1.0 / n) + EPS)


def _dot(a, b):
    return jnp.dot(a, b, preferred_element_type=f32)


def _dot_nt(a, b):
    return lax.dot_general(a, b, (((1,), (1,)), ((), ())), preferred_element_type=f32)


def _rope128(x, cos2, sin2):
    lane = lax.broadcasted_iota(jnp.int32, x.shape, 1)
    swapped = jnp.where(lane < D_ROPE // 2, pltpu.roll(x, LANES - D_ROPE // 2, 1),
                        pltpu.roll(x, D_ROPE // 2, 1))
    return x * cos2 + swapped * sin2


def _queries_t(ql, wqt_ref, cos_ref, sin_ref, qng_ref, qrg_ref, qt_ref):
    q_t = _dot_nt(wqt_ref[...], ql)
    tm = q_t.shape[1]
    cos, sin = cos_ref[...], sin_ref[...]
    half = D_ROPE // 2
    for h in range(N_HEADS):
        r0 = h * D_QK
        s0 = h * (D_NOPE + D_ROPE)
        qn = q_t[s0:s0 + D_NOPE]
        qn = qn * (_rms_cols(qn, D_NOPE) * EXP2_SCALE) * qng_ref[...]
        qt_ref[0, r0:r0 + D_NOPE, :] = qn.astype(bf16)
        qr = q_t[s0 + D_NOPE:s0 + D_NOPE + D_ROPE]
        qr = qr * (_rms_cols(qr, D_ROPE) * EXP2_SCALE) * qrg_ref[...]
        x1, x2 = qr[:half], qr[half:]
        qt_ref[0, r0 + D_NOPE:r0 + D_NOPE + half, :] = (x1 * cos - x2 * sin).astype(bf16)
        qt_ref[0, r0 + D_NOPE + half:r0 + D_NOPE + D_ROPE, :] = (x2 * cos + x1 * sin).astype(bf16)
        qt_ref[0, r0 + D_NOPE + D_ROPE:r0 + D_QK, :] = jnp.zeros((D_QK - D_NOPE - D_ROPE, tm), bf16)


def _keys_values(c, kr128, wk_ref, wvt_ref, kng_ref, k_ref, vt_ref):
    tm = c.shape[0]
    k = _dot(c, wk_ref[...])
    kr = kr128.astype(bf16)
    for h in range(N_HEADS):
        kn = k[:, h * D_NOPE:(h + 1) * D_NOPE]
        k_ref[:, h * D_QK:h * D_QK + D_NOPE] = (kn * _rms(kn, D_NOPE) * kng_ref[...]).astype(bf16)
        k_ref[:, h * D_QK + D_NOPE:(h + 1) * D_QK] = kr
    v_t = _dot_nt(wvt_ref[...], c).astype(bf16)
    n_blk, tk = vt_ref.shape[1], vt_ref.shape[4]
    row = lax.broadcasted_iota(jnp.int32, (D_VX - D_V, tk), 0)
    ones_row = jnp.where(row == 0, 1.0, 0.0).astype(bf16)
    for j in range(n_blk):
        for h in range(N_HEADS):
            vt_ref[0, j, h, 0:D_V, :] = v_t[h * D_V:(h + 1) * D_V, j * tk:(j + 1) * tk]
            vt_ref[0, j, h, D_V:D_VX, :] = ones_row


def _in_proj_body(*refs, nseg, seg_len, tiles_per_seq, carry, fuse_qkv):
    (x_ref, hist_ref, cos_ref, sin_ref, pos_ref, ng_ref, wt_ref, qg_ref, kvg_ref, krg_ref, wp_ref,
     ps_ref) = refs[:12]
    if fuse_qkv:
        (cost_ref, sint_ref, wqt_ref, qng_ref, qrg_ref, wk_ref, wvt_ref, kng_ref, x0_ref, wo32_ref,
         wg32_ref) = refs[12:23]
        (pact_ref, gm_ref, ckv_ref, kr_ref, npool_ref, qt_ref, k_ref, vt_ref, wo16_ref, wg16_ref,
         ucat_ref, xn_ref, xn_next_ref) = refs[23:]
    else:
        pact_ref, gm_ref, ckv_ref, kr_ref, npool_ref, ql_ref, ucat_ref = refs[12:]
    i = pl.program_id(0)
    tm = nseg * seg_len

    def normed(ref):
        x = ref[...]
        return (x * _rms(x, D_MODEL) * ng_ref[...]).astype(bf16)

    if fuse_qkv:
        @pl.when(i == 0)
        def _():
            xn_next_ref[...] = normed(x0_ref)
        xn_ref[...] = xn_next_ref[...]
        xn = xn_ref[...]
    else:
        xn = normed(x_ref)

    u = _dot_nt(xn, wt_ref[C_U:C_U + D_POOL, :])
    if carry:
        @pl.when(i % tiles_per_seq == 0)
        def _():
            ucat_ref[:, 0:HALO, :] = jnp.zeros((nseg, HALO, D_POOL), f32)
    else:
        ucat_ref[:, 0:HALO, :] = hist_ref[...]
    ucat_ref[:, HALO:HALO + seg_len, :] = u.reshape(nseg, seg_len, D_POOL)

    pos = pos_ref[...]
    gp = _dot_nt(xn, wt_ref[C_GP:C_GP + D_POOL, :])
    if fuse_qkv:
        xn_next_ref[...] = normed(x_ref)
    gate = gp * jax.nn.sigmoid(gp) * ps_ref[...]

    def pool_group(g):
        w = POOL_WINDOWS[g]
        sl = slice(g * POOL_GROUP, (g + 1) * POOL_GROUP)
        acc = ucat_ref[:, HALO:HALO + seg_len, sl]
        for k in range(1, w):
            acc = acc + ucat_ref[:, HALO - k:HALO - k + seg_len, sl]
        acc = acc.reshape(tm, POOL_GROUP)
        inv_cnt = 1.0 / jnp.minimum(pos + 1.0, float(w))
        d = (acc * inv_cnt - u[:, sl]).astype(bf16)
        mixed = _dot(d, wp_ref[g])
        pact_ref[:, sl] = (mixed * gate[:, sl]).astype(bf16)

    cq = _dot_nt(xn, wt_ref[C_Q:C_Q + Q_RANK, :])
    pool_group(0)
    pool_group(1)
    ql = (cq * _rms(cq, Q_RANK) * qg_ref[...]).astype(bf16)
    if fuse_qkv:
        _queries_t(ql, wqt_ref, cost_ref, sint_ref, qng_ref, qrg_ref, qt_ref)
    else:
        ql_ref[...] = ql

    ckv = _dot_nt(xn, wt_ref[C_KV:C_KV + KV_RANK, :])
    pool_group(2)
    ckv = ckv * _rms(ckv, KV_RANK) * kvg_ref[...]
    ckv_ref[...] = ckv

    kr = _dot_nt(xn, wt_ref[C_KR:C_KR + LANES, :])
    gm = _dot_nt(xn, wt_ref[C_GM:C_GM + D_MLA, :])
    pool_group(3)
    npool_ref[...] = ucat_ref[:, seg_len + 1:seg_len + HALO, :]
    if carry:
        ucat_ref[:, 0:HALO, :] = ucat_ref[:, seg_len:seg_len + HALO, :]
    lane = lax.broadcasted_iota(jnp.int32, (tm, LANES), 1)
    kr = jnp.where(lane < D_ROPE, kr, 0.0)
    kr = _rope128(kr * _rms(kr, D_ROPE) * krg_ref[...], cos_ref[...], sin_ref[...])
    kr_ref[...] = kr[:, :D_ROPE]
    gm_ref[...] = (gm * jax.nn.sigmoid(gm)).astype(bf16)
    if fuse_qkv:
        _keys_values(ckv.astype(bf16), kr, wk_ref, wvt_ref, kng_ref, k_ref, vt_ref)
        wo16_ref[...] = wo32_ref[...].astype(bf16)
        wg16_ref[...] = wg32_ref[...].astype(bf16)


def _in_proj(x2d, hist, cos2, sin2, pos, wts, *, nseg, seg_len, n_seq, carry, rope_t=None, kv_block=None):
    n_tok = x2d.shape[0]
    tm = nseg * seg_len
    n_tiles = n_tok // tm
    tiles_per_seq = n_tiles // n_seq if carry else 1
    tab_tiles = cos2.shape[0] // tm
    fuse_qkv = rope_t is not None
    row = lambda i: (i, 0)
    tab = lambda i: (i % tab_tiles, 0)
    if carry:
        hist_spec = pl.BlockSpec((1, HALO, D_POOL), lambda i: (0, 0, 0))
        npool_spec = pl.BlockSpec((1, POOL_HIST, D_POOL), lambda i: (i // tiles_per_seq, 0, 0))
    else:
        hist_spec = pl.BlockSpec((nseg, HALO, D_POOL), lambda i: (i, 0, 0))
        npool_spec = pl.BlockSpec((nseg, POOL_HIST, D_POOL), lambda i: (i, 0, 0))
    x_map = (lambda i: (jnp.minimum(i + 1, n_tiles - 1), 0)) if fuse_qkv else row
    in_specs = [
        pl.BlockSpec((tm, D_MODEL), x_map), hist_spec, pl.BlockSpec((tm, LANES), tab),
        pl.BlockSpec((tm, LANES), tab), pl.BlockSpec((tm, 1), tab), _resident((1, D_MODEL)),
        _resident((D_IN, D_MODEL)), _resident((1, Q_RANK)), _resident((1, KV_RANK)),
        _resident((1, LANES)), _resident((4, POOL_GROUP, POOL_GROUP)), _resident((1, D_POOL)),
    ]
    args = [x2d, hist, cos2, sin2, pos, wts["norm_g"], wts["w_in_t"], wts["q_norm_g"],
            wts["kv_norm_g"], wts["k_rope_g"], wts["w_pool"], wts["pool_scale"]]
    out_specs = [pl.BlockSpec((tm, D_POOL), row), pl.BlockSpec((tm, D_MLA), row),
                 pl.BlockSpec((tm, KV_RANK), row), pl.BlockSpec((tm, D_ROPE), row), npool_spec]
    out_shape = [jax.ShapeDtypeStruct((n_tok, D_POOL), bf16), jax.ShapeDtypeStruct((n_tok, D_MLA), bf16),
                 jax.ShapeDtypeStruct((n_tok, KV_RANK), f32), jax.ShapeDtypeStruct((n_tok, D_ROPE), f32),
                 jax.ShapeDtypeStruct((n_seq, POOL_HIST, D_POOL), f32)]
    if fuse_qkv:
        seq_tab = lambda i: (0, i % tiles_per_seq)
        in_specs += [pl.BlockSpec((D_ROPE // 2, tm), seq_tab), pl.BlockSpec((D_ROPE // 2, tm), seq_tab),
                     _resident((N_HEADS * (D_NOPE + D_ROPE), Q_RANK)), _resident((D_NOPE, 1)),
                     _resident((D_ROPE, 1)),
                     _resident((KV_RANK, N_HEADS * D_NOPE)), _resident((D_MLA, KV_RANK)),
                     _resident((1, D_NOPE)),
                     pl.BlockSpec((tm, D_MODEL), lambda i: (0, 0), pipeline_mode=pl.Buffered(1)),
                     pl.BlockSpec((D_MODEL // n_tiles, D_MODEL), row),
                     pl.BlockSpec((D_MODEL // n_tiles, D_MODEL), row)]
        args += [rope_t[0], rope_t[1], wts["w_uq_t"], wts["q_nope_g_col"], wts["q_rope_g_col"],
                 wts["w_uk"], wts["w_uv_t"], wts["k_nope_g"], x2d, wts["w_out_f32"], wts["w_ple_gate_f32"]]
        out_specs += [
            pl.BlockSpec((1, N_HEADS * D_QK, tm), lambda i: (i // tiles_per_seq, 0, i % tiles_per_seq)),
            pl.BlockSpec((tm, N_HEADS * D_QK), row),
            pl.BlockSpec((1, tm // kv_block, N_HEADS, D_VX, kv_block),
                         lambda i: (i // tiles_per_seq, i % tiles_per_seq, 0, 0, 0))]
        out_shape += [jax.ShapeDtypeStruct((n_seq, N_HEADS * D_QK, tiles_per_seq * tm), bf16),
                      jax.ShapeDtypeStruct((n_tok, N_HEADS * D_QK), bf16),
                      jax.ShapeDtypeStruct((n_seq, tiles_per_seq * tm // kv_block, N_HEADS, D_VX, kv_block),
                                           bf16)]
        out_specs += [pl.BlockSpec((D_MODEL // n_tiles, D_MODEL), row)] * 2
        out_shape += [jax.ShapeDtypeStruct((D_MODEL, D_MODEL), bf16)] * 2
        scratch = [pltpu.VMEM((tm, D_MODEL), bf16)] * 2
    else:
        out_specs.append(pl.BlockSpec((tm, Q_RANK), row))
        out_shape.append(jax.ShapeDtypeStruct((n_tok, Q_RANK), bf16))
        scratch = []
    body = functools.partial(_in_proj_body, nseg=nseg, seg_len=seg_len,
                             tiles_per_seq=tiles_per_seq, carry=carry, fuse_qkv=fuse_qkv)
    return pl.pallas_call(
        body,
        grid=(n_tiles,),
        in_specs=in_specs,
        out_specs=out_specs,
        out_shape=out_shape,
        scratch_shapes=[pltpu.VMEM((nseg, HALO + seg_len, D_POOL), f32)] + scratch,
        compiler_params=_cparams(1),
        name="in_proj",
    )(*args)


def _q_proj_sample_body(ql_ref, wq_ref, cos_ref, sin_ref, qng_ref, qrg_ref, kng_ref, wukt_ref,
                        qa_ref, qr_ref):
    q = _dot(ql_ref[...], wq_ref[...])
    cos2, sin2 = cos_ref[...], sin_ref[...]
    for h in range(N_HEADS):
        qn = q[:, h * D_QK:h * D_QK + D_NOPE]
        qn = qn * _rms(qn, D_NOPE) * qng_ref[...]
        qa_ref[h] = _dot((qn * kng_ref[...]).astype(bf16), wukt_ref[h]).astype(bf16)
        qr = q[:, h * D_QK + D_NOPE:(h + 1) * D_QK]
        qr = _rope128(qr * _rms(qr, D_ROPE) * qrg_ref[...], cos2, sin2)
        qr_ref[h] = qr[:, :D_ROPE].astype(bf16)


def _q_proj_sample(qlat, cos2, sin2, wts):
    n_tok = qlat.shape[0]
    return pl.pallas_call(
        _q_proj_sample_body,
        grid=(1,),
        in_specs=[_resident((n_tok, Q_RANK)), _resident((Q_RANK, N_HEADS * D_QK)),
                  _resident((n_tok, LANES)), _resident((n_tok, LANES)),
                  _resident((1, D_NOPE)), _resident((1, LANES)), _resident((1, D_NOPE)),
                  _resident((N_HEADS, D_NOPE, KV_RANK))],
        out_specs=[_resident((N_HEADS, n_tok, KV_RANK)), _resident((N_HEADS, n_tok, D_ROPE))],
        out_shape=[jax.ShapeDtypeStruct((N_HEADS, n_tok, KV_RANK), bf16),
                   jax.ShapeDtypeStruct((N_HEADS, n_tok, D_ROPE), bf16)],
        compiler_params=_cparams(1),
        name="q_proj_sample",
    )(qlat, wts["w_uq"], cos2, sin2, wts["q_nope_g"], wts["q_rope_g"], wts["k_nope_g"],
      wts["w_uk_t"])


def _attn_prompt_body(qt_ref, k_ref, vt_ref, gm_ref, o_ref, m_ref, acc_ref, s_ref, mx_ref, *, tq, tk):
    qi = pl.program_id(1)
    m_ref[...] = jnp.full(m_ref.shape, NEG_INF, f32)
    acc_ref[...] = jnp.zeros(acc_ref.shape, f32)

    def all_sublanes_max(x):
        for shift in (4, 2, 1):
            x = jnp.maximum(x, pltpu.roll(x, shift, 0))
        return x

    def scores(h, j, buf, masked):
        row0 = pl.multiple_of(j * tk, tk)
        kb = k_ref[pl.ds(row0, tk), h * D_QK:(h + 1) * D_QK]
        s = _dot(kb, qt_ref[0, h * D_QK:(h + 1) * D_QK, :])
        if masked:
            k_chunk = lax.broadcasted_iota(jnp.int32, (tk, tq), 0) // CHUNK
            q_chunk = lax.broadcasted_iota(jnp.int32, (tk, tq), 1) // CHUNK
            s = jnp.where(k_chunk <= q_chunk, s, NEG_INF)
        s_ref[buf, h] = s
        mx_ref[buf, h] = all_sublanes_max(jnp.max(s.reshape(tk // 8, 8, tq), axis=0))

    def values(h, j, buf):
        m_old = m_ref[h]
        m_new = jnp.maximum(m_old, mx_ref[buf, h])
        alpha = jnp.exp2(m_old - m_new)
        p = jnp.exp2(s_ref[buf, h].reshape(tk // 8, 8, tq) - m_new[None]).reshape(tk, tq)
        m_ref[h] = m_new
        acc = acc_ref[h].reshape(D_VX // 8, 8, tq) * alpha[None]
        acc_ref[h] = acc.reshape(D_VX, tq) + _dot(vt_ref[0, j, h], p.astype(bf16))

    for h in range(N_HEADS):
        scores(h, qi, 0, True)

    def step(t, buf):
        prev = jnp.where(t == 1, qi, t - 2)
        for h in range(N_HEADS):
            scores(h, t - 1, buf, False)
            values(h, prev, 1 - buf)

    def step_pair(u, carry):
        step(2 * u + 1, 1)
        step(2 * u + 2, 0)
        return carry

    lax.fori_loop(0, qi // 2, step_pair, 0)

    @pl.when(qi % 2 == 1)
    def _():
        step(qi, 1)
        for h in range(N_HEADS):
            values(h, qi - 1, 1)

    @pl.when(qi % 2 == 0)
    def _():
        last = jnp.where(qi == 0, qi, qi - 1)
        for h in range(N_HEADS):
            values(h, last, 0)

    for h in range(N_HEADS):
        hs = slice(h * D_V, (h + 1) * D_V)
        o = (acc_ref[h, 0:D_V, :] / acc_ref[h, D_V:D_V + 1, :]).T
        o_ref[:, hs] = (o * gm_ref[:, hs].astype(f32)).astype(bf16)


def _attn_prompt(qt, k, vt, gm, *, n_seq, seq_len, tq):
    tk = tq
    nq = seq_len // tq
    body = functools.partial(_attn_prompt_body, tq=tq, tk=tk)
    return pl.pallas_call(
        body,
        grid=(n_seq, nq),
        in_specs=[pl.BlockSpec((1, N_HEADS * D_QK, tq), lambda b, i: (b, 0, i)),
                  pl.BlockSpec((seq_len, N_HEADS * D_QK), lambda b, i: (b, 0)),
                  pl.BlockSpec((1, seq_len // tk, N_HEADS, D_VX, tk), lambda b, i: (b, 0, 0, 0, 0)),
                  pl.BlockSpec((tq, D_MLA), lambda b, i: (b * nq + i, 0))],
        out_specs=pl.BlockSpec((tq, D_MLA), lambda b, i: (b * nq + i, 0)),
        out_shape=jax.ShapeDtypeStruct((n_seq * seq_len, D_MLA), bf16),
        scratch_shapes=[pltpu.VMEM((N_HEADS, 8, tq), f32), pltpu.VMEM((N_HEADS, D_VX, tq), f32),
                        pltpu.VMEM((2, N_HEADS, tk, tq), f32), pltpu.VMEM((2, N_HEADS, 8, tq), f32)],
        compiler_params=_cparams(2),
        name="attn_prompt",
    )(qt, k, vt, gm)


def _attn_sample_body(cckv_ref, ckr_t_ref, nckv_ref, nkr_ref, qa_ref, qr_ref, gm_ref, wukt_ref, wuv_ref,
                      o_ref, c_ref, ktail_ref, s_ref, *, past, t_new, chunks, n_b):
    n_tail = c_ref.shape[1] - past
    rows = N_HEADS * t_new
    c_ref[:, past:past + t_new, :] = nckv_ref[...].astype(bf16)
    c_ref[:, past + t_new:, :] = jnp.zeros((n_b, n_tail - t_new, KV_RANK), bf16)
    ktail_ref[:, 0:t_new, :] = nkr_ref[...].astype(bf16)
    ktail_ref[:, t_new:, :] = jnp.zeros((n_b, n_tail - t_new, D_ROPE), bf16)

    def chunk_scores(b, start, size):
        cached = start < past
        if cached:
            c_ref[b, start:start + size, :] = cckv_ref[b, start:start + size, :].astype(bf16)
        c = c_ref[b, start:start + size, :]
        k_t = _dot_nt(wukt_ref[...], c)
        ssq = jnp.sum((k_t * k_t).reshape(N_HEADS, D_NOPE, size), axis=1)
        r = lax.rsqrt(ssq * (1.0 / D_NOPE) + EPS)
        qa = qa_ref[:, b * t_new:(b + 1) * t_new, :].reshape(rows, KV_RANK)
        qr = qr_ref[:, b * t_new:(b + 1) * t_new, :].reshape(rows, D_ROPE)
        s_nope = _dot_nt(qa, c)
        if cached:
            s_rope = _dot(qr, ckr_t_ref[b, :, start:start + size].astype(bf16))
        else:
            s_rope = _dot_nt(qr, ktail_ref[b])
        for h in range(N_HEADS):
            hs = slice(h * t_new, (h + 1) * t_new)
            s = (s_nope[hs] * r[h:h + 1, :] + s_rope[hs]) * ATTN_SCALE
            if not cached:
                key = lax.broadcasted_iota(jnp.int32, (t_new, size), 1)
                s = jnp.where(key < t_new, s, NEG_INF)
            s_ref[b, hs, start:start + size] = s

    def chunk_values(b, start, size, state):
        m, l, acc = state
        s = s_ref[b, :, start:start + size]
        m_new = jnp.maximum(m, jnp.max(s, axis=-1, keepdims=True))
        alpha = jnp.exp(m - m_new)
        p = jnp.exp(s - m_new)
        l = alpha * l + jnp.sum(p, axis=-1, keepdims=True)
        acc = alpha * acc + _dot(p.astype(bf16), c_ref[b, start:start + size, :])
        return m_new, l, acc

    def finish(b, state):
        _, l, acc = state
        o_lat = (acc / l).astype(bf16)
        for h in range(N_HEADS):
            o = _dot(o_lat[h * t_new:(h + 1) * t_new], wuv_ref[h])
            hs = slice(h * D_V, (h + 1) * D_V)
            bs = slice(b * t_new, (b + 1) * t_new)
            o_ref[bs, hs] = (o * gm_ref[bs, hs].astype(f32)).astype(bf16)

    items = [(b,) + ch for b in range(n_b) for ch in chunks]
    init = (jnp.full((rows, 1), NEG_INF, f32), jnp.zeros((rows, 1), f32),
            jnp.zeros((rows, KV_RANK), f32))
    state = init
    chunk_scores(*items[0])
    for nxt, cur in zip(items[1:] + [None], items):
        if nxt is not None:
            chunk_scores(*nxt)
        state = chunk_values(*cur, state)
        if nxt is None or nxt[0] != cur[0]:
            finish(cur[0], state)
            state = init


def _attn_sample(cache_ckv, cache_krope_t, ckv_new, krope_new, q_abs, q_rope, gm, wts, *, t_new, n_b):
    n_seq, past, _ = cache_ckv.shape
    chunk = 512
    chunks = tuple((s, min(chunk, past - s)) for s in range(0, past, chunk)) + ((past, LANES),)
    s_pad = past + LANES
    body = functools.partial(_attn_sample_body, past=past, t_new=t_new, chunks=chunks, n_b=n_b)
    return pl.pallas_call(
        body,
        grid=(n_seq // n_b,),
        in_specs=[pl.BlockSpec((n_b, past, KV_RANK), lambda b: (b, 0, 0)),
                  pl.BlockSpec((n_b, D_ROPE, past), lambda b: (b, 0, 0)),
                  pl.BlockSpec((n_b, t_new, KV_RANK), lambda b: (b, 0, 0)),
                  pl.BlockSpec((n_b, t_new, D_ROPE), lambda b: (b, 0, 0)),
                  pl.BlockSpec((N_HEADS, n_b * t_new, KV_RANK), lambda b: (0, b, 0)),
                  pl.BlockSpec((N_HEADS, n_b * t_new, D_ROPE), lambda b: (0, b, 0)),
                  pl.BlockSpec((n_b * t_new, D_MLA), lambda b: (b, 0)),
                  _resident((N_HEADS * D_NOPE, KV_RANK)),
                  _resident((N_HEADS, KV_RANK, D_V))],
        out_specs=pl.BlockSpec((n_b * t_new, D_MLA), lambda b: (b, 0)),
        out_shape=jax.ShapeDtypeStruct((n_seq * t_new, D_MLA), bf16),
        scratch_shapes=[pltpu.VMEM((n_b, s_pad, KV_RANK), bf16),
                        pltpu.VMEM((n_b, s_pad - past, D_ROPE), bf16),
                        pltpu.VMEM((n_b, N_HEADS * t_new, s_pad), f32)],
        compiler_params=_cparams(1),
        name="attn_sample",
    )(cache_ckv, cache_krope_t, ckv_new, krope_new, q_abs, q_rope, gm,
      wts["w_uk_t"].reshape(N_HEADS * D_NOPE, KV_RANK), wts["w_uv3"])


def _out_proj_body(x_ref, pa_ref, ma_ref, p_ref, wo_ref, png_ref, wg_ref, bg_ref, wple_ref, y_ref):
    h = x_ref[...] + _dot(pa_ref[...], wo_ref[0:D_POOL, :]) + _dot(ma_ref[...], wo_ref[D_POOL:, :])
    hn = (h * _rms(h, D_MODEL) * png_ref[...]).astype(bf16)
    gate = jax.nn.sigmoid(_dot(hn, wg_ref[...]) + bg_ref[...])
    y_ref[...] = h + gate * _dot(p_ref[...].astype(bf16), wple_ref[...])


def _out_proj(x2d, pool_act, mla_act, p2d, wts, *, tm):
    n_tok = x2d.shape[0]
    row = lambda i: (i, 0)
    return pl.pallas_call(
        _out_proj_body,
        grid=(n_tok // tm,),
        in_specs=[pl.BlockSpec((tm, D_MODEL), row), pl.BlockSpec((tm, D_POOL), row),
                  pl.BlockSpec((tm, D_MLA), row), pl.BlockSpec((tm, D_PLE), row),
                  _resident((D_MODEL, D_MODEL)), _resident((1, D_MODEL)),
                  _resident((D_MODEL, D_MODEL)), _resident((1, D_MODEL)),
                  _resident((D_PLE, D_MODEL))],
        out_specs=pl.BlockSpec((tm, D_MODEL), row),
        out_shape=jax.ShapeDtypeStruct((n_tok, D_MODEL), f32),
        compiler_params=_cparams(1),
        name="out_proj",
    )(x2d, pool_act, mla_act, p2d, wts["w_out"], wts["ple_norm_g"], wts["w_ple_gate"],
      wts["b_ple_gate"], wts["w_ple"])


def _rope_tables(pos0, t, reps=1):
    pos = (pos0 + np.arange(t)).astype(np.float64)
    inv = ROPE_THETA ** (-(np.arange(0, D_ROPE, 2, dtype=np.float64) / D_ROPE))
    ang = pos[:, None] * inv[None, :]
    cos, sin = np.cos(ang), np.sin(ang)
    zero = np.zeros((t, LANES - D_ROPE))
    rows = (np.concatenate([cos, cos, zero], axis=-1), np.concatenate([-sin, sin, zero], axis=-1),
            pos[:, None])
    as_f32 = lambda a: jnp.asarray(a.astype(np.float32))
    return tuple(as_f32(np.tile(a, (reps, 1))) for a in rows) + ((as_f32(cos.T), as_f32(sin.T)),)


def _prep_weights(norm_g, w_in, q_norm_g, w_uq, kv_norm_g, w_ukv, q_nope_g, q_rope_g, k_nope_g,
                  k_rope_g, w_pool, pool_scale, w_out, ple_norm_g, w_ple_gate, b_ple_gate, w_ple):
    w_uq_r = jnp.pad(w_uq.reshape(Q_RANK, N_HEADS, D_NOPE + D_ROPE),
                     ((0, 0), (0, 0), (0, D_QK - D_NOPE - D_ROPE))).reshape(Q_RANK, N_HEADS * D_QK)
    w_ukv3 = w_ukv.reshape(KV_RANK, N_HEADS, D_NOPE + D_V)
    w_uk3, w_uv3 = w_ukv3[..., :D_NOPE], w_ukv3[..., D_NOPE:]
    pad_rope = lambda g: jnp.pad(g, (0, LANES - D_ROPE))[None, :]
    return {
        "norm_g": norm_g[None, :], "w_in_t": w_in.T.astype(bf16), "q_norm_g": q_norm_g[None, :],
        "kv_norm_g": kv_norm_g[None, :], "k_rope_g": pad_rope(k_rope_g),
        "w_pool": w_pool.astype(bf16), "pool_scale": pool_scale[None, :],
        "w_uq": w_uq_r.astype(bf16), "q_nope_g": q_nope_g[None, :], "q_rope_g": pad_rope(q_rope_g),
        "w_uq_t": w_uq.T.astype(bf16),
        "q_nope_g_col": q_nope_g[:, None], "q_rope_g_col": q_rope_g[:, None],
        "k_nope_g": k_nope_g[None, :],
        "w_uk": w_uk3.reshape(KV_RANK, N_HEADS * D_NOPE).astype(bf16),
        "w_uv_t": w_uv3.reshape(KV_RANK, D_MLA).T.astype(bf16),
        "w_uk_t": jnp.transpose(w_uk3, (1, 2, 0)).astype(bf16),
        "w_uv3": jnp.transpose(w_uv3, (1, 0, 2)).astype(bf16),
        "w_out_f32": w_out, "w_ple_gate_f32": w_ple_gate,
        "ple_norm_g": ple_norm_g[None, :], "b_ple_gate": b_ple_gate[None, :],
        "w_ple": w_ple.astype(bf16),
    }


def _layer_prompt(x, p, wts):
    n_seq, seq_len, _ = x.shape
    x2d = x.reshape(n_seq * seq_len, D_MODEL)
    cos2, sin2, pos, rope_t = _rope_tables(0, seq_len)
    hist = jnp.zeros((1, HALO, D_POOL), f32)
    pact, gm, ckv, krope, npool, qt, k, vt, w_out16, w_gate16 = _in_proj(
        x2d, hist, cos2, sin2, pos, wts, nseg=1, seg_len=PROMPT_TILE, n_seq=n_seq, carry=True,
        rope_t=rope_t, kv_block=ATTN_BLOCK)
    wts["w_out"], wts["w_ple_gate"] = w_out16, w_gate16
    mact = _attn_prompt(qt, k, vt, gm, n_seq=n_seq, seq_len=seq_len, tq=ATTN_BLOCK)
    y = _out_proj(x2d, pact, mact, p.reshape(n_seq * seq_len, D_PLE), wts, tm=OUT_TILE)
    return (y.reshape(x.shape), ckv.reshape(n_seq, seq_len, KV_RANK),
            krope.reshape(n_seq, seq_len, D_ROPE), npool)


def _layer_sample(x, p, state_pool, cache_ckv, cache_krope, wts):
    n_seq, t_new, _ = x.shape
    past = cache_ckv.shape[1]
    n_tok = n_seq * t_new
    x2d = x.reshape(n_tok, D_MODEL)
    cos2, sin2, pos, _ = _rope_tables(past, t_new, reps=n_seq)
    hist = jnp.pad(state_pool, ((0, 0), (HALO - POOL_HIST, 0), (0, 0)))
    pact, gm, ckv, krope, npool, qlat = _in_proj(
        x2d, hist, cos2, sin2, pos, wts, nseg=n_seq, seg_len=t_new, n_seq=n_seq, carry=False)
    q_abs, q_rope = _q_proj_sample(qlat, cos2, sin2, wts)
    ckv3 = ckv.reshape(n_seq, t_new, KV_RANK)
    krope3 = krope.reshape(n_seq, t_new, D_ROPE)
    cache_krope_t = jnp.transpose(cache_krope, (0, 2, 1))
    mact = _attn_sample(cache_ckv, cache_krope_t, ckv3, krope3, q_abs, q_rope, gm, wts, t_new=t_new,
                        n_b=SAMPLE_BATCH_PER_STEP)
    y = _out_proj(x2d, pact, mact, p.reshape(n_tok, D_PLE), wts, tm=OUT_TILE)
    return y.reshape(x.shape), ckv3, krope3, npool


def kernel(x_prompt, x_sample, cache_ckv, cache_krope, state_pool, p_prompt, p_sample, norm_g, w_in,
           q_norm_g, w_uq, kv_norm_g, w_ukv, q_nope_g, q_rope_g, k_nope_g, k_rope_g, w_pool,
           pool_scale, w_out, ple_norm_g, w_ple_gate, b_ple_gate, w_ple):
    depth = norm_g.shape[0]
    layer_w = (norm_g, w_in, q_norm_g, w_uq, kv_norm_g, w_ukv, q_nope_g, q_rope_g, k_nope_g, k_rope_g,
               w_pool, pool_scale, w_out, ple_norm_g, w_ple_gate, b_ple_gate, w_ple)
    yp, ys = x_prompt, x_sample
    outs = [[] for _ in range(6)]
    for i in range(depth):
        wts = _prep_weights(*(w[i] for w in layer_w))
        yp, c1, k1, s1 = _layer_prompt(yp, p_prompt[i], wts)
        ys, c2, k2, s2 = _layer_sample(ys, p_sample[i], state_pool[i], cache_ckv[i], cache_krope[i], wts)
        for lst, val in zip(outs, (c1, k1, s1, c2, k2, s2)):
            lst.append(val)
    return (yp, ys) + tuple(jnp.stack(o) for o in outs)
```

```python
import functools

import jax
import jax.numpy as jnp
import numpy as np
from jax import lax
from jax.experimental import pallas as pl
from jax.experimental.pallas import tpu as pltpu

D_MODEL = 2048
CHUNK = 64
D_POOL = 1024
POOL_WINDOWS = (2, 4, 8, 16)
POOL_GROUP = 256
POOL_HIST = 15
HALO = 16
N_HEADS = 8
D_NOPE = 128
D_ROPE = 64
D_V = 128
D_VX = D_V + 16
D_MLA = N_HEADS * D_V
Q_RANK = 512
KV_RANK = 256
D_PLE = 256
D_QK = 256
ROPE_THETA = 10000.0
EPS = 1e-6
ATTN_SCALE = (D_NOPE + D_ROPE) ** -0.5
NEG_INF = -1e30
EXP2_SCALE = ATTN_SCALE * float(np.log2(np.e))
LANES = 128

C_U, C_GP, C_Q, C_KV, C_KR, C_GM = 0, 1024, 2048, 2560, 2816, 2880
D_IN = 3904

VMEM_LIMIT = 56 * 1024 * 1024
PROMPT_TILE = 256
W_CHUNK = 128
ATTN_BLOCK = 256
OUT_TILE = 512
SAMPLE_BATCH_PER_STEP = 4

f32 = jnp.float32
bf16 = jnp.bfloat16


def _cparams(n_axes):
    return pltpu.CompilerParams(dimension_semantics=("arbitrary",) * n_axes,
                                vmem_limit_bytes=VMEM_LIMIT)


def _resident(shape):
    nd = len(shape)
    return pl.BlockSpec(shape, lambda *_: (0,) * nd, pipeline_mode=pl.Buffered(1))


def _rms(x, n):
    return lax.rsqrt(jnp.sum(x * x, axis=-1, keepdims=True) * (1.0 / n) + EPS)


def _rms_cols(x_t, n):
    return lax.rsqrt(jnp.sum(x_t * x_t, axis=0, keepdims=True) * (1.0 / n) + EPS)


def _dot(a, b):
    return jnp.dot(a, b, preferred_element_type=f32)


def _dot_nt(a, b):
    return lax.dot_general(a, b, (((1,), (1,)), ((), ())), preferred_element_type=f32)


def _rope128(x, cos2, sin2):
    lane = lax.broadcasted_iota(jnp.int32, x.shape, 1)
    swapped = jnp.where(lane < D_ROPE // 2, pltpu.roll(x, LANES - D_ROPE // 2, 1),
                        pltpu.roll(x, D_ROPE // 2, 1))
    return x * cos2 + swapped * sin2


def _queries_t(ql, wqt_ref, cos_ref, sin_ref, qng_ref, qrg_ref, qt_ref):
    q_t = _dot_nt(wqt_ref[...], ql)
    tm = q_t.shape[1]
    cos, sin = cos_ref[...], sin_ref[...]
    half = D_ROPE // 2
    for h in range(N_HEADS):
        r0 = h * D_QK
        s0 = h * (D_NOPE + D_ROPE)
        qn = q_t[s0:s0 + D_NOPE]
        qn = qn * (_rms_cols(qn, D_NOPE) * EXP2_SCALE) * qng_ref[...]
        qt_ref[0, r0:r0 + D_NOPE, :] = qn.astype(bf16)
        qr = q_t[s0 + D_NOPE:s0 + D_NOPE + D_ROPE]
        qr = qr * (_rms_cols(qr, D_ROPE) * EXP2_SCALE) * qrg_ref[...]
        x1, x2 = qr[:half], qr[half:]
        qt_ref[0, r0 + D_NOPE:r0 + D_NOPE + half, :] = (x1 * cos - x2 * sin).astype(bf16)
        qt_ref[0, r0 + D_NOPE + half:r0 + D_NOPE + D_ROPE, :] = (x2 * cos + x1 * sin).astype(bf16)
        qt_ref[0, r0 + D_NOPE + D_ROPE:r0 + D_QK, :] = jnp.zeros((D_QK - D_NOPE - D_ROPE, tm), bf16)


def _keys_values(c, kr128, wk_ref, wvt_ref, kng_ref, k_ref, vt_ref):
    tm = c.shape[0]
    k = _dot(c, wk_ref[...])
    kr = kr128.astype(bf16)
    for h in range(N_HEADS):
        kn = k[:, h * D_NOPE:(h + 1) * D_NOPE]
        k_ref[:, h * D_QK:h * D_QK + D_NOPE] = (kn * _rms(kn, D_NOPE) * kng_ref[...]).astype(bf16)
        k_ref[:, h * D_QK + D_NOPE:(h + 1) * D_QK] = kr
    v_t = _dot_nt(wvt_ref[...], c).astype(bf16)
    n_blk, tk = vt_ref.shape[1], vt_ref.shape[4]
    row = lax.broadcasted_iota(jnp.int32, (D_VX - D_V, tk), 0)
    ones_row = jnp.where(row == 0, 1.0, 0.0).astype(bf16)
    for j in range(n_blk):
        for h in range(N_HEADS):
            vt_ref[0, j, h, 0:D_V, :] = v_t[h * D_V:(h + 1) * D_V, j * tk:(j + 1) * tk]
            vt_ref[0, j, h, D_V:D_VX, :] = ones_row


def _in_proj_body(*refs, n_conv, **static):
    if not static["fuse_qkv"]:
        _in_proj_tile(pl.program_id(0), refs, **static)
        return
    i = pl.program_id(0)
    w32_ref, w16_ref, wt_ref = refs[6], refs[-5], refs[-1]

    def convert(rows):
        chunk = w32_ref[0:rows, :].astype(bf16)
        w16_ref[0:rows, :] = chunk
        wt_ref[pl.ds(pl.multiple_of(i * W_CHUNK, W_CHUNK), rows), :] = chunk

    last_rows = D_IN - (n_conv - 1) * W_CHUNK
    pl.when(i < n_conv - 1)(lambda: convert(W_CHUNK))
    pl.when(i == n_conv - 1)(lambda: convert(last_rows))

    @pl.when(i >= n_conv)
    def _():
        _in_proj_tile(i - n_conv, refs[:6] + (wt_ref,) + refs[7:-5] + refs[-4:-1], **static)


def _in_proj_tile(i, refs, *, nseg, seg_len, tiles_per_seq, carry, fuse_qkv):
    (x_ref, hist_ref, cos_ref, sin_ref, pos_ref, ng_ref, wt_ref, qg_ref, kvg_ref, krg_ref, wp_ref,
     ps_ref) = refs[:12]
    if fuse_qkv:
        (cost_ref, sint_ref, wqt_ref, qng_ref, qrg_ref, wk_ref, wvt_ref, kng_ref, x0_ref, wo32_ref,
         wg32_ref) = refs[12:23]
        (pact_ref, gm_ref, ckv_ref, kr_ref, npool_ref, qt_ref, k_ref, vt_ref, wo16_ref, wg16_ref,
         ucat_ref, xn_ref, xn_next_ref) = refs[23:]
    else:
        pact_ref, gm_ref, ckv_ref, kr_ref, npool_ref, ql_ref, ucat_ref = refs[12:]
    tm = nseg * seg_len

    def normed(ref):
        x = ref[...]
        return (x * _rms(x, D_MODEL) * ng_ref[...]).astype(bf16)

    if fuse_qkv:
        @pl.when(i == 0)
        def _():
            xn_next_ref[...] = normed(x0_ref)
        xn_ref[...] = xn_next_ref[...]
        xn = xn_ref[...]
    else:
        xn = normed(x_ref)

    u = _dot_nt(xn, wt_ref[C_U:C_U + D_POOL, :])
    if carry:
        @pl.when(i % tiles_per_seq == 0)
        def _():
            ucat_ref[:, 0:HALO, :] = jnp.zeros((nseg, HALO, D_POOL), f32)
    else:
        ucat_ref[:, 0:HALO, :] = hist_ref[...]
    ucat_ref[:, HALO:HALO + seg_len, :] = u.reshape(nseg, seg_len, D_POOL)

    pos = pos_ref[...]
    gp = _dot_nt(xn, wt_ref[C_GP:C_GP + D_POOL, :])
    if fuse_qkv:
        xn_next_ref[...] = normed(x_ref)
    gate = gp * jax.nn.sigmoid(gp) * ps_ref[...]

    def pool_group(g):
        w = POOL_WINDOWS[g]
        sl = slice(g * POOL_GROUP, (g + 1) * POOL_GROUP)
        acc = ucat_ref[:, :, sl]
        shift = 1
        while shift < w:
            acc = acc + pltpu.roll(acc, shift, 1)
            shift *= 2
        acc = acc[:, HALO:, :].reshape(tm, POOL_GROUP)
        inv_cnt = 1.0 / jnp.minimum(pos + 1.0, float(w))
        d = (acc * inv_cnt - u[:, sl]).astype(bf16)
        mixed = _dot(d, wp_ref[g])
        pact_ref[:, sl] = (mixed * gate[:, sl]).astype(bf16)

    cq = _dot_nt(xn, wt_ref[C_Q:C_Q + Q_RANK, :])
    pool_group(0)
    pool_group(1)
    ql = (cq * _rms(cq, Q_RANK) * qg_ref[...]).astype(bf16)
    if fuse_qkv:
        _queries_t(ql, wqt_ref, cost_ref, sint_ref, qng_ref, qrg_ref, qt_ref)
    else:
        ql_ref[...] = ql

    ckv = _dot_nt(xn, wt_ref[C_KV:C_KV + KV_RANK, :])
    pool_group(2)
    ckv = ckv * _rms(ckv, KV_RANK) * kvg_ref[...]
    ckv_ref[...] = ckv

    kr = _dot_nt(xn, wt_ref[C_KR:C_KR + LANES, :])
    gm = _dot_nt(xn, wt_ref[C_GM:C_GM + D_MLA, :])
    pool_group(3)
    npool_ref[...] = ucat_ref[:, seg_len + 1:seg_len + HALO, :]
    if carry:
        ucat_ref[:, 0:HALO, :] = ucat_ref[:, seg_len:seg_len + HALO, :]
    lane = lax.broadcasted_iota(jnp.int32, (tm, LANES), 1)
    kr = jnp.where(lane < D_ROPE, kr, 0.0)
    kr = _rope128(kr * _rms(kr, D_ROPE) * krg_ref[...], cos_ref[...], sin_ref[...])
    kr_ref[...] = kr[:, :D_ROPE]
    gm_ref[...] = (gm * jax.nn.sigmoid(gm)).astype(bf16)
    if fuse_qkv:
        _keys_values(ckv.astype(bf16), kr, wk_ref, wvt_ref, kng_ref, k_ref, vt_ref)
        wo16_ref[...] = wo32_ref[...].astype(bf16)
        wg16_ref[...] = wg32_ref[...].astype(bf16)


def _in_proj(x2d, hist, cos2, sin2, pos, wts, *, nseg, seg_len, n_seq, carry, rope_t=None, kv_block=None):
    n_tok = x2d.shape[0]
    tm = nseg * seg_len
    n_tiles = n_tok // tm
    tiles_per_seq = n_tiles // n_seq if carry else 1
    tab_tiles = cos2.shape[0] // tm
    fuse_qkv = rope_t is not None
    n_conv = pl.cdiv(D_IN, W_CHUNK) if fuse_qkv else 0
    tile = (lambda i: jnp.maximum(i - n_conv, 0)) if n_conv else (lambda i: i)
    row = lambda i: (tile(i), 0)
    tab = lambda i: (tile(i) % tab_tiles, 0)
    if carry:
        hist_spec = pl.BlockSpec((1, HALO, D_POOL), lambda i: (0, 0, 0))
        npool_spec = pl.BlockSpec((1, POOL_HIST, D_POOL), lambda i: (tile(i) // tiles_per_seq, 0, 0))
    else:
        hist_spec = pl.BlockSpec((nseg, HALO, D_POOL), lambda i: (tile(i), 0, 0))
        npool_spec = pl.BlockSpec((nseg, POOL_HIST, D_POOL), lambda i: (tile(i), 0, 0))
    if fuse_qkv:
        x_spec = pl.BlockSpec((tm, D_MODEL), lambda i: (jnp.minimum(tile(i) + 1, n_tiles - 1), 0))
        w_chunk = lambda i: (jnp.minimum(i, n_conv - 1), 0)
        w_spec, w_arg = pl.BlockSpec((W_CHUNK, D_MODEL), w_chunk), wts["w_in_t_f32"]
    else:
        x_spec = pl.BlockSpec((tm, D_MODEL), row)
        w_spec, w_arg = _resident((D_IN, D_MODEL)), wts["w_in_t"]
    in_specs = [
        x_spec, hist_spec, pl.BlockSpec((tm, LANES), tab),
        pl.BlockSpec((tm, LANES), tab), pl.BlockSpec((tm, 1), tab), _resident((1, D_MODEL)),
        w_spec, _resident((1, Q_RANK)), _resident((1, KV_RANK)),
        _resident((1, LANES)), _resident((4, POOL_GROUP, POOL_GROUP)), _resident((1, D_POOL)),
    ]
    args = [x2d, hist, cos2, sin2, pos, wts["norm_g"], w_arg, wts["q_norm_g"],
            wts["kv_norm_g"], wts["k_rope_g"], wts["w_pool"], wts["pool_scale"]]
    out_specs = [pl.BlockSpec((tm, D_POOL), row), pl.BlockSpec((tm, D_MLA), row),
                 pl.BlockSpec((tm, KV_RANK), row), pl.BlockSpec((tm, D_ROPE), row), npool_spec]
    out_shape = [jax.ShapeDtypeStruct((n_tok, D_POOL), bf16), jax.ShapeDtypeStruct((n_tok, D_MLA), bf16),
                 jax.ShapeDtypeStruct((n_tok, KV_RANK), f32), jax.ShapeDtypeStruct((n_tok, D_ROPE), f32),
                 jax.ShapeDtypeStruct((n_seq, POOL_HIST, D_POOL), f32)]
    scratch = [pltpu.VMEM((nseg, HALO + seg_len, D_POOL), f32)]
    if fuse_qkv:
        seq_tab = lambda i: (0, tile(i) % tiles_per_seq)
        seq_blk = lambda i: (tile(i) // tiles_per_seq, 0, tile(i) % tiles_per_seq)
        in_specs += [pl.BlockSpec((D_ROPE // 2, tm), seq_tab), pl.BlockSpec((D_ROPE // 2, tm), seq_tab),
                     _resident((N_HEADS * (D_NOPE + D_ROPE), Q_RANK)), _resident((D_NOPE, 1)),
                     _resident((D_ROPE, 1)),
                     _resident((KV_RANK, N_HEADS * D_NOPE)), _resident((D_MLA, KV_RANK)),
                     _resident((1, D_NOPE)),
                     pl.BlockSpec((tm, D_MODEL), lambda i: (0, 0), pipeline_mode=pl.Buffered(1)),
                     pl.BlockSpec((D_MODEL // n_tiles, D_MODEL), row),
                     pl.BlockSpec((D_MODEL // n_tiles, D_MODEL), row)]
        args += [rope_t[0], rope_t[1], wts["w_uq_t"], wts["q_nope_g_col"], wts["q_rope_g_col"],
                 wts["w_uk"], wts["w_uv_t"], wts["k_nope_g"], x2d, wts["w_out_f32"], wts["w_ple_gate_f32"]]
        out_specs += [
            pl.BlockSpec((1, N_HEADS * D_QK, tm), seq_blk),
            pl.BlockSpec((tm, N_HEADS * D_QK), row),
            pl.BlockSpec((1, tm // kv_block, N_HEADS, D_VX, kv_block), lambda i: seq_blk(i)[::2] + (0, 0, 0)),
            pl.BlockSpec((D_MODEL // n_tiles, D_MODEL), row),
            pl.BlockSpec((D_MODEL // n_tiles, D_MODEL), row),
            pl.BlockSpec((W_CHUNK, D_MODEL), w_chunk)]
        out_shape += [jax.ShapeDtypeStruct((n_seq, N_HEADS * D_QK, tiles_per_seq * tm), bf16),
                      jax.ShapeDtypeStruct((n_tok, N_HEADS * D_QK), bf16),
                      jax.ShapeDtypeStruct((n_seq, tiles_per_seq * tm // kv_block, N_HEADS, D_VX, kv_block),
                                           bf16),
                      jax.ShapeDtypeStruct((D_MODEL, D_MODEL), bf16),
                      jax.ShapeDtypeStruct((D_MODEL, D_MODEL), bf16),
                      jax.ShapeDtypeStruct((D_IN, D_MODEL), bf16)]
        scratch += [pltpu.VMEM((tm, D_MODEL), bf16), pltpu.VMEM((tm, D_MODEL), bf16),
                    pltpu.VMEM((n_conv * W_CHUNK, D_MODEL), bf16)]
    else:
        out_specs.append(pl.BlockSpec((tm, Q_RANK), row))
        out_shape.append(jax.ShapeDtypeStruct((n_tok, Q_RANK), bf16))
    body = functools.partial(_in_proj_body, n_conv=n_conv, nseg=nseg, seg_len=seg_len,
                             tiles_per_seq=tiles_per_seq, carry=carry, fuse_qkv=fuse_qkv)
    return pl.pallas_call(
        body,
        grid=(n_conv + n_tiles,),
        in_specs=in_specs,
        out_specs=out_specs,
        out_shape=out_shape,
        scratch_shapes=scratch,
        compiler_params=_cparams(1),
        name="in_proj",
    )(*args)


def _q_proj_sample_body(ql_ref, wq_ref, cos_ref, sin_ref, qng_ref, qrg_ref, kng_ref, wukt_ref,
                        qa_ref, qr_ref):
    q = _dot(ql_ref[...], wq_ref[...])
    cos2, sin2 = cos_ref[...], sin_ref[...]
    for h in range(N_HEADS):
        qn = q[:, h * D_QK:h * D_QK + D_NOPE]
        qn = qn * _rms(qn, D_NOPE) * qng_ref[...]
        qa_ref[h] = _dot((qn * kng_ref[...]).astype(bf16), wukt_ref[h]).astype(bf16)
        qr = q[:, h * D_QK + D_NOPE:(h + 1) * D_QK]
        qr = _rope128(qr * _rms(qr, D_ROPE) * qrg_ref[...], cos2, sin2)
        qr_ref[h] = qr[:, :D_ROPE].astype(bf16)


def _q_proj_sample(qlat, cos2, sin2, wts):
    n_tok = qlat.shape[0]
    return pl.pallas_call(
        _q_proj_sample_body,
        grid=(1,),
        in_specs=[_resident((n_tok, Q_RANK)), _resident((Q_RANK, N_HEADS * D_QK)),
                  _resident((n_tok, LANES)), _resident((n_tok, LANES)),
                  _resident((1, D_NOPE)), _resident((1, LANES)), _resident((1, D_NOPE)),
                  _resident((N_HEADS, D_NOPE, KV_RANK))],
        out_specs=[_resident((N_HEADS, n_tok, KV_RANK)), _resident((N_HEADS, n_tok, D_ROPE))],
        out_shape=[jax.ShapeDtypeStruct((N_HEADS, n_tok, KV_RANK), bf16),
                   jax.ShapeDtypeStruct((N_HEADS, n_tok, D_ROPE), bf16)],
        compiler_params=_cparams(1),
        name="q_proj_sample",
    )(qlat, wts["w_uq"], cos2, sin2, wts["q_nope_g"], wts["q_rope_g"], wts["k_nope_g"],
      wts["w_uk_t"])


def _attn_prompt_body(qt_ref, k_ref, vt_ref, gm_ref, o_ref, m_ref, acc_ref, s_ref, mx_ref, *, tq, tk):
    qi = pl.program_id(1)
    m_ref[...] = jnp.full(m_ref.shape, NEG_INF, f32)
    acc_ref[...] = jnp.zeros(acc_ref.shape, f32)

    def all_sublanes_max(x):
        for shift in (4, 2, 1):
            x = jnp.maximum(x, pltpu.roll(x, shift, 0))
        return x

    def scores(h, j, buf, masked):
        row0 = pl.multiple_of(j * tk, tk)
        kb = k_ref[pl.ds(row0, tk), h * D_QK:(h + 1) * D_QK]
        s = _dot(kb, qt_ref[0, h * D_QK:(h + 1) * D_QK, :])
        if masked:
            k_chunk = lax.broadcasted_iota(jnp.int32, (tk, tq), 0) // CHUNK
            q_chunk = lax.broadcasted_iota(jnp.int32, (tk, tq), 1) // CHUNK
            s = jnp.where(k_chunk <= q_chunk, s, NEG_INF)
        s_ref[buf, h] = s
        mx_ref[buf, h] = all_sublanes_max(jnp.max(s.reshape(tk // 8, 8, tq), axis=0))

    def values(h, j, buf):
        m_old = m_ref[h]
        m_new = jnp.maximum(m_old, mx_ref[buf, h])
        alpha = jnp.exp2(m_old - m_new)
        p = jnp.exp2(s_ref[buf, h].reshape(tk // 8, 8, tq) - m_new[None]).reshape(tk, tq)
        m_ref[h] = m_new
        acc = acc_ref[h].reshape(D_VX // 8, 8, tq) * alpha[None]
        acc_ref[h] = acc.reshape(D_VX, tq) + _dot(vt_ref[0, j, h], p.astype(bf16))

    for h in range(N_HEADS):
        scores(h, qi, 0, True)

    def step(t, buf):
        prev = jnp.where(t == 1, qi, t - 2)
        for h in range(N_HEADS):
            scores(h, t - 1, buf, False)
            values(h, prev, 1 - buf)

    def step_pair(u, carry):
        step(2 * u + 1, 1)
        step(2 * u + 2, 0)
        return carry

    lax.fori_loop(0, qi // 2, step_pair, 0)

    @pl.when(qi % 2 == 1)
    def _():
        step(qi, 1)
        for h in range(N_HEADS):
            values(h, qi - 1, 1)

    @pl.when(qi % 2 == 0)
    def _():
        last = jnp.where(qi == 0, qi, qi - 1)
        for h in range(N_HEADS):
            values(h, last, 0)

    for h in range(N_HEADS):
        hs = slice(h * D_V, (h + 1) * D_V)
        o = (acc_ref[h, 0:D_V, :] / acc_ref[h, D_V:D_V + 1, :]).T
        o_ref[:, hs] = (o * gm_ref[:, hs].astype(f32)).astype(bf16)


def _attn_prompt(qt, k, vt, gm, *, n_seq, seq_len, tq):
    tk = tq
    nq = seq_len // tq
    body = functools.partial(_attn_prompt_body, tq=tq, tk=tk)
    return pl.pallas_call(
        body,
        grid=(n_seq, nq),
        in_specs=[pl.BlockSpec((1, N_HEADS * D_QK, tq), lambda b, i: (b, 0, i)),
                  pl.BlockSpec((seq_len, N_HEADS * D_QK), lambda b, i: (b, 0)),
                  pl.BlockSpec((1, seq_len // tk, N_HEADS, D_VX, tk), lambda b, i: (b, 0, 0, 0, 0)),
                  pl.BlockSpec((tq, D_MLA), lambda b, i: (b * nq + i, 0))],
        out_specs=pl.BlockSpec((tq, D_MLA), lambda b, i: (b * nq + i, 0)),
        out_shape=jax.ShapeDtypeStruct((n_seq * seq_len, D_MLA), bf16),
        scratch_shapes=[pltpu.VMEM((N_HEADS, 8, tq), f32), pltpu.VMEM((N_HEADS, D_VX, tq), f32),
                        pltpu.VMEM((2, N_HEADS, tk, tq), f32), pltpu.VMEM((2, N_HEADS, 8, tq), f32)],
        compiler_params=_cparams(2),
        name="attn_prompt",
    )(qt, k, vt, gm)


def _attn_sample_body(cckv_ref, ckr_t_ref, nckv_ref, nkr_ref, qa_ref, qr_ref, gm_ref, wukt_ref, wuv_ref,
                      o_ref, c_ref, ktail_ref, s_ref, *, past, t_new, chunks, n_b):
    n_tail = c_ref.shape[1] - past
    rows = N_HEADS * t_new
    c_ref[:, past:past + t_new, :] = nckv_ref[...].astype(bf16)
    c_ref[:, past + t_new:, :] = jnp.zeros((n_b, n_tail - t_new, KV_RANK), bf16)
    ktail_ref[:, 0:t_new, :] = nkr_ref[...].astype(bf16)
    ktail_ref[:, t_new:, :] = jnp.zeros((n_b, n_tail - t_new, D_ROPE), bf16)

    def chunk_scores(b, start, size):
        cached = start < past
        if cached:
            c_ref[b, start:start + size, :] = cckv_ref[b, start:start + size, :].astype(bf16)
        c = c_ref[b, start:start + size, :]
        k_t = _dot_nt(wukt_ref[...], c)
        ssq = jnp.sum((k_t * k_t).reshape(N_HEADS, D_NOPE, size), axis=1)
        r = lax.rsqrt(ssq * (1.0 / D_NOPE) + EPS)
        qa = qa_ref[:, b * t_new:(b + 1) * t_new, :].reshape(rows, KV_RANK)
        qr = qr_ref[:, b * t_new:(b + 1) * t_new, :].reshape(rows, D_ROPE)
        s_nope = _dot_nt(qa, c)
        if cached:
            s_rope = _dot(qr, ckr_t_ref[b, :, start:start + size].astype(bf16))
        else:
            s_rope = _dot_nt(qr, ktail_ref[b])
        for h in range(N_HEADS):
            hs = slice(h * t_new, (h + 1) * t_new)
            s = (s_nope[hs] * r[h:h + 1, :] + s_rope[hs]) * ATTN_SCALE
            if not cached:
                key = lax.broadcasted_iota(jnp.int32, (t_new, size), 1)
                s = jnp.where(key < t_new, s, NEG_INF)
            s_ref[b, hs, start:start + size] = s

    def chunk_values(b, start, size, state):
        m, l, acc = state
        s = s_ref[b, :, start:start + size]
        m_new = jnp.maximum(m, jnp.max(s, axis=-1, keepdims=True))
        alpha = jnp.exp(m - m_new)
        p = jnp.exp(s - m_new)
        l = alpha * l + jnp.sum(p, axis=-1, keepdims=True)
        acc = alpha * acc + _dot(p.astype(bf16), c_ref[b, start:start + size, :])
        return m_new, l, acc

    def finish(b, state):
        _, l, acc = state
        o_lat = (acc / l).astype(bf16)
        for h in range(N_HEADS):
            o = _dot(o_lat[h * t_new:(h + 1) * t_new], wuv_ref[h])
            hs = slice(h * D_V, (h + 1) * D_V)
            bs = slice(b * t_new, (b + 1) * t_new)
            o_ref[bs, hs] = (o * gm_ref[bs, hs].astype(f32)).astype(bf16)

    items = [(b,) + ch for b in range(n_b) for ch in chunks]
    init = (jnp.full((rows, 1), NEG_INF, f32), jnp.zeros((rows, 1), f32),
            jnp.zeros((rows, KV_RANK), f32))
    state = init
    chunk_scores(*items[0])
    for nxt, cur in zip(items[1:] + [None], items):
        if nxt is not None:
            chunk_scores(*nxt)
        state = chunk_values(*cur, state)
        if nxt is None or nxt[0] != cur[0]:
            finish(cur[0], state)
            state = init


def _attn_sample(cache_ckv, cache_krope_t, ckv_new, krope_new, q_abs, q_rope, gm, wts, *, t_new, n_b):
    n_seq, past, _ = cache_ckv.shape
    chunk = 512
    chunks = tuple((s, min(chunk, past - s)) for s in range(0, past, chunk)) + ((past, LANES),)
    s_pad = past + LANES
    body = functools.partial(_attn_sample_body, past=past, t_new=t_new, chunks=chunks, n_b=n_b)
    return pl.pallas_call(
        body,
        grid=(n_seq // n_b,),
        in_specs=[pl.BlockSpec((n_b, past, KV_RANK), lambda b: (b, 0, 0)),
                  pl.BlockSpec((n_b, D_ROPE, past), lambda b: (b, 0, 0)),
                  pl.BlockSpec((n_b, t_new, KV_RANK), lambda b: (b, 0, 0)),
                  pl.BlockSpec((n_b, t_new, D_ROPE), lambda b: (b, 0, 0)),
                  pl.BlockSpec((N_HEADS, n_b * t_new, KV_RANK), lambda b: (0, b, 0)),
                  pl.BlockSpec((N_HEADS, n_b * t_new, D_ROPE), lambda b: (0, b, 0)),
                  pl.BlockSpec((n_b * t_new, D_MLA), lambda b: (b, 0)),
                  _resident((N_HEADS * D_NOPE, KV_RANK)),
                  _resident((N_HEADS, KV_RANK, D_V))],
        out_specs=pl.BlockSpec((n_b * t_new, D_MLA), lambda b: (b, 0)),
        out_shape=jax.ShapeDtypeStruct((n_seq * t_new, D_MLA), bf16),
        scratch_shapes=[pltpu.VMEM((n_b, s_pad, KV_RANK), bf16),
                        pltpu.VMEM((n_b, s_pad - past, D_ROPE), bf16),
                        pltpu.VMEM((n_b, N_HEADS * t_new, s_pad), f32)],
        compiler_params=_cparams(1),
        name="attn_sample",
    )(cache_ckv, cache_krope_t, ckv_new, krope_new, q_abs, q_rope, gm,
      wts["w_uk_t"].reshape(N_HEADS * D_NOPE, KV_RANK), wts["w_uv3"])


def _out_proj_body(x_ref, pa_ref, ma_ref, p_ref, wo_ref, png_ref, wg_ref, bg_ref, wple_ref, y_ref):
    h = x_ref[...] + _dot(pa_ref[...], wo_ref[0:D_POOL, :]) + _dot(ma_ref[...], wo_ref[D_POOL:, :])
    hn = (h * _rms(h, D_MODEL) * png_ref[...]).astype(bf16)
    gate = jax.nn.sigmoid(_dot(hn, wg_ref[...]) + bg_ref[...])
    y_ref[...] = h + gate * _dot(p_ref[...].astype(bf16), wple_ref[...])


def _out_proj(x2d, pool_act, mla_act, p2d, wts, *, tm):
    n_tok = x2d.shape[0]
    row = lambda i: (i, 0)
    return pl.pallas_call(
        _out_proj_body,
        grid=(n_tok // tm,),
        in_specs=[pl.BlockSpec((tm, D_MODEL), row), pl.BlockSpec((tm, D_POOL), row),
                  pl.BlockSpec((tm, D_MLA), row), pl.BlockSpec((tm, D_PLE), row),
                  _resident((D_MODEL, D_MODEL)), _resident((1, D_MODEL)),
                  _resident((D_MODEL, D_MODEL)), _resident((1, D_MODEL)),
                  _resident((D_PLE, D_MODEL))],
        out_specs=pl.BlockSpec((tm, D_MODEL), row),
        out_shape=jax.ShapeDtypeStruct((n_tok, D_MODEL), f32),
        compiler_params=_cparams(1),
        name="out_proj",
    )(x2d, pool_act, mla_act, p2d, wts["w_out"], wts["ple_norm_g"], wts["w_ple_gate"],
      wts["b_ple_gate"], wts["w_ple"])


def _rope_tables(pos0, t, reps=1):
    pos = (pos0 + np.arange(t)).astype(np.float64)
    inv = ROPE_THETA ** (-(np.arange(0, D_ROPE, 2, dtype=np.float64) / D_ROPE))
    ang = pos[:, None] * inv[None, :]
    cos, sin = np.cos(ang), np.sin(ang)
    zero = np.zeros((t, LANES - D_ROPE))
    rows = (np.concatenate([cos, cos, zero], axis=-1), np.concatenate([-sin, sin, zero], axis=-1),
            pos[:, None])
    as_f32 = lambda a: jnp.asarray(a.astype(np.float32))
    return tuple(as_f32(np.tile(a, (reps, 1))) for a in rows) + ((as_f32(cos.T), as_f32(sin.T)),)


def _prep_weights(norm_g, w_in, q_norm_g, w_uq, kv_norm_g, w_ukv, q_nope_g, q_rope_g, k_nope_g,
                  k_rope_g, w_pool, pool_scale, w_out, ple_norm_g, w_ple_gate, b_ple_gate, w_ple):
    w_uq_r = jnp.pad(w_uq.reshape(Q_RANK, N_HEADS, D_NOPE + D_ROPE),
                     ((0, 0), (0, 0), (0, D_QK - D_NOPE - D_ROPE))).reshape(Q_RANK, N_HEADS * D_QK)
    w_ukv3 = w_ukv.reshape(KV_RANK, N_HEADS, D_NOPE + D_V)
    w_uk3, w_uv3 = w_ukv3[..., :D_NOPE], w_ukv3[..., D_NOPE:]
    pad_rope = lambda g: jnp.pad(g, (0, LANES - D_ROPE))[None, :]
    return {
        "norm_g": norm_g[None, :], "w_in_t_f32": w_in.T, "q_norm_g": q_norm_g[None, :],
        "kv_norm_g": kv_norm_g[None, :], "k_rope_g": pad_rope(k_rope_g),
        "w_pool": w_pool.astype(bf16), "pool_scale": pool_scale[None, :],
        "w_uq": w_uq_r.astype(bf16), "q_nope_g": q_nope_g[None, :], "q_rope_g": pad_rope(q_rope_g),
        "w_uq_t": w_uq.T.astype(bf16),
        "q_nope_g_col": q_nope_g[:, None], "q_rope_g_col": q_rope_g[:, None],
        "k_nope_g": k_nope_g[None, :],
        "w_uk": w_uk3.reshape(KV_RANK, N_HEADS * D_NOPE).astype(bf16),
        "w_uv_t": w_uv3.reshape(KV_RANK, D_MLA).T.astype(bf16),
        "w_uk_t": jnp.transpose(w_uk3, (1, 2, 0)).astype(bf16),
        "w_uv3": jnp.transpose(w_uv3, (1, 0, 2)).astype(bf16),
        "w_out_f32": w_out, "w_ple_gate_f32": w_ple_gate,
        "ple_norm_g": ple_norm_g[None, :], "b_ple_gate": b_ple_gate[None, :],
        "w_ple": w_ple.astype(bf16),
    }


def _layer_prompt(x, p, wts):
    n_seq, seq_len, _ = x.shape
    x2d = x.reshape(n_seq * seq_len, D_MODEL)
    cos2, sin2, pos, rope_t = _rope_tables(0, seq_len)
    hist = jnp.zeros((1, HALO, D_POOL), f32)
    pact, gm, ckv, krope, npool, qt, k, vt, w_out16, w_gate16, w_in_t16 = _in_proj(
        x2d, hist, cos2, sin2, pos, wts, nseg=1, seg_len=PROMPT_TILE, n_seq=n_seq, carry=True,
        rope_t=rope_t, kv_block=ATTN_BLOCK)
    wts.update(w_out=w_out16, w_ple_gate=w_gate16, w_in_t=w_in_t16)
    mact = _attn_prompt(qt, k, vt, gm, n_seq=n_seq, seq_len=seq_len, tq=ATTN_BLOCK)
    y = _out_proj(x2d, pact, mact, p.reshape(n_seq * seq_len, D_PLE), wts, tm=OUT_TILE)
    return (y.reshape(x.shape), ckv.reshape(n_seq, seq_len, KV_RANK),
            krope.reshape(n_seq, seq_len, D_ROPE), npool)


def _layer_sample(x, p, state_pool, cache_ckv, cache_krope, wts):
    n_seq, t_new, _ = x.shape
    past = cache_ckv.shape[1]
    n_tok = n_seq * t_new
    x2d = x.reshape(n_tok, D_MODEL)
    cos2, sin2, pos, _ = _rope_tables(past, t_new, reps=n_seq)
    hist = jnp.pad(state_pool, ((0, 0), (HALO - POOL_HIST, 0), (0, 0)))
    pact, gm, ckv, krope, npool, qlat = _in_proj(
        x2d, hist, cos2, sin2, pos, wts, nseg=n_seq, seg_len=t_new, n_seq=n_seq, carry=False)
    q_abs, q_rope = _q_proj_sample(qlat, cos2, sin2, wts)
    ckv3 = ckv.reshape(n_seq, t_new, KV_RANK)
    krope3 = krope.reshape(n_seq, t_new, D_ROPE)
    cache_krope_t = jnp.transpose(cache_krope, (0, 2, 1))
    mact = _attn_sample(cache_ckv, cache_krope_t, ckv3, krope3, q_abs, q_rope, gm, wts, t_new=t_new,
                        n_b=SAMPLE_BATCH_PER_STEP)
    y = _out_proj(x2d, pact, mact, p.reshape(n_tok, D_PLE), wts, tm=OUT_TILE)
    return y.reshape(x.shape), ckv3, krope3, npool


def kernel(x_prompt, x_sample, cache_ckv, cache_krope, state_pool, p_prompt, p_sample, norm_g, w_in,
           q_norm_g, w_uq, kv_norm_g, w_ukv, q_nope_g, q_rope_g, k_nope_g, k_rope_g, w_pool,
           pool_scale, w_out, ple_norm_g, w_ple_gate, b_ple_gate, w_ple):
    depth = norm_g.shape[0]
    layer_w = (norm_g, w_in, q_norm_g, w_uq, kv_norm_g, w_ukv, q_nope_g, q_rope_g, k_nope_g, k_rope_g,
               w_pool, pool_scale, w_out, ple_norm_g, w_ple_gate, b_ple_gate, w_ple)
    yp, ys = x_prompt, x_sample
    outs = [[] for _ in range(6)]
    for i in range(depth):
        wts = _prep_weights(*(w[i] for w in layer_w))
        yp, c1, k1, s1 = _layer_prompt(yp, p_prompt[i], wts)
        ys, c2, k2, s2 = _layer_sample(ys, p_sample[i], state_pool[i], cache_ckv[i], cache_krope[i], wts)
        for lst, val in zip(outs, (c1, k1, s1, c2, k2, s2)):
            lst.append(val)
    return (yp, ys) + tuple(jnp.stack(o) for o in outs)
```

```python
import functools

import jax
import jax.numpy as jnp
import numpy as np
from jax import lax
from jax.experimental import pallas as pl
from jax.experimental.pallas import tpu as pltpu

D_MODEL = 2048
CHUNK = 64
D_POOL = 1024
POOL_WINDOWS = (2, 4, 8, 16)
POOL_GROUP = 256
POOL_HIST = 15
HALO = 16
N_HEADS = 8
D_NOPE = 128
D_ROPE = 64
D_V = 128
D_VX = D_V + 16
D_MLA = N_HEADS * D_V
Q_RANK = 512
KV_RANK = 256
D_PLE = 256
D_QK = 256
ROPE_THETA = 10000.0
EPS = 1e-6
ATTN_SCALE = (D_NOPE + D_ROPE) ** -0.5
NEG_INF = -1e30
EXP2_SCALE = ATTN_SCALE * float(np.log2(np.e))
LANES = 128

C_U, C_GP, C_Q, C_KV, C_KR, C_GM = 0, 1024, 2048, 2560, 2816, 2880
D_IN = 3904

VMEM_LIMIT = 56 * 1024 * 1024
PROMPT_TILE = 256
W_CHUNK = 128
ATTN_BLOCK = 256
OUT_TILE = 512
SAMPLE_BATCH_PER_STEP = 4

f32 = jnp.float32
bf16 = jnp.bfloat16


def _cparams(n_axes):
    return pltpu.CompilerParams(dimension_semantics=("arbitrary",) * n_axes,
                                vmem_limit_bytes=VMEM_LIMIT)


def _resident(shape):
    nd = len(shape)
    return pl.BlockSpec(shape, lambda *_: (0,) * nd, pipeline_mode=pl.Buffered(1))


def _rms(x, n):
    return lax.rsqrt(jnp.sum(x * x, axis=-1, keepdims=True) * (1.0 / n) + EPS)


def _rms_cols(x_t, n):
    return lax.rsqrt(jnp.sum(x_t * x_t, axis=0, keepdims=True) * (1.0 / n) + EPS)


def _dot(a, b):
    return jnp.dot(a, b, preferred_element_type=f32)


def _dot_nt(a, b):
    return lax.dot_general(a, b, (((1,), (1,)), ((), ())), preferred_element_type=f32)


def _rope128(x, cos2, sin2):
    lane = lax.broadcasted_iota(jnp.int32, x.shape, 1)
    swapped = jnp.where(lane < D_ROPE // 2, pltpu.roll(x, LANES - D_ROPE // 2, 1),
                        pltpu.roll(x, D_ROPE // 2, 1))
    return x * cos2 + swapped * sin2


def _queries_t(ql, wqt_ref, cos_ref, sin_ref, qng_ref, qrg_ref, qt_ref):
    q_t = _dot_nt(wqt_ref[...], ql)
    tm = q_t.shape[1]
    cos, sin = cos_ref[...], sin_ref[...]
    half = D_ROPE // 2
    for h in range(N_HEADS):
        r0 = h * D_QK
        s0 = h * (D_NOPE + D_ROPE)
        qn = q_t[s0:s0 + D_NOPE]
        qn = qn * (_rms_cols(qn, D_NOPE) * EXP2_SCALE) * qng_ref[...]
        qt_ref[0, r0:r0 + D_NOPE, :] = qn.astype(bf16)
        qr = q_t[s0 + D_NOPE:s0 + D_NOPE + D_ROPE]
        qr = qr * (_rms_cols(qr, D_ROPE) * EXP2_SCALE) * qrg_ref[...]
        x1, x2 = qr[:half], qr[half:]
        qt_ref[0, r0 + D_NOPE:r0 + D_NOPE + half, :] = (x1 * cos - x2 * sin).astype(bf16)
        qt_ref[0, r0 + D_NOPE + half:r0 + D_NOPE + D_ROPE, :] = (x2 * cos + x1 * sin).astype(bf16)
        qt_ref[0, r0 + D_NOPE + D_ROPE:r0 + D_QK, :] = jnp.zeros((D_QK - D_NOPE - D_ROPE, tm), bf16)


def _keys_values(c, kr128, wk_ref, wvt_ref, kng_ref, k_ref, vt_ref):
    tm = c.shape[0]
    k = _dot(c, wk_ref[...])
    kr = kr128.astype(bf16)
    for h in range(N_HEADS):
        kn = k[:, h * D_NOPE:(h + 1) * D_NOPE]
        k_ref[:, h * D_QK:h * D_QK + D_NOPE] = (kn * _rms(kn, D_NOPE) * kng_ref[...]).astype(bf16)
        k_ref[:, h * D_QK + D_NOPE:(h + 1) * D_QK] = kr
    v_t = _dot_nt(wvt_ref[...], c).astype(bf16)
    n_blk, tk = vt_ref.shape[1], vt_ref.shape[4]
    row = lax.broadcasted_iota(jnp.int32, (D_VX - D_V, tk), 0)
    ones_row = jnp.where(row == 0, 1.0, 0.0).astype(bf16)
    for j in range(n_blk):
        for h in range(N_HEADS):
            vt_ref[0, j, h, 0:D_V, :] = v_t[h * D_V:(h + 1) * D_V, j * tk:(j + 1) * tk]
            vt_ref[0, j, h, D_V:D_VX, :] = ones_row


def _in_proj_body(*refs, n_conv, **static):
    if not static["fuse_qkv"]:
        _in_proj_tile(pl.program_id(0), refs, **static)
        return
    i = pl.program_id(0)
    w32_ref, w16_ref, wt_ref = refs[6], refs[-5], refs[-1]

    def convert(rows):
        chunk = w32_ref[0:rows, :].astype(bf16)
        w16_ref[0:rows, :] = chunk
        wt_ref[pl.ds(pl.multiple_of(i * W_CHUNK, W_CHUNK), rows), :] = chunk

    last_rows = D_IN - (n_conv - 1) * W_CHUNK
    pl.when(i < n_conv - 1)(lambda: convert(W_CHUNK))
    pl.when(i == n_conv - 1)(lambda: convert(last_rows))

    @pl.when(i >= n_conv)
    def _():
        _in_proj_tile(i - n_conv, refs[:6] + (wt_ref,) + refs[7:-5] + refs[-4:-1], **static)


def _in_proj_tile(i, refs, *, nseg, seg_len, tiles_per_seq, carry, fuse_qkv):
    (x_ref, hist_ref, cos_ref, sin_ref, pos_ref, ng_ref, wt_ref, qg_ref, kvg_ref, krg_ref, wp_ref,
     ps_ref) = refs[:12]
    if fuse_qkv:
        (cost_ref, sint_ref, wqt_ref, qng_ref, qrg_ref, wk_ref, wvt_ref, kng_ref, x0_ref, wo32_ref,
         wg32_ref) = refs[12:23]
        (pact_ref, gm_ref, ckv_ref, kr_ref, npool_ref, qt_ref, k_ref, vt_ref, wo16_ref, wg16_ref,
         ucat_ref, xn_ref, xn_next_ref) = refs[23:]
    else:
        pact_ref, gm_ref, ckv_ref, kr_ref, npool_ref, ql_ref, ucat_ref = refs[12:]
    tm = nseg * seg_len

    def normed(ref):
        x = ref[...]
        return (x * _rms(x, D_MODEL) * ng_ref[...]).astype(bf16)

    if fuse_qkv:
        @pl.when(i == 0)
        def _():
            xn_next_ref[...] = normed(x0_ref)
        xn_ref[...] = xn_next_ref[...]
        xn = xn_ref[...]
    else:
        xn = normed(x_ref)

    u = _dot_nt(xn, wt_ref[C_U:C_U + D_POOL, :])
    if carry:
        @pl.when(i % tiles_per_seq == 0)
        def _():
            ucat_ref[:, 0:HALO, :] = jnp.zeros((nseg, HALO, D_POOL), f32)
    else:
        ucat_ref[:, 0:HALO, :] = hist_ref[...]
    ucat_ref[:, HALO:HALO + seg_len, :] = u.reshape(nseg, seg_len, D_POOL)

    pos = pos_ref[...]
    gp = _dot_nt(xn, wt_ref[C_GP:C_GP + D_POOL, :])
    if fuse_qkv:
        xn_next_ref[...] = normed(x_ref)
    gate = gp * jax.nn.sigmoid(gp) * ps_ref[...]

    def pool_group(g):
        w = POOL_WINDOWS[g]
        sl = slice(g * POOL_GROUP, (g + 1) * POOL_GROUP)
        acc = ucat_ref[:, :, sl]
        shift = 1
        while shift < w:
            acc = acc + pltpu.roll(acc, shift, 1)
            shift *= 2
        acc = acc[:, HALO:, :].reshape(tm, POOL_GROUP)
        inv_cnt = 1.0 / jnp.minimum(pos + 1.0, float(w))
        d = (acc * inv_cnt - u[:, sl]).astype(bf16)
        mixed = _dot(d, wp_ref[g])
        pact_ref[:, sl] = (mixed * gate[:, sl]).astype(bf16)

    cq = _dot_nt(xn, wt_ref[C_Q:C_Q + Q_RANK, :])
    pool_group(0)
    pool_group(1)
    ql = (cq * _rms(cq, Q_RANK) * qg_ref[...]).astype(bf16)
    if fuse_qkv:
        _queries_t(ql, wqt_ref, cost_ref, sint_ref, qng_ref, qrg_ref, qt_ref)
    else:
        ql_ref[...] = ql

    ckv = _dot_nt(xn, wt_ref[C_KV:C_KV + KV_RANK, :])
    pool_group(2)
    ckv = ckv * _rms(ckv, KV_RANK) * kvg_ref[...]
    ckv_ref[...] = ckv

    kr = _dot_nt(xn, wt_ref[C_KR:C_KR + LANES, :])
    gm = _dot_nt(xn, wt_ref[C_GM:C_GM + D_MLA, :])
    pool_group(3)
    npool_ref[...] = ucat_ref[:, seg_len + 1:seg_len + HALO, :]
    if carry:
        ucat_ref[:, 0:HALO, :] = ucat_ref[:, seg_len:seg_len + HALO, :]
    lane = lax.broadcasted_iota(jnp.int32, (tm, LANES), 1)
    kr = jnp.where(lane < D_ROPE, kr, 0.0)
    kr = _rope128(kr * _rms(kr, D_ROPE) * krg_ref[...], cos_ref[...], sin_ref[...])
    kr_ref[...] = kr[:, :D_ROPE]
    gm_ref[...] = (gm * jax.nn.sigmoid(gm)).astype(bf16)
    if fuse_qkv:
        _keys_values(ckv.astype(bf16), kr, wk_ref, wvt_ref, kng_ref, k_ref, vt_ref)
        wo16_ref[...] = wo32_ref[...].astype(bf16)
        wg16_ref[...] = wg32_ref[...].astype(bf16)


def _in_proj(x2d, hist, cos2, sin2, pos, wts, *, nseg, seg_len, n_seq, carry, rope_t=None, kv_block=None):
    n_tok = x2d.shape[0]
    tm = nseg * seg_len
    n_tiles = n_tok // tm
    tiles_per_seq = n_tiles // n_seq if carry else 1
    tab_tiles = cos2.shape[0] // tm
    fuse_qkv = rope_t is not None
    n_conv = pl.cdiv(D_IN, W_CHUNK) if fuse_qkv else 0
    tile = (lambda i: jnp.maximum(i - n_conv, 0)) if n_conv else (lambda i: i)
    row = lambda i: (tile(i), 0)
    tab = lambda i: (tile(i) % tab_tiles, 0)
    if carry:
        hist_spec = pl.BlockSpec((1, HALO, D_POOL), lambda i: (0, 0, 0))
        npool_spec = pl.BlockSpec((1, POOL_HIST, D_POOL), lambda i: (tile(i) // tiles_per_seq, 0, 0))
    else:
        hist_spec = pl.BlockSpec((nseg, HALO, D_POOL), lambda i: (tile(i), 0, 0))
        npool_spec = pl.BlockSpec((nseg, POOL_HIST, D_POOL), lambda i: (tile(i), 0, 0))
    if fuse_qkv:
        x_spec = pl.BlockSpec((tm, D_MODEL), lambda i: (jnp.minimum(tile(i) + 1, n_tiles - 1), 0))
        w_chunk = lambda i: (jnp.minimum(i, n_conv - 1), 0)
        w_spec, w_arg = pl.BlockSpec((W_CHUNK, D_MODEL), w_chunk), wts["w_in_t_f32"]
    else:
        x_spec = pl.BlockSpec((tm, D_MODEL), row)
        w_spec, w_arg = _resident((D_IN, D_MODEL)), wts["w_in_t"]
    in_specs = [
        x_spec, hist_spec, pl.BlockSpec((tm, LANES), tab),
        pl.BlockSpec((tm, LANES), tab), pl.BlockSpec((tm, 1), tab), _resident((1, D_MODEL)),
        w_spec, _resident((1, Q_RANK)), _resident((1, KV_RANK)),
        _resident((1, LANES)), _resident((4, POOL_GROUP, POOL_GROUP)), _resident((1, D_POOL)),
    ]
    args = [x2d, hist, cos2, sin2, pos, wts["norm_g"], w_arg, wts["q_norm_g"],
            wts["kv_norm_g"], wts["k_rope_g"], wts["w_pool"], wts["pool_scale"]]
    out_specs = [pl.BlockSpec((tm, D_POOL), row), pl.BlockSpec((tm, D_MLA), row),
                 pl.BlockSpec((tm, KV_RANK), row), pl.BlockSpec((tm, D_ROPE), row), npool_spec]
    out_shape = [jax.ShapeDtypeStruct((n_tok, D_POOL), bf16), jax.ShapeDtypeStruct((n_tok, D_MLA), bf16),
                 jax.ShapeDtypeStruct((n_tok, KV_RANK), f32), jax.ShapeDtypeStruct((n_tok, D_ROPE), f32),
                 jax.ShapeDtypeStruct((n_seq, POOL_HIST, D_POOL), f32)]
    scratch = [pltpu.VMEM((nseg, HALO + seg_len, D_POOL), f32)]
    if fuse_qkv:
        seq_tab = lambda i: (0, tile(i) % tiles_per_seq)
        seq_blk = lambda i: (tile(i) // tiles_per_seq, 0, tile(i) % tiles_per_seq)
        in_specs += [pl.BlockSpec((D_ROPE // 2, tm), seq_tab), pl.BlockSpec((D_ROPE // 2, tm), seq_tab),
                     _resident((N_HEADS * (D_NOPE + D_ROPE), Q_RANK)), _resident((D_NOPE, 1)),
                     _resident((D_ROPE, 1)),
                     _resident((KV_RANK, N_HEADS * D_NOPE)), _resident((D_MLA, KV_RANK)),
                     _resident((1, D_NOPE)),
                     pl.BlockSpec((tm, D_MODEL), lambda i: (0, 0), pipeline_mode=pl.Buffered(1)),
                     pl.BlockSpec((D_MODEL // n_tiles, D_MODEL), row),
                     pl.BlockSpec((D_MODEL // n_tiles, D_MODEL), row)]
        args += [rope_t[0], rope_t[1], wts["w_uq_t"], wts["q_nope_g_col"], wts["q_rope_g_col"],
                 wts["w_uk"], wts["w_uv_t"], wts["k_nope_g"], x2d, wts["w_out_f32"], wts["w_ple_gate_f32"]]
        out_specs += [
            pl.BlockSpec((1, N_HEADS * D_QK, tm), seq_blk),
            pl.BlockSpec((tm, N_HEADS * D_QK), row),
            pl.BlockSpec((1, tm // kv_block, N_HEADS, D_VX, kv_block), lambda i: seq_blk(i)[::2] + (0, 0, 0)),
            pl.BlockSpec((D_MODEL // n_tiles, D_MODEL), row),
            pl.BlockSpec((D_MODEL // n_tiles, D_MODEL), row),
            pl.BlockSpec((W_CHUNK, D_MODEL), w_chunk)]
        out_shape += [jax.ShapeDtypeStruct((n_seq, N_HEADS * D_QK, tiles_per_seq * tm), bf16),
                      jax.ShapeDtypeStruct((n_tok, N_HEADS * D_QK), bf16),
                      jax.ShapeDtypeStruct((n_seq, tiles_per_seq * tm // kv_block, N_HEADS, D_VX, kv_block),
                                           bf16),
                      jax.ShapeDtypeStruct((D_MODEL, D_MODEL), bf16),
                      jax.ShapeDtypeStruct((D_MODEL, D_MODEL), bf16),
                      jax.ShapeDtypeStruct((D_IN, D_MODEL), bf16)]
        scratch += [pltpu.VMEM((tm, D_MODEL), bf16), pltpu.VMEM((tm, D_MODEL), bf16),
                    pltpu.VMEM((n_conv * W_CHUNK, D_MODEL), bf16)]
    else:
        out_specs.append(pl.BlockSpec((tm, Q_RANK), row))
        out_shape.append(jax.ShapeDtypeStruct((n_tok, Q_RANK), bf16))
    body = functools.partial(_in_proj_body, n_conv=n_conv, nseg=nseg, seg_len=seg_len,
                             tiles_per_seq=tiles_per_seq, carry=carry, fuse_qkv=fuse_qkv)
    return pl.pallas_call(
        body,
        grid=(n_conv + n_tiles,),
        in_specs=in_specs,
        out_specs=out_specs,
        out_shape=out_shape,
        scratch_shapes=scratch,
        compiler_params=_cparams(1),
        name="in_proj",
    )(*args)


def _q_proj_sample_body(ql_ref, wq_ref, cos_ref, sin_ref, qng_ref, qrg_ref, kng_ref, wukt_ref,
                        qa_ref, qr_ref):
    q = _dot(ql_ref[...], wq_ref[...])
    cos2, sin2 = cos_ref[...], sin_ref[...]
    for h in range(N_HEADS):
        qn = q[:, h * D_QK:h * D_QK + D_NOPE]
        qn = qn * _rms(qn, D_NOPE) * qng_ref[...]
        qa_ref[h] = _dot((qn * kng_ref[...]).astype(bf16), wukt_ref[h]).astype(bf16)
        qr = q[:, h * D_QK + D_NOPE:(h + 1) * D_QK]
        qr = _rope128(qr * _rms(qr, D_ROPE) * qrg_ref[...], cos2, sin2)
        qr_ref[h] = qr[:, :D_ROPE].astype(bf16)


def _q_proj_sample(qlat, cos2, sin2, wts):
    n_tok = qlat.shape[0]
    return pl.pallas_call(
        _q_proj_sample_body,
        grid=(1,),
        in_specs=[_resident((n_tok, Q_RANK)), _resident((Q_RANK, N_HEADS * D_QK)),
                  _resident((n_tok, LANES)), _resident((n_tok, LANES)),
                  _resident((1, D_NOPE)), _resident((1, LANES)), _resident((1, D_NOPE)),
                  _resident((N_HEADS, D_NOPE, KV_RANK))],
        out_specs=[_resident((N_HEADS, n_tok, KV_RANK)), _resident((N_HEADS, n_tok, D_ROPE))],
        out_shape=[jax.ShapeDtypeStruct((N_HEADS, n_tok, KV_RANK), bf16),
                   jax.ShapeDtypeStruct((N_HEADS, n_tok, D_ROPE), bf16)],
        compiler_params=_cparams(1),
        name="q_proj_sample",
    )(qlat, wts["w_uq"], cos2, sin2, wts["q_nope_g"], wts["q_rope_g"], wts["k_nope_g"],
      wts["w_uk_t"])


def _attn_prompt_body(qt_ref, k_ref, vt_ref, gm_ref, o_ref, m_ref, acc_ref, s_ref, mx_ref, *, tq, tk):
    qi = pl.program_id(1)
    m_ref[...] = jnp.full(m_ref.shape, NEG_INF, f32)
    acc_ref[...] = jnp.zeros(acc_ref.shape, f32)

    def all_sublanes_max(x):
        for shift in (4, 2, 1):
            x = jnp.maximum(x, pltpu.roll(x, shift, 0))
        return x

    def scores(h, j, buf, masked):
        row0 = pl.multiple_of(j * tk, tk)
        kb = k_ref[pl.ds(row0, tk), h * D_QK:(h + 1) * D_QK]
        s = _dot(kb, qt_ref[0, h * D_QK:(h + 1) * D_QK, :])
        if masked:
            k_chunk = lax.broadcasted_iota(jnp.int32, (tk, tq), 0) // CHUNK
            q_chunk = lax.broadcasted_iota(jnp.int32, (tk, tq), 1) // CHUNK
            s = jnp.where(k_chunk <= q_chunk, s, NEG_INF)
        s_ref[buf, h] = s
        mx_ref[buf, h] = all_sublanes_max(jnp.max(s.reshape(tk // 8, 8, tq), axis=0))

    def values(h, j, buf):
        m_old = m_ref[h]
        m_new = jnp.maximum(m_old, mx_ref[buf, h])
        alpha = jnp.exp2(m_old - m_new)
        p = jnp.exp2(s_ref[buf, h].reshape(tk // 8, 8, tq) - m_new[None]).reshape(tk, tq)
        m_ref[h] = m_new
        acc = acc_ref[h].reshape(D_VX // 8, 8, tq) * alpha[None]
        acc_ref[h] = acc.reshape(D_VX, tq) + _dot(vt_ref[0, j, h], p.astype(bf16))

    for h in range(N_HEADS):
        scores(h, qi, 0, True)

    def step(t, buf):
        prev = jnp.where(t == 1, qi, t - 2)
        for h in range(N_HEADS):
            scores(h, t - 1, buf, False)
            values(h, prev, 1 - buf)

    def step_pair(u, carry):
        step(2 * u + 1, 1)
        step(2 * u + 2, 0)
        return carry

    lax.fori_loop(0, qi // 2, step_pair, 0)

    @pl.when(qi % 2 == 1)
    def _():
        step(qi, 1)
        for h in range(N_HEADS):
            values(h, qi - 1, 1)

    @pl.when(qi % 2 == 0)
    def _():
        last = jnp.where(qi == 0, qi, qi - 1)
        for h in range(N_HEADS):
            values(h, last, 0)

    for h in range(N_HEADS):
        hs = slice(h * D_V, (h + 1) * D_V)
        o = (acc_ref[h, 0:D_V, :] / acc_ref[h, D_V:D_V + 1, :]).T
        o_ref[:, hs] = (o * gm_ref[:, hs].astype(f32)).astype(bf16)


def _attn_prompt(qt, k, vt, gm, *, n_seq, seq_len, tq):
    tk = tq
    nq = seq_len // tq
    body = functools.partial(_attn_prompt_body, tq=tq, tk=tk)
    return pl.pallas_call(
        body,
        grid=(n_seq, nq),
        in_specs=[pl.BlockSpec((1, N_HEADS * D_QK, tq), lambda b, i: (b, 0, i)),
                  pl.BlockSpec((seq_len, N_HEADS * D_QK), lambda b, i: (b, 0)),
                  pl.BlockSpec((1, seq_len // tk, N_HEADS, D_VX, tk), lambda b, i: (b, 0, 0, 0, 0)),
                  pl.BlockSpec((tq, D_MLA), lambda b, i: (b * nq + i, 0))],
        out_specs=pl.BlockSpec((tq, D_MLA), lambda b, i: (b * nq + i, 0)),
        out_shape=jax.ShapeDtypeStruct((n_seq * seq_len, D_MLA), bf16),
        scratch_shapes=[pltpu.VMEM((N_HEADS, 8, tq), f32), pltpu.VMEM((N_HEADS, D_VX, tq), f32),
                        pltpu.VMEM((2, N_HEADS, tk, tq), f32), pltpu.VMEM((2, N_HEADS, 8, tq), f32)],
        compiler_params=_cparams(2),
        name="attn_prompt",
    )(qt, k, vt, gm)


def _attn_sample_body(cckv_ref, ckr_t_ref, nckv_ref, nkr_ref, qa_ref, qr_ref, gm_ref, wukt_ref, wuv_ref,
                      o_ref, c_ref, ktail_ref, s_ref, *, past, t_new, chunks, n_b):
    n_tail = c_ref.shape[1] - past
    rows = N_HEADS * t_new
    c_ref[:, past:past + t_new, :] = nckv_ref[...].astype(bf16)
    c_ref[:, past + t_new:, :] = jnp.zeros((n_b, n_tail - t_new, KV_RANK), bf16)
    ktail_ref[:, 0:t_new, :] = nkr_ref[...].astype(bf16)
    ktail_ref[:, t_new:, :] = jnp.zeros((n_b, n_tail - t_new, D_ROPE), bf16)

    def chunk_scores(b, start, size):
        cached = start < past
        qa = qa_ref[:, b * t_new:(b + 1) * t_new, :].reshape(rows, KV_RANK)
        qr = qr_ref[:, b * t_new:(b + 1) * t_new, :].reshape(rows, D_ROPE)
        if cached:
            c32 = cckv_ref[b, start:start + size, :]
            c_ref[b, start:start + size, :] = c32.astype(bf16)
            c_t = c32.T.astype(bf16)
            k_t = _dot(wukt_ref[...], c_t)
            s_nope = _dot(qa, c_t)
            s_rope = _dot(qr, ckr_t_ref[b, :, start:start + size].astype(bf16))
        else:
            c = c_ref[b, start:start + size, :]
            k_t = _dot_nt(wukt_ref[...], c)
            s_nope = _dot_nt(qa, c)
            s_rope = _dot_nt(qr, ktail_ref[b])
        ssq = jnp.sum((k_t * k_t).reshape(N_HEADS, D_NOPE, size), axis=1)
        r = lax.rsqrt(ssq * (1.0 / D_NOPE) + EPS)
        for h in range(N_HEADS):
            hs = slice(h * t_new, (h + 1) * t_new)
            s = (s_nope[hs] * r[h:h + 1, :] + s_rope[hs]) * ATTN_SCALE
            if not cached:
                key = lax.broadcasted_iota(jnp.int32, (t_new, size), 1)
                s = jnp.where(key < t_new, s, NEG_INF)
            s_ref[b, hs, start:start + size] = s

    def chunk_values(b, start, size, state):
        m, l, acc = state
        s = s_ref[b, :, start:start + size]
        m_new = jnp.maximum(m, jnp.max(s, axis=-1, keepdims=True))
        alpha = jnp.exp(m - m_new)
        p = jnp.exp(s - m_new)
        l = alpha * l + jnp.sum(p, axis=-1, keepdims=True)
        acc = alpha * acc + _dot(p.astype(bf16), c_ref[b, start:start + size, :])
        return m_new, l, acc

    def finish(b, state):
        _, l, acc = state
        o_lat = (acc / l).astype(bf16)
        for h in range(N_HEADS):
            o = _dot(o_lat[h * t_new:(h + 1) * t_new], wuv_ref[h])
            hs = slice(h * D_V, (h + 1) * D_V)
            bs = slice(b * t_new, (b + 1) * t_new)
            o_ref[bs, hs] = (o * gm_ref[bs, hs].astype(f32)).astype(bf16)

    items = [(b,) + ch for b in range(n_b) for ch in chunks]
    init = (jnp.full((rows, 1), NEG_INF, f32), jnp.zeros((rows, 1), f32),
            jnp.zeros((rows, KV_RANK), f32))
    state = init
    chunk_scores(*items[0])
    for nxt, cur in zip(items[1:] + [None], items):
        if nxt is not None:
            chunk_scores(*nxt)
        state = chunk_values(*cur, state)
        if nxt is None or nxt[0] != cur[0]:
            finish(cur[0], state)
            state = init


def _attn_sample(cache_ckv, cache_krope_t, ckv_new, krope_new, q_abs, q_rope, gm, wts, *, t_new, n_b):
    n_seq, past, _ = cache_ckv.shape
    chunk = 512
    chunks = tuple((s, min(chunk, past - s)) for s in range(0, past, chunk)) + ((past, LANES),)
    s_pad = past + LANES
    body = functools.partial(_attn_sample_body, past=past, t_new=t_new, chunks=chunks, n_b=n_b)
    return pl.pallas_call(
        body,
        grid=(n_seq // n_b,),
        in_specs=[pl.BlockSpec((n_b, past, KV_RANK), lambda b: (b, 0, 0)),
                  pl.BlockSpec((n_b, D_ROPE, past), lambda b: (b, 0, 0)),
                  pl.BlockSpec((n_b, t_new, KV_RANK), lambda b: (b, 0, 0)),
                  pl.BlockSpec((n_b, t_new, D_ROPE), lambda b: (b, 0, 0)),
                  pl.BlockSpec((N_HEADS, n_b * t_new, KV_RANK), lambda b: (0, b, 0)),
                  pl.BlockSpec((N_HEADS, n_b * t_new, D_ROPE), lambda b: (0, b, 0)),
                  pl.BlockSpec((n_b * t_new, D_MLA), lambda b: (b, 0)),
                  _resident((N_HEADS * D_NOPE, KV_RANK)),
                  _resident((N_HEADS, KV_RANK, D_V))],
        out_specs=pl.BlockSpec((n_b * t_new, D_MLA), lambda b: (b, 0)),
        out_shape=jax.ShapeDtypeStruct((n_seq * t_new, D_MLA), bf16),
        scratch_shapes=[pltpu.VMEM((n_b, s_pad, KV_RANK), bf16),
                        pltpu.VMEM((n_b, s_pad - past, D_ROPE), bf16),
                        pltpu.VMEM((n_b, N_HEADS * t_new, s_pad), f32)],
        compiler_params=_cparams(1),
        name="attn_sample",
    )(cache_ckv, cache_krope_t, ckv_new, krope_new, q_abs, q_rope, gm,
      wts["w_uk_t"].reshape(N_HEADS * D_NOPE, KV_RANK), wts["w_uv3"])


def _out_proj_body(x_ref, pa_ref, ma_ref, p_ref, wo_ref, png_ref, wg_ref, bg_ref, wple_ref, y_ref):
    h = x_ref[...] + _dot(pa_ref[...], wo_ref[0:D_POOL, :]) + _dot(ma_ref[...], wo_ref[D_POOL:, :])
    hn = (h * _rms(h, D_MODEL) * png_ref[...]).astype(bf16)
    gate = jax.nn.sigmoid(_dot(hn, wg_ref[...]) + bg_ref[...])
    y_ref[...] = h + gate * _dot(p_ref[...].astype(bf16), wple_ref[...])


def _out_proj(x2d, pool_act, mla_act, p2d, wts, *, tm):
    n_tok = x2d.shape[0]
    row = lambda i: (i, 0)
    return pl.pallas_call(
        _out_proj_body,
        grid=(n_tok // tm,),
        in_specs=[pl.BlockSpec((tm, D_MODEL), row), pl.BlockSpec((tm, D_POOL), row),
                  pl.BlockSpec((tm, D_MLA), row), pl.BlockSpec((tm, D_PLE), row),
                  _resident((D_MODEL, D_MODEL)), _resident((1, D_MODEL)),
                  _resident((D_MODEL, D_MODEL)), _resident((1, D_MODEL)),
                  _resident((D_PLE, D_MODEL))],
        out_specs=pl.BlockSpec((tm, D_MODEL), row),
        out_shape=jax.ShapeDtypeStruct((n_tok, D_MODEL), f32),
        compiler_params=_cparams(1),
        name="out_proj",
    )(x2d, pool_act, mla_act, p2d, wts["w_out"], wts["ple_norm_g"], wts["w_ple_gate"],
      wts["b_ple_gate"], wts["w_ple"])


def _rope_tables(pos0, t, reps=1):
    pos = (pos0 + np.arange(t)).astype(np.float64)
    inv = ROPE_THETA ** (-(np.arange(0, D_ROPE, 2, dtype=np.float64) / D_ROPE))
    ang = pos[:, None] * inv[None, :]
    cos, sin = np.cos(ang), np.sin(ang)
    zero = np.zeros((t, LANES - D_ROPE))
    rows = (np.concatenate([cos, cos, zero], axis=-1), np.concatenate([-sin, sin, zero], axis=-1),
            pos[:, None])
    as_f32 = lambda a: jnp.asarray(a.astype(np.float32))
    return tuple(as_f32(np.tile(a, (reps, 1))) for a in rows) + ((as_f32(cos.T), as_f32(sin.T)),)


def _prep_weights(norm_g, w_in, q_norm_g, w_uq, kv_norm_g, w_ukv, q_nope_g, q_rope_g, k_nope_g,
                  k_rope_g, w_pool, pool_scale, w_out, ple_norm_g, w_ple_gate, b_ple_gate, w_ple):
    w_uq_r = jnp.pad(w_uq.reshape(Q_RANK, N_HEADS, D_NOPE + D_ROPE),
                     ((0, 0), (0, 0), (0, D_QK - D_NOPE - D_ROPE))).reshape(Q_RANK, N_HEADS * D_QK)
    w_ukv3 = w_ukv.reshape(KV_RANK, N_HEADS, D_NOPE + D_V)
    w_uk3, w_uv3 = w_ukv3[..., :D_NOPE], w_ukv3[..., D_NOPE:]
    pad_rope = lambda g: jnp.pad(g, (0, LANES - D_ROPE))[None, :]
    return {
        "norm_g": norm_g[None, :], "w_in_t_f32": w_in.T, "q_norm_g": q_norm_g[None, :],
        "kv_norm_g": kv_norm_g[None, :], "k_rope_g": pad_rope(k_rope_g),
        "w_pool": w_pool.astype(bf16), "pool_scale": pool_scale[None, :],
        "w_uq": w_uq_r.astype(bf16), "q_nope_g": q_nope_g[None, :], "q_rope_g": pad_rope(q_rope_g),
        "w_uq_t": w_uq.T.astype(bf16),
        "q_nope_g_col": q_nope_g[:, None], "q_rope_g_col": q_rope_g[:, None],
        "k_nope_g": k_nope_g[None, :],
        "w_uk": w_uk3.reshape(KV_RANK, N_HEADS * D_NOPE).astype(bf16),
        "w_uv_t": w_uv3.reshape(KV_RANK, D_MLA).T.astype(bf16),
        "w_uk_t": jnp.transpose(w_uk3, (1, 2, 0)).astype(bf16),
        "w_uv3": jnp.transpose(w_uv3, (1, 0, 2)).astype(bf16),
        "w_out_f32": w_out, "w_ple_gate_f32": w_ple_gate,
        "ple_norm_g": ple_norm_g[None, :], "b_ple_gate": b_ple_gate[None, :],
        "w_ple": w_ple.astype(bf16),
    }


def _layer_prompt(x, p, wts):
    n_seq, seq_len, _ = x.shape
    x2d = x.reshape(n_seq * seq_len, D_MODEL)
    cos2, sin2, pos, rope_t = _rope_tables(0, seq_len)
    hist = jnp.zeros((1, HALO, D_POOL), f32)
    pact, gm, ckv, krope, npool, qt, k, vt, w_out16, w_gate16, w_in_t16 = _in_proj(
        x2d, hist, cos2, sin2, pos, wts, nseg=1, seg_len=PROMPT_TILE, n_seq=n_seq, carry=True,
        rope_t=rope_t, kv_block=ATTN_BLOCK)
    wts.update(w_out=w_out16, w_ple_gate=w_gate16, w_in_t=w_in_t16)
    mact = _attn_prompt(qt, k, vt, gm, n_seq=n_seq, seq_len=seq_len, tq=ATTN_BLOCK)
    y = _out_proj(x2d, pact, mact, p.reshape(n_seq * seq_len, D_PLE), wts, tm=OUT_TILE)
    return (y.reshape(x.shape), ckv.reshape(n_seq, seq_len, KV_RANK),
            krope.reshape(n_seq, seq_len, D_ROPE), npool)


def _layer_sample(x, p, state_pool, cache_ckv, cache_krope, wts):
    n_seq, t_new, _ = x.shape
    past = cache_ckv.shape[1]
    n_tok = n_seq * t_new
    x2d = x.reshape(n_tok, D_MODEL)
    cos2, sin2, pos, _ = _rope_tables(past, t_new, reps=n_seq)
    hist = jnp.pad(state_pool, ((0, 0), (HALO - POOL_HIST, 0), (0, 0)))
    pact, gm, ckv, krope, npool, qlat = _in_proj(
        x2d, hist, cos2, sin2, pos, wts, nseg=n_seq, seg_len=t_new, n_seq=n_seq, carry=False)
    q_abs, q_rope = _q_proj_sample(qlat, cos2, sin2, wts)
    ckv3 = ckv.reshape(n_seq, t_new, KV_RANK)
    krope3 = krope.reshape(n_seq, t_new, D_ROPE)
    cache_krope_t = jnp.transpose(cache_krope, (0, 2, 1))
    mact = _attn_sample(cache_ckv, cache_krope_t, ckv3, krope3, q_abs, q_rope, gm, wts, t_new=t_new,
                        n_b=SAMPLE_BATCH_PER_STEP)
    y = _out_proj(x2d, pact, mact, p.reshape(n_tok, D_PLE), wts, tm=OUT_TILE)
    return y.reshape(x.shape), ckv3, krope3, npool


def kernel(x_prompt, x_sample, cache_ckv, cache_krope, state_pool, p_prompt, p_sample, norm_g, w_in,
           q_norm_g, w_uq, kv_norm_g, w_ukv, q_nope_g, q_rope_g, k_nope_g, k_rope_g, w_pool,
           pool_scale, w_out, ple_norm_g, w_ple_gate, b_ple_gate, w_ple):
    depth = norm_g.shape[0]
    layer_w = (norm_g, w_in, q_norm_g, w_uq, kv_norm_g, w_ukv, q_nope_g, q_rope_g, k_nope_g, k_rope_g,
               w_pool, pool_scale, w_out, ple_norm_g, w_ple_gate, b_ple_gate, w_ple)
    yp, ys = x_prompt, x_sample
    outs = [[] for _ in range(6)]
    for i in range(depth):
        wts = _prep_weights(*(w[i] for w in layer_w))
        yp, c1, k1, s1 = _layer_prompt(yp, p_prompt[i], wts)
        ys, c2, k2, s2 = _layer_sample(ys, p_sample[i], state_pool[i], cache_ckv[i], cache_krope[i], wts)
        for lst, val in zip(outs, (c1, k1, s1, c2, k2, s2)):
            lst.append(val)
    return (yp, ys) + tuple(jnp.stack(o) for o in outs)
```

```python
import functools

import jax
import jax.numpy as jnp
import numpy as np
from jax import lax
from jax.experimental import pallas as pl
from jax.experimental.pallas import tpu as pltpu

D_MODEL = 2048
CHUNK = 64
D_POOL = 1024
POOL_WINDOWS = (2, 4, 8, 16)
POOL_GROUP = 256
POOL_HIST = 15
HALO = 16
N_HEADS = 8
D_NOPE = 128
D_ROPE = 64
D_V = 128
D_VX = D_V + 16
D_MLA = N_HEADS * D_V
Q_RANK = 512
KV_RANK = 256
D_PLE = 256
D_QK = 256
ROPE_THETA = 10000.0
EPS = 1e-6
ATTN_SCALE = (D_NOPE + D_ROPE) ** -0.5
NEG_INF = -1e30
EXP2_SCALE = ATTN_SCALE * float(np.log2(np.e))
LANES = 128

C_U, C_GP, C_Q, C_KV, C_KR, C_GM = 0, 1024, 2048, 2560, 2816, 2880
D_IN = 3904

VMEM_LIMIT = 56 * 1024 * 1024
PROMPT_TILE = 256
W_CHUNK = 256
ATTN_BLOCK = 256
OUT_TILE = 512
SAMPLE_BATCH_PER_STEP = 4
SAMPLE_SEGS_PER_TILE = 16

f32 = jnp.float32
bf16 = jnp.bfloat16


def _cparams(n_axes):
    return pltpu.CompilerParams(dimension_semantics=("arbitrary",) * n_axes,
                                vmem_limit_bytes=VMEM_LIMIT)


def _resident(shape):
    nd = len(shape)
    return pl.BlockSpec(shape, lambda *_: (0,) * nd, pipeline_mode=pl.Buffered(1))


def _rms(x, n):
    return lax.rsqrt(jnp.sum(x * x, axis=-1, keepdims=True) * (1.0 / n) + EPS)


def _rms_cols(x_t, n):
    return lax.rsqrt(jnp.sum(x_t * x_t, axis=0, keepdims=True) * (1.0 / n) + EPS)


def _dot(a, b):
    return jnp.dot(a, b, preferred_element_type=f32)


def _dot_nt(a, b):
    return lax.dot_general(a, b, (((1,), (1,)), ((), ())), preferred_element_type=f32)


def _rope128(x, cos2, sin2):
    lane = lax.broadcasted_iota(jnp.int32, x.shape, 1)
    swapped = jnp.where(lane < D_ROPE // 2, pltpu.roll(x, LANES - D_ROPE // 2, 1),
                        pltpu.roll(x, D_ROPE // 2, 1))
    return x * cos2 + swapped * sin2


def _queries_t(ql, wqt_ref, cos_ref, sin_ref, qng_ref, qrg_ref, qt_ref):
    q_t = _dot_nt(wqt_ref[...], ql)
    tm = q_t.shape[1]
    cos, sin = cos_ref[...], sin_ref[...]
    half = D_ROPE // 2
    for h in range(N_HEADS):
        r0 = h * D_QK
        s0 = h * (D_NOPE + D_ROPE)
        qn = q_t[s0:s0 + D_NOPE]
        qn = qn * (_rms_cols(qn, D_NOPE) * EXP2_SCALE) * qng_ref[...]
        qt_ref[0, r0:r0 + D_NOPE, :] = qn.astype(bf16)
        qr = q_t[s0 + D_NOPE:s0 + D_NOPE + D_ROPE]
        qr = qr * (_rms_cols(qr, D_ROPE) * EXP2_SCALE) * qrg_ref[...]
        x1, x2 = qr[:half], qr[half:]
        qt_ref[0, r0 + D_NOPE:r0 + D_NOPE + half, :] = (x1 * cos - x2 * sin).astype(bf16)
        qt_ref[0, r0 + D_NOPE + half:r0 + D_NOPE + D_ROPE, :] = (x2 * cos + x1 * sin).astype(bf16)
        qt_ref[0, r0 + D_NOPE + D_ROPE:r0 + D_QK, :] = jnp.zeros((D_QK - D_NOPE - D_ROPE, tm), bf16)


def _queries_absorbed(ql, wq_ref, cos_ref, sin_ref, qng_ref, qrg_ref, kng_ref, wukt_ref, qa_ref, qr_ref):
    q = _dot(ql, wq_ref[...])
    cos2, sin2 = cos_ref[...], sin_ref[...]
    for h in range(N_HEADS):
        qn = q[:, h * D_QK:h * D_QK + D_NOPE]
        qn = qn * _rms(qn, D_NOPE) * qng_ref[...]
        qa_ref[h] = _dot((qn * kng_ref[...]).astype(bf16), wukt_ref[h]).astype(bf16)
        qr = q[:, h * D_QK + D_NOPE:(h + 1) * D_QK]
        qr = _rope128(qr * _rms(qr, D_ROPE) * qrg_ref[...], cos2, sin2)
        qr_ref[h] = qr[:, :D_ROPE].astype(bf16)


def _keys_values(c, kr128, wk_ref, wvt_ref, kng_ref, k_ref, vt_ref):
    tm = c.shape[0]
    k = _dot(c, wk_ref[...])
    kr = kr128.astype(bf16)
    for h in range(N_HEADS):
        kn = k[:, h * D_NOPE:(h + 1) * D_NOPE]
        k_ref[:, h * D_QK:h * D_QK + D_NOPE] = (kn * _rms(kn, D_NOPE) * kng_ref[...]).astype(bf16)
        k_ref[:, h * D_QK + D_NOPE:(h + 1) * D_QK] = kr
    v_t = _dot_nt(wvt_ref[...], c).astype(bf16)
    n_blk, tk = vt_ref.shape[1], vt_ref.shape[4]
    row = lax.broadcasted_iota(jnp.int32, (D_VX - D_V, tk), 0)
    ones_row = jnp.where(row == 0, 1.0, 0.0).astype(bf16)
    for j in range(n_blk):
        for h in range(N_HEADS):
            vt_ref[0, j, h, 0:D_V, :] = v_t[h * D_V:(h + 1) * D_V, j * tk:(j + 1) * tk]
            vt_ref[0, j, h, D_V:D_VX, :] = ones_row


def _in_proj_body(*refs, n_conv, **static):
    if not static["fuse_qkv"]:
        _in_proj_tile(pl.program_id(0), refs, **static)
        return
    i = pl.program_id(0)
    w32_ref, w16_ref, wt_ref = refs[6], refs[-5], refs[-1]

    def convert(rows):
        chunk = w32_ref[0:rows, :].astype(bf16)
        w16_ref[0:rows, :] = chunk
        wt_ref[pl.ds(pl.multiple_of(i * W_CHUNK, W_CHUNK), rows), :] = chunk

    last_rows = D_IN - (n_conv - 1) * W_CHUNK
    pl.when(i < n_conv - 1)(lambda: convert(W_CHUNK))
    pl.when(i == n_conv - 1)(lambda: convert(last_rows))

    @pl.when(i >= n_conv)
    def _():
        _in_proj_tile(i - n_conv, refs[:6] + (wt_ref,) + refs[7:-5] + refs[-4:-1], **static)


def _in_proj_tile(i, refs, *, nseg, seg_len, tiles_per_seq, carry, fuse_qkv):
    (x_ref, hist_ref, cos_ref, sin_ref, pos_ref, ng_ref, wt_ref, qg_ref, kvg_ref, krg_ref, wp_ref,
     ps_ref) = refs[:12]
    if fuse_qkv:
        (cost_ref, sint_ref, wqt_ref, qng_ref, qrg_ref, wk_ref, wvt_ref, kng_ref, x0_ref, wo32_ref,
         wg32_ref) = refs[12:23]
        (pact_ref, gm_ref, ckv_ref, kr_ref, npool_ref, qt_ref, k_ref, vt_ref, wo16_ref, wg16_ref,
         ucat_ref, xn_ref, xn_next_ref) = refs[23:]
    else:
        wq_ref, qng_ref, qrg_ref, kng_ref, wukt_ref = refs[12:17]
        pact_ref, gm_ref, ckv_ref, kr_ref, npool_ref, qa_ref, qr_ref, ucat_ref = refs[17:]
    tm = nseg * seg_len

    def normed(ref):
        x = ref[...]
        return (x * _rms(x, D_MODEL) * ng_ref[...]).astype(bf16)

    if fuse_qkv:
        @pl.when(i == 0)
        def _():
            xn_next_ref[...] = normed(x0_ref)
        xn_ref[...] = xn_next_ref[...]
        xn = xn_ref[...]
    else:
        xn = normed(x_ref)

    u = _dot_nt(xn, wt_ref[C_U:C_U + D_POOL, :])
    if carry:
        @pl.when(i % tiles_per_seq == 0)
        def _():
            ucat_ref[:, 0:HALO, :] = jnp.zeros((nseg, HALO, D_POOL), f32)
    else:
        ucat_ref[:, 0:HALO, :] = hist_ref[...]
    ucat_ref[:, HALO:HALO + seg_len, :] = u.reshape(nseg, seg_len, D_POOL)

    pos = pos_ref[...]
    gp = _dot_nt(xn, wt_ref[C_GP:C_GP + D_POOL, :])
    if fuse_qkv:
        xn_next_ref[...] = normed(x_ref)
    gate = gp * jax.nn.sigmoid(gp) * ps_ref[...]

    def pool_group(g):
        w = POOL_WINDOWS[g]
        sl = slice(g * POOL_GROUP, (g + 1) * POOL_GROUP)
        acc = ucat_ref[:, :, sl]
        shift = 1
        while shift < w:
            acc = acc + pltpu.roll(acc, shift, 1)
            shift *= 2
        acc = acc[:, HALO:, :].reshape(tm, POOL_GROUP)
        inv_cnt = 1.0 / jnp.minimum(pos + 1.0, float(w))
        d = (acc * inv_cnt - u[:, sl]).astype(bf16)
        mixed = _dot(d, wp_ref[g])
        pact_ref[:, sl] = (mixed * gate[:, sl]).astype(bf16)

    cq = _dot_nt(xn, wt_ref[C_Q:C_Q + Q_RANK, :])
    pool_group(0)
    pool_group(1)
    ql = (cq * _rms(cq, Q_RANK) * qg_ref[...]).astype(bf16)
    if fuse_qkv:
        _queries_t(ql, wqt_ref, cost_ref, sint_ref, qng_ref, qrg_ref, qt_ref)
    else:
        _queries_absorbed(ql, wq_ref, cos_ref, sin_ref, qng_ref, qrg_ref, kng_ref, wukt_ref, qa_ref, qr_ref)

    ckv = _dot_nt(xn, wt_ref[C_KV:C_KV + KV_RANK, :])
    pool_group(2)
    ckv = ckv * _rms(ckv, KV_RANK) * kvg_ref[...]
    ckv_ref[...] = ckv

    kr = _dot_nt(xn, wt_ref[C_KR:C_KR + LANES, :])
    gm = _dot_nt(xn, wt_ref[C_GM:C_GM + D_MLA, :])
    pool_group(3)
    npool_ref[...] = ucat_ref[:, seg_len + 1:seg_len + HALO, :]
    if carry:
        ucat_ref[:, 0:HALO, :] = ucat_ref[:, seg_len:seg_len + HALO, :]
    lane = lax.broadcasted_iota(jnp.int32, (tm, LANES), 1)
    kr = jnp.where(lane < D_ROPE, kr, 0.0)
    kr = _rope128(kr * _rms(kr, D_ROPE) * krg_ref[...], cos_ref[...], sin_ref[...])
    kr_ref[...] = kr[:, :D_ROPE]
    gm_ref[...] = (gm * jax.nn.sigmoid(gm)).astype(bf16)
    if fuse_qkv:
        _keys_values(ckv.astype(bf16), kr, wk_ref, wvt_ref, kng_ref, k_ref, vt_ref)
        wo16_ref[...] = wo32_ref[...].astype(bf16)
        wg16_ref[...] = wg32_ref[...].astype(bf16)


def _in_proj(x2d, hist, cos2, sin2, pos, wts, *, nseg, seg_len, n_seq, carry, rope_t=None, kv_block=None):
    n_tok = x2d.shape[0]
    tm = nseg * seg_len
    n_tiles = n_tok // tm
    tiles_per_seq = n_tiles // n_seq if carry else 1
    tab_tiles = cos2.shape[0] // tm
    fuse_qkv = rope_t is not None
    n_conv = pl.cdiv(D_IN, W_CHUNK) if fuse_qkv else 0
    tile = (lambda i: jnp.maximum(i - n_conv, 0)) if n_conv else (lambda i: i)
    row = lambda i: (tile(i), 0)
    tab = lambda i: (tile(i) % tab_tiles, 0)
    if carry:
        hist_spec = pl.BlockSpec((1, HALO, D_POOL), lambda i: (0, 0, 0))
        npool_spec = pl.BlockSpec((1, POOL_HIST, D_POOL), lambda i: (tile(i) // tiles_per_seq, 0, 0))
    else:
        hist_spec = pl.BlockSpec((nseg, HALO, D_POOL), lambda i: (tile(i), 0, 0))
        npool_spec = pl.BlockSpec((nseg, POOL_HIST, D_POOL), lambda i: (tile(i), 0, 0))
    if fuse_qkv:
        x_spec = pl.BlockSpec((tm, D_MODEL), lambda i: (jnp.minimum(tile(i) + 1, n_tiles - 1), 0))
        w_chunk = lambda i: (jnp.minimum(i, n_conv - 1), 0)
        w_spec, w_arg = pl.BlockSpec((W_CHUNK, D_MODEL), w_chunk), wts["w_in_t_f32"]
    else:
        x_spec = pl.BlockSpec((tm, D_MODEL), row)
        w_spec, w_arg = _resident((D_IN, D_MODEL)), wts["w_in_t"]
    in_specs = [
        x_spec, hist_spec, pl.BlockSpec((tm, LANES), tab),
        pl.BlockSpec((tm, LANES), tab), pl.BlockSpec((tm, 1), tab), _resident((1, D_MODEL)),
        w_spec, _resident((1, Q_RANK)), _resident((1, KV_RANK)),
        _resident((1, LANES)), _resident((4, POOL_GROUP, POOL_GROUP)), _resident((1, D_POOL)),
    ]
    args = [x2d, hist, cos2, sin2, pos, wts["norm_g"], w_arg, wts["q_norm_g"],
            wts["kv_norm_g"], wts["k_rope_g"], wts["w_pool"], wts["pool_scale"]]
    out_specs = [pl.BlockSpec((tm, D_POOL), row), pl.BlockSpec((tm, D_MLA), row),
                 pl.BlockSpec((tm, KV_RANK), row), pl.BlockSpec((tm, D_ROPE), row), npool_spec]
    out_shape = [jax.ShapeDtypeStruct((n_tok, D_POOL), bf16), jax.ShapeDtypeStruct((n_tok, D_MLA), bf16),
                 jax.ShapeDtypeStruct((n_tok, KV_RANK), f32), jax.ShapeDtypeStruct((n_tok, D_ROPE), f32),
                 jax.ShapeDtypeStruct((n_seq, POOL_HIST, D_POOL), f32)]
    scratch = [pltpu.VMEM((nseg, HALO + seg_len, D_POOL), f32)]
    if fuse_qkv:
        seq_tab = lambda i: (0, tile(i) % tiles_per_seq)
        seq_blk = lambda i: (tile(i) // tiles_per_seq, 0, tile(i) % tiles_per_seq)
        in_specs += [pl.BlockSpec((D_ROPE // 2, tm), seq_tab), pl.BlockSpec((D_ROPE // 2, tm), seq_tab),
                     _resident((N_HEADS * (D_NOPE + D_ROPE), Q_RANK)), _resident((D_NOPE, 1)),
                     _resident((D_ROPE, 1)),
                     _resident((KV_RANK, N_HEADS * D_NOPE)), _resident((D_MLA, KV_RANK)),
                     _resident((1, D_NOPE)),
                     pl.BlockSpec((tm, D_MODEL), lambda i: (0, 0), pipeline_mode=pl.Buffered(1)),
                     pl.BlockSpec((D_MODEL // n_tiles, D_MODEL), row),
                     pl.BlockSpec((D_MODEL // n_tiles, D_MODEL), row)]
        args += [rope_t[0], rope_t[1], wts["w_uq_t"], wts["q_nope_g_col"], wts["q_rope_g_col"],
                 wts["w_uk"], wts["w_uv_t"], wts["k_nope_g"], x2d, wts["w_out_f32"], wts["w_ple_gate_f32"]]
        out_specs += [
            pl.BlockSpec((1, N_HEADS * D_QK, tm), seq_blk),
            pl.BlockSpec((tm, N_HEADS * D_QK), row),
            pl.BlockSpec((1, tm // kv_block, N_HEADS, D_VX, kv_block), lambda i: seq_blk(i)[::2] + (0, 0, 0)),
            pl.BlockSpec((D_MODEL // n_tiles, D_MODEL), row),
            pl.BlockSpec((D_MODEL // n_tiles, D_MODEL), row),
            pl.BlockSpec((W_CHUNK, D_MODEL), w_chunk)]
        out_shape += [jax.ShapeDtypeStruct((n_seq, N_HEADS * D_QK, tiles_per_seq * tm), bf16),
                      jax.ShapeDtypeStruct((n_tok, N_HEADS * D_QK), bf16),
                      jax.ShapeDtypeStruct((n_seq, tiles_per_seq * tm // kv_block, N_HEADS, D_VX, kv_block),
                                           bf16),
                      jax.ShapeDtypeStruct((D_MODEL, D_MODEL), bf16),
                      jax.ShapeDtypeStruct((D_MODEL, D_MODEL), bf16),
                      jax.ShapeDtypeStruct((D_IN, D_MODEL), bf16)]
        scratch += [pltpu.VMEM((tm, D_MODEL), bf16), pltpu.VMEM((tm, D_MODEL), bf16),
                    pltpu.VMEM((n_conv * W_CHUNK, D_MODEL), bf16)]
    else:
        in_specs += [_resident((Q_RANK, N_HEADS * D_QK)), _resident((1, D_NOPE)), _resident((1, LANES)),
                     _resident((1, D_NOPE)), _resident((N_HEADS, D_NOPE, KV_RANK))]
        args += [wts["w_uq"], wts["q_nope_g"], wts["q_rope_g"], wts["k_nope_g"], wts["w_uk_t"]]
        out_specs += [pl.BlockSpec((N_HEADS, tm, KV_RANK), lambda i: (0, i, 0)),
                      pl.BlockSpec((N_HEADS, tm, D_ROPE), lambda i: (0, i, 0))]
        out_shape += [jax.ShapeDtypeStruct((N_HEADS, n_tok, KV_RANK), bf16),
                      jax.ShapeDtypeStruct((N_HEADS, n_tok, D_ROPE), bf16)]
    body = functools.partial(_in_proj_body, n_conv=n_conv, nseg=nseg, seg_len=seg_len,
                             tiles_per_seq=tiles_per_seq, carry=carry, fuse_qkv=fuse_qkv)
    return pl.pallas_call(
        body,
        grid=(n_conv + n_tiles,),
        in_specs=in_specs,
        out_specs=out_specs,
        out_shape=out_shape,
        scratch_shapes=scratch,
        compiler_params=_cparams(1),
        name="in_proj",
    )(*args)


def _attn_prompt_body(qt_ref, k_ref, vt_ref, gm_ref, o_ref, m_ref, acc_ref, s_ref, mx_ref, *, tq, tk):
    qi = pl.program_id(1)
    m_ref[...] = jnp.full(m_ref.shape, NEG_INF, f32)
    acc_ref[...] = jnp.zeros(acc_ref.shape, f32)

    def all_sublanes_max(x):
        for shift in (4, 2, 1):
            x = jnp.maximum(x, pltpu.roll(x, shift, 0))
        return x

    def scores(h, j, buf, masked):
        row0 = pl.multiple_of(j * tk, tk)
        kb = k_ref[pl.ds(row0, tk), h * D_QK:(h + 1) * D_QK]
        s = _dot(kb, qt_ref[0, h * D_QK:(h + 1) * D_QK, :])
        if masked:
            k_chunk = lax.broadcasted_iota(jnp.int32, (tk, tq), 0) // CHUNK
            q_chunk = lax.broadcasted_iota(jnp.int32, (tk, tq), 1) // CHUNK
            s = jnp.where(k_chunk <= q_chunk, s, NEG_INF)
        s_ref[buf, h] = s
        mx_ref[buf, h] = all_sublanes_max(jnp.max(s.reshape(tk // 8, 8, tq), axis=0))

    def values(h, j, buf):
        m_old = m_ref[h]
        m_new = jnp.maximum(m_old, mx_ref[buf, h])
        alpha = jnp.exp2(m_old - m_new)
        p = jnp.exp2(s_ref[buf, h].reshape(tk // 8, 8, tq) - m_new[None]).reshape(tk, tq)
        m_ref[h] = m_new
        acc = acc_ref[h].reshape(D_VX // 8, 8, tq) * alpha[None]
        acc_ref[h] = acc.reshape(D_VX, tq) + _dot(vt_ref[0, j, h], p.astype(bf16))

    for h in range(N_HEADS):
        scores(h, qi, 0, True)

    def step(t, buf):
        prev = jnp.where(t == 1, qi, t - 2)
        for h in range(N_HEADS):
            scores(h, t - 1, buf, False)
            values(h, prev, 1 - buf)

    def step_pair(u, carry):
        step(2 * u + 1, 1)
        step(2 * u + 2, 0)
        return carry

    lax.fori_loop(0, qi // 2, step_pair, 0)

    @pl.when(qi % 2 == 1)
    def _():
        step(qi, 1)
        for h in range(N_HEADS):
            values(h, qi - 1, 1)

    @pl.when(qi % 2 == 0)
    def _():
        last = jnp.where(qi == 0, qi, qi - 1)
        for h in range(N_HEADS):
            values(h, last, 0)

    for h in range(N_HEADS):
        hs = slice(h * D_V, (h + 1) * D_V)
        o = (acc_ref[h, 0:D_V, :] / acc_ref[h, D_V:D_V + 1, :]).T
        o_ref[:, hs] = (o * gm_ref[:, hs].astype(f32)).astype(bf16)


def _attn_prompt(qt, k, vt, gm, *, n_seq, seq_len, tq):
    tk = tq
    nq = seq_len // tq
    body = functools.partial(_attn_prompt_body, tq=tq, tk=tk)
    return pl.pallas_call(
        body,
        grid=(n_seq, nq),
        in_specs=[pl.BlockSpec((1, N_HEADS * D_QK, tq), lambda b, i: (b, 0, i)),
                  pl.BlockSpec((seq_len, N_HEADS * D_QK), lambda b, i: (b, 0)),
                  pl.BlockSpec((1, seq_len // tk, N_HEADS, D_VX, tk), lambda b, i: (b, 0, 0, 0, 0)),
                  pl.BlockSpec((tq, D_MLA), lambda b, i: (b * nq + i, 0))],
        out_specs=pl.BlockSpec((tq, D_MLA), lambda b, i: (b * nq + i, 0)),
        out_shape=jax.ShapeDtypeStruct((n_seq * seq_len, D_MLA), bf16),
        scratch_shapes=[pltpu.VMEM((N_HEADS, 8, tq), f32), pltpu.VMEM((N_HEADS, D_VX, tq), f32),
                        pltpu.VMEM((2, N_HEADS, tk, tq), f32), pltpu.VMEM((2, N_HEADS, 8, tq), f32)],
        compiler_params=_cparams(2),
        name="attn_prompt",
    )(qt, k, vt, gm)


def _attn_sample_body(cckv_ref, ckr_t_ref, nckv_ref, nkr_ref, qa_ref, qr_ref, gm_ref, wukt_ref, wuv_ref,
                      o_ref, c_ref, ktail_ref, s_ref, *, past, t_new, chunks, n_b):
    n_tail = c_ref.shape[1] - past
    rows = N_HEADS * t_new
    c_ref[:, past:past + t_new, :] = nckv_ref[...].astype(bf16)
    c_ref[:, past + t_new:, :] = jnp.zeros((n_b, n_tail - t_new, KV_RANK), bf16)
    ktail_ref[:, 0:t_new, :] = nkr_ref[...].astype(bf16)
    ktail_ref[:, t_new:, :] = jnp.zeros((n_b, n_tail - t_new, D_ROPE), bf16)

    def chunk_scores(b, start, size):
        cached = start < past
        qa = qa_ref[:, b * t_new:(b + 1) * t_new, :].reshape(rows, KV_RANK)
        qr = qr_ref[:, b * t_new:(b + 1) * t_new, :].reshape(rows, D_ROPE)
        if cached:
            c32 = cckv_ref[b, start:start + size, :]
            c_ref[b, start:start + size, :] = c32.astype(bf16)
            c_t = c32.T.astype(bf16)
            k_t = _dot(wukt_ref[...], c_t)
            s_nope = _dot(qa, c_t)
            s_rope = _dot(qr, ckr_t_ref[b, :, start:start + size].astype(bf16))
        else:
            c = c_ref[b, start:start + size, :]
            k_t = _dot_nt(wukt_ref[...], c)
            s_nope = _dot_nt(qa, c)
            s_rope = _dot_nt(qr, ktail_ref[b])
        ssq = jnp.sum((k_t * k_t).reshape(N_HEADS, D_NOPE, size), axis=1)
        r = lax.rsqrt(ssq * (1.0 / D_NOPE) + EPS)
        for h in range(N_HEADS):
            hs = slice(h * t_new, (h + 1) * t_new)
            s = (s_nope[hs] * r[h:h + 1, :] + s_rope[hs]) * ATTN_SCALE
            if not cached:
                key = lax.broadcasted_iota(jnp.int32, (t_new, size), 1)
                s = jnp.where(key < t_new, s, NEG_INF)
            s_ref[b, hs, start:start + size] = s

    def chunk_values(b, start, size, state):
        m, l, acc = state
        s = s_ref[b, :, start:start + size]
        m_new = jnp.maximum(m, jnp.max(s, axis=-1, keepdims=True))
        alpha = jnp.exp(m - m_new)
        p = jnp.exp(s - m_new)
        l = alpha * l + jnp.sum(p, axis=-1, keepdims=True)
        acc = alpha * acc + _dot(p.astype(bf16), c_ref[b, start:start + size, :])
        return m_new, l, acc

    def finish(b, state):
        _, l, acc = state
        o_lat = (acc / l).astype(bf16)
        for h in range(N_HEADS):
            o = _dot(o_lat[h * t_new:(h + 1) * t_new], wuv_ref[h])
            hs = slice(h * D_V, (h + 1) * D_V)
            bs = slice(b * t_new, (b + 1) * t_new)
            o_ref[bs, hs] = (o * gm_ref[bs, hs].astype(f32)).astype(bf16)

    items = [(b,) + ch for b in range(n_b) for ch in chunks]
    init = (jnp.full((rows, 1), NEG_INF, f32), jnp.zeros((rows, 1), f32),
            jnp.zeros((rows, KV_RANK), f32))
    state = init
    chunk_scores(*items[0])
    for nxt, cur in zip(items[1:] + [None], items):
        if nxt is not None:
            chunk_scores(*nxt)
        state = chunk_values(*cur, state)
        if nxt is None or nxt[0] != cur[0]:
            finish(cur[0], state)
            state = init


def _attn_sample(cache_ckv, cache_krope_t, ckv_new, krope_new, q_abs, q_rope, gm, wts, *, t_new, n_b):
    n_seq, past, _ = cache_ckv.shape
    chunk = 512
    chunks = tuple((s, min(chunk, past - s)) for s in range(0, past, chunk)) + ((past, LANES),)
    s_pad = past + LANES
    body = functools.partial(_attn_sample_body, past=past, t_new=t_new, chunks=chunks, n_b=n_b)
    return pl.pallas_call(
        body,
        grid=(n_seq // n_b,),
        in_specs=[pl.BlockSpec((n_b, past, KV_RANK), lambda b: (b, 0, 0)),
                  pl.BlockSpec((n_b, D_ROPE, past), lambda b: (b, 0, 0)),
                  pl.BlockSpec((n_b, t_new, KV_RANK), lambda b: (b, 0, 0)),
                  pl.BlockSpec((n_b, t_new, D_ROPE), lambda b: (b, 0, 0)),
                  pl.BlockSpec((N_HEADS, n_b * t_new, KV_RANK), lambda b: (0, b, 0)),
                  pl.BlockSpec((N_HEADS, n_b * t_new, D_ROPE), lambda b: (0, b, 0)),
                  pl.BlockSpec((n_b * t_new, D_MLA), lambda b: (b, 0)),
                  _resident((N_HEADS * D_NOPE, KV_RANK)),
                  _resident((N_HEADS, KV_RANK, D_V))],
        out_specs=pl.BlockSpec((n_b * t_new, D_MLA), lambda b: (b, 0)),
        out_shape=jax.ShapeDtypeStruct((n_seq * t_new, D_MLA), bf16),
        scratch_shapes=[pltpu.VMEM((n_b, s_pad, KV_RANK), bf16),
                        pltpu.VMEM((n_b, s_pad - past, D_ROPE), bf16),
                        pltpu.VMEM((n_b, N_HEADS * t_new, s_pad), f32)],
        compiler_params=_cparams(1),
        name="attn_sample",
    )(cache_ckv, cache_krope_t, ckv_new, krope_new, q_abs, q_rope, gm,
      wts["w_uk_t"].reshape(N_HEADS * D_NOPE, KV_RANK), wts["w_uv3"])


def _out_proj_body(x_ref, pa_ref, ma_ref, p_ref, wo_ref, png_ref, wg_ref, bg_ref, wple_ref, y_ref):
    h = x_ref[...] + _dot(pa_ref[...], wo_ref[0:D_POOL, :]) + _dot(ma_ref[...], wo_ref[D_POOL:, :])
    hn = (h * _rms(h, D_MODEL) * png_ref[...]).astype(bf16)
    gate = jax.nn.sigmoid(_dot(hn, wg_ref[...]) + bg_ref[...])
    y_ref[...] = h + gate * _dot(p_ref[...].astype(bf16), wple_ref[...])


def _out_proj(x2d, pool_act, mla_act, p2d, wts, *, tm):
    n_tok = x2d.shape[0]
    row = lambda i: (i, 0)
    return pl.pallas_call(
        _out_proj_body,
        grid=(n_tok // tm,),
        in_specs=[pl.BlockSpec((tm, D_MODEL), row), pl.BlockSpec((tm, D_POOL), row),
                  pl.BlockSpec((tm, D_MLA), row), pl.BlockSpec((tm, D_PLE), row),
                  _resident((D_MODEL, D_MODEL)), _resident((1, D_MODEL)),
                  _resident((D_MODEL, D_MODEL)), _resident((1, D_MODEL)),
                  _resident((D_PLE, D_MODEL))],
        out_specs=pl.BlockSpec((tm, D_MODEL), row),
        out_shape=jax.ShapeDtypeStruct((n_tok, D_MODEL), f32),
        compiler_params=_cparams(1),
        name="out_proj",
    )(x2d, pool_act, mla_act, p2d, wts["w_out"], wts["ple_norm_g"], wts["w_ple_gate"],
      wts["b_ple_gate"], wts["w_ple"])


def _rope_tables(pos0, t, reps=1):
    pos = (pos0 + np.arange(t)).astype(np.float64)
    inv = ROPE_THETA ** (-(np.arange(0, D_ROPE, 2, dtype=np.float64) / D_ROPE))
    ang = pos[:, None] * inv[None, :]
    cos, sin = np.cos(ang), np.sin(ang)
    zero = np.zeros((t, LANES - D_ROPE))
    rows = (np.concatenate([cos, cos, zero], axis=-1), np.concatenate([-sin, sin, zero], axis=-1),
            pos[:, None])
    as_f32 = lambda a: jnp.asarray(a.astype(np.float32))
    return tuple(as_f32(np.tile(a, (reps, 1))) for a in rows) + ((as_f32(cos.T), as_f32(sin.T)),)


def _prep_weights(norm_g, w_in, q_norm_g, w_uq, kv_norm_g, w_ukv, q_nope_g, q_rope_g, k_nope_g,
                  k_rope_g, w_pool, pool_scale, w_out, ple_norm_g, w_ple_gate, b_ple_gate, w_ple):
    w_uq_r = jnp.pad(w_uq.reshape(Q_RANK, N_HEADS, D_NOPE + D_ROPE),
                     ((0, 0), (0, 0), (0, D_QK - D_NOPE - D_ROPE))).reshape(Q_RANK, N_HEADS * D_QK)
    w_ukv3 = w_ukv.reshape(KV_RANK, N_HEADS, D_NOPE + D_V)
    w_uk3, w_uv3 = w_ukv3[..., :D_NOPE], w_ukv3[..., D_NOPE:]
    pad_rope = lambda g: jnp.pad(g, (0, LANES - D_ROPE))[None, :]
    return {
        "norm_g": norm_g[None, :], "w_in_t_f32": w_in.T, "q_norm_g": q_norm_g[None, :],
        "kv_norm_g": kv_norm_g[None, :], "k_rope_g": pad_rope(k_rope_g),
        "w_pool": w_pool.astype(bf16), "pool_scale": pool_scale[None, :],
        "w_uq": w_uq_r.astype(bf16), "q_nope_g": q_nope_g[None, :], "q_rope_g": pad_rope(q_rope_g),
        "w_uq_t": w_uq.T.astype(bf16),
        "q_nope_g_col": q_nope_g[:, None], "q_rope_g_col": q_rope_g[:, None],
        "k_nope_g": k_nope_g[None, :],
        "w_uk": w_uk3.reshape(KV_RANK, N_HEADS * D_NOPE).astype(bf16),
        "w_uv_t": w_uv3.reshape(KV_RANK, D_MLA).T.astype(bf16),
        "w_uk_t": jnp.transpose(w_uk3, (1, 2, 0)).astype(bf16),
        "w_uv3": jnp.transpose(w_uv3, (1, 0, 2)).astype(bf16),
        "w_out_f32": w_out, "w_ple_gate_f32": w_ple_gate,
        "ple_norm_g": ple_norm_g[None, :], "b_ple_gate": b_ple_gate[None, :],
        "w_ple": w_ple.astype(bf16),
    }


def _layer_prompt(x, p, wts):
    n_seq, seq_len, _ = x.shape
    x2d = x.reshape(n_seq * seq_len, D_MODEL)
    cos2, sin2, pos, rope_t = _rope_tables(0, seq_len)
    hist = jnp.zeros((1, HALO, D_POOL), f32)
    pact, gm, ckv, krope, npool, qt, k, vt, w_out16, w_gate16, w_in_t16 = _in_proj(
        x2d, hist, cos2, sin2, pos, wts, nseg=1, seg_len=PROMPT_TILE, n_seq=n_seq, carry=True,
        rope_t=rope_t, kv_block=ATTN_BLOCK)
    wts.update(w_out=w_out16, w_ple_gate=w_gate16, w_in_t=w_in_t16)
    mact = _attn_prompt(qt, k, vt, gm, n_seq=n_seq, seq_len=seq_len, tq=ATTN_BLOCK)
    y = _out_proj(x2d, pact, mact, p.reshape(n_seq * seq_len, D_PLE), wts, tm=OUT_TILE)
    return (y.reshape(x.shape), ckv.reshape(n_seq, seq_len, KV_RANK),
            krope.reshape(n_seq, seq_len, D_ROPE), npool)


def _layer_sample(x, p, state_pool, cache_ckv, cache_krope, wts):
    n_seq, t_new, _ = x.shape
    past = cache_ckv.shape[1]
    n_tok = n_seq * t_new
    x2d = x.reshape(n_tok, D_MODEL)
    cos2, sin2, pos, _ = _rope_tables(past, t_new, reps=n_seq)
    hist = jnp.pad(state_pool, ((0, 0), (HALO - POOL_HIST, 0), (0, 0)))
    pact, gm, ckv, krope, npool, q_abs, q_rope = _in_proj(
        x2d, hist, cos2, sin2, pos, wts, nseg=SAMPLE_SEGS_PER_TILE, seg_len=t_new, n_seq=n_seq,
        carry=False)
    ckv3 = ckv.reshape(n_seq, t_new, KV_RANK)
    krope3 = krope.reshape(n_seq, t_new, D_ROPE)
    cache_krope_t = jnp.transpose(cache_krope, (0, 2, 1))
    mact = _attn_sample(cache_ckv, cache_krope_t, ckv3, krope3, q_abs, q_rope, gm, wts, t_new=t_new,
                        n_b=SAMPLE_BATCH_PER_STEP)
    y = _out_proj(x2d, pact, mact, p.reshape(n_tok, D_PLE), wts, tm=OUT_TILE)
    return y.reshape(x.shape), ckv3, krope3, npool


def kernel(x_prompt, x_sample, cache_ckv, cache_krope, state_pool, p_prompt, p_sample, norm_g, w_in,
           q_norm_g, w_uq, kv_norm_g, w_ukv, q_nope_g, q_rope_g, k_nope_g, k_rope_g, w_pool,
           pool_scale, w_out, ple_norm_g, w_ple_gate, b_ple_gate, w_ple):
    depth = norm_g.shape[0]
    layer_w = (norm_g, w_in, q_norm_g, w_uq, kv_norm_g, w_ukv, q_nope_g, q_rope_g, k_nope_g, k_rope_g,
               w_pool, pool_scale, w_out, ple_norm_g, w_ple_gate, b_ple_gate, w_ple)
    yp, ys = x_prompt, x_sample
    outs = [[] for _ in range(6)]
    for i in range(depth):
        wts = _prep_weights(*(w[i] for w in layer_w))
        yp, c1, k1, s1 = _layer_prompt(yp, p_prompt[i], wts)
        ys, c2, k2, s2 = _layer_sample(ys, p_sample[i], state_pool[i], cache_ckv[i], cache_krope[i], wts)
        for lst, val in zip(outs, (c1, k1, s1, c2, k2, s2)):
            lst.append(val)
    return (yp, ys) + tuple(jnp.stack(o) for o in outs)
```

```python
import functools

import jax
import jax.numpy as jnp
import numpy as np
from jax import lax
from jax.experimental import pallas as pl
from jax.experimental.pallas import tpu as pltpu

D_MODEL = 2048
CHUNK = 64
D_POOL = 1024
POOL_WINDOWS = (2, 4, 8, 16)
POOL_GROUP = 256
POOL_HIST = 15
HALO = 16
N_HEADS = 8
D_NOPE = 128
D_ROPE = 64
D_V = 128
D_VX = D_V + 16
D_MLA = N_HEADS * D_V
Q_RANK = 512
KV_RANK = 256
D_PLE = 256
D_QK = 256
ROPE_THETA = 10000.0
EPS = 1e-6
ATTN_SCALE = (D_NOPE + D_ROPE) ** -0.5
NEG_INF = -1e30
EXP2_SCALE = ATTN_SCALE * float(np.log2(np.e))
LANES = 128
SUBLANES = 8

C_U, C_GP, C_Q, C_KV, C_KR, C_GM = 0, 1024, 2048, 2560, 2816, 2880
D_IN = 3904

VMEM_LIMIT = 56 * 1024 * 1024
PROMPT_TILE = 256
W_CHUNK = 512
ATTN_BLOCK = 256
OUT_TILE = 512
SAMPLE_BATCH_PER_STEP = 4
SAMPLE_SEGS_PER_TILE = 16

f32 = jnp.float32
bf16 = jnp.bfloat16


def _cparams(n_axes):
    return pltpu.CompilerParams(dimension_semantics=("arbitrary",) * n_axes,
                                vmem_limit_bytes=VMEM_LIMIT)


def _resident(shape):
    nd = len(shape)
    return pl.BlockSpec(shape, lambda *_: (0,) * nd, pipeline_mode=pl.Buffered(1))


def _rms(x, n):
    return lax.rsqrt(jnp.sum(x * x, axis=-1, keepdims=True) * (1.0 / n) + EPS)


def _rms_cols(x_t, n):
    return lax.rsqrt(jnp.sum(x_t * x_t, axis=0, keepdims=True) * (1.0 / n) + EPS)


def _dot(a, b):
    return jnp.dot(a, b, preferred_element_type=f32)


def _dot_nt(a, b):
    return lax.dot_general(a, b, (((1,), (1,)), ((), ())), preferred_element_type=f32)


def _rope128(x, cos2, sin2):
    lane = lax.broadcasted_iota(jnp.int32, x.shape, 1)
    swapped = jnp.where(lane < D_ROPE // 2, pltpu.roll(x, LANES - D_ROPE // 2, 1),
                        pltpu.roll(x, D_ROPE // 2, 1))
    return x * cos2 + swapped * sin2


def _queries_t(ql, wqt_ref, cos_ref, sin_ref, qng_ref, qrg_ref, qt_ref):
    q_t = _dot_nt(wqt_ref[...], ql)
    tm = q_t.shape[1]
    cos, sin = cos_ref[...], sin_ref[...]
    half = D_ROPE // 2
    for h in range(N_HEADS):
        r0 = h * D_QK
        s0 = h * (D_NOPE + D_ROPE)
        qn = q_t[s0:s0 + D_NOPE]
        qn = qn * (_rms_cols(qn, D_NOPE) * EXP2_SCALE) * qng_ref[...]
        qt_ref[0, r0:r0 + D_NOPE, :] = qn.astype(bf16)
        qr = q_t[s0 + D_NOPE:s0 + D_NOPE + D_ROPE]
        qr = qr * (_rms_cols(qr, D_ROPE) * EXP2_SCALE) * qrg_ref[...]
        x1, x2 = qr[:half], qr[half:]
        qt_ref[0, r0 + D_NOPE:r0 + D_NOPE + half, :] = (x1 * cos - x2 * sin).astype(bf16)
        qt_ref[0, r0 + D_NOPE + half:r0 + D_NOPE + D_ROPE, :] = (x2 * cos + x1 * sin).astype(bf16)
        qt_ref[0, r0 + D_NOPE + D_ROPE:r0 + D_QK, :] = jnp.zeros((D_QK - D_NOPE - D_ROPE, tm), bf16)


def _queries_absorbed(ql, wq_ref, cos_ref, sin_ref, qng_ref, qrg_ref, kng_ref, wukt_ref, qa_ref, qr_ref):
    q = _dot(ql, wq_ref[...])
    cos2, sin2 = cos_ref[...], sin_ref[...]
    for h in range(N_HEADS):
        qn = q[:, h * D_QK:h * D_QK + D_NOPE]
        qn = qn * _rms(qn, D_NOPE) * qng_ref[...]
        qa_ref[h] = _dot((qn * kng_ref[...]).astype(bf16), wukt_ref[h]).astype(bf16)
        qr = q[:, h * D_QK + D_NOPE:(h + 1) * D_QK]
        qr = _rope128(qr * _rms(qr, D_ROPE) * qrg_ref[...], cos2, sin2)
        qr_ref[h] = qr[:, :D_ROPE].astype(bf16)


def _keys_values(c, kr128, wk_ref, wvt_ref, kng_ref, k_ref, vt_ref):
    tm = c.shape[0]
    k = _dot(c, wk_ref[...])
    kr = kr128.astype(bf16)
    for h in range(N_HEADS):
        kn = k[:, h * D_NOPE:(h + 1) * D_NOPE]
        k_ref[:, h * D_QK:h * D_QK + D_NOPE] = (kn * _rms(kn, D_NOPE) * kng_ref[...]).astype(bf16)
        k_ref[:, h * D_QK + D_NOPE:(h + 1) * D_QK] = kr
    v_t = _dot_nt(wvt_ref[...], c).astype(bf16)
    n_blk, tk = vt_ref.shape[1], vt_ref.shape[4]
    row = lax.broadcasted_iota(jnp.int32, (D_VX - D_V, tk), 0)
    ones_row = jnp.where(row == 0, 1.0, 0.0).astype(bf16)
    for j in range(n_blk):
        for h in range(N_HEADS):
            vt_ref[0, j, h, 0:D_V, :] = v_t[h * D_V:(h + 1) * D_V, j * tk:(j + 1) * tk]
            vt_ref[0, j, h, D_V:D_VX, :] = ones_row


def _in_proj_body(*refs, n_conv, **static):
    if not static["fuse_qkv"]:
        _in_proj_tile(pl.program_id(0), refs, **static)
        return
    i = pl.program_id(0)
    w32_ref, w16_ref, wt_ref = refs[6], refs[-5], refs[-1]

    def convert(rows):
        chunk = w32_ref[0:rows, :].astype(bf16)
        w16_ref[0:rows, :] = chunk
        wt_ref[pl.ds(pl.multiple_of(i * W_CHUNK, W_CHUNK), rows), :] = chunk

    last_rows = D_IN - (n_conv - 1) * W_CHUNK
    pl.when(i < n_conv - 1)(lambda: convert(W_CHUNK))
    pl.when(i == n_conv - 1)(lambda: convert(last_rows))

    @pl.when(i >= n_conv)
    def _():
        _in_proj_tile(i - n_conv, refs[:6] + (wt_ref,) + refs[7:-5] + refs[-4:-1], **static)


def _in_proj_tile(i, refs, *, nseg, seg_len, tiles_per_seq, carry, fuse_qkv):
    (x_ref, hist_ref, cos_ref, sin_ref, pos_ref, ng_ref, wt_ref, qg_ref, kvg_ref, krg_ref, wp_ref,
     ps_ref) = refs[:12]
    if fuse_qkv:
        (cost_ref, sint_ref, wqt_ref, qng_ref, qrg_ref, wk_ref, wvt_ref, kng_ref, x0_ref, wo32_ref,
         wg32_ref) = refs[12:23]
        (pact_ref, gm_ref, ckv_ref, kr_ref, npool_ref, qt_ref, k_ref, vt_ref, wo16_ref, wg16_ref,
         ucat_ref, xn_ref, xn_next_ref) = refs[23:]
    else:
        wq_ref, qng_ref, qrg_ref, kng_ref, wukt_ref = refs[12:17]
        pact_ref, gm_ref, ckv_ref, kr_ref, npool_ref, qa_ref, qr_ref, ucat_ref = refs[17:]
    tm = nseg * seg_len

    def normed(ref):
        x = ref[...]
        return (x * _rms(x, D_MODEL) * ng_ref[...]).astype(bf16)

    if fuse_qkv:
        @pl.when(i == 0)
        def _():
            xn_next_ref[...] = normed(x0_ref)
        xn_ref[...] = xn_next_ref[...]
        xn = xn_ref[...]
    else:
        xn = normed(x_ref)

    u = _dot_nt(xn, wt_ref[C_U:C_U + D_POOL, :])
    if carry:
        @pl.when(i % tiles_per_seq == 0)
        def _():
            ucat_ref[:, 0:HALO, :] = jnp.zeros((nseg, HALO, D_POOL), f32)
    else:
        ucat_ref[:, 0:HALO, :] = hist_ref[...]
    ucat_ref[:, HALO:HALO + seg_len, :] = u.reshape(nseg, seg_len, D_POOL)

    pos = pos_ref[...]
    gp = _dot_nt(xn, wt_ref[C_GP:C_GP + D_POOL, :])
    if fuse_qkv:
        xn_next_ref[...] = normed(x_ref)
    gate = gp * jax.nn.sigmoid(gp) * ps_ref[...]

    def pool_group(g):
        w = POOL_WINDOWS[g]
        sl = slice(g * POOL_GROUP, (g + 1) * POOL_GROUP)
        acc = ucat_ref[:, :, sl]
        shift = 1
        while shift < w:
            acc = acc + pltpu.roll(acc, shift, 1)
            shift *= 2
        acc = acc[:, HALO:, :].reshape(tm, POOL_GROUP)
        inv_cnt = 1.0 / jnp.minimum(pos + 1.0, float(w))
        d = (acc * inv_cnt - u[:, sl]).astype(bf16)
        mixed = _dot(d, wp_ref[g])
        pact_ref[:, sl] = (mixed * gate[:, sl]).astype(bf16)

    cq = _dot_nt(xn, wt_ref[C_Q:C_Q + Q_RANK, :])
    pool_group(0)
    pool_group(1)
    ql = (cq * _rms(cq, Q_RANK) * qg_ref[...]).astype(bf16)
    if fuse_qkv:
        _queries_t(ql, wqt_ref, cost_ref, sint_ref, qng_ref, qrg_ref, qt_ref)
    else:
        _queries_absorbed(ql, wq_ref, cos_ref, sin_ref, qng_ref, qrg_ref, kng_ref, wukt_ref, qa_ref, qr_ref)

    ckv = _dot_nt(xn, wt_ref[C_KV:C_KV + KV_RANK, :])
    pool_group(2)
    ckv = ckv * _rms(ckv, KV_RANK) * kvg_ref[...]
    ckv_ref[...] = ckv

    kr = _dot_nt(xn, wt_ref[C_KR:C_KR + LANES, :])
    gm = _dot_nt(xn, wt_ref[C_GM:C_GM + D_MLA, :])
    pool_group(3)
    npool_ref[...] = ucat_ref[:, seg_len + 1:seg_len + HALO, :]
    if carry:
        ucat_ref[:, 0:HALO, :] = ucat_ref[:, seg_len:seg_len + HALO, :]
    lane = lax.broadcasted_iota(jnp.int32, (tm, LANES), 1)
    kr = jnp.where(lane < D_ROPE, kr, 0.0)
    kr = _rope128(kr * _rms(kr, D_ROPE) * krg_ref[...], cos_ref[...], sin_ref[...])
    kr_ref[...] = kr[:, :D_ROPE]
    gm_ref[...] = (gm * jax.nn.sigmoid(gm)).astype(bf16)
    if fuse_qkv:
        _keys_values(ckv.astype(bf16), kr, wk_ref, wvt_ref, kng_ref, k_ref, vt_ref)
        wo16_ref[...] = wo32_ref[...].astype(bf16)
        wg16_ref[...] = wg32_ref[...].astype(bf16)


def _in_proj(x2d, hist, cos2, sin2, pos, wts, *, nseg, seg_len, n_seq, carry, rope_t=None, kv_block=None):
    n_tok = x2d.shape[0]
    tm = nseg * seg_len
    n_tiles = n_tok // tm
    tiles_per_seq = n_tiles // n_seq if carry else 1
    tab_tiles = cos2.shape[0] // tm
    fuse_qkv = rope_t is not None
    n_conv = pl.cdiv(D_IN, W_CHUNK) if fuse_qkv else 0
    tile = (lambda i: jnp.maximum(i - n_conv, 0)) if n_conv else (lambda i: i)
    row = lambda i: (tile(i), 0)
    tab = lambda i: (tile(i) % tab_tiles, 0)
    if carry:
        hist_spec = pl.BlockSpec((1, HALO, D_POOL), lambda i: (0, 0, 0))
        npool_spec = pl.BlockSpec((1, POOL_HIST, D_POOL), lambda i: (tile(i) // tiles_per_seq, 0, 0))
    else:
        hist_spec = pl.BlockSpec((nseg, HALO, D_POOL), lambda i: (tile(i), 0, 0))
        npool_spec = pl.BlockSpec((nseg, POOL_HIST, D_POOL), lambda i: (tile(i), 0, 0))
    if fuse_qkv:
        x_spec = pl.BlockSpec((tm, D_MODEL), lambda i: (jnp.minimum(tile(i) + 1, n_tiles - 1), 0))
        w_chunk = lambda i: (jnp.minimum(i, n_conv - 1), 0)
        w_spec, w_arg = pl.BlockSpec((W_CHUNK, D_MODEL), w_chunk), wts["w_in_t_f32"]
    else:
        x_spec = pl.BlockSpec((tm, D_MODEL), row)
        w_spec, w_arg = _resident((D_IN, D_MODEL)), wts["w_in_t"]
    in_specs = [
        x_spec, hist_spec, pl.BlockSpec((tm, LANES), tab),
        pl.BlockSpec((tm, LANES), tab), pl.BlockSpec((tm, 1), tab), _resident((1, D_MODEL)),
        w_spec, _resident((1, Q_RANK)), _resident((1, KV_RANK)),
        _resident((1, LANES)), _resident((4, POOL_GROUP, POOL_GROUP)), _resident((1, D_POOL)),
    ]
    args = [x2d, hist, cos2, sin2, pos, wts["norm_g"], w_arg, wts["q_norm_g"],
            wts["kv_norm_g"], wts["k_rope_g"], wts["w_pool"], wts["pool_scale"]]
    out_specs = [pl.BlockSpec((tm, D_POOL), row), pl.BlockSpec((tm, D_MLA), row),
                 pl.BlockSpec((tm, KV_RANK), row), pl.BlockSpec((tm, D_ROPE), row), npool_spec]
    out_shape = [jax.ShapeDtypeStruct((n_tok, D_POOL), bf16), jax.ShapeDtypeStruct((n_tok, D_MLA), bf16),
                 jax.ShapeDtypeStruct((n_tok, KV_RANK), f32), jax.ShapeDtypeStruct((n_tok, D_ROPE), f32),
                 jax.ShapeDtypeStruct((n_seq, POOL_HIST, D_POOL), f32)]
    scratch = [pltpu.VMEM((nseg, HALO + seg_len, D_POOL), f32)]
    if fuse_qkv:
        seq_tab = lambda i: (0, tile(i) % tiles_per_seq)
        seq_blk = lambda i: (tile(i) // tiles_per_seq, 0, tile(i) % tiles_per_seq)
        in_specs += [pl.BlockSpec((D_ROPE // 2, tm), seq_tab), pl.BlockSpec((D_ROPE // 2, tm), seq_tab),
                     _resident((N_HEADS * (D_NOPE + D_ROPE), Q_RANK)), _resident((D_NOPE, 1)),
                     _resident((D_ROPE, 1)),
                     _resident((KV_RANK, N_HEADS * D_NOPE)), _resident((D_MLA, KV_RANK)),
                     _resident((1, D_NOPE)),
                     pl.BlockSpec((tm, D_MODEL), lambda i: (0, 0), pipeline_mode=pl.Buffered(1)),
                     pl.BlockSpec((D_MODEL // n_tiles, D_MODEL), row),
                     pl.BlockSpec((D_MODEL // n_tiles, D_MODEL), row)]
        args += [rope_t[0], rope_t[1], wts["w_uq_t"], wts["q_nope_g_col"], wts["q_rope_g_col"],
                 wts["w_uk"], wts["w_uv_t"], wts["k_nope_g"], x2d, wts["w_out_f32"], wts["w_ple_gate_f32"]]
        out_specs += [
            pl.BlockSpec((1, N_HEADS * D_QK, tm), seq_blk),
            pl.BlockSpec((tm, N_HEADS * D_QK), row),
            pl.BlockSpec((1, tm // kv_block, N_HEADS, D_VX, kv_block), lambda i: seq_blk(i)[::2] + (0, 0, 0)),
            pl.BlockSpec((D_MODEL // n_tiles, D_MODEL), row),
            pl.BlockSpec((D_MODEL // n_tiles, D_MODEL), row),
            pl.BlockSpec((W_CHUNK, D_MODEL), w_chunk)]
        out_shape += [jax.ShapeDtypeStruct((n_seq, N_HEADS * D_QK, tiles_per_seq * tm), bf16),
                      jax.ShapeDtypeStruct((n_tok, N_HEADS * D_QK), bf16),
                      jax.ShapeDtypeStruct((n_seq, tiles_per_seq * tm // kv_block, N_HEADS, D_VX, kv_block),
                                           bf16),
                      jax.ShapeDtypeStruct((D_MODEL, D_MODEL), bf16),
                      jax.ShapeDtypeStruct((D_MODEL, D_MODEL), bf16),
                      jax.ShapeDtypeStruct((D_IN, D_MODEL), bf16)]
        scratch += [pltpu.VMEM((tm, D_MODEL), bf16), pltpu.VMEM((tm, D_MODEL), bf16),
                    pltpu.VMEM((n_conv * W_CHUNK, D_MODEL), bf16)]
    else:
        in_specs += [_resident((Q_RANK, N_HEADS * D_QK)), _resident((1, D_NOPE)), _resident((1, LANES)),
                     _resident((1, D_NOPE)), _resident((N_HEADS, D_NOPE, KV_RANK))]
        args += [wts["w_uq"], wts["q_nope_g"], wts["q_rope_g"], wts["k_nope_g"], wts["w_uk_t"]]
        out_specs += [pl.BlockSpec((N_HEADS, tm, KV_RANK), lambda i: (0, i, 0)),
                      pl.BlockSpec((N_HEADS, tm, D_ROPE), lambda i: (0, i, 0))]
        out_shape += [jax.ShapeDtypeStruct((N_HEADS, n_tok, KV_RANK), bf16),
                      jax.ShapeDtypeStruct((N_HEADS, n_tok, D_ROPE), bf16)]
    body = functools.partial(_in_proj_body, n_conv=n_conv, nseg=nseg, seg_len=seg_len,
                             tiles_per_seq=tiles_per_seq, carry=carry, fuse_qkv=fuse_qkv)
    return pl.pallas_call(
        body,
        grid=(n_conv + n_tiles,),
        in_specs=in_specs,
        out_specs=out_specs,
        out_shape=out_shape,
        scratch_shapes=scratch,
        compiler_params=_cparams(1),
        name="in_proj",
    )(*args)


def _attn_prompt_body(qt_ref, k_ref, vt_ref, gm_ref, o_ref, m_ref, acc_ref, s_ref, mx_ref, *, tk):
    g = pl.program_id(1)
    tq = 2 * tk
    hi = slice(tk, tq)
    m_ref[...] = jnp.full(m_ref.shape, NEG_INF, f32)
    acc_ref[...] = jnp.zeros(acc_ref.shape, f32)

    def column_max(s):
        x = jnp.max(s.reshape(s.shape[0] // SUBLANES, SUBLANES, s.shape[1]), axis=0)
        for shift in (4, 2, 1):
            x = jnp.maximum(x, pltpu.roll(x, shift, 0))
        return x

    def keys(h, j):
        return k_ref[pl.ds(pl.multiple_of(j * tk, tk), tk), h * D_QK:(h + 1) * D_QK]

    def chunk_visible(n_q):
        k_chunk = lax.broadcasted_iota(jnp.int32, (tk, n_q), 0) // CHUNK
        q_chunk = lax.broadcasted_iota(jnp.int32, (tk, n_q), 1) // CHUNK
        return k_chunk <= q_chunk

    def scores(h, j, buf, first_diag=False):
        s = _dot(keys(h, j), qt_ref[0, h * D_QK:(h + 1) * D_QK, :])
        if first_diag:
            s = jnp.where(chunk_visible(tq), s, NEG_INF)
        s_ref[buf, h] = s
        mx_ref[buf, h] = column_max(s)

    def scores_hi(h, j, buf):
        s = _dot(keys(h, j), qt_ref[0, h * D_QK:(h + 1) * D_QK, hi])
        s = jnp.where(chunk_visible(tk), s, NEG_INF)
        s_ref[buf, h, :, hi] = s
        mx_ref[buf, h, :, hi] = column_max(s)

    def values(h, j, buf, cols=slice(None)):
        m_old = m_ref[h, :, cols]
        m_new = jnp.maximum(m_old, mx_ref[buf, h, :, cols])
        n_q = m_new.shape[1]
        alpha = jnp.exp2(m_old - m_new)
        s = s_ref[buf, h, :, cols]
        p = jnp.exp2(s.reshape(tk // SUBLANES, SUBLANES, n_q) - m_new[None]).reshape(tk, n_q)
        m_ref[h, :, cols] = m_new
        acc = acc_ref[h, :, cols].reshape(D_VX // SUBLANES, SUBLANES, n_q) * alpha[None]
        acc_ref[h, :, cols] = acc.reshape(D_VX, n_q) + _dot(vt_ref[0, j, h], p.astype(bf16))

    def both(score_fn, value_fn):
        for h in range(N_HEADS):
            score_fn(h)
            value_fn(h)

    d0, d1 = 2 * g, 2 * g + 1
    for h in range(N_HEADS):
        scores(h, d0, 0, first_diag=True)
    both(lambda h: scores_hi(h, d1, 1), lambda h: values(h, d0, 0))

    @pl.when(g == 0)
    def _():
        for h in range(N_HEADS):
            values(h, d1, 1, hi)

    @pl.when(g > 0)
    def _():
        both(lambda h: scores(h, 0, 0), lambda h: values(h, d1, 1, hi))

        def step_pair(u, carry):
            both(lambda h: scores(h, 2 * u + 1, 1), lambda h: values(h, 2 * u, 0))
            both(lambda h: scores(h, 2 * u + 2, 0), lambda h: values(h, 2 * u + 1, 1))
            return carry

        lax.fori_loop(0, g - 1, step_pair, 0)
        both(lambda h: scores(h, d0 - 1, 1), lambda h: values(h, d0 - 2, 0))
        for h in range(N_HEADS):
            values(h, d0 - 1, 1)

    for h in range(N_HEADS):
        hs = slice(h * D_V, (h + 1) * D_V)
        o = (acc_ref[h, 0:D_V, :] / acc_ref[h, D_V:D_V + 1, :]).T
        o_ref[:, hs] = (o * gm_ref[:, hs].astype(f32)).astype(bf16)


def _attn_prompt(qt, k, vt, gm, *, n_seq, seq_len, tk):
    tq = 2 * tk
    nq = seq_len // tq
    body = functools.partial(_attn_prompt_body, tk=tk)
    return pl.pallas_call(
        body,
        grid=(n_seq, nq),
        in_specs=[pl.BlockSpec((1, N_HEADS * D_QK, tq), lambda b, i: (b, 0, i)),
                  pl.BlockSpec((seq_len, N_HEADS * D_QK), lambda b, i: (b, 0)),
                  pl.BlockSpec((1, seq_len // tk, N_HEADS, D_VX, tk), lambda b, i: (b, 0, 0, 0, 0)),
                  pl.BlockSpec((tq, D_MLA), lambda b, i: (b * nq + i, 0))],
        out_specs=pl.BlockSpec((tq, D_MLA), lambda b, i: (b * nq + i, 0)),
        out_shape=jax.ShapeDtypeStruct((n_seq * seq_len, D_MLA), bf16),
        scratch_shapes=[pltpu.VMEM((N_HEADS, SUBLANES, tq), f32), pltpu.VMEM((N_HEADS, D_VX, tq), f32),
                        pltpu.VMEM((2, N_HEADS, tk, tq), f32),
                        pltpu.VMEM((2, N_HEADS, SUBLANES, tq), f32)],
        compiler_params=_cparams(2),
        name="attn_prompt",
    )(qt, k, vt, gm)


def _attn_sample_body(cckv_ref, ckr_t_ref, nckv_ref, nkr_ref, qa_ref, qr_ref, gm_ref, wukt_ref, wuv_ref,
                      o_ref, c_ref, ktail_ref, s_ref, *, past, t_new, chunks, n_b):
    n_tail = c_ref.shape[1] - past
    rows = N_HEADS * t_new
    c_ref[:, past:past + t_new, :] = nckv_ref[...].astype(bf16)
    c_ref[:, past + t_new:, :] = jnp.zeros((n_b, n_tail - t_new, KV_RANK), bf16)
    ktail_ref[:, 0:t_new, :] = nkr_ref[...].astype(bf16)
    ktail_ref[:, t_new:, :] = jnp.zeros((n_b, n_tail - t_new, D_ROPE), bf16)

    def chunk_scores(b, start, size):
        cached = start < past
        qa = qa_ref[:, b * t_new:(b + 1) * t_new, :].reshape(rows, KV_RANK)
        qr = qr_ref[:, b * t_new:(b + 1) * t_new, :].reshape(rows, D_ROPE)
        if cached:
            c32 = cckv_ref[b, start:start + size, :]
            c_ref[b, start:start + size, :] = c32.astype(bf16)
            c_t = c32.T.astype(bf16)
            k_t = _dot(wukt_ref[...], c_t)
            s_nope = _dot(qa, c_t)
            s_rope = _dot(qr, ckr_t_ref[b, :, start:start + size].astype(bf16))
        else:
            c = c_ref[b, start:start + size, :]
            k_t = _dot_nt(wukt_ref[...], c)
            s_nope = _dot_nt(qa, c)
            s_rope = _dot_nt(qr, ktail_ref[b])
        ssq = jnp.sum((k_t * k_t).reshape(N_HEADS, D_NOPE, size), axis=1)
        r = lax.rsqrt(ssq * (1.0 / D_NOPE) + EPS)
        for h in range(N_HEADS):
            hs = slice(h * t_new, (h + 1) * t_new)
            s = (s_nope[hs] * r[h:h + 1, :] + s_rope[hs]) * ATTN_SCALE
            if not cached:
                key = lax.broadcasted_iota(jnp.int32, (t_new, size), 1)
                s = jnp.where(key < t_new, s, NEG_INF)
            s_ref[b, hs, start:start + size] = s

    def chunk_values(b, start, size, state):
        m, l, acc = state
        s = s_ref[b, :, start:start + size]
        m_new = jnp.maximum(m, jnp.max(s, axis=-1, keepdims=True))
        alpha = jnp.exp(m - m_new)
        p = jnp.exp(s - m_new)
        l = alpha * l + jnp.sum(p, axis=-1, keepdims=True)
        acc = alpha * acc + _dot(p.astype(bf16), c_ref[b, start:start + size, :])
        return m_new, l, acc

    def finish(b, state):
        _, l, acc = state
        o_lat = (acc / l).astype(bf16)
        for h in range(N_HEADS):
            o = _dot(o_lat[h * t_new:(h + 1) * t_new], wuv_ref[h])
            hs = slice(h * D_V, (h + 1) * D_V)
            bs = slice(b * t_new, (b + 1) * t_new)
            o_ref[bs, hs] = (o * gm_ref[bs, hs].astype(f32)).astype(bf16)

    items = [(b,) + ch for b in range(n_b) for ch in chunks]
    init = (jnp.full((rows, 1), NEG_INF, f32), jnp.zeros((rows, 1), f32),
            jnp.zeros((rows, KV_RANK), f32))
    state = init
    chunk_scores(*items[0])
    for nxt, cur in zip(items[1:] + [None], items):
        if nxt is not None:
            chunk_scores(*nxt)
        state = chunk_values(*cur, state)
        if nxt is None or nxt[0] != cur[0]:
            finish(cur[0], state)
            state = init


def _attn_sample(cache_ckv, cache_krope_t, ckv_new, krope_new, q_abs, q_rope, gm, wts, *, t_new, n_b):
    n_seq, past, _ = cache_ckv.shape
    chunk = 512
    chunks = tuple((s, min(chunk, past - s)) for s in range(0, past, chunk)) + ((past, LANES),)
    s_pad = past + LANES
    body = functools.partial(_attn_sample_body, past=past, t_new=t_new, chunks=chunks, n_b=n_b)
    return pl.pallas_call(
        body,
        grid=(n_seq // n_b,),
        in_specs=[pl.BlockSpec((n_b, past, KV_RANK), lambda b: (b, 0, 0)),
                  pl.BlockSpec((n_b, D_ROPE, past), lambda b: (b, 0, 0)),
                  pl.BlockSpec((n_b, t_new, KV_RANK), lambda b: (b, 0, 0)),
                  pl.BlockSpec((n_b, t_new, D_ROPE), lambda b: (b, 0, 0)),
                  pl.BlockSpec((N_HEADS, n_b * t_new, KV_RANK), lambda b: (0, b, 0)),
                  pl.BlockSpec((N_HEADS, n_b * t_new, D_ROPE), lambda b: (0, b, 0)),
                  pl.BlockSpec((n_b * t_new, D_MLA), lambda b: (b, 0)),
                  _resident((N_HEADS * D_NOPE, KV_RANK)),
                  _resident((N_HEADS, KV_RANK, D_V))],
        out_specs=pl.BlockSpec((n_b * t_new, D_MLA), lambda b: (b, 0)),
        out_shape=jax.ShapeDtypeStruct((n_seq * t_new, D_MLA), bf16),
        scratch_shapes=[pltpu.VMEM((n_b, s_pad, KV_RANK), bf16),
                        pltpu.VMEM((n_b, s_pad - past, D_ROPE), bf16),
                        pltpu.VMEM((n_b, N_HEADS * t_new, s_pad), f32)],
        compiler_params=_cparams(1),
        name="attn_sample",
    )(cache_ckv, cache_krope_t, ckv_new, krope_new, q_abs, q_rope, gm,
      wts["w_uk_t"].reshape(N_HEADS * D_NOPE, KV_RANK), wts["w_uv3"])


def _out_proj_body(x_ref, pa_ref, ma_ref, p_ref, wo_ref, png_ref, wg_ref, bg_ref, wple_ref, y_ref):
    h = x_ref[...] + _dot(pa_ref[...], wo_ref[0:D_POOL, :]) + _dot(ma_ref[...], wo_ref[D_POOL:, :])
    hn = (h * _rms(h, D_MODEL) * png_ref[...]).astype(bf16)
    gate = jax.nn.sigmoid(_dot(hn, wg_ref[...]) + bg_ref[...])
    y_ref[...] = h + gate * _dot(p_ref[...].astype(bf16), wple_ref[...])


def _out_proj(x2d, pool_act, mla_act, p2d, wts, *, tm):
    n_tok = x2d.shape[0]
    row = lambda i: (i, 0)
    return pl.pallas_call(
        _out_proj_body,
        grid=(n_tok // tm,),
        in_specs=[pl.BlockSpec((tm, D_MODEL), row), pl.BlockSpec((tm, D_POOL), row),
                  pl.BlockSpec((tm, D_MLA), row), pl.BlockSpec((tm, D_PLE), row),
                  _resident((D_MODEL, D_MODEL)), _resident((1, D_MODEL)),
                  _resident((D_MODEL, D_MODEL)), _resident((1, D_MODEL)),
                  _resident((D_PLE, D_MODEL))],
        out_specs=pl.BlockSpec((tm, D_MODEL), row),
        out_shape=jax.ShapeDtypeStruct((n_tok, D_MODEL), f32),
        compiler_params=_cparams(1),
        name="out_proj",
    )(x2d, pool_act, mla_act, p2d, wts["w_out"], wts["ple_norm_g"], wts["w_ple_gate"],
      wts["b_ple_gate"], wts["w_ple"])


def _rope_tables(pos0, t, reps=1):
    pos = (pos0 + np.arange(t)).astype(np.float64)
    inv = ROPE_THETA ** (-(np.arange(0, D_ROPE, 2, dtype=np.float64) / D_ROPE))
    ang = pos[:, None] * inv[None, :]
    cos, sin = np.cos(ang), np.sin(ang)
    zero = np.zeros((t, LANES - D_ROPE))
    rows = (np.concatenate([cos, cos, zero], axis=-1), np.concatenate([-sin, sin, zero], axis=-1),
            pos[:, None])
    as_f32 = lambda a: jnp.asarray(a.astype(np.float32))
    return tuple(as_f32(np.tile(a, (reps, 1))) for a in rows) + ((as_f32(cos.T), as_f32(sin.T)),)


def _prep_weights(norm_g, w_in, q_norm_g, w_uq, kv_norm_g, w_ukv, q_nope_g, q_rope_g, k_nope_g,
                  k_rope_g, w_pool, pool_scale, w_out, ple_norm_g, w_ple_gate, b_ple_gate, w_ple):
    w_uq_r = jnp.pad(w_uq.reshape(Q_RANK, N_HEADS, D_NOPE + D_ROPE),
                     ((0, 0), (0, 0), (0, D_QK - D_NOPE - D_ROPE))).reshape(Q_RANK, N_HEADS * D_QK)
    w_ukv3 = w_ukv.reshape(KV_RANK, N_HEADS, D_NOPE + D_V)
    w_uk3, w_uv3 = w_ukv3[..., :D_NOPE], w_ukv3[..., D_NOPE:]
    pad_rope = lambda g: jnp.pad(g, (0, LANES - D_ROPE))[None, :]
    return {
        "norm_g": norm_g[None, :], "w_in_t_f32": w_in.T, "q_norm_g": q_norm_g[None, :],
        "kv_norm_g": kv_norm_g[None, :], "k_rope_g": pad_rope(k_rope_g),
        "w_pool": w_pool.astype(bf16), "pool_scale": pool_scale[None, :],
        "w_uq": w_uq_r.astype(bf16), "q_nope_g": q_nope_g[None, :], "q_rope_g": pad_rope(q_rope_g),
        "w_uq_t": w_uq.T.astype(bf16),
        "q_nope_g_col": q_nope_g[:, None], "q_rope_g_col": q_rope_g[:, None],
        "k_nope_g": k_nope_g[None, :],
        "w_uk": w_uk3.reshape(KV_RANK, N_HEADS * D_NOPE).astype(bf16),
        "w_uv_t": w_uv3.reshape(KV_RANK, D_MLA).T.astype(bf16),
        "w_uk_t": jnp.transpose(w_uk3, (1, 2, 0)).astype(bf16),
        "w_uv3": jnp.transpose(w_uv3, (1, 0, 2)).astype(bf16),
        "w_out_f32": w_out, "w_ple_gate_f32": w_ple_gate,
        "ple_norm_g": ple_norm_g[None, :], "b_ple_gate": b_ple_gate[None, :],
        "w_ple": w_ple.astype(bf16),
    }


def _layer_prompt(x, p, wts):
    n_seq, seq_len, _ = x.shape
    x2d = x.reshape(n_seq * seq_len, D_MODEL)
    cos2, sin2, pos, rope_t = _rope_tables(0, seq_len)
    hist = jnp.zeros((1, HALO, D_POOL), f32)
    pact, gm, ckv, krope, npool, qt, k, vt, w_out16, w_gate16, w_in_t16 = _in_proj(
        x2d, hist, cos2, sin2, pos, wts, nseg=1, seg_len=PROMPT_TILE, n_seq=n_seq, carry=True,
        rope_t=rope_t, kv_block=ATTN_BLOCK)
    wts.update(w_out=w_out16, w_ple_gate=w_gate16, w_in_t=w_in_t16)
    mact = _attn_prompt(qt, k, vt, gm, n_seq=n_seq, seq_len=seq_len, tk=ATTN_BLOCK)
    y = _out_proj(x2d, pact, mact, p.reshape(n_seq * seq_len, D_PLE), wts, tm=OUT_TILE)
    return (y.reshape(x.shape), ckv.reshape(n_seq, seq_len, KV_RANK),
            krope.reshape(n_seq, seq_len, D_ROPE), npool)


def _layer_sample(x, p, state_pool, cache_ckv, cache_krope, wts):
    n_seq, t_new, _ = x.shape
    past = cache_ckv.shape[1]
    n_tok = n_seq * t_new
    x2d = x.reshape(n_tok, D_MODEL)
    cos2, sin2, pos, _ = _rope_tables(past, t_new, reps=n_seq)
    hist = jnp.pad(state_pool, ((0, 0), (HALO - POOL_HIST, 0), (0, 0)))
    pact, gm, ckv, krope, npool, q_abs, q_rope = _in_proj(
        x2d, hist, cos2, sin2, pos, wts, nseg=SAMPLE_SEGS_PER_TILE, seg_len=t_new, n_seq=n_seq,
        carry=False)
    ckv3 = ckv.reshape(n_seq, t_new, KV_RANK)
    krope3 = krope.reshape(n_seq, t_new, D_ROPE)
    cache_krope_t = jnp.transpose(cache_krope, (0, 2, 1))
    mact = _attn_sample(cache_ckv, cache_krope_t, ckv3, krope3, q_abs, q_rope, gm, wts, t_new=t_new,
                        n_b=SAMPLE_BATCH_PER_STEP)
    y = _out_proj(x2d, pact, mact, p.reshape(n_tok, D_PLE), wts, tm=OUT_TILE)
    return y.reshape(x.shape), ckv3, krope3, npool


def kernel(x_prompt, x_sample, cache_ckv, cache_krope, state_pool, p_prompt, p_sample, norm_g, w_in,
           q_norm_g, w_uq, kv_norm_g, w_ukv, q_nope_g, q_rope_g, k_nope_g, k_rope_g, w_pool,
           pool_scale, w_out, ple_norm_g, w_ple_gate, b_ple_gate, w_ple):
    depth = norm_g.shape[0]
    layer_w = (norm_g, w_in, q_norm_g, w_uq, kv_norm_g, w_ukv, q_nope_g, q_rope_g, k_nope_g, k_rope_g,
               w_pool, pool_scale, w_out, ple_norm_g, w_ple_gate, b_ple_gate, w_ple)
    yp, ys = x_prompt, x_sample
    outs = [[] for _ in range(6)]
    for i in range(depth):
        wts = _prep_weights(*(w[i] for w in layer_w))
        yp, c1, k1, s1 = _layer_prompt(yp, p_prompt[i], wts)
        ys, c2, k2, s2 = _layer_sample(ys, p_sample[i], state_pool[i], cache_ckv[i], cache_krope[i], wts)
        for lst, val in zip(outs, (c1, k1, s1, c2, k2, s2)):
            lst.append(val)
    return (yp, ys) + tuple(jnp.stack(o) for o in outs)
```

```python
import functools

import jax
import jax.numpy as jnp
import numpy as np
from jax import lax
from jax.experimental import pallas as pl
from jax.experimental.pallas import tpu as pltpu

D_MODEL = 2048
CHUNK = 64
D_POOL = 1024
POOL_WINDOWS = (2, 4, 8, 16)
POOL_GROUP = 256
POOL_HIST = 15
HALO = 16
N_HEADS = 8
D_NOPE = 128
D_ROPE = 64
D_V = 128
D_VX = D_V + 16
D_MLA = N_HEADS * D_V
Q_RANK = 512
KV_RANK = 256
D_PLE = 256
D_QK = 256
ROPE_THETA = 10000.0
EPS = 1e-6
ATTN_SCALE = (D_NOPE + D_ROPE) ** -0.5
NEG_INF = -1e30
EXP2_SCALE = ATTN_SCALE * float(np.log2(np.e))
LANES = 128
SUBLANES = 8

C_U, C_GP, C_Q, C_KV, C_KR, C_GM = 0, 1024, 2048, 2560, 2816, 2880
D_IN = 3904

VMEM_LIMIT = 56 * 1024 * 1024
PROMPT_TILE = 256
W_CHUNK = 512
ATTN_BLOCK = 256
OUT_TILE = 512
SAMPLE_BATCH_PER_STEP = 4
SAMPLE_SEGS_PER_TILE = 16

f32 = jnp.float32
bf16 = jnp.bfloat16


def _cparams(n_axes):
    return pltpu.CompilerParams(dimension_semantics=("arbitrary",) * n_axes,
                                vmem_limit_bytes=VMEM_LIMIT)


def _resident(shape):
    nd = len(shape)
    return pl.BlockSpec(shape, lambda *_: (0,) * nd, pipeline_mode=pl.Buffered(1))


def _rms(x, n):
    return lax.rsqrt(jnp.sum(x * x, axis=-1, keepdims=True) * (1.0 / n) + EPS)


def _rms_cols(x_t, n):
    return lax.rsqrt(jnp.sum(x_t * x_t, axis=0, keepdims=True) * (1.0 / n) + EPS)


def _dot(a, b):
    return jnp.dot(a, b, preferred_element_type=f32)


def _dot_nt(a, b):
    return lax.dot_general(a, b, (((1,), (1,)), ((), ())), preferred_element_type=f32)


def _rope128(x, cos2, sin2):
    lane = lax.broadcasted_iota(jnp.int32, x.shape, 1)
    swapped = jnp.where(lane < D_ROPE // 2, pltpu.roll(x, LANES - D_ROPE // 2, 1),
                        pltpu.roll(x, D_ROPE // 2, 1))
    return x * cos2 + swapped * sin2


def _queries_t(ql, wqt_ref, cos_ref, sin_ref, qng_ref, qrg_ref, qt_ref):
    q_t = _dot_nt(wqt_ref[...], ql)
    tm = q_t.shape[1]
    cos, sin = cos_ref[...], sin_ref[...]
    half = D_ROPE // 2
    for h in range(N_HEADS):
        r0 = h * D_QK
        s0 = h * (D_NOPE + D_ROPE)
        qn = q_t[s0:s0 + D_NOPE]
        qn = qn * (_rms_cols(qn, D_NOPE) * EXP2_SCALE) * qng_ref[...]
        qt_ref[0, r0:r0 + D_NOPE, :] = qn.astype(bf16)
        qr = q_t[s0 + D_NOPE:s0 + D_NOPE + D_ROPE]
        qr = qr * (_rms_cols(qr, D_ROPE) * EXP2_SCALE) * qrg_ref[...]
        x1, x2 = qr[:half], qr[half:]
        qt_ref[0, r0 + D_NOPE:r0 + D_NOPE + half, :] = (x1 * cos - x2 * sin).astype(bf16)
        qt_ref[0, r0 + D_NOPE + half:r0 + D_NOPE + D_ROPE, :] = (x2 * cos + x1 * sin).astype(bf16)
        qt_ref[0, r0 + D_NOPE + D_ROPE:r0 + D_QK, :] = jnp.zeros((D_QK - D_NOPE - D_ROPE, tm), bf16)


def _queries_absorbed(ql, wq_ref, cos_ref, sin_ref, qng_ref, qrg_ref, kng_ref, wukt_ref, qa_ref, qr_ref):
    q = _dot(ql, wq_ref[...])
    cos2, sin2 = cos_ref[...], sin_ref[...]
    for h in range(N_HEADS):
        qn = q[:, h * D_QK:h * D_QK + D_NOPE]
        qn = qn * _rms(qn, D_NOPE) * qng_ref[...]
        qa_ref[h] = _dot((qn * kng_ref[...]).astype(bf16), wukt_ref[h]).astype(bf16)
        qr = q[:, h * D_QK + D_NOPE:(h + 1) * D_QK]
        qr = _rope128(qr * _rms(qr, D_ROPE) * qrg_ref[...], cos2, sin2)
        qr_ref[h] = qr[:, :D_ROPE].astype(bf16)


def _keys_values(c, kr128, wk_ref, wvt_ref, kng_ref, k_ref, vt_ref):
    tm = c.shape[0]
    k = _dot(c, wk_ref[...])
    kr = kr128.astype(bf16)
    for h in range(N_HEADS):
        kn = k[:, h * D_NOPE:(h + 1) * D_NOPE]
        k_ref[:, h * D_QK:h * D_QK + D_NOPE] = (kn * _rms(kn, D_NOPE) * kng_ref[...]).astype(bf16)
        k_ref[:, h * D_QK + D_NOPE:(h + 1) * D_QK] = kr
    v_t = _dot_nt(wvt_ref[...], c).astype(bf16)
    n_blk, tk = vt_ref.shape[1], vt_ref.shape[4]
    row = lax.broadcasted_iota(jnp.int32, (D_VX - D_V, tk), 0)
    ones_row = jnp.where(row == 0, 1.0, 0.0).astype(bf16)
    for j in range(n_blk):
        for h in range(N_HEADS):
            vt_ref[0, j, h, 0:D_V, :] = v_t[h * D_V:(h + 1) * D_V, j * tk:(j + 1) * tk]
            vt_ref[0, j, h, D_V:D_VX, :] = ones_row


def _in_proj_body(*refs, n_conv, **static):
    if not static["fuse_qkv"]:
        _in_proj_tile(pl.program_id(0), refs, **static)
        return
    i = pl.program_id(0)
    w32_ref, w16_ref, wt_ref = refs[6], refs[-3], refs[-1]

    def convert(rows):
        chunk = w32_ref[0:rows, :].astype(bf16)
        w16_ref[0:rows, :] = chunk
        wt_ref[pl.ds(pl.multiple_of(i * W_CHUNK, W_CHUNK), rows), :] = chunk

    last_rows = D_IN - (n_conv - 1) * W_CHUNK
    pl.when(i < n_conv - 1)(lambda: convert(W_CHUNK))
    pl.when(i == n_conv - 1)(lambda: convert(last_rows))

    @pl.when(i >= n_conv)
    def _():
        _in_proj_tile(i - n_conv, refs[:6] + (wt_ref,) + refs[7:-3] + refs[-2:-1], **static)


def _in_proj_tile(i, refs, *, nseg, seg_len, tiles_per_seq, carry, fuse_qkv):
    (x_ref, hist_ref, cos_ref, sin_ref, pos_ref, ng_ref, wt_ref, qg_ref, kvg_ref, krg_ref, wp_ref,
     ps_ref) = refs[:12]
    if fuse_qkv:
        (cost_ref, sint_ref, wqt_ref, qng_ref, qrg_ref, wk_ref, wvt_ref, kng_ref, wo32_ref,
         wg32_ref) = refs[12:22]
        (pact_ref, gm_ref, ckv_ref, kr_ref, npool_ref, qt_ref, k_ref, vt_ref, wo16_ref, wg16_ref,
         ucat_ref) = refs[22:]
    else:
        wq_ref, qng_ref, qrg_ref, kng_ref, wukt_ref = refs[12:17]
        pact_ref, gm_ref, ckv_ref, kr_ref, npool_ref, qa_ref, qr_ref, ucat_ref = refs[17:]
    tm = nseg * seg_len

    x = x_ref[...]
    xn = (x * _rms(x, D_MODEL) * ng_ref[...]).astype(bf16)

    u = _dot_nt(xn, wt_ref[C_U:C_U + D_POOL, :])
    if carry:
        @pl.when(i % tiles_per_seq == 0)
        def _():
            ucat_ref[:, 0:HALO, :] = jnp.zeros((nseg, HALO, D_POOL), f32)
    else:
        ucat_ref[:, 0:HALO - POOL_HIST, :] = jnp.zeros((nseg, HALO - POOL_HIST, D_POOL), f32)
        for t in range(POOL_HIST):
            ucat_ref[:, HALO - POOL_HIST + t, :] = hist_ref[t]
    ucat_ref[:, HALO:HALO + seg_len, :] = u.reshape(nseg, seg_len, D_POOL)

    pos = pos_ref[...]
    gp = _dot_nt(xn, wt_ref[C_GP:C_GP + D_POOL, :])
    gate = gp * jax.nn.sigmoid(gp) * ps_ref[...]

    def pool_group(g):
        w = POOL_WINDOWS[g]
        sl = slice(g * POOL_GROUP, (g + 1) * POOL_GROUP)
        acc = ucat_ref[:, :, sl]
        shift = 1
        while shift < w:
            acc = acc + pltpu.roll(acc, shift, 1)
            shift *= 2
        acc = acc[:, HALO:, :].reshape(tm, POOL_GROUP)
        inv_cnt = 1.0 / jnp.minimum(pos + 1.0, float(w))
        d = (acc * inv_cnt - u[:, sl]).astype(bf16)
        mixed = _dot(d, wp_ref[g])
        pact_ref[:, sl] = (mixed * gate[:, sl]).astype(bf16)

    cq = _dot_nt(xn, wt_ref[C_Q:C_Q + Q_RANK, :])
    pool_group(0)
    pool_group(1)
    ql = (cq * _rms(cq, Q_RANK) * qg_ref[...]).astype(bf16)
    if fuse_qkv:
        _queries_t(ql, wqt_ref, cost_ref, sint_ref, qng_ref, qrg_ref, qt_ref)
    else:
        _queries_absorbed(ql, wq_ref, cos_ref, sin_ref, qng_ref, qrg_ref, kng_ref, wukt_ref, qa_ref, qr_ref)

    ckv = _dot_nt(xn, wt_ref[C_KV:C_KV + KV_RANK, :])
    pool_group(2)
    ckv = ckv * _rms(ckv, KV_RANK) * kvg_ref[...]
    ckv_ref[...] = ckv

    kr = _dot_nt(xn, wt_ref[C_KR:C_KR + LANES, :])
    gm = _dot_nt(xn, wt_ref[C_GM:C_GM + D_MLA, :])
    pool_group(3)
    if carry:
        npool_ref[...] = ucat_ref[:, seg_len + 1:seg_len + HALO, :]
        ucat_ref[:, 0:HALO, :] = ucat_ref[:, seg_len:seg_len + HALO, :]
    else:
        for t in range(POOL_HIST):
            npool_ref[t] = ucat_ref[:, seg_len + 1 + t, :]
    lane = lax.broadcasted_iota(jnp.int32, (tm, LANES), 1)
    kr = jnp.where(lane < D_ROPE, kr, 0.0)
    kr = _rope128(kr * _rms(kr, D_ROPE) * krg_ref[...], cos_ref[...], sin_ref[...])
    kr_ref[...] = kr[:, :D_ROPE]
    gm_ref[...] = (gm * jax.nn.sigmoid(gm)).astype(bf16)
    if fuse_qkv:
        _keys_values(ckv.astype(bf16), kr, wk_ref, wvt_ref, kng_ref, k_ref, vt_ref)
        wo16_ref[...] = wo32_ref[...].astype(bf16)
        wg16_ref[...] = wg32_ref[...].astype(bf16)


def _in_proj(x2d, hist, cos2, sin2, pos, wts, *, nseg, seg_len, n_seq, carry, rope_t=None, kv_block=None):
    n_tok = x2d.shape[0]
    tm = nseg * seg_len
    n_tiles = n_tok // tm
    tiles_per_seq = n_tiles // n_seq if carry else 1
    tab_tiles = cos2.shape[0] // tm
    fuse_qkv = rope_t is not None
    n_conv = pl.cdiv(D_IN, W_CHUNK) if fuse_qkv else 0
    tile = (lambda i: jnp.maximum(i - n_conv, 0)) if n_conv else (lambda i: i)
    row = lambda i: (tile(i), 0)
    tab = lambda i: (tile(i) % tab_tiles, 0)
    if carry:
        hist_spec = pl.BlockSpec((1, HALO, D_POOL), lambda i: (0, 0, 0))
        npool_spec = pl.BlockSpec((1, POOL_HIST, D_POOL), lambda i: (tile(i) // tiles_per_seq, 0, 0))
    else:
        hist_spec = pl.BlockSpec((POOL_HIST, nseg, D_POOL), lambda i: (0, tile(i), 0))
        npool_spec = pl.BlockSpec((POOL_HIST, nseg, D_POOL), lambda i: (0, tile(i), 0))
    x_spec = pl.BlockSpec((tm, D_MODEL), row)
    if fuse_qkv:
        w_chunk = lambda i: (jnp.minimum(i, n_conv - 1), 0)
        w_spec, w_arg = pl.BlockSpec((W_CHUNK, D_MODEL), w_chunk), wts["w_in_t_f32"]
    else:
        w_spec, w_arg = _resident((D_IN, D_MODEL)), wts["w_in_t"]
    in_specs = [
        x_spec, hist_spec, pl.BlockSpec((tm, LANES), tab),
        pl.BlockSpec((tm, LANES), tab), pl.BlockSpec((tm, 1), tab), _resident((1, D_MODEL)),
        w_spec, _resident((1, Q_RANK)), _resident((1, KV_RANK)),
        _resident((1, LANES)), _resident((4, POOL_GROUP, POOL_GROUP)), _resident((1, D_POOL)),
    ]
    args = [x2d, hist, cos2, sin2, pos, wts["norm_g"], w_arg, wts["q_norm_g"],
            wts["kv_norm_g"], wts["k_rope_g"], wts["w_pool"], wts["pool_scale"]]
    out_specs = [pl.BlockSpec((tm, D_POOL), row), pl.BlockSpec((tm, D_MLA), row),
                 pl.BlockSpec((tm, KV_RANK), row), pl.BlockSpec((tm, D_ROPE), row), npool_spec]
    out_shape = [jax.ShapeDtypeStruct((n_tok, D_POOL), bf16), jax.ShapeDtypeStruct((n_tok, D_MLA), bf16),
                 jax.ShapeDtypeStruct((n_tok, KV_RANK), f32), jax.ShapeDtypeStruct((n_tok, D_ROPE), f32),
                 jax.ShapeDtypeStruct((n_seq, POOL_HIST, D_POOL) if carry else (POOL_HIST, n_seq, D_POOL), f32)]
    scratch = [pltpu.VMEM((nseg, HALO + seg_len, D_POOL), f32)]
    if fuse_qkv:
        seq_tab = lambda i: (0, tile(i) % tiles_per_seq)
        seq_blk = lambda i: (tile(i) // tiles_per_seq, 0, tile(i) % tiles_per_seq)
        in_specs += [pl.BlockSpec((D_ROPE // 2, tm), seq_tab), pl.BlockSpec((D_ROPE // 2, tm), seq_tab),
                     _resident((N_HEADS * (D_NOPE + D_ROPE), Q_RANK)), _resident((D_NOPE, 1)),
                     _resident((D_ROPE, 1)),
                     _resident((KV_RANK, N_HEADS * D_NOPE)), _resident((D_MLA, KV_RANK)),
                     _resident((1, D_NOPE)),
                     pl.BlockSpec((D_MODEL // n_tiles, D_MODEL), row),
                     pl.BlockSpec((D_MODEL // n_tiles, D_MODEL), row)]
        args += [rope_t[0], rope_t[1], wts["w_uq_t"], wts["q_nope_g_col"], wts["q_rope_g_col"],
                 wts["w_uk"], wts["w_uv_t"], wts["k_nope_g"], wts["w_out_f32"], wts["w_ple_gate_f32"]]
        out_specs += [
            pl.BlockSpec((1, N_HEADS * D_QK, tm), seq_blk),
            pl.BlockSpec((tm, N_HEADS * D_QK), row),
            pl.BlockSpec((1, tm // kv_block, N_HEADS, D_VX, kv_block), lambda i: seq_blk(i)[::2] + (0, 0, 0)),
            pl.BlockSpec((D_MODEL // n_tiles, D_MODEL), row),
            pl.BlockSpec((D_MODEL // n_tiles, D_MODEL), row),
            pl.BlockSpec((W_CHUNK, D_MODEL), w_chunk)]
        out_shape += [jax.ShapeDtypeStruct((n_seq, N_HEADS * D_QK, tiles_per_seq * tm), bf16),
                      jax.ShapeDtypeStruct((n_tok, N_HEADS * D_QK), bf16),
                      jax.ShapeDtypeStruct((n_seq, tiles_per_seq * tm // kv_block, N_HEADS, D_VX, kv_block),
                                           bf16),
                      jax.ShapeDtypeStruct((D_MODEL, D_MODEL), bf16),
                      jax.ShapeDtypeStruct((D_MODEL, D_MODEL), bf16),
                      jax.ShapeDtypeStruct((D_IN, D_MODEL), bf16)]
        scratch += [pltpu.VMEM((n_conv * W_CHUNK, D_MODEL), bf16)]
    else:
        in_specs += [_resident((Q_RANK, N_HEADS * D_QK)), _resident((1, D_NOPE)), _resident((1, LANES)),
                     _resident((1, D_NOPE)), _resident((N_HEADS, D_NOPE, KV_RANK))]
        args += [wts["w_uq"], wts["q_nope_g"], wts["q_rope_g"], wts["k_nope_g"], wts["w_uk_t"]]
        out_specs += [pl.BlockSpec((N_HEADS, tm, KV_RANK), lambda i: (0, i, 0)),
                      pl.BlockSpec((N_HEADS, tm, D_ROPE), lambda i: (0, i, 0))]
        out_shape += [jax.ShapeDtypeStruct((N_HEADS, n_tok, KV_RANK), bf16),
                      jax.ShapeDtypeStruct((N_HEADS, n_tok, D_ROPE), bf16)]
    body = functools.partial(_in_proj_body, n_conv=n_conv, nseg=nseg, seg_len=seg_len,
                             tiles_per_seq=tiles_per_seq, carry=carry, fuse_qkv=fuse_qkv)
    return pl.pallas_call(
        body,
        grid=(n_conv + n_tiles,),
        in_specs=in_specs,
        out_specs=out_specs,
        out_shape=out_shape,
        scratch_shapes=scratch,
        compiler_params=_cparams(1),
        name="in_proj",
    )(*args)


def _attn_prompt_body(qt_ref, k_ref, vt_ref, gm_ref, o_ref, m_ref, acc_ref, s_ref, mx_ref, *, tk):
    g = pl.program_id(1)
    tq = 2 * tk
    hi = slice(tk, tq)
    m_ref[...] = jnp.full(m_ref.shape, NEG_INF, f32)
    acc_ref[...] = jnp.zeros(acc_ref.shape, f32)

    def column_max(s):
        x = jnp.max(s.reshape(s.shape[0] // SUBLANES, SUBLANES, s.shape[1]), axis=0)
        for shift in (4, 2, 1):
            x = jnp.maximum(x, pltpu.roll(x, shift, 0))
        return x

    def keys(h, j):
        return k_ref[pl.ds(pl.multiple_of(j * tk, tk), tk), h * D_QK:(h + 1) * D_QK]

    def chunk_visible(n_q):
        k_chunk = lax.broadcasted_iota(jnp.int32, (tk, n_q), 0) // CHUNK
        q_chunk = lax.broadcasted_iota(jnp.int32, (tk, n_q), 1) // CHUNK
        return k_chunk <= q_chunk

    def scores(h, j, buf, first_diag=False):
        s = _dot(keys(h, j), qt_ref[0, h * D_QK:(h + 1) * D_QK, :])
        if first_diag:
            s = jnp.where(chunk_visible(tq), s, NEG_INF)
        s_ref[buf, h] = s
        mx_ref[buf, h] = column_max(s)

    def scores_hi(h, j, buf):
        s = _dot(keys(h, j), qt_ref[0, h * D_QK:(h + 1) * D_QK, hi])
        s = jnp.where(chunk_visible(tk), s, NEG_INF)
        s_ref[buf, h, :, hi] = s
        mx_ref[buf, h, :, hi] = column_max(s)

    def values(h, j, buf, cols=slice(None)):
        m_old = m_ref[h, :, cols]
        m_new = jnp.maximum(m_old, mx_ref[buf, h, :, cols])
        n_q = m_new.shape[1]
        alpha = jnp.exp2(m_old - m_new)
        s = s_ref[buf, h, :, cols]
        p = jnp.exp2(s.reshape(tk // SUBLANES, SUBLANES, n_q) - m_new[None]).reshape(tk, n_q)
        m_ref[h, :, cols] = m_new
        acc = acc_ref[h, :, cols].reshape(D_VX // SUBLANES, SUBLANES, n_q) * alpha[None]
        acc_ref[h, :, cols] = acc.reshape(D_VX, n_q) + _dot(vt_ref[0, j, h], p.astype(bf16))

    def both(score_fn, value_fn):
        for h in range(N_HEADS):
            score_fn(h)
            value_fn(h)

    d0, d1 = 2 * g, 2 * g + 1
    for h in range(N_HEADS):
        scores(h, d0, 0, first_diag=True)
    both(lambda h: scores_hi(h, d1, 1), lambda h: values(h, d0, 0))

    @pl.when(g == 0)
    def _():
        for h in range(N_HEADS):
            values(h, d1, 1, hi)

    @pl.when(g > 0)
    def _():
        both(lambda h: scores(h, 0, 0), lambda h: values(h, d1, 1, hi))

        def step_pair(u, carry):
            both(lambda h: scores(h, 2 * u + 1, 1), lambda h: values(h, 2 * u, 0))
            both(lambda h: scores(h, 2 * u + 2, 0), lambda h: values(h, 2 * u + 1, 1))
            return carry

        lax.fori_loop(0, g - 1, step_pair, 0)
        both(lambda h: scores(h, d0 - 1, 1), lambda h: values(h, d0 - 2, 0))
        for h in range(N_HEADS):
            values(h, d0 - 1, 1)

    for h in range(N_HEADS):
        hs = slice(h * D_V, (h + 1) * D_V)
        o = (acc_ref[h, 0:D_V, :] / acc_ref[h, D_V:D_V + 1, :]).T
        o_ref[:, hs] = (o * gm_ref[:, hs].astype(f32)).astype(bf16)


def _attn_prompt(qt, k, vt, gm, *, n_seq, seq_len, tk):
    tq = 2 * tk
    nq = seq_len // tq
    body = functools.partial(_attn_prompt_body, tk=tk)
    return pl.pallas_call(
        body,
        grid=(n_seq, nq),
        in_specs=[pl.BlockSpec((1, N_HEADS * D_QK, tq), lambda b, i: (b, 0, i)),
                  pl.BlockSpec((seq_len, N_HEADS * D_QK), lambda b, i: (b, 0)),
                  pl.BlockSpec((1, seq_len // tk, N_HEADS, D_VX, tk), lambda b, i: (b, 0, 0, 0, 0)),
                  pl.BlockSpec((tq, D_MLA), lambda b, i: (b * nq + i, 0))],
        out_specs=pl.BlockSpec((tq, D_MLA), lambda b, i: (b * nq + i, 0)),
        out_shape=jax.ShapeDtypeStruct((n_seq * seq_len, D_MLA), bf16),
        scratch_shapes=[pltpu.VMEM((N_HEADS, SUBLANES, tq), f32), pltpu.VMEM((N_HEADS, D_VX, tq), f32),
                        pltpu.VMEM((2, N_HEADS, tk, tq), f32),
                        pltpu.VMEM((2, N_HEADS, SUBLANES, tq), f32)],
        compiler_params=_cparams(2),
        name="attn_prompt",
    )(qt, k, vt, gm)


def _attn_sample_body(cckv_ref, ckr_t_ref, nckv_ref, nkr_ref, qa_ref, qr_ref, gm_ref, wukt_ref, wuv_ref,
                      o_ref, c_ref, ktail_ref, s_ref, *, past, t_new, chunks, n_b):
    n_tail = c_ref.shape[1] - past
    rows = N_HEADS * t_new
    c_ref[:, past:past + t_new, :] = nckv_ref[...].astype(bf16)
    c_ref[:, past + t_new:, :] = jnp.zeros((n_b, n_tail - t_new, KV_RANK), bf16)
    ktail_ref[:, 0:t_new, :] = nkr_ref[...].astype(bf16)
    ktail_ref[:, t_new:, :] = jnp.zeros((n_b, n_tail - t_new, D_ROPE), bf16)

    def chunk_scores(b, start, size):
        cached = start < past
        qa = qa_ref[:, b * t_new:(b + 1) * t_new, :].reshape(rows, KV_RANK)
        qr = qr_ref[:, b * t_new:(b + 1) * t_new, :].reshape(rows, D_ROPE)
        if cached:
            c32 = cckv_ref[b, start:start + size, :]
            c_ref[b, start:start + size, :] = c32.astype(bf16)
            c_t = c32.T.astype(bf16)
            k_t = _dot(wukt_ref[...], c_t)
            s_nope = _dot(qa, c_t)
            s_rope = _dot(qr, ckr_t_ref[b, :, start:start + size].astype(bf16))
        else:
            c = c_ref[b, start:start + size, :]
            k_t = _dot_nt(wukt_ref[...], c)
            s_nope = _dot_nt(qa, c)
            s_rope = _dot_nt(qr, ktail_ref[b])
        ssq = jnp.sum((k_t * k_t).reshape(N_HEADS, D_NOPE, size), axis=1)
        r = lax.rsqrt(ssq * (1.0 / D_NOPE) + EPS)
        for h in range(N_HEADS):
            hs = slice(h * t_new, (h + 1) * t_new)
            s = (s_nope[hs] * r[h:h + 1, :] + s_rope[hs]) * ATTN_SCALE
            if not cached:
                key = lax.broadcasted_iota(jnp.int32, (t_new, size), 1)
                s = jnp.where(key < t_new, s, NEG_INF)
            s_ref[b, hs, start:start + size] = s

    def chunk_values(b, start, size, state):
        m, l, acc = state
        s = s_ref[b, :, start:start + size]
        m_new = jnp.maximum(m, jnp.max(s, axis=-1, keepdims=True))
        alpha = jnp.exp(m - m_new)
        p = jnp.exp(s - m_new)
        l = alpha * l + jnp.sum(p, axis=-1, keepdims=True)
        acc = alpha * acc + _dot(p.astype(bf16), c_ref[b, start:start + size, :])
        return m_new, l, acc

    def finish(b, state):
        _, l, acc = state
        o_lat = (acc / l).astype(bf16)
        for h in range(N_HEADS):
            o = _dot(o_lat[h * t_new:(h + 1) * t_new], wuv_ref[h])
            hs = slice(h * D_V, (h + 1) * D_V)
            bs = slice(b * t_new, (b + 1) * t_new)
            o_ref[bs, hs] = (o * gm_ref[bs, hs].astype(f32)).astype(bf16)

    items = [(b,) + ch for b in range(n_b) for ch in chunks]
    init = (jnp.full((rows, 1), NEG_INF, f32), jnp.zeros((rows, 1), f32),
            jnp.zeros((rows, KV_RANK), f32))
    state = init
    chunk_scores(*items[0])
    for nxt, cur in zip(items[1:] + [None], items):
        if nxt is not None:
            chunk_scores(*nxt)
        state = chunk_values(*cur, state)
        if nxt is None or nxt[0] != cur[0]:
            finish(cur[0], state)
            state = init


def _attn_sample(cache_ckv, cache_krope_t, ckv_new, krope_new, q_abs, q_rope, gm, wts, *, t_new, n_b):
    n_seq, past, _ = cache_ckv.shape
    chunk = 512
    chunks = tuple((s, min(chunk, past - s)) for s in range(0, past, chunk)) + ((past, LANES),)
    s_pad = past + LANES
    body = functools.partial(_attn_sample_body, past=past, t_new=t_new, chunks=chunks, n_b=n_b)
    return pl.pallas_call(
        body,
        grid=(n_seq // n_b,),
        in_specs=[pl.BlockSpec((n_b, past, KV_RANK), lambda b: (b, 0, 0)),
                  pl.BlockSpec((n_b, D_ROPE, past), lambda b: (b, 0, 0)),
                  pl.BlockSpec((n_b, t_new, KV_RANK), lambda b: (b, 0, 0)),
                  pl.BlockSpec((n_b, t_new, D_ROPE), lambda b: (b, 0, 0)),
                  pl.BlockSpec((N_HEADS, n_b * t_new, KV_RANK), lambda b: (0, b, 0)),
                  pl.BlockSpec((N_HEADS, n_b * t_new, D_ROPE), lambda b: (0, b, 0)),
                  pl.BlockSpec((n_b * t_new, D_MLA), lambda b: (b, 0)),
                  _resident((N_HEADS * D_NOPE, KV_RANK)),
                  _resident((N_HEADS, KV_RANK, D_V))],
        out_specs=pl.BlockSpec((n_b * t_new, D_MLA), lambda b: (b, 0)),
        out_shape=jax.ShapeDtypeStruct((n_seq * t_new, D_MLA), bf16),
        scratch_shapes=[pltpu.VMEM((n_b, s_pad, KV_RANK), bf16),
                        pltpu.VMEM((n_b, s_pad - past, D_ROPE), bf16),
                        pltpu.VMEM((n_b, N_HEADS * t_new, s_pad), f32)],
        compiler_params=_cparams(1),
        name="attn_sample",
    )(cache_ckv, cache_krope_t, ckv_new, krope_new, q_abs, q_rope, gm,
      wts["w_uk_t"].reshape(N_HEADS * D_NOPE, KV_RANK), wts["w_uv3"])


def _out_proj_body(x_ref, pa_ref, ma_ref, p_ref, wo_ref, png_ref, wg_ref, bg_ref, wple_ref, y_ref):
    h = x_ref[...] + _dot(pa_ref[...], wo_ref[0:D_POOL, :]) + _dot(ma_ref[...], wo_ref[D_POOL:, :])
    hn = (h * _rms(h, D_MODEL) * png_ref[...]).astype(bf16)
    gate = jax.nn.sigmoid(_dot(hn, wg_ref[...]) + bg_ref[...])
    y_ref[...] = h + gate * _dot(p_ref[...].astype(bf16), wple_ref[...])


def _out_proj(x2d, pool_act, mla_act, p2d, wts, *, tm):
    n_tok = x2d.shape[0]
    row = lambda i: (i, 0)
    return pl.pallas_call(
        _out_proj_body,
        grid=(n_tok // tm,),
        in_specs=[pl.BlockSpec((tm, D_MODEL), row), pl.BlockSpec((tm, D_POOL), row),
                  pl.BlockSpec((tm, D_MLA), row), pl.BlockSpec((tm, D_PLE), row),
                  _resident((D_MODEL, D_MODEL)), _resident((1, D_MODEL)),
                  _resident((D_MODEL, D_MODEL)), _resident((1, D_MODEL)),
                  _resident((D_PLE, D_MODEL))],
        out_specs=pl.BlockSpec((tm, D_MODEL), row),
        out_shape=jax.ShapeDtypeStruct((n_tok, D_MODEL), f32),
        compiler_params=_cparams(1),
        name="out_proj",
    )(x2d, pool_act, mla_act, p2d, wts["w_out"], wts["ple_norm_g"], wts["w_ple_gate"],
      wts["b_ple_gate"], wts["w_ple"])


def _rope_tables(pos0, t, reps=1):
    pos = (pos0 + np.arange(t)).astype(np.float64)
    inv = ROPE_THETA ** (-(np.arange(0, D_ROPE, 2, dtype=np.float64) / D_ROPE))
    ang = pos[:, None] * inv[None, :]
    cos, sin = np.cos(ang), np.sin(ang)
    zero = np.zeros((t, LANES - D_ROPE))
    rows = (np.concatenate([cos, cos, zero], axis=-1), np.concatenate([-sin, sin, zero], axis=-1),
            pos[:, None])
    as_f32 = lambda a: jnp.asarray(a.astype(np.float32))
    return tuple(as_f32(np.tile(a, (reps, 1))) for a in rows) + ((as_f32(cos.T), as_f32(sin.T)),)


def _prep_weights(norm_g, w_in, q_norm_g, w_uq, kv_norm_g, w_ukv, q_nope_g, q_rope_g, k_nope_g,
                  k_rope_g, w_pool, pool_scale, w_out, ple_norm_g, w_ple_gate, b_ple_gate, w_ple):
    w_uq_r = jnp.pad(w_uq.reshape(Q_RANK, N_HEADS, D_NOPE + D_ROPE),
                     ((0, 0), (0, 0), (0, D_QK - D_NOPE - D_ROPE))).reshape(Q_RANK, N_HEADS * D_QK)
    w_ukv3 = w_ukv.reshape(KV_RANK, N_HEADS, D_NOPE + D_V)
    w_uk3, w_uv3 = w_ukv3[..., :D_NOPE], w_ukv3[..., D_NOPE:]
    pad_rope = lambda g: jnp.pad(g, (0, LANES - D_ROPE))[None, :]
    return {
        "norm_g": norm_g[None, :], "w_in_t_f32": w_in.T, "q_norm_g": q_norm_g[None, :],
        "kv_norm_g": kv_norm_g[None, :], "k_rope_g": pad_rope(k_rope_g),
        "w_pool": w_pool.astype(bf16), "pool_scale": pool_scale[None, :],
        "w_uq": w_uq_r.astype(bf16), "q_nope_g": q_nope_g[None, :], "q_rope_g": pad_rope(q_rope_g),
        "w_uq_t": w_uq.T.astype(bf16),
        "q_nope_g_col": q_nope_g[:, None], "q_rope_g_col": q_rope_g[:, None],
        "k_nope_g": k_nope_g[None, :],
        "w_uk": w_uk3.reshape(KV_RANK, N_HEADS * D_NOPE).astype(bf16),
        "w_uv_t": w_uv3.reshape(KV_RANK, D_MLA).T.astype(bf16),
        "w_uk_t": jnp.transpose(w_uk3, (1, 2, 0)).astype(bf16),
        "w_uv3": jnp.transpose(w_uv3, (1, 0, 2)).astype(bf16),
        "w_out_f32": w_out, "w_ple_gate_f32": w_ple_gate,
        "ple_norm_g": ple_norm_g[None, :], "b_ple_gate": b_ple_gate[None, :],
        "w_ple": w_ple.astype(bf16),
    }


def _layer_prompt(x, p, wts):
    n_seq, seq_len, _ = x.shape
    x2d = x.reshape(n_seq * seq_len, D_MODEL)
    cos2, sin2, pos, rope_t = _rope_tables(0, seq_len)
    hist = jnp.zeros((1, HALO, D_POOL), f32)
    pact, gm, ckv, krope, npool, qt, k, vt, w_out16, w_gate16, w_in_t16 = _in_proj(
        x2d, hist, cos2, sin2, pos, wts, nseg=1, seg_len=PROMPT_TILE, n_seq=n_seq, carry=True,
        rope_t=rope_t, kv_block=ATTN_BLOCK)
    wts.update(w_out=w_out16, w_ple_gate=w_gate16, w_in_t=w_in_t16)
    mact = _attn_prompt(qt, k, vt, gm, n_seq=n_seq, seq_len=seq_len, tk=ATTN_BLOCK)
    y = _out_proj(x2d, pact, mact, p.reshape(n_seq * seq_len, D_PLE), wts, tm=OUT_TILE)
    return (y.reshape(x.shape), ckv.reshape(n_seq, seq_len, KV_RANK),
            krope.reshape(n_seq, seq_len, D_ROPE), npool)


def _layer_sample(x, p, state_pool, cache_ckv, cache_krope, wts):
    n_seq, t_new, _ = x.shape
    past = cache_ckv.shape[1]
    n_tok = n_seq * t_new
    x2d = x.reshape(n_tok, D_MODEL)
    cos2, sin2, pos, _ = _rope_tables(past, t_new, reps=n_seq)
    hist = jnp.transpose(state_pool, (1, 0, 2))
    pact, gm, ckv, krope, npool, q_abs, q_rope = _in_proj(
        x2d, hist, cos2, sin2, pos, wts, nseg=SAMPLE_SEGS_PER_TILE, seg_len=t_new, n_seq=n_seq,
        carry=False)
    ckv3 = ckv.reshape(n_seq, t_new, KV_RANK)
    krope3 = krope.reshape(n_seq, t_new, D_ROPE)
    cache_krope_t = jnp.transpose(cache_krope, (0, 2, 1))
    mact = _attn_sample(cache_ckv, cache_krope_t, ckv3, krope3, q_abs, q_rope, gm, wts, t_new=t_new,
                        n_b=SAMPLE_BATCH_PER_STEP)
    y = _out_proj(x2d, pact, mact, p.reshape(n_tok, D_PLE), wts, tm=OUT_TILE)
    return y.reshape(x.shape), ckv3, krope3, jnp.transpose(npool, (1, 0, 2))


def kernel(x_prompt, x_sample, cache_ckv, cache_krope, state_pool, p_prompt, p_sample, norm_g, w_in,
           q_norm_g, w_uq, kv_norm_g, w_ukv, q_nope_g, q_rope_g, k_nope_g, k_rope_g, w_pool,
           pool_scale, w_out, ple_norm_g, w_ple_gate, b_ple_gate, w_ple):
    depth = norm_g.shape[0]
    layer_w = (norm_g, w_in, q_norm_g, w_uq, kv_norm_g, w_ukv, q_nope_g, q_rope_g, k_nope_g, k_rope_g,
               w_pool, pool_scale, w_out, ple_norm_g, w_ple_gate, b_ple_gate, w_ple)
    yp, ys = x_prompt, x_sample
    outs = [[] for _ in range(6)]
    for i in range(depth):
        wts = _prep_weights(*(w[i] for w in layer_w))
        yp, c1, k1, s1 = _layer_prompt(yp, p_prompt[i], wts)
        ys, c2, k2, s2 = _layer_sample(ys, p_sample[i], state_pool[i], cache_ckv[i], cache_krope[i], wts)
        for lst, val in zip(outs, (c1, k1, s1, c2, k2, s2)):
            lst.append(val)
    return (yp, ys) + tuple(jnp.stack(o) for o in outs)
```

```python
import functools

import jax
import jax.numpy as jnp
import numpy as np
from jax import lax
from jax.experimental import pallas as pl
from jax.experimental.pallas import tpu as pltpu

D_MODEL = 2048
CHUNK = 64
D_POOL = 1024
POOL_WINDOWS = (2, 4, 8, 16)
POOL_GROUP = 256
POOL_HIST = 15
HALO = 16
N_HEADS = 8
D_NOPE = 128
D_ROPE = 64
D_V = 128
D_VX = D_V + 16
D_MLA = N_HEADS * D_V
Q_RANK = 512
KV_RANK = 256
D_PLE = 256
D_QK = 256
D_K = N_HEADS * D_NOPE + (D_QK - D_NOPE)
ROPE_THETA = 10000.0
EPS = 1e-6
ATTN_SCALE = (D_NOPE + D_ROPE) ** -0.5
NEG_INF = -1e30
EXP2_SCALE = ATTN_SCALE * float(np.log2(np.e))
LANES = 128
SUBLANES = 8

C_U, C_GP, C_Q, C_KV, C_KR, C_GM = 0, 1024, 2048, 2560, 2816, 2880
D_IN = 3904

VMEM_LIMIT = 56 * 1024 * 1024
PROMPT_TILE = 256
W_CHUNK = 512
ATTN_BLOCK = 256
OUT_TILE = 512
SAMPLE_BATCH_PER_STEP = 4
SAMPLE_SEGS_PER_TILE = 16

f32 = jnp.float32
bf16 = jnp.bfloat16


def _cparams(n_axes):
    return pltpu.CompilerParams(dimension_semantics=("arbitrary",) * n_axes,
                                vmem_limit_bytes=VMEM_LIMIT)


def _resident(shape):
    nd = len(shape)
    return pl.BlockSpec(shape, lambda *_: (0,) * nd, pipeline_mode=pl.Buffered(1))


def _rms(x, n):
    return lax.rsqrt(jnp.sum(x * x, axis=-1, keepdims=True) * (1.0 / n) + EPS)


def _rms_cols(x_t, n):
    return lax.rsqrt(jnp.sum(x_t * x_t, axis=0, keepdims=True) * (1.0 / n) + EPS)


def _dot(a, b):
    return jnp.dot(a, b, preferred_element_type=f32)


def _dot_nt(a, b):
    return lax.dot_general(a, b, (((1,), (1,)), ((), ())), preferred_element_type=f32)


def _rope128(x, cos2, sin2):
    lane = lax.broadcasted_iota(jnp.int32, x.shape, 1)
    swapped = jnp.where(lane < D_ROPE // 2, pltpu.roll(x, LANES - D_ROPE // 2, 1),
                        pltpu.roll(x, D_ROPE // 2, 1))
    return x * cos2 + swapped * sin2


def _queries_t(ql, wqt_ref, cos_ref, sin_ref, qng_ref, qrg_ref, qt_ref):
    q_t = _dot_nt(wqt_ref[...], ql)
    tm = q_t.shape[1]
    cos, sin = cos_ref[...], sin_ref[...]
    half = D_ROPE // 2
    for h in range(N_HEADS):
        r0 = h * D_QK
        s0 = h * (D_NOPE + D_ROPE)
        qn = q_t[s0:s0 + D_NOPE]
        qn = qn * (_rms_cols(qn, D_NOPE) * EXP2_SCALE) * qng_ref[...]
        qt_ref[0, r0:r0 + D_NOPE, :] = qn.astype(bf16)
        qr = q_t[s0 + D_NOPE:s0 + D_NOPE + D_ROPE]
        qr = qr * (_rms_cols(qr, D_ROPE) * EXP2_SCALE) * qrg_ref[...]
        x1, x2 = qr[:half], qr[half:]
        qt_ref[0, r0 + D_NOPE:r0 + D_NOPE + half, :] = (x1 * cos - x2 * sin).astype(bf16)
        qt_ref[0, r0 + D_NOPE + half:r0 + D_NOPE + D_ROPE, :] = (x2 * cos + x1 * sin).astype(bf16)
        qt_ref[0, r0 + D_NOPE + D_ROPE:r0 + D_QK, :] = jnp.zeros((D_QK - D_NOPE - D_ROPE, tm), bf16)


def _queries_absorbed(ql, wq_ref, cos_ref, sin_ref, qng_ref, qrg_ref, kng_ref, wukt_ref, qa_ref, qr_ref):
    q = _dot(ql, wq_ref[...])
    cos2, sin2 = cos_ref[...], sin_ref[...]
    for h in range(N_HEADS):
        qn = q[:, h * D_QK:h * D_QK + D_NOPE]
        qn = qn * _rms(qn, D_NOPE) * qng_ref[...]
        qa_ref[h] = _dot((qn * kng_ref[...]).astype(bf16), wukt_ref[h]).astype(bf16)
        qr = q[:, h * D_QK + D_NOPE:(h + 1) * D_QK]
        qr = _rope128(qr * _rms(qr, D_ROPE) * qrg_ref[...], cos2, sin2)
        qr_ref[h] = qr[:, :D_ROPE].astype(bf16)


def _keys_values(c, kr128, wk_ref, wvt_ref, kng_ref, k_ref, vt_ref):
    tm = c.shape[0]
    k = _dot(c, wk_ref[...])
    for h in range(N_HEADS):
        kn = k[:, h * D_NOPE:(h + 1) * D_NOPE]
        k_ref[:, h * D_NOPE:(h + 1) * D_NOPE] = (kn * _rms(kn, D_NOPE) * kng_ref[...]).astype(bf16)
    k_ref[:, N_HEADS * D_NOPE:] = kr128.astype(bf16)
    v_t = _dot_nt(wvt_ref[...], c).astype(bf16)
    n_blk, tk = vt_ref.shape[1], vt_ref.shape[4]
    row = lax.broadcasted_iota(jnp.int32, (D_VX - D_V, tk), 0)
    ones_row = jnp.where(row == 0, 1.0, 0.0).astype(bf16)
    for j in range(n_blk):
        for h in range(N_HEADS):
            vt_ref[0, j, h, 0:D_V, :] = v_t[h * D_V:(h + 1) * D_V, j * tk:(j + 1) * tk]
            vt_ref[0, j, h, D_V:D_VX, :] = ones_row


def _in_proj_body(*refs, n_conv, **static):
    if not static["fuse_qkv"]:
        _in_proj_tile(pl.program_id(0), refs, **static)
        return
    i = pl.program_id(0)
    w32_ref, w16_ref, wt_ref = refs[6], refs[-3], refs[-1]

    def convert(rows):
        chunk = w32_ref[0:rows, :].astype(bf16)
        w16_ref[0:rows, :] = chunk
        wt_ref[pl.ds(pl.multiple_of(i * W_CHUNK, W_CHUNK), rows), :] = chunk

    last_rows = D_IN - (n_conv - 1) * W_CHUNK
    pl.when(i < n_conv - 1)(lambda: convert(W_CHUNK))
    pl.when(i == n_conv - 1)(lambda: convert(last_rows))

    @pl.when(i >= n_conv)
    def _():
        _in_proj_tile(i - n_conv, refs[:6] + (wt_ref,) + refs[7:-3] + refs[-2:-1], **static)


def _in_proj_tile(i, refs, *, nseg, seg_len, tiles_per_seq, carry, fuse_qkv):
    (x_ref, hist_ref, cos_ref, sin_ref, pos_ref, ng_ref, wt_ref, qg_ref, kvg_ref, krg_ref, wp_ref,
     ps_ref) = refs[:12]
    if fuse_qkv:
        (cost_ref, sint_ref, wqt_ref, qng_ref, qrg_ref, wk_ref, wvt_ref, kng_ref, wo32_ref,
         wg32_ref) = refs[12:22]
        (pact_ref, gm_ref, ckv_ref, kr_ref, npool_ref, qt_ref, k_ref, vt_ref, wo16_ref, wg16_ref,
         ucat_ref) = refs[22:]
    else:
        wq_ref, qng_ref, qrg_ref, kng_ref, wukt_ref = refs[12:17]
        pact_ref, gm_ref, ckv_ref, kr_ref, npool_ref, qa_ref, qr_ref, ucat_ref = refs[17:]
    tm = nseg * seg_len

    x = x_ref[...]
    xn = (x * _rms(x, D_MODEL) * ng_ref[...]).astype(bf16)

    u = _dot_nt(xn, wt_ref[C_U:C_U + D_POOL, :])
    if carry:
        @pl.when(i % tiles_per_seq == 0)
        def _():
            ucat_ref[:, 0:HALO, :] = jnp.zeros((nseg, HALO, D_POOL), f32)
    else:
        ucat_ref[:, 0:HALO - POOL_HIST, :] = jnp.zeros((nseg, HALO - POOL_HIST, D_POOL), f32)
        for t in range(POOL_HIST):
            ucat_ref[:, HALO - POOL_HIST + t, :] = hist_ref[t]
    ucat_ref[:, HALO:HALO + seg_len, :] = u.reshape(nseg, seg_len, D_POOL)

    pos = pos_ref[...]
    gp = _dot_nt(xn, wt_ref[C_GP:C_GP + D_POOL, :])
    gate = gp * jax.nn.sigmoid(gp) * ps_ref[...]

    def pool_group(g):
        w = POOL_WINDOWS[g]
        sl = slice(g * POOL_GROUP, (g + 1) * POOL_GROUP)
        acc = ucat_ref[:, :, sl]
        shift = 1
        while shift < w:
            acc = acc + pltpu.roll(acc, shift, 1)
            shift *= 2
        acc = acc[:, HALO:, :].reshape(tm, POOL_GROUP)
        inv_cnt = 1.0 / jnp.minimum(pos + 1.0, float(w))
        d = (acc * inv_cnt - u[:, sl]).astype(bf16)
        mixed = _dot(d, wp_ref[g])
        pact_ref[:, sl] = (mixed * gate[:, sl]).astype(bf16)

    cq = _dot_nt(xn, wt_ref[C_Q:C_Q + Q_RANK, :])
    pool_group(0)
    pool_group(1)
    ql = (cq * _rms(cq, Q_RANK) * qg_ref[...]).astype(bf16)
    if fuse_qkv:
        _queries_t(ql, wqt_ref, cost_ref, sint_ref, qng_ref, qrg_ref, qt_ref)
    else:
        _queries_absorbed(ql, wq_ref, cos_ref, sin_ref, qng_ref, qrg_ref, kng_ref, wukt_ref, qa_ref, qr_ref)

    ckv = _dot_nt(xn, wt_ref[C_KV:C_KV + KV_RANK, :])
    pool_group(2)
    ckv = ckv * _rms(ckv, KV_RANK) * kvg_ref[...]
    ckv_ref[...] = ckv

    kr = _dot_nt(xn, wt_ref[C_KR:C_KR + LANES, :])
    gm = _dot_nt(xn, wt_ref[C_GM:C_GM + D_MLA, :])
    pool_group(3)
    if carry:
        npool_ref[...] = ucat_ref[:, seg_len + 1:seg_len + HALO, :]
        ucat_ref[:, 0:HALO, :] = ucat_ref[:, seg_len:seg_len + HALO, :]
    else:
        for t in range(POOL_HIST):
            npool_ref[t] = ucat_ref[:, seg_len + 1 + t, :]
    lane = lax.broadcasted_iota(jnp.int32, (tm, LANES), 1)
    kr = jnp.where(lane < D_ROPE, kr, 0.0)
    kr = _rope128(kr * _rms(kr, D_ROPE) * krg_ref[...], cos_ref[...], sin_ref[...])
    kr_ref[...] = kr[:, :D_ROPE]
    gm_ref[...] = (gm * jax.nn.sigmoid(gm)).astype(bf16)
    if fuse_qkv:
        _keys_values(ckv.astype(bf16), kr, wk_ref, wvt_ref, kng_ref, k_ref, vt_ref)
        wo16_ref[...] = wo32_ref[...].astype(bf16)
        wg16_ref[...] = wg32_ref[...].astype(bf16)


def _in_proj(x2d, hist, cos2, sin2, pos, wts, *, nseg, seg_len, n_seq, carry, rope_t=None, kv_block=None):
    n_tok = x2d.shape[0]
    tm = nseg * seg_len
    n_tiles = n_tok // tm
    tiles_per_seq = n_tiles // n_seq if carry else 1
    tab_tiles = cos2.shape[0] // tm
    fuse_qkv = rope_t is not None
    n_conv = pl.cdiv(D_IN, W_CHUNK) if fuse_qkv else 0
    tile = (lambda i: jnp.maximum(i - n_conv, 0)) if n_conv else (lambda i: i)
    row = lambda i: (tile(i), 0)
    tab = lambda i: (tile(i) % tab_tiles, 0)
    if carry:
        hist_spec = pl.BlockSpec((1, HALO, D_POOL), lambda i: (0, 0, 0))
        npool_spec = pl.BlockSpec((1, POOL_HIST, D_POOL), lambda i: (tile(i) // tiles_per_seq, 0, 0))
    else:
        hist_spec = pl.BlockSpec((POOL_HIST, nseg, D_POOL), lambda i: (0, tile(i), 0))
        npool_spec = pl.BlockSpec((POOL_HIST, nseg, D_POOL), lambda i: (0, tile(i), 0))
    x_spec = pl.BlockSpec((tm, D_MODEL), row)
    if fuse_qkv:
        w_chunk = lambda i: (jnp.minimum(i, n_conv - 1), 0)
        w_spec, w_arg = pl.BlockSpec((W_CHUNK, D_MODEL), w_chunk), wts["w_in_t_f32"]
    else:
        w_spec, w_arg = _resident((D_IN, D_MODEL)), wts["w_in_t"]
    in_specs = [
        x_spec, hist_spec, pl.BlockSpec((tm, LANES), tab),
        pl.BlockSpec((tm, LANES), tab), pl.BlockSpec((tm, 1), tab), _resident((1, D_MODEL)),
        w_spec, _resident((1, Q_RANK)), _resident((1, KV_RANK)),
        _resident((1, LANES)), _resident((4, POOL_GROUP, POOL_GROUP)), _resident((1, D_POOL)),
    ]
    args = [x2d, hist, cos2, sin2, pos, wts["norm_g"], w_arg, wts["q_norm_g"],
            wts["kv_norm_g"], wts["k_rope_g"], wts["w_pool"], wts["pool_scale"]]
    out_specs = [pl.BlockSpec((tm, D_POOL), row), pl.BlockSpec((tm, D_MLA), row),
                 pl.BlockSpec((tm, KV_RANK), row), pl.BlockSpec((tm, D_ROPE), row), npool_spec]
    out_shape = [jax.ShapeDtypeStruct((n_tok, D_POOL), bf16), jax.ShapeDtypeStruct((n_tok, D_MLA), bf16),
                 jax.ShapeDtypeStruct((n_tok, KV_RANK), f32), jax.ShapeDtypeStruct((n_tok, D_ROPE), f32),
                 jax.ShapeDtypeStruct((n_seq, POOL_HIST, D_POOL) if carry else (POOL_HIST, n_seq, D_POOL), f32)]
    scratch = [pltpu.VMEM((nseg, HALO + seg_len, D_POOL), f32)]
    if fuse_qkv:
        seq_tab = lambda i: (0, tile(i) % tiles_per_seq)
        seq_blk = lambda i: (tile(i) // tiles_per_seq, 0, tile(i) % tiles_per_seq)
        in_specs += [pl.BlockSpec((D_ROPE // 2, tm), seq_tab), pl.BlockSpec((D_ROPE // 2, tm), seq_tab),
                     _resident((N_HEADS * (D_NOPE + D_ROPE), Q_RANK)), _resident((D_NOPE, 1)),
                     _resident((D_ROPE, 1)),
                     _resident((KV_RANK, N_HEADS * D_NOPE)), _resident((D_MLA, KV_RANK)),
                     _resident((1, D_NOPE)),
                     pl.BlockSpec((D_MODEL // n_tiles, D_MODEL), row),
                     pl.BlockSpec((D_MODEL // n_tiles, D_MODEL), row)]
        args += [rope_t[0], rope_t[1], wts["w_uq_t"], wts["q_nope_g_col"], wts["q_rope_g_col"],
                 wts["w_uk"], wts["w_uv_t"], wts["k_nope_g"], wts["w_out_f32"], wts["w_ple_gate_f32"]]
        out_specs += [
            pl.BlockSpec((1, N_HEADS * D_QK, tm), seq_blk),
            pl.BlockSpec((tm, D_K), row),
            pl.BlockSpec((1, tm // kv_block, N_HEADS, D_VX, kv_block), lambda i: seq_blk(i)[::2] + (0, 0, 0)),
            pl.BlockSpec((D_MODEL // n_tiles, D_MODEL), row),
            pl.BlockSpec((D_MODEL // n_tiles, D_MODEL), row),
            pl.BlockSpec((W_CHUNK, D_MODEL), w_chunk)]
        out_shape += [jax.ShapeDtypeStruct((n_seq, N_HEADS * D_QK, tiles_per_seq * tm), bf16),
                      jax.ShapeDtypeStruct((n_tok, D_K), bf16),
                      jax.ShapeDtypeStruct((n_seq, tiles_per_seq * tm // kv_block, N_HEADS, D_VX, kv_block),
                                           bf16),
                      jax.ShapeDtypeStruct((D_MODEL, D_MODEL), bf16),
                      jax.ShapeDtypeStruct((D_MODEL, D_MODEL), bf16),
                      jax.ShapeDtypeStruct((D_IN, D_MODEL), bf16)]
        scratch += [pltpu.VMEM((n_conv * W_CHUNK, D_MODEL), bf16)]
    else:
        in_specs += [_resident((Q_RANK, N_HEADS * D_QK)), _resident((1, D_NOPE)), _resident((1, LANES)),
                     _resident((1, D_NOPE)), _resident((N_HEADS, D_NOPE, KV_RANK))]
        args += [wts["w_uq"], wts["q_nope_g"], wts["q_rope_g"], wts["k_nope_g"], wts["w_uk_t"]]
        out_specs += [pl.BlockSpec((N_HEADS, tm, KV_RANK), lambda i: (0, i, 0)),
                      pl.BlockSpec((N_HEADS, tm, D_ROPE), lambda i: (0, i, 0))]
        out_shape += [jax.ShapeDtypeStruct((N_HEADS, n_tok, KV_RANK), bf16),
                      jax.ShapeDtypeStruct((N_HEADS, n_tok, D_ROPE), bf16)]
    body = functools.partial(_in_proj_body, n_conv=n_conv, nseg=nseg, seg_len=seg_len,
                             tiles_per_seq=tiles_per_seq, carry=carry, fuse_qkv=fuse_qkv)
    return pl.pallas_call(
        body,
        grid=(n_conv + n_tiles,),
        in_specs=in_specs,
        out_specs=out_specs,
        out_shape=out_shape,
        scratch_shapes=scratch,
        compiler_params=_cparams(1),
        name="in_proj",
    )(*args)


def _attn_prompt_body(qt_ref, k_ref, vt_ref, gm_ref, o_ref, m_ref, acc_ref, s_ref, mx_ref, *, tk):
    g = pl.program_id(1)
    tq = 2 * tk
    hi = slice(tk, tq)
    m_ref[...] = jnp.full(m_ref.shape, NEG_INF, f32)
    acc_ref[...] = jnp.zeros(acc_ref.shape, f32)

    def column_max(s):
        x = jnp.max(s.reshape(s.shape[0] // SUBLANES, SUBLANES, s.shape[1]), axis=0)
        for shift in (4, 2, 1):
            x = jnp.maximum(x, pltpu.roll(x, shift, 0))
        return x

    def keys(h, j):
        rows = pl.ds(pl.multiple_of(j * tk, tk), tk)
        return jnp.concatenate([k_ref[rows, h * D_NOPE:(h + 1) * D_NOPE], k_ref[rows, N_HEADS * D_NOPE:]],
                               axis=1)

    def chunk_visible(n_q):
        k_chunk = lax.broadcasted_iota(jnp.int32, (tk, n_q), 0) // CHUNK
        q_chunk = lax.broadcasted_iota(jnp.int32, (tk, n_q), 1) // CHUNK
        return k_chunk <= q_chunk

    def scores(h, j, buf, first_diag=False):
        s = _dot(keys(h, j), qt_ref[0, h * D_QK:(h + 1) * D_QK, :])
        if first_diag:
            s = jnp.where(chunk_visible(tq), s, NEG_INF)
        s_ref[buf, h] = s
        mx_ref[buf, h] = column_max(s)

    def scores_hi(h, j, buf):
        s = _dot(keys(h, j), qt_ref[0, h * D_QK:(h + 1) * D_QK, hi])
        s = jnp.where(chunk_visible(tk), s, NEG_INF)
        s_ref[buf, h, :, hi] = s
        mx_ref[buf, h, :, hi] = column_max(s)

    def values(h, j, buf, cols=slice(None)):
        m_old = m_ref[h, :, cols]
        m_new = jnp.maximum(m_old, mx_ref[buf, h, :, cols])
        n_q = m_new.shape[1]
        alpha = jnp.exp2(m_old - m_new)
        s = s_ref[buf, h, :, cols]
        p = jnp.exp2(s.reshape(tk // SUBLANES, SUBLANES, n_q) - m_new[None]).reshape(tk, n_q)
        m_ref[h, :, cols] = m_new
        acc = acc_ref[h, :, cols].reshape(D_VX // SUBLANES, SUBLANES, n_q) * alpha[None]
        acc_ref[h, :, cols] = acc.reshape(D_VX, n_q) + _dot(vt_ref[0, j, h], p.astype(bf16))

    def both(score_fn, value_fn):
        for h in range(N_HEADS):
            score_fn(h)
            value_fn(h)

    d0, d1 = 2 * g, 2 * g + 1
    for h in range(N_HEADS):
        scores(h, d0, 0, first_diag=True)
    both(lambda h: scores_hi(h, d1, 1), lambda h: values(h, d0, 0))

    @pl.when(g == 0)
    def _():
        for h in range(N_HEADS):
            values(h, d1, 1, hi)

    @pl.when(g > 0)
    def _():
        both(lambda h: scores(h, 0, 0), lambda h: values(h, d1, 1, hi))

        def step_pair(u, carry):
            both(lambda h: scores(h, 2 * u + 1, 1), lambda h: values(h, 2 * u, 0))
            both(lambda h: scores(h, 2 * u + 2, 0), lambda h: values(h, 2 * u + 1, 1))
            return carry

        lax.fori_loop(0, g - 1, step_pair, 0)
        both(lambda h: scores(h, d0 - 1, 1), lambda h: values(h, d0 - 2, 0))
        for h in range(N_HEADS):
            values(h, d0 - 1, 1)

    for h in range(N_HEADS):
        hs = slice(h * D_V, (h + 1) * D_V)
        o = (acc_ref[h, 0:D_V, :] / acc_ref[h, D_V:D_V + 1, :]).T
        o_ref[:, hs] = (o * gm_ref[:, hs].astype(f32)).astype(bf16)


def _attn_prompt(qt, k, vt, gm, *, n_seq, seq_len, tk):
    tq = 2 * tk
    nq = seq_len // tq
    body = functools.partial(_attn_prompt_body, tk=tk)
    return pl.pallas_call(
        body,
        grid=(n_seq, nq),
        in_specs=[pl.BlockSpec((1, N_HEADS * D_QK, tq), lambda b, i: (b, 0, i)),
                  pl.BlockSpec((seq_len, D_K), lambda b, i: (b, 0)),
                  pl.BlockSpec((1, seq_len // tk, N_HEADS, D_VX, tk), lambda b, i: (b, 0, 0, 0, 0)),
                  pl.BlockSpec((tq, D_MLA), lambda b, i: (b * nq + i, 0))],
        out_specs=pl.BlockSpec((tq, D_MLA), lambda b, i: (b * nq + i, 0)),
        out_shape=jax.ShapeDtypeStruct((n_seq * seq_len, D_MLA), bf16),
        scratch_shapes=[pltpu.VMEM((N_HEADS, SUBLANES, tq), f32), pltpu.VMEM((N_HEADS, D_VX, tq), f32),
                        pltpu.VMEM((2, N_HEADS, tk, tq), f32),
                        pltpu.VMEM((2, N_HEADS, SUBLANES, tq), f32)],
        compiler_params=_cparams(2),
        name="attn_prompt",
    )(qt, k, vt, gm)


def _attn_sample_body(cckv_ref, ckr_t_ref, nckv_ref, nkr_ref, qa_ref, qr_ref, gm_ref, wukt_ref, wuv_ref,
                      o_ref, c_ref, ktail_ref, s_ref, *, past, t_new, chunks, n_b):
    n_tail = c_ref.shape[1] - past
    rows = N_HEADS * t_new
    c_ref[:, past:past + t_new, :] = nckv_ref[...].astype(bf16)
    c_ref[:, past + t_new:, :] = jnp.zeros((n_b, n_tail - t_new, KV_RANK), bf16)
    ktail_ref[:, 0:t_new, :] = nkr_ref[...].astype(bf16)
    ktail_ref[:, t_new:, :] = jnp.zeros((n_b, n_tail - t_new, D_ROPE), bf16)

    def chunk_scores(b, start, size):
        cached = start < past
        qa = qa_ref[:, b * t_new:(b + 1) * t_new, :].reshape(rows, KV_RANK)
        qr = qr_ref[:, b * t_new:(b + 1) * t_new, :].reshape(rows, D_ROPE)
        if cached:
            c32 = cckv_ref[b, start:start + size, :]
            c_ref[b, start:start + size, :] = c32.astype(bf16)
            c_t = c32.T.astype(bf16)
            k_t = _dot(wukt_ref[...], c_t)
            s_nope = _dot(qa, c_t)
            s_rope = _dot(qr, ckr_t_ref[b, :, start:start + size].astype(bf16))
        else:
            c = c_ref[b, start:start + size, :]
            k_t = _dot_nt(wukt_ref[...], c)
            s_nope = _dot_nt(qa, c)
            s_rope = _dot_nt(qr, ktail_ref[b])
        ssq = jnp.sum((k_t * k_t).reshape(N_HEADS, D_NOPE, size), axis=1)
        r = lax.rsqrt(ssq * (1.0 / D_NOPE) + EPS)
        for h in range(N_HEADS):
            hs = slice(h * t_new, (h + 1) * t_new)
            s = (s_nope[hs] * r[h:h + 1, :] + s_rope[hs]) * ATTN_SCALE
            if not cached:
                key = lax.broadcasted_iota(jnp.int32, (t_new, size), 1)
                s = jnp.where(key < t_new, s, NEG_INF)
            s_ref[b, hs, start:start + size] = s

    def chunk_values(b, start, size, state):
        m, l, acc = state
        s = s_ref[b, :, start:start + size]
        m_new = jnp.maximum(m, jnp.max(s, axis=-1, keepdims=True))
        alpha = jnp.exp(m - m_new)
        p = jnp.exp(s - m_new)
        l = alpha * l + jnp.sum(p, axis=-1, keepdims=True)
        acc = alpha * acc + _dot(p.astype(bf16), c_ref[b, start:start + size, :])
        return m_new, l, acc

    def finish(b, state):
        _, l, acc = state
        o_lat = (acc / l).astype(bf16)
        for h in range(N_HEADS):
            o = _dot(o_lat[h * t_new:(h + 1) * t_new], wuv_ref[h])
            hs = slice(h * D_V, (h + 1) * D_V)
            bs = slice(b * t_new, (b + 1) * t_new)
            o_ref[bs, hs] = (o * gm_ref[bs, hs].astype(f32)).astype(bf16)

    items = [(b,) + ch for b in range(n_b) for ch in chunks]
    init = (jnp.full((rows, 1), NEG_INF, f32), jnp.zeros((rows, 1), f32),
            jnp.zeros((rows, KV_RANK), f32))
    state = init
    chunk_scores(*items[0])
    for nxt, cur in zip(items[1:] + [None], items):
        if nxt is not None:
            chunk_scores(*nxt)
        state = chunk_values(*cur, state)
        if nxt is None or nxt[0] != cur[0]:
            finish(cur[0], state)
            state = init


def _attn_sample(cache_ckv, cache_krope_t, ckv_new, krope_new, q_abs, q_rope, gm, wts, *, t_new, n_b):
    n_seq, past, _ = cache_ckv.shape
    chunk = 512
    chunks = tuple((s, min(chunk, past - s)) for s in range(0, past, chunk)) + ((past, LANES),)
    s_pad = past + LANES
    body = functools.partial(_attn_sample_body, past=past, t_new=t_new, chunks=chunks, n_b=n_b)
    return pl.pallas_call(
        body,
        grid=(n_seq // n_b,),
        in_specs=[pl.BlockSpec((n_b, past, KV_RANK), lambda b: (b, 0, 0)),
                  pl.BlockSpec((n_b, D_ROPE, past), lambda b: (b, 0, 0)),
                  pl.BlockSpec((n_b, t_new, KV_RANK), lambda b: (b, 0, 0)),
                  pl.BlockSpec((n_b, t_new, D_ROPE), lambda b: (b, 0, 0)),
                  pl.BlockSpec((N_HEADS, n_b * t_new, KV_RANK), lambda b: (0, b, 0)),
                  pl.BlockSpec((N_HEADS, n_b * t_new, D_ROPE), lambda b: (0, b, 0)),
                  pl.BlockSpec((n_b * t_new, D_MLA), lambda b: (b, 0)),
                  _resident((N_HEADS * D_NOPE, KV_RANK)),
                  _resident((N_HEADS, KV_RANK, D_V))],
        out_specs=pl.BlockSpec((n_b * t_new, D_MLA), lambda b: (b, 0)),
        out_shape=jax.ShapeDtypeStruct((n_seq * t_new, D_MLA), bf16),
        scratch_shapes=[pltpu.VMEM((n_b, s_pad, KV_RANK), bf16),
                        pltpu.VMEM((n_b, s_pad - past, D_ROPE), bf16),
                        pltpu.VMEM((n_b, N_HEADS * t_new, s_pad), f32)],
        compiler_params=_cparams(1),
        name="attn_sample",
    )(cache_ckv, cache_krope_t, ckv_new, krope_new, q_abs, q_rope, gm,
      wts["w_uk_t"].reshape(N_HEADS * D_NOPE, KV_RANK), wts["w_uv3"])


def _out_proj_body(x_ref, pa_ref, ma_ref, p_ref, wo_ref, png_ref, wg_ref, bg_ref, wple_ref, y_ref):
    h = x_ref[...] + _dot(pa_ref[...], wo_ref[0:D_POOL, :]) + _dot(ma_ref[...], wo_ref[D_POOL:, :])
    hn = (h * _rms(h, D_MODEL) * png_ref[...]).astype(bf16)
    gate = jax.nn.sigmoid(_dot(hn, wg_ref[...]) + bg_ref[...])
    y_ref[...] = h + gate * _dot(p_ref[...].astype(bf16), wple_ref[...])


def _out_proj(x2d, pool_act, mla_act, p2d, wts, *, tm):
    n_tok = x2d.shape[0]
    row = lambda i: (i, 0)
    return pl.pallas_call(
        _out_proj_body,
        grid=(n_tok // tm,),
        in_specs=[pl.BlockSpec((tm, D_MODEL), row), pl.BlockSpec((tm, D_POOL), row),
                  pl.BlockSpec((tm, D_MLA), row), pl.BlockSpec((tm, D_PLE), row),
                  _resident((D_MODEL, D_MODEL)), _resident((1, D_MODEL)),
                  _resident((D_MODEL, D_MODEL)), _resident((1, D_MODEL)),
                  _resident((D_PLE, D_MODEL))],
        out_specs=pl.BlockSpec((tm, D_MODEL), row),
        out_shape=jax.ShapeDtypeStruct((n_tok, D_MODEL), f32),
        compiler_params=_cparams(1),
        name="out_proj",
    )(x2d, pool_act, mla_act, p2d, wts["w_out"], wts["ple_norm_g"], wts["w_ple_gate"],
      wts["b_ple_gate"], wts["w_ple"])


def _rope_tables(pos0, t, reps=1):
    pos = (pos0 + np.arange(t)).astype(np.float64)
    inv = ROPE_THETA ** (-(np.arange(0, D_ROPE, 2, dtype=np.float64) / D_ROPE))
    ang = pos[:, None] * inv[None, :]
    cos, sin = np.cos(ang), np.sin(ang)
    zero = np.zeros((t, LANES - D_ROPE))
    rows = (np.concatenate([cos, cos, zero], axis=-1), np.concatenate([-sin, sin, zero], axis=-1),
            pos[:, None])
    as_f32 = lambda a: jnp.asarray(a.astype(np.float32))
    return tuple(as_f32(np.tile(a, (reps, 1))) for a in rows) + ((as_f32(cos.T), as_f32(sin.T)),)


def _prep_weights(norm_g, w_in, q_norm_g, w_uq, kv_norm_g, w_ukv, q_nope_g, q_rope_g, k_nope_g,
                  k_rope_g, w_pool, pool_scale, w_out, ple_norm_g, w_ple_gate, b_ple_gate, w_ple):
    w_uq_r = jnp.pad(w_uq.reshape(Q_RANK, N_HEADS, D_NOPE + D_ROPE),
                     ((0, 0), (0, 0), (0, D_QK - D_NOPE - D_ROPE))).reshape(Q_RANK, N_HEADS * D_QK)
    w_ukv3 = w_ukv.reshape(KV_RANK, N_HEADS, D_NOPE + D_V)
    w_uk3, w_uv3 = w_ukv3[..., :D_NOPE], w_ukv3[..., D_NOPE:]
    pad_rope = lambda g: jnp.pad(g, (0, LANES - D_ROPE))[None, :]
    return {
        "norm_g": norm_g[None, :], "w_in_t_f32": w_in.T, "q_norm_g": q_norm_g[None, :],
        "kv_norm_g": kv_norm_g[None, :], "k_rope_g": pad_rope(k_rope_g),
        "w_pool": w_pool.astype(bf16), "pool_scale": pool_scale[None, :],
        "w_uq": w_uq_r.astype(bf16), "q_nope_g": q_nope_g[None, :], "q_rope_g": pad_rope(q_rope_g),
        "w_uq_t": w_uq.T.astype(bf16),
        "q_nope_g_col": q_nope_g[:, None], "q_rope_g_col": q_rope_g[:, None],
        "k_nope_g": k_nope_g[None, :],
        "w_uk": w_uk3.reshape(KV_RANK, N_HEADS * D_NOPE).astype(bf16),
        "w_uv_t": w_uv3.reshape(KV_RANK, D_MLA).T.astype(bf16),
        "w_uk_t": jnp.transpose(w_uk3, (1, 2, 0)).astype(bf16),
        "w_uv3": jnp.transpose(w_uv3, (1, 0, 2)).astype(bf16),
        "w_out_f32": w_out, "w_ple_gate_f32": w_ple_gate,
        "ple_norm_g": ple_norm_g[None, :], "b_ple_gate": b_ple_gate[None, :],
        "w_ple": w_ple.astype(bf16),
    }


def _layer_prompt(x, p, wts):
    n_seq, seq_len, _ = x.shape
    x2d = x.reshape(n_seq * seq_len, D_MODEL)
    cos2, sin2, pos, rope_t = _rope_tables(0, seq_len)
    hist = jnp.zeros((1, HALO, D_POOL), f32)
    pact, gm, ckv, krope, npool, qt, k, vt, w_out16, w_gate16, w_in_t16 = _in_proj(
        x2d, hist, cos2, sin2, pos, wts, nseg=1, seg_len=PROMPT_TILE, n_seq=n_seq, carry=True,
        rope_t=rope_t, kv_block=ATTN_BLOCK)
    wts.update(w_out=w_out16, w_ple_gate=w_gate16, w_in_t=w_in_t16)
    mact = _attn_prompt(qt, k, vt, gm, n_seq=n_seq, seq_len=seq_len, tk=ATTN_BLOCK)
    y = _out_proj(x2d, pact, mact, p.reshape(n_seq * seq_len, D_PLE), wts, tm=OUT_TILE)
    return (y.reshape(x.shape), ckv.reshape(n_seq, seq_len, KV_RANK),
            krope.reshape(n_seq, seq_len, D_ROPE), npool)


def _layer_sample(x, p, state_pool, cache_ckv, cache_krope, wts):
    n_seq, t_new, _ = x.shape
    past = cache_ckv.shape[1]
    n_tok = n_seq * t_new
    x2d = x.reshape(n_tok, D_MODEL)
    cos2, sin2, pos, _ = _rope_tables(past, t_new, reps=n_seq)
    hist = jnp.transpose(state_pool, (1, 0, 2))
    pact, gm, ckv, krope, npool, q_abs, q_rope = _in_proj(
        x2d, hist, cos2, sin2, pos, wts, nseg=SAMPLE_SEGS_PER_TILE, seg_len=t_new, n_seq=n_seq,
        carry=False)
    ckv3 = ckv.reshape(n_seq, t_new, KV_RANK)
    krope3 = krope.reshape(n_seq, t_new, D_ROPE)
    cache_krope_t = jnp.transpose(cache_krope, (0, 2, 1))
    mact = _attn_sample(cache_ckv, cache_krope_t, ckv3, krope3, q_abs, q_rope, gm, wts, t_new=t_new,
                        n_b=SAMPLE_BATCH_PER_STEP)
    y = _out_proj(x2d, pact, mact, p.reshape(n_tok, D_PLE), wts, tm=OUT_TILE)
    return y.reshape(x.shape), ckv3, krope3, jnp.transpose(npool, (1, 0, 2))


def kernel(x_prompt, x_sample, cache_ckv, cache_krope, state_pool, p_prompt, p_sample, norm_g, w_in,
           q_norm_g, w_uq, kv_norm_g, w_ukv, q_nope_g, q_rope_g, k_nope_g, k_rope_g, w_pool,
           pool_scale, w_out, ple_norm_g, w_ple_gate, b_ple_gate, w_ple):
    depth = norm_g.shape[0]
    layer_w = (norm_g, w_in, q_norm_g, w_uq, kv_norm_g, w_ukv, q_nope_g, q_rope_g, k_nope_g, k_rope_g,
               w_pool, pool_scale, w_out, ple_norm_g, w_ple_gate, b_ple_gate, w_ple)
    yp, ys = x_prompt, x_sample
    outs = [[] for _ in range(6)]
    for i in range(depth):
        wts = _prep_weights(*(w[i] for w in layer_w))
        yp, c1, k1, s1 = _layer_prompt(yp, p_prompt[i], wts)
        ys, c2, k2, s2 = _layer_sample(ys, p_sample[i], state_pool[i], cache_ckv[i], cache_krope[i], wts)
        for lst, val in zip(outs, (c1, k1, s1, c2, k2, s2)):
            lst.append(val)
    return (yp, ys) + tuple(jnp.stack(o) for o in outs)
```

```python
import functools

import jax
import jax.numpy as jnp
import numpy as np
from jax import lax
from jax.experimental import pallas as pl
from jax.experimental.pallas import tpu as pltpu

D_MODEL = 2048
CHUNK = 64
D_POOL = 1024
POOL_WINDOWS = (2, 4, 8, 16)
POOL_GROUP = 256
POOL_HIST = 15
HALO = 16
N_HEADS = 8
D_NOPE = 128
D_ROPE = 64
D_V = 128
D_VX = D_V + 16
D_MLA = N_HEADS * D_V
Q_RANK = 512
KV_RANK = 256
D_PLE = 256
D_QK = 256
D_K = N_HEADS * D_NOPE + (D_QK - D_NOPE)
ROPE_THETA = 10000.0
EPS = 1e-6
ATTN_SCALE = (D_NOPE + D_ROPE) ** -0.5
NEG_INF = -1e30
EXP2_SCALE = ATTN_SCALE * float(np.log2(np.e))
LANES = 128
SUBLANES = 8

C_U, C_GP, C_Q, C_KV, C_KR, C_GM = 0, 1024, 2048, 2560, 2816, 2880
D_IN = 3904

VMEM_LIMIT = 56 * 1024 * 1024
PROMPT_TILE = 256
W_CHUNK = 512
ATTN_BLOCK = 256
OUT_TILE = 512
SAMPLE_OUT_TILE = 256
SAMPLE_BATCH_PER_STEP = 4
SAMPLE_SEGS_PER_TILE = 16

f32 = jnp.float32
bf16 = jnp.bfloat16


def _cparams(n_axes):
    return pltpu.CompilerParams(dimension_semantics=("arbitrary",) * n_axes,
                                vmem_limit_bytes=VMEM_LIMIT)


def _resident(shape):
    nd = len(shape)
    return pl.BlockSpec(shape, lambda *_: (0,) * nd, pipeline_mode=pl.Buffered(1))


def _rms(x, n):
    return lax.rsqrt(jnp.sum(x * x, axis=-1, keepdims=True) * (1.0 / n) + EPS)


def _rms_cols(x_t, n):
    return lax.rsqrt(jnp.sum(x_t * x_t, axis=0, keepdims=True) * (1.0 / n) + EPS)


def _dot(a, b):
    return jnp.dot(a, b, preferred_element_type=f32)


def _dot_nt(a, b):
    return lax.dot_general(a, b, (((1,), (1,)), ((), ())), preferred_element_type=f32)


def _rope128(x, cos2, sin2):
    lane = lax.broadcasted_iota(jnp.int32, x.shape, 1)
    swapped = jnp.where(lane < D_ROPE // 2, pltpu.roll(x, LANES - D_ROPE // 2, 1),
                        pltpu.roll(x, D_ROPE // 2, 1))
    return x * cos2 + swapped * sin2


def _queries_t(ql, wqt_ref, cos_ref, sin_ref, qng_ref, qrg_ref, qt_ref):
    q_t = _dot_nt(wqt_ref[...], ql)
    tm = q_t.shape[1]
    cos, sin = cos_ref[...], sin_ref[...]
    half = D_ROPE // 2
    for h in range(N_HEADS):
        r0 = h * D_QK
        s0 = h * (D_NOPE + D_ROPE)
        qn = q_t[s0:s0 + D_NOPE]
        qn = qn * (_rms_cols(qn, D_NOPE) * EXP2_SCALE) * qng_ref[...]
        qt_ref[0, r0:r0 + D_NOPE, :] = qn.astype(bf16)
        qr = q_t[s0 + D_NOPE:s0 + D_NOPE + D_ROPE]
        qr = qr * (_rms_cols(qr, D_ROPE) * EXP2_SCALE) * qrg_ref[...]
        x1, x2 = qr[:half], qr[half:]
        qt_ref[0, r0 + D_NOPE:r0 + D_NOPE + half, :] = (x1 * cos - x2 * sin).astype(bf16)
        qt_ref[0, r0 + D_NOPE + half:r0 + D_NOPE + D_ROPE, :] = (x2 * cos + x1 * sin).astype(bf16)
        qt_ref[0, r0 + D_NOPE + D_ROPE:r0 + D_QK, :] = jnp.zeros((D_QK - D_NOPE - D_ROPE, tm), bf16)


def _queries_absorbed(ql, wq_ref, cos_ref, sin_ref, qng_ref, qrg_ref, kng_ref, wukt_ref, qa_ref, qr_ref):
    q = _dot(ql, wq_ref[...])
    cos2, sin2 = cos_ref[...], sin_ref[...]
    for h in range(N_HEADS):
        qn = q[:, h * D_QK:h * D_QK + D_NOPE]
        qn = qn * _rms(qn, D_NOPE) * qng_ref[...]
        qa_ref[h] = _dot((qn * kng_ref[...]).astype(bf16), wukt_ref[h]).astype(bf16)
        qr = q[:, h * D_QK + D_NOPE:(h + 1) * D_QK]
        qr = _rope128(qr * _rms(qr, D_ROPE) * qrg_ref[...], cos2, sin2)
        qr_ref[h] = qr[:, :D_ROPE].astype(bf16)


def _keys_values(c, kr128, wk_ref, wvt_ref, kng_ref, k_ref, vt_ref):
    tm = c.shape[0]
    k = _dot(c, wk_ref[...])
    for h in range(N_HEADS):
        kn = k[:, h * D_NOPE:(h + 1) * D_NOPE]
        k_ref[:, h * D_NOPE:(h + 1) * D_NOPE] = (kn * _rms(kn, D_NOPE) * kng_ref[...]).astype(bf16)
    k_ref[:, N_HEADS * D_NOPE:] = kr128.astype(bf16)
    v_t = _dot_nt(wvt_ref[...], c).astype(bf16)
    n_blk, tk = vt_ref.shape[1], vt_ref.shape[4]
    row = lax.broadcasted_iota(jnp.int32, (D_VX - D_V, tk), 0)
    ones_row = jnp.where(row == 0, 1.0, 0.0).astype(bf16)
    for j in range(n_blk):
        for h in range(N_HEADS):
            vt_ref[0, j, h, 0:D_V, :] = v_t[h * D_V:(h + 1) * D_V, j * tk:(j + 1) * tk]
            vt_ref[0, j, h, D_V:D_VX, :] = ones_row


def _in_proj_body(*refs, n_conv, **static):
    if not static["fuse_qkv"]:
        _in_proj_tile(pl.program_id(0), refs, **static)
        return
    i = pl.program_id(0)
    w32_ref, w16_ref, wt_ref = refs[6], refs[-3], refs[-1]

    def convert(rows):
        chunk = w32_ref[0:rows, :].astype(bf16)
        w16_ref[0:rows, :] = chunk
        wt_ref[pl.ds(pl.multiple_of(i * W_CHUNK, W_CHUNK), rows), :] = chunk

    last_rows = D_IN - (n_conv - 1) * W_CHUNK
    pl.when(i < n_conv - 1)(lambda: convert(W_CHUNK))
    pl.when(i == n_conv - 1)(lambda: convert(last_rows))

    @pl.when(i >= n_conv)
    def _():
        _in_proj_tile(i - n_conv, refs[:6] + (wt_ref,) + refs[7:-3] + refs[-2:-1], **static)


def _in_proj_tile(i, refs, *, nseg, seg_len, tiles_per_seq, carry, fuse_qkv):
    (x_ref, hist_ref, cos_ref, sin_ref, pos_ref, ng_ref, wt_ref, qg_ref, kvg_ref, krg_ref, wp_ref,
     ps_ref) = refs[:12]
    if fuse_qkv:
        (cost_ref, sint_ref, wqt_ref, qng_ref, qrg_ref, wk_ref, wvt_ref, kng_ref, wo32_ref,
         wg32_ref) = refs[12:22]
        (pact_ref, gm_ref, ckv_ref, kr_ref, npool_ref, qt_ref, k_ref, vt_ref, wo16_ref, wg16_ref,
         ucat_ref) = refs[22:]
    else:
        wq_ref, qng_ref, qrg_ref, kng_ref, wukt_ref = refs[12:17]
        pact_ref, gm_ref, ckv_ref, kr_ref, npool_ref, qa_ref, qr_ref, ucat_ref = refs[17:]
    tm = nseg * seg_len

    x = x_ref[...]
    xn = (x * _rms(x, D_MODEL) * ng_ref[...]).astype(bf16)

    u = _dot_nt(xn, wt_ref[C_U:C_U + D_POOL, :])
    if carry:
        @pl.when(i % tiles_per_seq == 0)
        def _():
            ucat_ref[:, 0:HALO, :] = jnp.zeros((nseg, HALO, D_POOL), f32)
    else:
        ucat_ref[:, 0:HALO - POOL_HIST, :] = jnp.zeros((nseg, HALO - POOL_HIST, D_POOL), f32)
        for t in range(POOL_HIST):
            ucat_ref[:, HALO - POOL_HIST + t, :] = hist_ref[t]
    ucat_ref[:, HALO:HALO + seg_len, :] = u.reshape(nseg, seg_len, D_POOL)

    pos = pos_ref[...]
    gp = _dot_nt(xn, wt_ref[C_GP:C_GP + D_POOL, :])
    gate = gp * jax.nn.sigmoid(gp) * ps_ref[...]

    def pool_group(g):
        w = POOL_WINDOWS[g]
        sl = slice(g * POOL_GROUP, (g + 1) * POOL_GROUP)
        acc = ucat_ref[:, :, sl]
        shift = 1
        while shift < w:
            acc = acc + pltpu.roll(acc, shift, 1)
            shift *= 2
        acc = acc[:, HALO:, :].reshape(tm, POOL_GROUP)
        inv_cnt = 1.0 / jnp.minimum(pos + 1.0, float(w))
        d = (acc * inv_cnt - u[:, sl]).astype(bf16)
        mixed = _dot(d, wp_ref[g])
        pact_ref[:, sl] = (mixed * gate[:, sl]).astype(bf16)

    cq = _dot_nt(xn, wt_ref[C_Q:C_Q + Q_RANK, :])
    pool_group(0)
    pool_group(1)
    ql = (cq * _rms(cq, Q_RANK) * qg_ref[...]).astype(bf16)
    if fuse_qkv:
        _queries_t(ql, wqt_ref, cost_ref, sint_ref, qng_ref, qrg_ref, qt_ref)
    else:
        _queries_absorbed(ql, wq_ref, cos_ref, sin_ref, qng_ref, qrg_ref, kng_ref, wukt_ref, qa_ref, qr_ref)

    ckv = _dot_nt(xn, wt_ref[C_KV:C_KV + KV_RANK, :])
    pool_group(2)
    ckv = ckv * _rms(ckv, KV_RANK) * kvg_ref[...]
    ckv_ref[...] = ckv

    kr = _dot_nt(xn, wt_ref[C_KR:C_KR + LANES, :])
    gm = _dot_nt(xn, wt_ref[C_GM:C_GM + D_MLA, :])
    pool_group(3)
    if carry:
        npool_ref[...] = ucat_ref[:, seg_len + 1:seg_len + HALO, :]
        ucat_ref[:, 0:HALO, :] = ucat_ref[:, seg_len:seg_len + HALO, :]
    else:
        for t in range(POOL_HIST):
            npool_ref[t] = ucat_ref[:, seg_len + 1 + t, :]
    lane = lax.broadcasted_iota(jnp.int32, (tm, LANES), 1)
    kr = jnp.where(lane < D_ROPE, kr, 0.0)
    kr = _rope128(kr * _rms(kr, D_ROPE) * krg_ref[...], cos_ref[...], sin_ref[...])
    kr_ref[...] = kr[:, :D_ROPE]
    gm_ref[...] = (gm * jax.nn.sigmoid(gm)).astype(bf16)
    if fuse_qkv:
        _keys_values(ckv.astype(bf16), kr, wk_ref, wvt_ref, kng_ref, k_ref, vt_ref)
        wo16_ref[...] = wo32_ref[...].astype(bf16)
        wg16_ref[...] = wg32_ref[...].astype(bf16)


def _in_proj(x2d, hist, cos2, sin2, pos, wts, *, nseg, seg_len, n_seq, carry, rope_t=None, kv_block=None):
    n_tok = x2d.shape[0]
    tm = nseg * seg_len
    n_tiles = n_tok // tm
    tiles_per_seq = n_tiles // n_seq if carry else 1
    tab_tiles = cos2.shape[0] // tm
    fuse_qkv = rope_t is not None
    n_conv = pl.cdiv(D_IN, W_CHUNK) if fuse_qkv else 0
    tile = (lambda i: jnp.maximum(i - n_conv, 0)) if n_conv else (lambda i: i)
    row = lambda i: (tile(i), 0)
    tab = lambda i: (tile(i) % tab_tiles, 0)
    if carry:
        hist_spec = pl.BlockSpec((1, HALO, D_POOL), lambda i: (0, 0, 0))
        npool_spec = pl.BlockSpec((1, POOL_HIST, D_POOL), lambda i: (tile(i) // tiles_per_seq, 0, 0))
    else:
        hist_spec = pl.BlockSpec((POOL_HIST, nseg, D_POOL), lambda i: (0, tile(i), 0))
        npool_spec = pl.BlockSpec((POOL_HIST, nseg, D_POOL), lambda i: (0, tile(i), 0))
    x_spec = pl.BlockSpec((tm, D_MODEL), row)
    if fuse_qkv:
        w_chunk = lambda i: (jnp.minimum(i, n_conv - 1), 0)
        w_spec, w_arg = pl.BlockSpec((W_CHUNK, D_MODEL), w_chunk), wts["w_in_t_f32"]
    else:
        w_spec, w_arg = _resident((D_IN, D_MODEL)), wts["w_in_t"]
    in_specs = [
        x_spec, hist_spec, pl.BlockSpec((tm, LANES), tab),
        pl.BlockSpec((tm, LANES), tab), pl.BlockSpec((tm, 1), tab), _resident((1, D_MODEL)),
        w_spec, _resident((1, Q_RANK)), _resident((1, KV_RANK)),
        _resident((1, LANES)), _resident((4, POOL_GROUP, POOL_GROUP)), _resident((1, D_POOL)),
    ]
    args = [x2d, hist, cos2, sin2, pos, wts["norm_g"], w_arg, wts["q_norm_g"],
            wts["kv_norm_g"], wts["k_rope_g"], wts["w_pool"], wts["pool_scale"]]
    out_specs = [pl.BlockSpec((tm, D_POOL), row), pl.BlockSpec((tm, D_MLA), row),
                 pl.BlockSpec((tm, KV_RANK), row), pl.BlockSpec((tm, D_ROPE), row), npool_spec]
    out_shape = [jax.ShapeDtypeStruct((n_tok, D_POOL), bf16), jax.ShapeDtypeStruct((n_tok, D_MLA), bf16),
                 jax.ShapeDtypeStruct((n_tok, KV_RANK), f32), jax.ShapeDtypeStruct((n_tok, D_ROPE), f32),
                 jax.ShapeDtypeStruct((n_seq, POOL_HIST, D_POOL) if carry else (POOL_HIST, n_seq, D_POOL), f32)]
    scratch = [pltpu.VMEM((nseg, HALO + seg_len, D_POOL), f32)]
    if fuse_qkv:
        seq_tab = lambda i: (0, tile(i) % tiles_per_seq)
        seq_blk = lambda i: (tile(i) // tiles_per_seq, 0, tile(i) % tiles_per_seq)
        in_specs += [pl.BlockSpec((D_ROPE // 2, tm), seq_tab), pl.BlockSpec((D_ROPE // 2, tm), seq_tab),
                     _resident((N_HEADS * (D_NOPE + D_ROPE), Q_RANK)), _resident((D_NOPE, 1)),
                     _resident((D_ROPE, 1)),
                     _resident((KV_RANK, N_HEADS * D_NOPE)), _resident((D_MLA, KV_RANK)),
                     _resident((1, D_NOPE)),
                     pl.BlockSpec((D_MODEL // n_tiles, D_MODEL), row),
                     pl.BlockSpec((D_MODEL // n_tiles, D_MODEL), row)]
        args += [rope_t[0], rope_t[1], wts["w_uq_t"], wts["q_nope_g_col"], wts["q_rope_g_col"],
                 wts["w_uk"], wts["w_uv_t"], wts["k_nope_g"], wts["w_out_f32"], wts["w_ple_gate_f32"]]
        out_specs += [
            pl.BlockSpec((1, N_HEADS * D_QK, tm), seq_blk),
            pl.BlockSpec((tm, D_K), row),
            pl.BlockSpec((1, tm // kv_block, N_HEADS, D_VX, kv_block), lambda i: seq_blk(i)[::2] + (0, 0, 0)),
            pl.BlockSpec((D_MODEL // n_tiles, D_MODEL), row),
            pl.BlockSpec((D_MODEL // n_tiles, D_MODEL), row),
            pl.BlockSpec((W_CHUNK, D_MODEL), w_chunk)]
        out_shape += [jax.ShapeDtypeStruct((n_seq, N_HEADS * D_QK, tiles_per_seq * tm), bf16),
                      jax.ShapeDtypeStruct((n_tok, D_K), bf16),
                      jax.ShapeDtypeStruct((n_seq, tiles_per_seq * tm // kv_block, N_HEADS, D_VX, kv_block),
                                           bf16),
                      jax.ShapeDtypeStruct((D_MODEL, D_MODEL), bf16),
                      jax.ShapeDtypeStruct((D_MODEL, D_MODEL), bf16),
                      jax.ShapeDtypeStruct((D_IN, D_MODEL), bf16)]
        scratch += [pltpu.VMEM((n_conv * W_CHUNK, D_MODEL), bf16)]
    else:
        in_specs += [_resident((Q_RANK, N_HEADS * D_QK)), _resident((1, D_NOPE)), _resident((1, LANES)),
                     _resident((1, D_NOPE)), _resident((N_HEADS, D_NOPE, KV_RANK))]
        args += [wts["w_uq"], wts["q_nope_g"], wts["q_rope_g"], wts["k_nope_g"], wts["w_uk_t"]]
        out_specs += [pl.BlockSpec((N_HEADS, tm, KV_RANK), lambda i: (0, i, 0)),
                      pl.BlockSpec((N_HEADS, tm, D_ROPE), lambda i: (0, i, 0))]
        out_shape += [jax.ShapeDtypeStruct((N_HEADS, n_tok, KV_RANK), bf16),
                      jax.ShapeDtypeStruct((N_HEADS, n_tok, D_ROPE), bf16)]
    body = functools.partial(_in_proj_body, n_conv=n_conv, nseg=nseg, seg_len=seg_len,
                             tiles_per_seq=tiles_per_seq, carry=carry, fuse_qkv=fuse_qkv)
    return pl.pallas_call(
        body,
        grid=(n_conv + n_tiles,),
        in_specs=in_specs,
        out_specs=out_specs,
        out_shape=out_shape,
        scratch_shapes=scratch,
        compiler_params=_cparams(1),
        name="in_proj",
    )(*args)


def _attn_prompt_body(qt_ref, k_ref, vt_ref, gm_ref, o_ref, m_ref, acc_ref, s_ref, mx_ref, *, tk):
    g = pl.program_id(1)
    tq = 2 * tk
    hi = slice(tk, tq)

    def column_max(s):
        x = jnp.max(s.reshape(s.shape[0] // SUBLANES, SUBLANES, s.shape[1]), axis=0)
        for shift in (4, 2, 1):
            x = jnp.maximum(x, pltpu.roll(x, shift, 0))
        return x

    def keys(h, j):
        rows = pl.ds(pl.multiple_of(j * tk, tk), tk)
        return jnp.concatenate([k_ref[rows, h * D_NOPE:(h + 1) * D_NOPE], k_ref[rows, N_HEADS * D_NOPE:]],
                               axis=1)

    def chunk_visible(n_q):
        k_chunk = lax.broadcasted_iota(jnp.int32, (tk, n_q), 0) // CHUNK
        q_chunk = lax.broadcasted_iota(jnp.int32, (tk, n_q), 1) // CHUNK
        return k_chunk <= q_chunk

    def scores(h, j, buf, first_diag=False):
        s = _dot(keys(h, j), qt_ref[0, h * D_QK:(h + 1) * D_QK, :])
        if first_diag:
            s = jnp.where(chunk_visible(tq), s, NEG_INF)
        s_ref[buf, h] = s
        mx_ref[buf, h] = column_max(s)

    def scores_hi(h, j, buf):
        s = _dot(keys(h, j), qt_ref[0, h * D_QK:(h + 1) * D_QK, hi])
        s = jnp.where(chunk_visible(tk), s, NEG_INF)
        s_ref[buf, h, :, hi] = s
        mx_ref[buf, h, :, hi] = column_max(s)

    def values(h, j, buf, cols=slice(None)):
        m_old = m_ref[h, :, cols]
        m_new = jnp.maximum(m_old, mx_ref[buf, h, :, cols])
        n_q = m_new.shape[1]
        alpha = jnp.exp2(m_old - m_new)
        s = s_ref[buf, h, :, cols]
        p = jnp.exp2(s.reshape(tk // SUBLANES, SUBLANES, n_q) - m_new[None]).reshape(tk, n_q)
        m_ref[h, :, cols] = m_new
        acc = acc_ref[h, :, cols].reshape(D_VX // SUBLANES, SUBLANES, n_q) * alpha[None]
        acc_ref[h, :, cols] = acc.reshape(D_VX, n_q) + _dot(vt_ref[0, j, h], p.astype(bf16))

    def first_values(h, j, buf):
        m_new = mx_ref[buf, h]
        p = jnp.exp2(s_ref[buf, h].reshape(tk // SUBLANES, SUBLANES, tq) - m_new[None]).reshape(tk, tq)
        m_ref[h] = m_new
        acc_ref[h] = _dot(vt_ref[0, j, h], p.astype(bf16))

    def both(score_fn, value_fn):
        for h in range(N_HEADS):
            score_fn(h)
            value_fn(h)

    d0, d1 = 2 * g, 2 * g + 1
    for h in range(N_HEADS):
        scores(h, d0, 0, first_diag=True)
    both(lambda h: scores_hi(h, d1, 1), lambda h: first_values(h, d0, 0))

    @pl.when(g == 0)
    def _():
        for h in range(N_HEADS):
            values(h, d1, 1, hi)

    @pl.when(g > 0)
    def _():
        both(lambda h: scores(h, 0, 0), lambda h: values(h, d1, 1, hi))

        def step_pair(u, carry):
            both(lambda h: scores(h, 2 * u + 1, 1), lambda h: values(h, 2 * u, 0))
            both(lambda h: scores(h, 2 * u + 2, 0), lambda h: values(h, 2 * u + 1, 1))
            return carry

        lax.fori_loop(0, g - 1, step_pair, 0)
        both(lambda h: scores(h, d0 - 1, 1), lambda h: values(h, d0 - 2, 0))
        for h in range(N_HEADS):
            values(h, d0 - 1, 1)

    for h in range(N_HEADS):
        hs = slice(h * D_V, (h + 1) * D_V)
        o = (acc_ref[h, 0:D_V, :] * (1.0 / acc_ref[h, D_V:D_V + 1, :])).T
        o_ref[:, hs] = (o * gm_ref[:, hs].astype(f32)).astype(bf16)


def _attn_prompt(qt, k, vt, gm, *, n_seq, seq_len, tk):
    tq = 2 * tk
    nq = seq_len // tq
    body = functools.partial(_attn_prompt_body, tk=tk)
    return pl.pallas_call(
        body,
        grid=(n_seq, nq),
        in_specs=[pl.BlockSpec((1, N_HEADS * D_QK, tq), lambda b, i: (b, 0, i)),
                  pl.BlockSpec((seq_len, D_K), lambda b, i: (b, 0)),
                  pl.BlockSpec((1, seq_len // tk, N_HEADS, D_VX, tk), lambda b, i: (b, 0, 0, 0, 0)),
                  pl.BlockSpec((tq, D_MLA), lambda b, i: (b * nq + i, 0))],
        out_specs=pl.BlockSpec((tq, D_MLA), lambda b, i: (b * nq + i, 0)),
        out_shape=jax.ShapeDtypeStruct((n_seq * seq_len, D_MLA), bf16),
        scratch_shapes=[pltpu.VMEM((N_HEADS, SUBLANES, tq), f32), pltpu.VMEM((N_HEADS, D_VX, tq), f32),
                        pltpu.VMEM((2, N_HEADS, tk, tq), f32),
                        pltpu.VMEM((2, N_HEADS, SUBLANES, tq), f32)],
        compiler_params=_cparams(2),
        name="attn_prompt",
    )(qt, k, vt, gm)


def _attn_sample_body(cckv_ref, ckr_t_ref, nckv_ref, nkr_ref, qa_ref, qr_ref, gm_ref, wukt_ref, wuv_ref,
                      o_ref, c_ref, ktail_ref, s_ref, *, past, t_new, chunks, n_b):
    n_tail = c_ref.shape[1] - past
    rows = N_HEADS * t_new
    c_ref[:, past:past + t_new, :] = nckv_ref[...].astype(bf16)
    c_ref[:, past + t_new:, :] = jnp.zeros((n_b, n_tail - t_new, KV_RANK), bf16)
    ktail_ref[:, 0:t_new, :] = nkr_ref[...].astype(bf16)
    ktail_ref[:, t_new:, :] = jnp.zeros((n_b, n_tail - t_new, D_ROPE), bf16)

    def chunk_scores(b, start, size):
        cached = start < past
        qa = qa_ref[:, b * t_new:(b + 1) * t_new, :].reshape(rows, KV_RANK)
        qr = qr_ref[:, b * t_new:(b + 1) * t_new, :].reshape(rows, D_ROPE)
        if cached:
            c32 = cckv_ref[b, start:start + size, :]
            c_ref[b, start:start + size, :] = c32.astype(bf16)
            c_t = c32.T.astype(bf16)
            k_t = _dot(wukt_ref[...], c_t)
            s_nope = _dot(qa, c_t)
            s_rope = _dot(qr, ckr_t_ref[b, :, start:start + size].astype(bf16))
        else:
            c = c_ref[b, start:start + size, :]
            k_t = _dot_nt(wukt_ref[...], c)
            s_nope = _dot_nt(qa, c)
            s_rope = _dot_nt(qr, ktail_ref[b])
        ssq = jnp.sum((k_t * k_t).reshape(N_HEADS, D_NOPE, size), axis=1)
        r = lax.rsqrt(ssq * (1.0 / D_NOPE) + EPS)
        for h in range(N_HEADS):
            hs = slice(h * t_new, (h + 1) * t_new)
            s = (s_nope[hs] * r[h:h + 1, :] + s_rope[hs]) * ATTN_SCALE
            if not cached:
                key = lax.broadcasted_iota(jnp.int32, (t_new, size), 1)
                s = jnp.where(key < t_new, s, NEG_INF)
            s_ref[b, hs, start:start + size] = s

    def chunk_values(b, start, size, state):
        m, l, acc = state
        s = s_ref[b, :, start:start + size]
        m_new = jnp.maximum(m, jnp.max(s, axis=-1, keepdims=True))
        alpha = jnp.exp(m - m_new)
        p = jnp.exp(s - m_new)
        l = alpha * l + jnp.sum(p, axis=-1, keepdims=True)
        acc = alpha * acc + _dot(p.astype(bf16), c_ref[b, start:start + size, :])
        return m_new, l, acc

    def finish(b, state):
        _, l, acc = state
        o_lat = (acc / l).astype(bf16)
        for h in range(N_HEADS):
            o = _dot(o_lat[h * t_new:(h + 1) * t_new], wuv_ref[h])
            hs = slice(h * D_V, (h + 1) * D_V)
            bs = slice(b * t_new, (b + 1) * t_new)
            o_ref[bs, hs] = (o * gm_ref[bs, hs].astype(f32)).astype(bf16)

    items = [(b,) + ch for b in range(n_b) for ch in chunks]
    init = (jnp.full((rows, 1), NEG_INF, f32), jnp.zeros((rows, 1), f32),
            jnp.zeros((rows, KV_RANK), f32))
    state = init
    chunk_scores(*items[0])
    for nxt, cur in zip(items[1:] + [None], items):
        if nxt is not None:
            chunk_scores(*nxt)
        state = chunk_values(*cur, state)
        if nxt is None or nxt[0] != cur[0]:
            finish(cur[0], state)
            state = init


def _attn_sample(cache_ckv, cache_krope_t, ckv_new, krope_new, q_abs, q_rope, gm, wts, *, t_new, n_b):
    n_seq, past, _ = cache_ckv.shape
    chunk = 512
    chunks = tuple((s, min(chunk, past - s)) for s in range(0, past, chunk)) + ((past, LANES),)
    s_pad = past + LANES
    body = functools.partial(_attn_sample_body, past=past, t_new=t_new, chunks=chunks, n_b=n_b)
    return pl.pallas_call(
        body,
        grid=(n_seq // n_b,),
        in_specs=[pl.BlockSpec((n_b, past, KV_RANK), lambda b: (b, 0, 0)),
                  pl.BlockSpec((n_b, D_ROPE, past), lambda b: (b, 0, 0)),
                  pl.BlockSpec((n_b, t_new, KV_RANK), lambda b: (b, 0, 0)),
                  pl.BlockSpec((n_b, t_new, D_ROPE), lambda b: (b, 0, 0)),
                  pl.BlockSpec((N_HEADS, n_b * t_new, KV_RANK), lambda b: (0, b, 0)),
                  pl.BlockSpec((N_HEADS, n_b * t_new, D_ROPE), lambda b: (0, b, 0)),
                  pl.BlockSpec((n_b * t_new, D_MLA), lambda b: (b, 0)),
                  _resident((N_HEADS * D_NOPE, KV_RANK)),
                  _resident((N_HEADS, KV_RANK, D_V))],
        out_specs=pl.BlockSpec((n_b * t_new, D_MLA), lambda b: (b, 0)),
        out_shape=jax.ShapeDtypeStruct((n_seq * t_new, D_MLA), bf16),
        scratch_shapes=[pltpu.VMEM((n_b, s_pad, KV_RANK), bf16),
                        pltpu.VMEM((n_b, s_pad - past, D_ROPE), bf16),
                        pltpu.VMEM((n_b, N_HEADS * t_new, s_pad), f32)],
        compiler_params=_cparams(1),
        name="attn_sample",
    )(cache_ckv, cache_krope_t, ckv_new, krope_new, q_abs, q_rope, gm,
      wts["w_uk_t"].reshape(N_HEADS * D_NOPE, KV_RANK), wts["w_uv3"])


def _out_proj_body(x_ref, pa_ref, ma_ref, p_ref, wo_ref, png_ref, wg_ref, bg_ref, wple_ref, y_ref):
    h = x_ref[...] + _dot(pa_ref[...], wo_ref[0:D_POOL, :]) + _dot(ma_ref[...], wo_ref[D_POOL:, :])
    hn = (h * _rms(h, D_MODEL) * png_ref[...]).astype(bf16)
    gate = jax.nn.sigmoid(_dot(hn, wg_ref[...]) + bg_ref[...])
    y_ref[...] = h + gate * _dot(p_ref[...].astype(bf16), wple_ref[...])


def _out_proj(x2d, pool_act, mla_act, p2d, wts, *, tm):
    n_tok = x2d.shape[0]
    row = lambda i: (i, 0)
    return pl.pallas_call(
        _out_proj_body,
        grid=(n_tok // tm,),
        in_specs=[pl.BlockSpec((tm, D_MODEL), row), pl.BlockSpec((tm, D_POOL), row),
                  pl.BlockSpec((tm, D_MLA), row), pl.BlockSpec((tm, D_PLE), row),
                  _resident((D_MODEL, D_MODEL)), _resident((1, D_MODEL)),
                  _resident((D_MODEL, D_MODEL)), _resident((1, D_MODEL)),
                  _resident((D_PLE, D_MODEL))],
        out_specs=pl.BlockSpec((tm, D_MODEL), row),
        out_shape=jax.ShapeDtypeStruct((n_tok, D_MODEL), f32),
        compiler_params=_cparams(1),
        name="out_proj",
    )(x2d, pool_act, mla_act, p2d, wts["w_out"], wts["ple_norm_g"], wts["w_ple_gate"],
      wts["b_ple_gate"], wts["w_ple"])


def _rope_tables(pos0, t, reps=1):
    pos = (pos0 + np.arange(t)).astype(np.float64)
    inv = ROPE_THETA ** (-(np.arange(0, D_ROPE, 2, dtype=np.float64) / D_ROPE))
    ang = pos[:, None] * inv[None, :]
    cos, sin = np.cos(ang), np.sin(ang)
    zero = np.zeros((t, LANES - D_ROPE))
    rows = (np.concatenate([cos, cos, zero], axis=-1), np.concatenate([-sin, sin, zero], axis=-1),
            pos[:, None])
    as_f32 = lambda a: jnp.asarray(a.astype(np.float32))
    return tuple(as_f32(np.tile(a, (reps, 1))) for a in rows) + ((as_f32(cos.T), as_f32(sin.T)),)


def _prep_weights(norm_g, w_in, q_norm_g, w_uq, kv_norm_g, w_ukv, q_nope_g, q_rope_g, k_nope_g,
                  k_rope_g, w_pool, pool_scale, w_out, ple_norm_g, w_ple_gate, b_ple_gate, w_ple):
    w_uq_r = jnp.pad(w_uq.reshape(Q_RANK, N_HEADS, D_NOPE + D_ROPE),
                     ((0, 0), (0, 0), (0, D_QK - D_NOPE - D_ROPE))).reshape(Q_RANK, N_HEADS * D_QK)
    w_ukv3 = w_ukv.reshape(KV_RANK, N_HEADS, D_NOPE + D_V)
    w_uk3, w_uv3 = w_ukv3[..., :D_NOPE], w_ukv3[..., D_NOPE:]
    pad_rope = lambda g: jnp.pad(g, (0, LANES - D_ROPE))[None, :]
    return {
        "norm_g": norm_g[None, :], "w_in_t_f32": w_in.T, "q_norm_g": q_norm_g[None, :],
        "kv_norm_g": kv_norm_g[None, :], "k_rope_g": pad_rope(k_rope_g),
        "w_pool": w_pool.astype(bf16), "pool_scale": pool_scale[None, :],
        "w_uq": w_uq_r.astype(bf16), "q_nope_g": q_nope_g[None, :], "q_rope_g": pad_rope(q_rope_g),
        "w_uq_t": w_uq.T.astype(bf16),
        "q_nope_g_col": q_nope_g[:, None], "q_rope_g_col": q_rope_g[:, None],
        "k_nope_g": k_nope_g[None, :],
        "w_uk": w_uk3.reshape(KV_RANK, N_HEADS * D_NOPE).astype(bf16),
        "w_uv_t": w_uv3.reshape(KV_RANK, D_MLA).T.astype(bf16),
        "w_uk_t": jnp.transpose(w_uk3, (1, 2, 0)).astype(bf16),
        "w_uv3": jnp.transpose(w_uv3, (1, 0, 2)).astype(bf16),
        "w_out_f32": w_out, "w_ple_gate_f32": w_ple_gate,
        "ple_norm_g": ple_norm_g[None, :], "b_ple_gate": b_ple_gate[None, :],
        "w_ple": w_ple.astype(bf16),
    }


def _layer_prompt(x, p, wts):
    n_seq, seq_len, _ = x.shape
    x2d = x.reshape(n_seq * seq_len, D_MODEL)
    cos2, sin2, pos, rope_t = _rope_tables(0, seq_len)
    hist = jnp.zeros((1, HALO, D_POOL), f32)
    pact, gm, ckv, krope, npool, qt, k, vt, w_out16, w_gate16, w_in_t16 = _in_proj(
        x2d, hist, cos2, sin2, pos, wts, nseg=1, seg_len=PROMPT_TILE, n_seq=n_seq, carry=True,
        rope_t=rope_t, kv_block=ATTN_BLOCK)
    wts.update(w_out=w_out16, w_ple_gate=w_gate16, w_in_t=w_in_t16)
    mact = _attn_prompt(qt, k, vt, gm, n_seq=n_seq, seq_len=seq_len, tk=ATTN_BLOCK)
    y = _out_proj(x2d, pact, mact, p.reshape(n_seq * seq_len, D_PLE), wts, tm=OUT_TILE)
    return (y.reshape(x.shape), ckv.reshape(n_seq, seq_len, KV_RANK),
            krope.reshape(n_seq, seq_len, D_ROPE), npool)


def _layer_sample(x, p, state_pool, cache_ckv, cache_krope, wts):
    n_seq, t_new, _ = x.shape
    past = cache_ckv.shape[1]
    n_tok = n_seq * t_new
    x2d = x.reshape(n_tok, D_MODEL)
    cos2, sin2, pos, _ = _rope_tables(past, t_new, reps=n_seq)
    hist = jnp.transpose(state_pool, (1, 0, 2))
    pact, gm, ckv, krope, npool, q_abs, q_rope = _in_proj(
        x2d, hist, cos2, sin2, pos, wts, nseg=SAMPLE_SEGS_PER_TILE, seg_len=t_new, n_seq=n_seq,
        carry=False)
    ckv3 = ckv.reshape(n_seq, t_new, KV_RANK)
    krope3 = krope.reshape(n_seq, t_new, D_ROPE)
    cache_krope_t = jnp.transpose(cache_krope, (0, 2, 1))
    mact = _attn_sample(cache_ckv, cache_krope_t, ckv3, krope3, q_abs, q_rope, gm, wts, t_new=t_new,
                        n_b=SAMPLE_BATCH_PER_STEP)
    y = _out_proj(x2d, pact, mact, p.reshape(n_tok, D_PLE), wts, tm=SAMPLE_OUT_TILE)
    return y.reshape(x.shape), ckv3, krope3, jnp.transpose(npool, (1, 0, 2))


def kernel(x_prompt, x_sample, cache_ckv, cache_krope, state_pool, p_prompt, p_sample, norm_g, w_in,
           q_norm_g, w_uq, kv_norm_g, w_ukv, q_nope_g, q_rope_g, k_nope_g, k_rope_g, w_pool,
           pool_scale, w_out, ple_norm_g, w_ple_gate, b_ple_gate, w_ple):
    depth = norm_g.shape[0]
    layer_w = (norm_g, w_in, q_norm_g, w_uq, kv_norm_g, w_ukv, q_nope_g, q_rope_g, k_nope_g, k_rope_g,
               w_pool, pool_scale, w_out, ple_norm_g, w_ple_gate, b_ple_gate, w_ple)
    yp, ys = x_prompt, x_sample
    outs = [[] for _ in range(6)]
    for i in range(depth):
        wts = _prep_weights(*(w[i] for w in layer_w))
        yp, c1, k1, s1 = _layer_prompt(yp, p_prompt[i], wts)
        ys, c2, k2, s2 = _layer_sample(ys, p_sample[i], state_pool[i], cache_ckv[i], cache_krope[i], wts)
        for lst, val in zip(outs, (c1, k1, s1, c2, k2, s2)):
            lst.append(val)
    return (yp, ys) + tuple(jnp.stack(o) for o in outs)
```

```python
import functools

import jax
import jax.numpy as jnp
import numpy as np
from jax import lax
from jax.experimental import pallas as pl
from jax.experimental.pallas import tpu as pltpu

D_MODEL = 2048
CHUNK = 64
D_POOL = 1024
POOL_WINDOWS = (2, 4, 8, 16)
POOL_GROUP = 256
POOL_HIST = 15
HALO = 16
N_HEADS = 8
D_NOPE = 128
D_ROPE = 64
D_V = 128
D_VX = D_V + 16
D_MLA = N_HEADS * D_V
Q_RANK = 512
KV_RANK = 256
D_PLE = 256
D_QK = 256
D_K = N_HEADS * D_NOPE + (D_QK - D_NOPE)
ROPE_THETA = 10000.0
EPS = 1e-6
ATTN_SCALE = (D_NOPE + D_ROPE) ** -0.5
NEG_INF = -1e30
EXP2_SCALE = ATTN_SCALE * float(np.log2(np.e))
LANES = 128
SUBLANES = 8

C_U, C_GP, C_Q, C_KV, C_KR, C_GM = 0, 1024, 2048, 2560, 2816, 2880
D_IN = 3904

VMEM_LIMIT = 56 * 1024 * 1024
PROMPT_TILE = 256
W_CHUNK = 512
ATTN_BLOCK = 256
OUT_TILE = 512
SAMPLE_OUT_TILE = 256
SAMPLE_BATCH_PER_STEP = 4
SAMPLE_SEGS_PER_TILE = 16

f32 = jnp.float32
bf16 = jnp.bfloat16


def _cparams(n_axes):
    return pltpu.CompilerParams(dimension_semantics=("arbitrary",) * n_axes,
                                vmem_limit_bytes=VMEM_LIMIT)


def _resident(shape):
    nd = len(shape)
    return pl.BlockSpec(shape, lambda *_: (0,) * nd, pipeline_mode=pl.Buffered(1))


def _rms(x, n):
    return lax.rsqrt(jnp.sum(x * x, axis=-1, keepdims=True) * (1.0 / n) + EPS)


def _rms_cols(x_t, n):
    return lax.rsqrt(jnp.sum(x_t * x_t, axis=0, keepdims=True) * (1.0 / n) + EPS)


def _dot(a, b):
    return jnp.dot(a, b, preferred_element_type=f32)


def _dot_nt(a, b):
    return lax.dot_general(a, b, (((1,), (1,)), ((), ())), preferred_element_type=f32)


def _rope128(x, cos2, sin2):
    lane = lax.broadcasted_iota(jnp.int32, x.shape, 1)
    swapped = jnp.where(lane < D_ROPE // 2, pltpu.roll(x, LANES - D_ROPE // 2, 1),
                        pltpu.roll(x, D_ROPE // 2, 1))
    return x * cos2 + swapped * sin2


def _queries_t(ql, wqt_ref, cos_ref, sin_ref, qng_ref, qrg_ref, qt_ref):
    q_t = _dot_nt(wqt_ref[...], ql)
    tm = q_t.shape[1]
    cos, sin = cos_ref[...], sin_ref[...]
    half = D_ROPE // 2
    for h in range(N_HEADS):
        r0 = h * D_QK
        s0 = h * (D_NOPE + D_ROPE)
        qn = q_t[s0:s0 + D_NOPE]
        qn = qn * (_rms_cols(qn, D_NOPE) * EXP2_SCALE) * qng_ref[...]
        qt_ref[0, r0:r0 + D_NOPE, :] = qn.astype(bf16)
        qr = q_t[s0 + D_NOPE:s0 + D_NOPE + D_ROPE]
        qr = qr * (_rms_cols(qr, D_ROPE) * EXP2_SCALE) * qrg_ref[...]
        x1, x2 = qr[:half], qr[half:]
        qt_ref[0, r0 + D_NOPE:r0 + D_NOPE + half, :] = (x1 * cos - x2 * sin).astype(bf16)
        qt_ref[0, r0 + D_NOPE + half:r0 + D_NOPE + D_ROPE, :] = (x2 * cos + x1 * sin).astype(bf16)
        qt_ref[0, r0 + D_NOPE + D_ROPE:r0 + D_QK, :] = jnp.zeros((D_QK - D_NOPE - D_ROPE, tm), bf16)


def _queries_absorbed(ql, wq_ref, cos_ref, sin_ref, qng_ref, qrg_ref, kng_ref, wukt_ref, qa_ref, qr_ref):
    q = _dot(ql, wq_ref[...])
    cos2, sin2 = cos_ref[...], sin_ref[...]
    for h in range(N_HEADS):
        qn = q[:, h * D_QK:h * D_QK + D_NOPE]
        qn = qn * _rms(qn, D_NOPE) * qng_ref[...]
        qa_ref[h] = _dot((qn * kng_ref[...]).astype(bf16), wukt_ref[h]).astype(bf16)
        qr = q[:, h * D_QK + D_NOPE:(h + 1) * D_QK]
        qr = _rope128(qr * _rms(qr, D_ROPE) * qrg_ref[...], cos2, sin2)
        qr_ref[h] = qr[:, :D_ROPE].astype(bf16)


def _keys_values(c, kr128, wk_ref, wvt_ref, kng_ref, k_ref, vt_ref):
    tm = c.shape[0]
    k = _dot(c, wk_ref[...])
    for h in range(N_HEADS):
        kn = k[:, h * D_NOPE:(h + 1) * D_NOPE]
        k_ref[:, h * D_NOPE:(h + 1) * D_NOPE] = (kn * _rms(kn, D_NOPE) * kng_ref[...]).astype(bf16)
    k_ref[:, N_HEADS * D_NOPE:] = kr128.astype(bf16)
    v_t = _dot_nt(wvt_ref[...], c).astype(bf16)
    n_blk, tk = vt_ref.shape[1], vt_ref.shape[4]
    row = lax.broadcasted_iota(jnp.int32, (D_VX - D_V, tk), 0)
    ones_row = jnp.where(row == 0, 1.0, 0.0).astype(bf16)
    for j in range(n_blk):
        for h in range(N_HEADS):
            vt_ref[0, j, h, 0:D_V, :] = v_t[h * D_V:(h + 1) * D_V, j * tk:(j + 1) * tk]
            vt_ref[0, j, h, D_V:D_VX, :] = ones_row


def _in_proj_body(*refs, n_conv, **static):
    if not static["fuse_qkv"]:
        _in_proj_tile(pl.program_id(0), refs, **static)
        return
    i = pl.program_id(0)
    w32_ref, w16_ref, wt_ref = refs[6], refs[-3], refs[-1]

    def convert(rows):
        chunk = w32_ref[0:rows, :].astype(bf16)
        w16_ref[0:rows, :] = chunk
        wt_ref[pl.ds(pl.multiple_of(i * W_CHUNK, W_CHUNK), rows), :] = chunk

    last_rows = D_IN - (n_conv - 1) * W_CHUNK
    pl.when(i < n_conv - 1)(lambda: convert(W_CHUNK))
    pl.when(i == n_conv - 1)(lambda: convert(last_rows))

    @pl.when(i >= n_conv)
    def _():
        _in_proj_tile(i - n_conv, refs[:6] + (wt_ref,) + refs[7:-3] + refs[-2:-1], **static)


def _in_proj_tile(i, refs, *, nseg, seg_len, tiles_per_seq, carry, fuse_qkv):
    (x_ref, hist_ref, cos_ref, sin_ref, pos_ref, ng_ref, wt_ref, qg_ref, kvg_ref, krg_ref, wp_ref,
     ps_ref) = refs[:12]
    if fuse_qkv:
        (cost_ref, sint_ref, wqt_ref, qng_ref, qrg_ref, wk_ref, wvt_ref, kng_ref, wo32_ref,
         wg32_ref) = refs[12:22]
        (pact_ref, gm_ref, ckv_ref, kr_ref, npool_ref, qt_ref, k_ref, vt_ref, wo16_ref, wg16_ref,
         ucat_ref) = refs[22:]
    else:
        wq_ref, qng_ref, qrg_ref, kng_ref, wukt_ref = refs[12:17]
        pact_ref, gm_ref, ckv_ref, kr_ref, npool_ref, qa_ref, qr_ref, ucat_ref = refs[17:]
    tm = nseg * seg_len

    x = x_ref[...]
    xn = (x * _rms(x, D_MODEL) * ng_ref[...]).astype(bf16)

    u = _dot_nt(xn, wt_ref[C_U:C_U + D_POOL, :])
    if carry:
        @pl.when(i % tiles_per_seq == 0)
        def _():
            ucat_ref[:, 0:HALO, :] = jnp.zeros((nseg, HALO, D_POOL), f32)
    else:
        ucat_ref[:, 0:HALO - POOL_HIST, :] = jnp.zeros((nseg, HALO - POOL_HIST, D_POOL), f32)
        for t in range(POOL_HIST):
            ucat_ref[:, HALO - POOL_HIST + t, :] = hist_ref[t]
    ucat_ref[:, HALO:HALO + seg_len, :] = u.reshape(nseg, seg_len, D_POOL)

    pos = pos_ref[...]
    gp = _dot_nt(xn, wt_ref[C_GP:C_GP + D_POOL, :])
    gate = gp * jax.nn.sigmoid(gp) * ps_ref[...]

    def pool_group(g):
        w = POOL_WINDOWS[g]
        sl = slice(g * POOL_GROUP, (g + 1) * POOL_GROUP)
        acc = ucat_ref[:, :, sl]
        shift = 1
        while shift < w:
            acc = acc + pltpu.roll(acc, shift, 1)
            shift *= 2
        acc = acc[:, HALO:, :].reshape(tm, POOL_GROUP)
        inv_cnt = 1.0 / jnp.minimum(pos + 1.0, float(w))
        d = (acc * inv_cnt - u[:, sl]).astype(bf16)
        mixed = _dot(d, wp_ref[g])
        pact_ref[:, sl] = (mixed * gate[:, sl]).astype(bf16)

    cq = _dot_nt(xn, wt_ref[C_Q:C_Q + Q_RANK, :])
    pool_group(0)
    pool_group(1)
    ql = (cq * _rms(cq, Q_RANK) * qg_ref[...]).astype(bf16)
    if fuse_qkv:
        _queries_t(ql, wqt_ref, cost_ref, sint_ref, qng_ref, qrg_ref, qt_ref)
    else:
        _queries_absorbed(ql, wq_ref, cos_ref, sin_ref, qng_ref, qrg_ref, kng_ref, wukt_ref, qa_ref, qr_ref)

    ckv = _dot_nt(xn, wt_ref[C_KV:C_KV + KV_RANK, :])
    pool_group(2)
    ckv = ckv * _rms(ckv, KV_RANK) * kvg_ref[...]
    ckv_ref[...] = ckv

    kr = _dot_nt(xn, wt_ref[C_KR:C_KR + LANES, :])
    gm = _dot_nt(xn, wt_ref[C_GM:C_GM + D_MLA, :])
    pool_group(3)
    if carry:
        npool_ref[...] = ucat_ref[:, seg_len + 1:seg_len + HALO, :]
        ucat_ref[:, 0:HALO, :] = ucat_ref[:, seg_len:seg_len + HALO, :]
    else:
        for t in range(POOL_HIST):
            npool_ref[t] = ucat_ref[:, seg_len + 1 + t, :]
    lane = lax.broadcasted_iota(jnp.int32, (tm, LANES), 1)
    kr = jnp.where(lane < D_ROPE, kr, 0.0)
    kr = _rope128(kr * _rms(kr, D_ROPE) * krg_ref[...], cos_ref[...], sin_ref[...])
    kr_ref[...] = kr[:, :D_ROPE]
    gm_ref[...] = (gm * jax.nn.sigmoid(gm)).astype(bf16)
    if fuse_qkv:
        _keys_values(ckv.astype(bf16), kr, wk_ref, wvt_ref, kng_ref, k_ref, vt_ref)
        wo16_ref[...] = wo32_ref[...].astype(bf16)
        wg16_ref[...] = wg32_ref[...].astype(bf16)


def _in_proj(x2d, hist, cos2, sin2, pos, wts, *, nseg, seg_len, n_seq, carry, rope_t=None, kv_block=None):
    n_tok = x2d.shape[0]
    tm = nseg * seg_len
    n_tiles = n_tok // tm
    tiles_per_seq = n_tiles // n_seq if carry else 1
    tab_tiles = cos2.shape[0] // tm
    fuse_qkv = rope_t is not None
    n_conv = pl.cdiv(D_IN, W_CHUNK) if fuse_qkv else 0
    tile = (lambda i: jnp.maximum(i - n_conv, 0)) if n_conv else (lambda i: i)
    row = lambda i: (tile(i), 0)
    tab = lambda i: (tile(i) % tab_tiles, 0)
    if carry:
        hist_spec = pl.BlockSpec((1, HALO, D_POOL), lambda i: (0, 0, 0))
        npool_spec = pl.BlockSpec((1, POOL_HIST, D_POOL), lambda i: (tile(i) // tiles_per_seq, 0, 0))
    else:
        hist_spec = pl.BlockSpec((POOL_HIST, nseg, D_POOL), lambda i: (0, tile(i), 0))
        npool_spec = pl.BlockSpec((POOL_HIST, nseg, D_POOL), lambda i: (0, tile(i), 0))
    x_spec = pl.BlockSpec((tm, D_MODEL), row)
    if fuse_qkv:
        w_chunk = lambda i: (jnp.minimum(i, n_conv - 1), 0)
        w_spec, w_arg = pl.BlockSpec((W_CHUNK, D_MODEL), w_chunk), wts["w_in_t_f32"]
    else:
        w_spec, w_arg = _resident((D_IN, D_MODEL)), wts["w_in_t"]
    in_specs = [
        x_spec, hist_spec, pl.BlockSpec((tm, LANES), tab),
        pl.BlockSpec((tm, LANES), tab), pl.BlockSpec((tm, 1), tab), _resident((1, D_MODEL)),
        w_spec, _resident((1, Q_RANK)), _resident((1, KV_RANK)),
        _resident((1, LANES)), _resident((4, POOL_GROUP, POOL_GROUP)), _resident((1, D_POOL)),
    ]
    args = [x2d, hist, cos2, sin2, pos, wts["norm_g"], w_arg, wts["q_norm_g"],
            wts["kv_norm_g"], wts["k_rope_g"], wts["w_pool"], wts["pool_scale"]]
    out_specs = [pl.BlockSpec((tm, D_POOL), row), pl.BlockSpec((tm, D_MLA), row),
                 pl.BlockSpec((tm, KV_RANK), row), pl.BlockSpec((tm, D_ROPE), row), npool_spec]
    out_shape = [jax.ShapeDtypeStruct((n_tok, D_POOL), bf16), jax.ShapeDtypeStruct((n_tok, D_MLA), bf16),
                 jax.ShapeDtypeStruct((n_tok, KV_RANK), f32), jax.ShapeDtypeStruct((n_tok, D_ROPE), f32),
                 jax.ShapeDtypeStruct((n_seq, POOL_HIST, D_POOL) if carry else (POOL_HIST, n_seq, D_POOL), f32)]
    scratch = [pltpu.VMEM((nseg, HALO + seg_len, D_POOL), f32)]
    if fuse_qkv:
        seq_tab = lambda i: (0, tile(i) % tiles_per_seq)
        seq_blk = lambda i: (tile(i) // tiles_per_seq, 0, tile(i) % tiles_per_seq)
        in_specs += [pl.BlockSpec((D_ROPE // 2, tm), seq_tab), pl.BlockSpec((D_ROPE // 2, tm), seq_tab),
                     _resident((N_HEADS * (D_NOPE + D_ROPE), Q_RANK)), _resident((D_NOPE, 1)),
                     _resident((D_ROPE, 1)),
                     _resident((KV_RANK, N_HEADS * D_NOPE)), _resident((D_MLA, KV_RANK)),
                     _resident((1, D_NOPE)),
                     pl.BlockSpec((D_MODEL // n_tiles, D_MODEL), row),
                     pl.BlockSpec((D_MODEL // n_tiles, D_MODEL), row)]
        args += [rope_t[0], rope_t[1], wts["w_uq_t"], wts["q_nope_g_col"], wts["q_rope_g_col"],
                 wts["w_uk"], wts["w_uv_t"], wts["k_nope_g"], wts["w_out_f32"], wts["w_ple_gate_f32"]]
        out_specs += [
            pl.BlockSpec((1, N_HEADS * D_QK, tm), seq_blk),
            pl.BlockSpec((tm, D_K), row),
            pl.BlockSpec((1, tm // kv_block, N_HEADS, D_VX, kv_block), lambda i: seq_blk(i)[::2] + (0, 0, 0)),
            pl.BlockSpec((D_MODEL // n_tiles, D_MODEL), row),
            pl.BlockSpec((D_MODEL // n_tiles, D_MODEL), row),
            pl.BlockSpec((W_CHUNK, D_MODEL), w_chunk)]
        out_shape += [jax.ShapeDtypeStruct((n_seq, N_HEADS * D_QK, tiles_per_seq * tm), bf16),
                      jax.ShapeDtypeStruct((n_tok, D_K), bf16),
                      jax.ShapeDtypeStruct((n_seq, tiles_per_seq * tm // kv_block, N_HEADS, D_VX, kv_block),
                                           bf16),
                      jax.ShapeDtypeStruct((D_MODEL, D_MODEL), bf16),
                      jax.ShapeDtypeStruct((D_MODEL, D_MODEL), bf16),
                      jax.ShapeDtypeStruct((D_IN, D_MODEL), bf16)]
        scratch += [pltpu.VMEM((n_conv * W_CHUNK, D_MODEL), bf16)]
    else:
        in_specs += [_resident((Q_RANK, N_HEADS * D_QK)), _resident((1, D_NOPE)), _resident((1, LANES)),
                     _resident((1, D_NOPE)), _resident((N_HEADS, D_NOPE, KV_RANK))]
        args += [wts["w_uq"], wts["q_nope_g"], wts["q_rope_g"], wts["k_nope_g"], wts["w_uk_t"]]
        out_specs += [pl.BlockSpec((N_HEADS, tm, KV_RANK), lambda i: (0, i, 0)),
                      pl.BlockSpec((N_HEADS, tm, D_ROPE), lambda i: (0, i, 0))]
        out_shape += [jax.ShapeDtypeStruct((N_HEADS, n_tok, KV_RANK), bf16),
                      jax.ShapeDtypeStruct((N_HEADS, n_tok, D_ROPE), bf16)]
    body = functools.partial(_in_proj_body, n_conv=n_conv, nseg=nseg, seg_len=seg_len,
                             tiles_per_seq=tiles_per_seq, carry=carry, fuse_qkv=fuse_qkv)
    return pl.pallas_call(
        body,
        grid=(n_conv + n_tiles,),
        in_specs=in_specs,
        out_specs=out_specs,
        out_shape=out_shape,
        scratch_shapes=scratch,
        compiler_params=_cparams(1),
        name="in_proj",
    )(*args)


def _attn_prompt_body(qt_ref, k_ref, vt_ref, gm_ref, o_ref, m_ref, acc_ref, s_ref, mx_ref, *, tk):
    g = pl.program_id(1)
    tq = 2 * tk
    hi = slice(tk, tq)

    def column_max(s):
        x = jnp.max(s.reshape(s.shape[0] // SUBLANES, SUBLANES, s.shape[1]), axis=0)
        for shift in (4, 2, 1):
            x = jnp.maximum(x, pltpu.roll(x, shift, 0))
        return x

    def keys(h, j):
        rows = pl.ds(pl.multiple_of(j * tk, tk), tk)
        return jnp.concatenate([k_ref[rows, h * D_NOPE:(h + 1) * D_NOPE], k_ref[rows, N_HEADS * D_NOPE:]],
                               axis=1)

    def chunk_visible(n_q):
        k_chunk = lax.broadcasted_iota(jnp.int32, (tk, n_q), 0) // CHUNK
        q_chunk = lax.broadcasted_iota(jnp.int32, (tk, n_q), 1) // CHUNK
        return k_chunk <= q_chunk

    def scores(h, j, buf, first_diag=False):
        s = _dot(keys(h, j), qt_ref[0, h * D_QK:(h + 1) * D_QK, :])
        if first_diag:
            s = jnp.where(chunk_visible(tq), s, NEG_INF)
        s_ref[buf, h] = s
        mx_ref[buf, h] = column_max(s)

    def scores_hi(h, j, buf):
        s = _dot(keys(h, j), qt_ref[0, h * D_QK:(h + 1) * D_QK, hi])
        s = jnp.where(chunk_visible(tk), s, NEG_INF)
        s_ref[buf, h, :, hi] = s
        mx_ref[buf, h, :, hi] = column_max(s)

    def values(h, j, buf, cols=slice(None)):
        m_old = m_ref[h, :, cols]
        m_new = jnp.maximum(m_old, mx_ref[buf, h, :, cols])
        n_q = m_new.shape[1]
        alpha = jnp.exp2(m_old - m_new)
        s = s_ref[buf, h, :, cols]
        p = jnp.exp2(s.reshape(tk // SUBLANES, SUBLANES, n_q) - m_new[None]).reshape(tk, n_q)
        m_ref[h, :, cols] = m_new
        acc = acc_ref[h, :, cols].reshape(D_VX // SUBLANES, SUBLANES, n_q) * alpha[None]
        acc_ref[h, :, cols] = acc.reshape(D_VX, n_q) + _dot(vt_ref[0, j, h], p.astype(bf16))

    def first_values(h, j, buf):
        m_new = mx_ref[buf, h]
        p = jnp.exp2(s_ref[buf, h].reshape(tk // SUBLANES, SUBLANES, tq) - m_new[None]).reshape(tk, tq)
        m_ref[h] = m_new
        acc_ref[h] = _dot(vt_ref[0, j, h], p.astype(bf16))

    def both(score_fn, value_fn):
        for h in range(N_HEADS):
            score_fn(h)
            value_fn(h)

    d0, d1 = 2 * g, 2 * g + 1
    for h in range(N_HEADS):
        scores(h, d0, 0, first_diag=True)
    both(lambda h: scores_hi(h, d1, 1), lambda h: first_values(h, d0, 0))

    @pl.when(g == 0)
    def _():
        for h in range(N_HEADS):
            values(h, d1, 1, hi)

    @pl.when(g > 0)
    def _():
        both(lambda h: scores(h, 0, 0), lambda h: values(h, d1, 1, hi))

        def step_pair(u, carry):
            both(lambda h: scores(h, 2 * u + 1, 1), lambda h: values(h, 2 * u, 0))
            both(lambda h: scores(h, 2 * u + 2, 0), lambda h: values(h, 2 * u + 1, 1))
            return carry

        lax.fori_loop(0, g - 1, step_pair, 0)
        both(lambda h: scores(h, d0 - 1, 1), lambda h: values(h, d0 - 2, 0))
        for h in range(N_HEADS):
            values(h, d0 - 1, 1)

    for h in range(N_HEADS):
        hs = slice(h * D_V, (h + 1) * D_V)
        o = (acc_ref[h, 0:D_V, :] * (1.0 / acc_ref[h, D_V:D_V + 1, :])).T
        o_ref[:, hs] = (o * gm_ref[:, hs].astype(f32)).astype(bf16)


def _attn_prompt(qt, k, vt, gm, *, n_seq, seq_len, tk):
    tq = 2 * tk
    nq = seq_len // tq
    body = functools.partial(_attn_prompt_body, tk=tk)
    return pl.pallas_call(
        body,
        grid=(n_seq, nq),
        in_specs=[pl.BlockSpec((1, N_HEADS * D_QK, tq), lambda b, i: (b, 0, i)),
                  pl.BlockSpec((seq_len, D_K), lambda b, i: (b, 0)),
                  pl.BlockSpec((1, seq_len // tk, N_HEADS, D_VX, tk), lambda b, i: (b, 0, 0, 0, 0)),
                  pl.BlockSpec((tq, D_MLA), lambda b, i: (b * nq + i, 0))],
        out_specs=pl.BlockSpec((tq, D_MLA), lambda b, i: (b * nq + i, 0)),
        out_shape=jax.ShapeDtypeStruct((n_seq * seq_len, D_MLA), bf16),
        scratch_shapes=[pltpu.VMEM((N_HEADS, SUBLANES, tq), f32), pltpu.VMEM((N_HEADS, D_VX, tq), f32),
                        pltpu.VMEM((2, N_HEADS, tk, tq), f32),
                        pltpu.VMEM((2, N_HEADS, SUBLANES, tq), f32)],
        compiler_params=_cparams(2),
        name="attn_prompt",
    )(qt, k, vt, gm)


def _attn_sample_body(cckv_ref, ckr_t_ref, nckv_ref, nkr_ref, qa_ref, qr_ref, gm_ref, wukt_ref, wuv_ref,
                      o_ref, c_ref, ktail_ref, s_ref, olat_ref, *, past, t_new, chunks, n_b):
    n_tail = c_ref.shape[1] - past
    rows = N_HEADS * t_new
    c_ref[:, past:past + t_new, :] = nckv_ref[...].astype(bf16)
    c_ref[:, past + t_new:, :] = jnp.zeros((n_b, n_tail - t_new, KV_RANK), bf16)
    ktail_ref[:, 0:t_new, :] = nkr_ref[...].astype(bf16)
    ktail_ref[:, t_new:, :] = jnp.zeros((n_b, n_tail - t_new, D_ROPE), bf16)

    def chunk_scores(b, start, size):
        cached = start < past
        qa = qa_ref[:, b * t_new:(b + 1) * t_new, :].reshape(rows, KV_RANK)
        qr = qr_ref[:, b * t_new:(b + 1) * t_new, :].reshape(rows, D_ROPE)
        if cached:
            c32 = cckv_ref[b, start:start + size, :]
            c_ref[b, start:start + size, :] = c32.astype(bf16)
            c_t = c32.T.astype(bf16)
            k_t = _dot(wukt_ref[...], c_t)
            s_nope = _dot(qa, c_t)
            s_rope = _dot(qr, ckr_t_ref[b, :, start:start + size].astype(bf16))
        else:
            c = c_ref[b, start:start + size, :]
            k_t = _dot_nt(wukt_ref[...], c)
            s_nope = _dot_nt(qa, c)
            s_rope = _dot_nt(qr, ktail_ref[b])
        ssq = jnp.sum((k_t * k_t).reshape(N_HEADS, D_NOPE, size), axis=1)
        r = lax.rsqrt(ssq * (1.0 / D_NOPE) + EPS)
        for h in range(N_HEADS):
            hs = slice(h * t_new, (h + 1) * t_new)
            s = (s_nope[hs] * r[h:h + 1, :] + s_rope[hs]) * ATTN_SCALE
            if not cached:
                key = lax.broadcasted_iota(jnp.int32, (t_new, size), 1)
                s = jnp.where(key < t_new, s, NEG_INF)
            s_ref[b, hs, start:start + size] = s

    def chunk_values(b, start, size, state):
        m, l, acc = state
        s = s_ref[b, :, start:start + size]
        m_new = jnp.maximum(m, jnp.max(s, axis=-1, keepdims=True))
        alpha = jnp.exp(m - m_new)
        p = jnp.exp(s - m_new)
        l = alpha * l + jnp.sum(p, axis=-1, keepdims=True)
        acc = alpha * acc + _dot(p.astype(bf16), c_ref[b, start:start + size, :])
        return m_new, l, acc

    def finish(b, state):
        _, l, acc = state
        olat_ref[b] = (acc / l).astype(bf16)

    def value_up_projection():
        for h in range(N_HEADS):
            o_lat = olat_ref[:, h * t_new:(h + 1) * t_new, :].reshape(n_b * t_new, KV_RANK)
            hs = slice(h * D_V, (h + 1) * D_V)
            o_ref[:, hs] = (_dot(o_lat, wuv_ref[h]) * gm_ref[:, hs].astype(f32)).astype(bf16)

    items = [(b,) + ch for b in range(n_b) for ch in chunks]
    init = (jnp.full((rows, 1), NEG_INF, f32), jnp.zeros((rows, 1), f32),
            jnp.zeros((rows, KV_RANK), f32))
    state = init
    chunk_scores(*items[0])
    for nxt, cur in zip(items[1:] + [None], items):
        if nxt is not None:
            chunk_scores(*nxt)
        state = chunk_values(*cur, state)
        if nxt is None or nxt[0] != cur[0]:
            finish(cur[0], state)
            state = init
    value_up_projection()


def _attn_sample(cache_ckv, cache_krope_t, ckv_new, krope_new, q_abs, q_rope, gm, wts, *, t_new, n_b):
    n_seq, past, _ = cache_ckv.shape
    chunk = 512
    chunks = tuple((s, min(chunk, past - s)) for s in range(0, past, chunk)) + ((past, LANES),)
    s_pad = past + LANES
    body = functools.partial(_attn_sample_body, past=past, t_new=t_new, chunks=chunks, n_b=n_b)
    return pl.pallas_call(
        body,
        grid=(n_seq // n_b,),
        in_specs=[pl.BlockSpec((n_b, past, KV_RANK), lambda b: (b, 0, 0)),
                  pl.BlockSpec((n_b, D_ROPE, past), lambda b: (b, 0, 0)),
                  pl.BlockSpec((n_b, t_new, KV_RANK), lambda b: (b, 0, 0)),
                  pl.BlockSpec((n_b, t_new, D_ROPE), lambda b: (b, 0, 0)),
                  pl.BlockSpec((N_HEADS, n_b * t_new, KV_RANK), lambda b: (0, b, 0)),
                  pl.BlockSpec((N_HEADS, n_b * t_new, D_ROPE), lambda b: (0, b, 0)),
                  pl.BlockSpec((n_b * t_new, D_MLA), lambda b: (b, 0)),
                  _resident((N_HEADS * D_NOPE, KV_RANK)),
                  _resident((N_HEADS, KV_RANK, D_V))],
        out_specs=pl.BlockSpec((n_b * t_new, D_MLA), lambda b: (b, 0)),
        out_shape=jax.ShapeDtypeStruct((n_seq * t_new, D_MLA), bf16),
        scratch_shapes=[pltpu.VMEM((n_b, s_pad, KV_RANK), bf16),
                        pltpu.VMEM((n_b, s_pad - past, D_ROPE), bf16),
                        pltpu.VMEM((n_b, N_HEADS * t_new, s_pad), f32),
                        pltpu.VMEM((n_b, N_HEADS * t_new, KV_RANK), bf16)],
        compiler_params=_cparams(1),
        name="attn_sample",
    )(cache_ckv, cache_krope_t, ckv_new, krope_new, q_abs, q_rope, gm,
      wts["w_uk_t"].reshape(N_HEADS * D_NOPE, KV_RANK), wts["w_uv3"])


def _out_proj_body(x_ref, pa_ref, ma_ref, p_ref, wo_ref, png_ref, wg_ref, bg_ref, wple_ref, y_ref):
    h = x_ref[...] + _dot(pa_ref[...], wo_ref[0:D_POOL, :]) + _dot(ma_ref[...], wo_ref[D_POOL:, :])
    hn = (h * _rms(h, D_MODEL) * png_ref[...]).astype(bf16)
    gate = jax.nn.sigmoid(_dot(hn, wg_ref[...]) + bg_ref[...])
    y_ref[...] = h + gate * _dot(p_ref[...].astype(bf16), wple_ref[...])


def _out_proj(x2d, pool_act, mla_act, p2d, wts, *, tm):
    n_tok = x2d.shape[0]
    row = lambda i: (i, 0)
    return pl.pallas_call(
        _out_proj_body,
        grid=(n_tok // tm,),
        in_specs=[pl.BlockSpec((tm, D_MODEL), row), pl.BlockSpec((tm, D_POOL), row),
                  pl.BlockSpec((tm, D_MLA), row), pl.BlockSpec((tm, D_PLE), row),
                  _resident((D_MODEL, D_MODEL)), _resident((1, D_MODEL)),
                  _resident((D_MODEL, D_MODEL)), _resident((1, D_MODEL)),
                  _resident((D_PLE, D_MODEL))],
        out_specs=pl.BlockSpec((tm, D_MODEL), row),
        out_shape=jax.ShapeDtypeStruct((n_tok, D_MODEL), f32),
        compiler_params=_cparams(1),
        name="out_proj",
    )(x2d, pool_act, mla_act, p2d, wts["w_out"], wts["ple_norm_g"], wts["w_ple_gate"],
      wts["b_ple_gate"], wts["w_ple"])


def _rope_tables(pos0, t, reps=1):
    pos = (pos0 + np.arange(t)).astype(np.float64)
    inv = ROPE_THETA ** (-(np.arange(0, D_ROPE, 2, dtype=np.float64) / D_ROPE))
    ang = pos[:, None] * inv[None, :]
    cos, sin = np.cos(ang), np.sin(ang)
    zero = np.zeros((t, LANES - D_ROPE))
    rows = (np.concatenate([cos, cos, zero], axis=-1), np.concatenate([-sin, sin, zero], axis=-1),
            pos[:, None])
    as_f32 = lambda a: jnp.asarray(a.astype(np.float32))
    return tuple(as_f32(np.tile(a, (reps, 1))) for a in rows) + ((as_f32(cos.T), as_f32(sin.T)),)


def _prep_weights(norm_g, w_in, q_norm_g, w_uq, kv_norm_g, w_ukv, q_nope_g, q_rope_g, k_nope_g,
                  k_rope_g, w_pool, pool_scale, w_out, ple_norm_g, w_ple_gate, b_ple_gate, w_ple):
    w_uq_r = jnp.pad(w_uq.reshape(Q_RANK, N_HEADS, D_NOPE + D_ROPE),
                     ((0, 0), (0, 0), (0, D_QK - D_NOPE - D_ROPE))).reshape(Q_RANK, N_HEADS * D_QK)
    w_ukv3 = w_ukv.reshape(KV_RANK, N_HEADS, D_NOPE + D_V)
    w_uk3, w_uv3 = w_ukv3[..., :D_NOPE], w_ukv3[..., D_NOPE:]
    pad_rope = lambda g: jnp.pad(g, (0, LANES - D_ROPE))[None, :]
    return {
        "norm_g": norm_g[None, :], "w_in_t_f32": w_in.T, "q_norm_g": q_norm_g[None, :],
        "kv_norm_g": kv_norm_g[None, :], "k_rope_g": pad_rope(k_rope_g),
        "w_pool": w_pool.astype(bf16), "pool_scale": pool_scale[None, :],
        "w_uq": w_uq_r.astype(bf16), "q_nope_g": q_nope_g[None, :], "q_rope_g": pad_rope(q_rope_g),
        "w_uq_t": w_uq.T.astype(bf16),
        "q_nope_g_col": q_nope_g[:, None], "q_rope_g_col": q_rope_g[:, None],
        "k_nope_g": k_nope_g[None, :],
        "w_uk": w_uk3.reshape(KV_RANK, N_HEADS * D_NOPE).astype(bf16),
        "w_uv_t": w_uv3.reshape(KV_RANK, D_MLA).T.astype(bf16),
        "w_uk_t": jnp.transpose(w_uk3, (1, 2, 0)).astype(bf16),
        "w_uv3": jnp.transpose(w_uv3, (1, 0, 2)).astype(bf16),
        "w_out_f32": w_out, "w_ple_gate_f32": w_ple_gate,
        "ple_norm_g": ple_norm_g[None, :], "b_ple_gate": b_ple_gate[None, :],
        "w_ple": w_ple.astype(bf16),
    }


def _layer_prompt(x, p, wts):
    n_seq, seq_len, _ = x.shape
    x2d = x.reshape(n_seq * seq_len, D_MODEL)
    cos2, sin2, pos, rope_t = _rope_tables(0, seq_len)
    hist = jnp.zeros((1, HALO, D_POOL), f32)
    pact, gm, ckv, krope, npool, qt, k, vt, w_out16, w_gate16, w_in_t16 = _in_proj(
        x2d, hist, cos2, sin2, pos, wts, nseg=1, seg_len=PROMPT_TILE, n_seq=n_seq, carry=True,
        rope_t=rope_t, kv_block=ATTN_BLOCK)
    wts.update(w_out=w_out16, w_ple_gate=w_gate16, w_in_t=w_in_t16)
    mact = _attn_prompt(qt, k, vt, gm, n_seq=n_seq, seq_len=seq_len, tk=ATTN_BLOCK)
    y = _out_proj(x2d, pact, mact, p.reshape(n_seq * seq_len, D_PLE), wts, tm=OUT_TILE)
    return (y.reshape(x.shape), ckv.reshape(n_seq, seq_len, KV_RANK),
            krope.reshape(n_seq, seq_len, D_ROPE), npool)


def _layer_sample(x, p, state_pool, cache_ckv, cache_krope, wts):
    n_seq, t_new, _ = x.shape
    past = cache_ckv.shape[1]
    n_tok = n_seq * t_new
    x2d = x.reshape(n_tok, D_MODEL)
    cos2, sin2, pos, _ = _rope_tables(past, t_new, reps=n_seq)
    hist = jnp.transpose(state_pool, (1, 0, 2))
    pact, gm, ckv, krope, npool, q_abs, q_rope = _in_proj(
        x2d, hist, cos2, sin2, pos, wts, nseg=SAMPLE_SEGS_PER_TILE, seg_len=t_new, n_seq=n_seq,
        carry=False)
    ckv3 = ckv.reshape(n_seq, t_new, KV_RANK)
    krope3 = krope.reshape(n_seq, t_new, D_ROPE)
    cache_krope_t = jnp.transpose(cache_krope, (0, 2, 1))
    mact = _attn_sample(cache_ckv, cache_krope_t, ckv3, krope3, q_abs, q_rope, gm, wts, t_new=t_new,
                        n_b=SAMPLE_BATCH_PER_STEP)
    y = _out_proj(x2d, pact, mact, p.reshape(n_tok, D_PLE), wts, tm=SAMPLE_OUT_TILE)
    return y.reshape(x.shape), ckv3, krope3, jnp.transpose(npool, (1, 0, 2))


def kernel(x_prompt, x_sample, cache_ckv, cache_krope, state_pool, p_prompt, p_sample, norm_g, w_in,
           q_norm_g, w_uq, kv_norm_g, w_ukv, q_nope_g, q_rope_g, k_nope_g, k_rope_g, w_pool,
           pool_scale, w_out, ple_norm_g, w_ple_gate, b_ple_gate, w_ple):
    depth = norm_g.shape[0]
    layer_w = (norm_g, w_in, q_norm_g, w_uq, kv_norm_g, w_ukv, q_nope_g, q_rope_g, k_nope_g, k_rope_g,
               w_pool, pool_scale, w_out, ple_norm_g, w_ple_gate, b_ple_gate, w_ple)
    yp, ys = x_prompt, x_sample
    outs = [[] for _ in range(6)]
    for i in range(depth):
        wts = _prep_weights(*(w[i] for w in layer_w))
        yp, c1, k1, s1 = _layer_prompt(yp, p_prompt[i], wts)
        ys, c2, k2, s2 = _layer_sample(ys, p_sample[i], state_pool[i], cache_ckv[i], cache_krope[i], wts)
        for lst, val in zip(outs, (c1, k1, s1, c2, k2, s2)):
            lst.append(val)
    return (yp, ys) + tuple(jnp.stack(o) for o in outs)
```

```python
import functools

import jax
import jax.numpy as jnp
import numpy as np
from jax import lax
from jax.experimental import pallas as pl
from jax.experimental.pallas import tpu as pltpu

D_MODEL = 2048
CHUNK = 64
D_POOL = 1024
POOL_WINDOWS = (2, 4, 8, 16)
POOL_GROUP = 256
POOL_HIST = 15
HALO = 16
N_HEADS = 8
D_NOPE = 128
D_ROPE = 64
D_V = 128
D_VX = D_V + 16
D_MLA = N_HEADS * D_V
Q_RANK = 512
KV_RANK = 256
D_PLE = 256
D_QK = 256
D_K = N_HEADS * D_NOPE + (D_QK - D_NOPE)
ROPE_THETA = 10000.0
EPS = 1e-6
ATTN_SCALE = (D_NOPE + D_ROPE) ** -0.5
NEG_INF = -1e30
EXP2_SCALE = ATTN_SCALE * float(np.log2(np.e))
LANES = 128
SUBLANES = 8

C_U, C_GP, C_Q, C_KV, C_KR, C_GM = 0, 1024, 2048, 2560, 2816, 2880
D_IN = 3904

VMEM_LIMIT = 56 * 1024 * 1024
PROMPT_TILE = 256
W_CHUNK = 512
ATTN_BLOCK = 256
OUT_TILE = 512
SAMPLE_OUT_TILE = 256
SAMPLE_BATCH_PER_STEP = 8
SAMPLE_SEGS_PER_TILE = 16

f32 = jnp.float32
bf16 = jnp.bfloat16


def _cparams(n_axes):
    return pltpu.CompilerParams(dimension_semantics=("arbitrary",) * n_axes,
                                vmem_limit_bytes=VMEM_LIMIT)


def _resident(shape):
    nd = len(shape)
    return pl.BlockSpec(shape, lambda *_: (0,) * nd, pipeline_mode=pl.Buffered(1))


def _rms(x, n):
    return lax.rsqrt(jnp.sum(x * x, axis=-1, keepdims=True) * (1.0 / n) + EPS)


def _rms_cols(x_t, n):
    return lax.rsqrt(jnp.sum(x_t * x_t, axis=0, keepdims=True) * (1.0 / n) + EPS)


def _dot(a, b):
    return jnp.dot(a, b, preferred_element_type=f32)


def _dot_nt(a, b):
    return lax.dot_general(a, b, (((1,), (1,)), ((), ())), preferred_element_type=f32)


def _rope128(x, cos2, sin2):
    lane = lax.broadcasted_iota(jnp.int32, x.shape, 1)
    swapped = jnp.where(lane < D_ROPE // 2, pltpu.roll(x, LANES - D_ROPE // 2, 1),
                        pltpu.roll(x, D_ROPE // 2, 1))
    return x * cos2 + swapped * sin2


def _queries_t(ql, wqt_ref, cos_ref, sin_ref, qng_ref, qrg_ref, qt_ref):
    q_t = _dot_nt(wqt_ref[...], ql)
    tm = q_t.shape[1]
    cos, sin = cos_ref[...], sin_ref[...]
    half = D_ROPE // 2
    for h in range(N_HEADS):
        r0 = h * D_QK
        s0 = h * (D_NOPE + D_ROPE)
        qn = q_t[s0:s0 + D_NOPE]
        qn = qn * (_rms_cols(qn, D_NOPE) * EXP2_SCALE) * qng_ref[...]
        qt_ref[0, r0:r0 + D_NOPE, :] = qn.astype(bf16)
        qr = q_t[s0 + D_NOPE:s0 + D_NOPE + D_ROPE]
        qr = qr * (_rms_cols(qr, D_ROPE) * EXP2_SCALE) * qrg_ref[...]
        x1, x2 = qr[:half], qr[half:]
        qt_ref[0, r0 + D_NOPE:r0 + D_NOPE + half, :] = (x1 * cos - x2 * sin).astype(bf16)
        qt_ref[0, r0 + D_NOPE + half:r0 + D_NOPE + D_ROPE, :] = (x2 * cos + x1 * sin).astype(bf16)
        qt_ref[0, r0 + D_NOPE + D_ROPE:r0 + D_QK, :] = jnp.zeros((D_QK - D_NOPE - D_ROPE, tm), bf16)


def _queries_absorbed(ql, wq_ref, cos_ref, sin_ref, qng_ref, qrg_ref, kng_ref, wukt_ref, qa_ref, qr_ref):
    q = _dot(ql, wq_ref[...])
    cos2, sin2 = cos_ref[...], sin_ref[...]
    for h in range(N_HEADS):
        qn = q[:, h * D_QK:h * D_QK + D_NOPE]
        qn = qn * _rms(qn, D_NOPE) * qng_ref[...]
        qa_ref[h] = _dot((qn * kng_ref[...]).astype(bf16), wukt_ref[h]).astype(bf16)
        qr = q[:, h * D_QK + D_NOPE:(h + 1) * D_QK]
        qr = _rope128(qr * _rms(qr, D_ROPE) * qrg_ref[...], cos2, sin2)
        qr_ref[h] = qr[:, :D_ROPE].astype(bf16)


def _keys_values(c, kr128, wk_ref, wvt_ref, kng_ref, k_ref, vt_ref):
    tm = c.shape[0]
    k = _dot(c, wk_ref[...])
    for h in range(N_HEADS):
        kn = k[:, h * D_NOPE:(h + 1) * D_NOPE]
        k_ref[:, h * D_NOPE:(h + 1) * D_NOPE] = (kn * _rms(kn, D_NOPE) * kng_ref[...]).astype(bf16)
    k_ref[:, N_HEADS * D_NOPE:] = kr128.astype(bf16)
    v_t = _dot_nt(wvt_ref[...], c).astype(bf16)
    n_blk, tk = vt_ref.shape[1], vt_ref.shape[4]
    row = lax.broadcasted_iota(jnp.int32, (D_VX - D_V, tk), 0)
    ones_row = jnp.where(row == 0, 1.0, 0.0).astype(bf16)
    for j in range(n_blk):
        for h in range(N_HEADS):
            vt_ref[0, j, h, 0:D_V, :] = v_t[h * D_V:(h + 1) * D_V, j * tk:(j + 1) * tk]
            vt_ref[0, j, h, D_V:D_VX, :] = ones_row


def _in_proj_body(*refs, n_conv, **static):
    if not static["fuse_qkv"]:
        _in_proj_tile(pl.program_id(0), refs, **static)
        return
    i = pl.program_id(0)
    w32_ref, w16_ref, wt_ref = refs[6], refs[-3], refs[-1]

    def convert(rows):
        chunk = w32_ref[0:rows, :].astype(bf16)
        w16_ref[0:rows, :] = chunk
        wt_ref[pl.ds(pl.multiple_of(i * W_CHUNK, W_CHUNK), rows), :] = chunk

    last_rows = D_IN - (n_conv - 1) * W_CHUNK
    pl.when(i < n_conv - 1)(lambda: convert(W_CHUNK))
    pl.when(i == n_conv - 1)(lambda: convert(last_rows))

    @pl.when(i >= n_conv)
    def _():
        _in_proj_tile(i - n_conv, refs[:6] + (wt_ref,) + refs[7:-3] + refs[-2:-1], **static)


def _in_proj_tile(i, refs, *, nseg, seg_len, tiles_per_seq, carry, fuse_qkv):
    (x_ref, hist_ref, cos_ref, sin_ref, pos_ref, ng_ref, wt_ref, qg_ref, kvg_ref, krg_ref, wp_ref,
     ps_ref) = refs[:12]
    if fuse_qkv:
        (cost_ref, sint_ref, wqt_ref, qng_ref, qrg_ref, wk_ref, wvt_ref, kng_ref, wo32_ref,
         wg32_ref) = refs[12:22]
        (pact_ref, gm_ref, ckv_ref, kr_ref, npool_ref, qt_ref, k_ref, vt_ref, wo16_ref, wg16_ref,
         ucat_ref) = refs[22:]
    else:
        wq_ref, qng_ref, qrg_ref, kng_ref, wukt_ref = refs[12:17]
        pact_ref, gm_ref, ckv_ref, kr_ref, npool_ref, qa_ref, qr_ref, ucat_ref = refs[17:]
    tm = nseg * seg_len

    x = x_ref[...]
    xn = (x * _rms(x, D_MODEL) * ng_ref[...]).astype(bf16)

    u = _dot_nt(xn, wt_ref[C_U:C_U + D_POOL, :])
    if carry:
        @pl.when(i % tiles_per_seq == 0)
        def _():
            ucat_ref[:, 0:HALO, :] = jnp.zeros((nseg, HALO, D_POOL), f32)
    else:
        ucat_ref[:, 0:HALO - POOL_HIST, :] = jnp.zeros((nseg, HALO - POOL_HIST, D_POOL), f32)
        for t in range(POOL_HIST):
            ucat_ref[:, HALO - POOL_HIST + t, :] = hist_ref[t]
    ucat_ref[:, HALO:HALO + seg_len, :] = u.reshape(nseg, seg_len, D_POOL)

    pos = pos_ref[...]
    gp = _dot_nt(xn, wt_ref[C_GP:C_GP + D_POOL, :])
    gate = gp * jax.nn.sigmoid(gp) * ps_ref[...]

    def pool_group(g):
        w = POOL_WINDOWS[g]
        sl = slice(g * POOL_GROUP, (g + 1) * POOL_GROUP)
        acc = ucat_ref[:, :, sl]
        shift = 1
        while shift < w:
            acc = acc + pltpu.roll(acc, shift, 1)
            shift *= 2
        acc = acc[:, HALO:, :].reshape(tm, POOL_GROUP)
        inv_cnt = 1.0 / jnp.minimum(pos + 1.0, float(w))
        d = (acc * inv_cnt - u[:, sl]).astype(bf16)
        mixed = _dot(d, wp_ref[g])
        pact_ref[:, sl] = (mixed * gate[:, sl]).astype(bf16)

    cq = _dot_nt(xn, wt_ref[C_Q:C_Q + Q_RANK, :])
    pool_group(0)
    pool_group(1)
    ql = (cq * _rms(cq, Q_RANK) * qg_ref[...]).astype(bf16)
    if fuse_qkv:
        _queries_t(ql, wqt_ref, cost_ref, sint_ref, qng_ref, qrg_ref, qt_ref)
    else:
        _queries_absorbed(ql, wq_ref, cos_ref, sin_ref, qng_ref, qrg_ref, kng_ref, wukt_ref, qa_ref, qr_ref)

    ckv = _dot_nt(xn, wt_ref[C_KV:C_KV + KV_RANK, :])
    pool_group(2)
    ckv = ckv * _rms(ckv, KV_RANK) * kvg_ref[...]
    ckv_ref[...] = ckv

    kr = _dot_nt(xn, wt_ref[C_KR:C_KR + LANES, :])
    gm = _dot_nt(xn, wt_ref[C_GM:C_GM + D_MLA, :])
    pool_group(3)
    if carry:
        npool_ref[...] = ucat_ref[:, seg_len + 1:seg_len + HALO, :]
        ucat_ref[:, 0:HALO, :] = ucat_ref[:, seg_len:seg_len + HALO, :]
    else:
        for t in range(POOL_HIST):
            npool_ref[t] = ucat_ref[:, seg_len + 1 + t, :]
    lane = lax.broadcasted_iota(jnp.int32, (tm, LANES), 1)
    kr = jnp.where(lane < D_ROPE, kr, 0.0)
    kr = _rope128(kr * _rms(kr, D_ROPE) * krg_ref[...], cos_ref[...], sin_ref[...])
    kr_ref[...] = kr[:, :D_ROPE]
    gm_ref[...] = (gm * jax.nn.sigmoid(gm)).astype(bf16)
    if fuse_qkv:
        _keys_values(ckv.astype(bf16), kr, wk_ref, wvt_ref, kng_ref, k_ref, vt_ref)
        wo16_ref[...] = wo32_ref[...].astype(bf16)
        wg16_ref[...] = wg32_ref[...].astype(bf16)


def _in_proj(x2d, hist, cos2, sin2, pos, wts, *, nseg, seg_len, n_seq, carry, rope_t=None, kv_block=None):
    n_tok = x2d.shape[0]
    tm = nseg * seg_len
    n_tiles = n_tok // tm
    tiles_per_seq = n_tiles // n_seq if carry else 1
    tab_tiles = cos2.shape[0] // tm
    fuse_qkv = rope_t is not None
    n_conv = pl.cdiv(D_IN, W_CHUNK) if fuse_qkv else 0
    tile = (lambda i: jnp.maximum(i - n_conv, 0)) if n_conv else (lambda i: i)
    row = lambda i: (tile(i), 0)
    tab = lambda i: (tile(i) % tab_tiles, 0)
    if carry:
        hist_spec = pl.BlockSpec((1, HALO, D_POOL), lambda i: (0, 0, 0))
        npool_spec = pl.BlockSpec((1, POOL_HIST, D_POOL), lambda i: (tile(i) // tiles_per_seq, 0, 0))
    else:
        hist_spec = pl.BlockSpec((POOL_HIST, nseg, D_POOL), lambda i: (0, tile(i), 0))
        npool_spec = pl.BlockSpec((POOL_HIST, nseg, D_POOL), lambda i: (0, tile(i), 0))
    x_spec = pl.BlockSpec((tm, D_MODEL), row)
    if fuse_qkv:
        w_chunk = lambda i: (jnp.minimum(i, n_conv - 1), 0)
        w_spec, w_arg = pl.BlockSpec((W_CHUNK, D_MODEL), w_chunk), wts["w_in_t_f32"]
    else:
        w_spec, w_arg = _resident((D_IN, D_MODEL)), wts["w_in_t"]
    in_specs = [
        x_spec, hist_spec, pl.BlockSpec((tm, LANES), tab),
        pl.BlockSpec((tm, LANES), tab), pl.BlockSpec((tm, 1), tab), _resident((1, D_MODEL)),
        w_spec, _resident((1, Q_RANK)), _resident((1, KV_RANK)),
        _resident((1, LANES)), _resident((4, POOL_GROUP, POOL_GROUP)), _resident((1, D_POOL)),
    ]
    args = [x2d, hist, cos2, sin2, pos, wts["norm_g"], w_arg, wts["q_norm_g"],
            wts["kv_norm_g"], wts["k_rope_g"], wts["w_pool"], wts["pool_scale"]]
    out_specs = [pl.BlockSpec((tm, D_POOL), row), pl.BlockSpec((tm, D_MLA), row),
                 pl.BlockSpec((tm, KV_RANK), row), pl.BlockSpec((tm, D_ROPE), row), npool_spec]
    out_shape = [jax.ShapeDtypeStruct((n_tok, D_POOL), bf16), jax.ShapeDtypeStruct((n_tok, D_MLA), bf16),
                 jax.ShapeDtypeStruct((n_tok, KV_RANK), f32), jax.ShapeDtypeStruct((n_tok, D_ROPE), f32),
                 jax.ShapeDtypeStruct((n_seq, POOL_HIST, D_POOL) if carry else (POOL_HIST, n_seq, D_POOL), f32)]
    scratch = [pltpu.VMEM((nseg, HALO + seg_len, D_POOL), f32)]
    if fuse_qkv:
        seq_tab = lambda i: (0, tile(i) % tiles_per_seq)
        seq_blk = lambda i: (tile(i) // tiles_per_seq, 0, tile(i) % tiles_per_seq)
        in_specs += [pl.BlockSpec((D_ROPE // 2, tm), seq_tab), pl.BlockSpec((D_ROPE // 2, tm), seq_tab),
                     _resident((N_HEADS * (D_NOPE + D_ROPE), Q_RANK)), _resident((D_NOPE, 1)),
                     _resident((D_ROPE, 1)),
                     _resident((KV_RANK, N_HEADS * D_NOPE)), _resident((D_MLA, KV_RANK)),
                     _resident((1, D_NOPE)),
                     pl.BlockSpec((D_MODEL // n_tiles, D_MODEL), row),
                     pl.BlockSpec((D_MODEL // n_tiles, D_MODEL), row)]
        args += [rope_t[0], rope_t[1], wts["w_uq_t"], wts["q_nope_g_col"], wts["q_rope_g_col"],
                 wts["w_uk"], wts["w_uv_t"], wts["k_nope_g"], wts["w_out_f32"], wts["w_ple_gate_f32"]]
        out_specs += [
            pl.BlockSpec((1, N_HEADS * D_QK, tm), seq_blk),
            pl.BlockSpec((tm, D_K), row),
            pl.BlockSpec((1, tm // kv_block, N_HEADS, D_VX, kv_block), lambda i: seq_blk(i)[::2] + (0, 0, 0)),
            pl.BlockSpec((D_MODEL // n_tiles, D_MODEL), row),
            pl.BlockSpec((D_MODEL // n_tiles, D_MODEL), row),
            pl.BlockSpec((W_CHUNK, D_MODEL), w_chunk)]
        out_shape += [jax.ShapeDtypeStruct((n_seq, N_HEADS * D_QK, tiles_per_seq * tm), bf16),
                      jax.ShapeDtypeStruct((n_tok, D_K), bf16),
                      jax.ShapeDtypeStruct((n_seq, tiles_per_seq * tm // kv_block, N_HEADS, D_VX, kv_block),
                                           bf16),
                      jax.ShapeDtypeStruct((D_MODEL, D_MODEL), bf16),
                      jax.ShapeDtypeStruct((D_MODEL, D_MODEL), bf16),
                      jax.ShapeDtypeStruct((D_IN, D_MODEL), bf16)]
        scratch += [pltpu.VMEM((n_conv * W_CHUNK, D_MODEL), bf16)]
    else:
        in_specs += [_resident((Q_RANK, N_HEADS * D_QK)), _resident((1, D_NOPE)), _resident((1, LANES)),
                     _resident((1, D_NOPE)), _resident((N_HEADS, D_NOPE, KV_RANK))]
        args += [wts["w_uq"], wts["q_nope_g"], wts["q_rope_g"], wts["k_nope_g"], wts["w_uk_t"]]
        out_specs += [pl.BlockSpec((N_HEADS, tm, KV_RANK), lambda i: (0, i, 0)),
                      pl.BlockSpec((N_HEADS, tm, D_ROPE), lambda i: (0, i, 0))]
        out_shape += [jax.ShapeDtypeStruct((N_HEADS, n_tok, KV_RANK), bf16),
                      jax.ShapeDtypeStruct((N_HEADS, n_tok, D_ROPE), bf16)]
    body = functools.partial(_in_proj_body, n_conv=n_conv, nseg=nseg, seg_len=seg_len,
                             tiles_per_seq=tiles_per_seq, carry=carry, fuse_qkv=fuse_qkv)
    return pl.pallas_call(
        body,
        grid=(n_conv + n_tiles,),
        in_specs=in_specs,
        out_specs=out_specs,
        out_shape=out_shape,
        scratch_shapes=scratch,
        compiler_params=_cparams(1),
        name="in_proj",
    )(*args)


def _attn_prompt_body(qt_ref, k_ref, vt_ref, gm_ref, o_ref, m_ref, acc_ref, s_ref, mx_ref, *, tk):
    g = pl.program_id(1)
    tq = 2 * tk
    hi = slice(tk, tq)

    def column_max(s):
        x = jnp.max(s.reshape(s.shape[0] // SUBLANES, SUBLANES, s.shape[1]), axis=0)
        for shift in (4, 2, 1):
            x = jnp.maximum(x, pltpu.roll(x, shift, 0))
        return x

    def keys(h, j):
        rows = pl.ds(pl.multiple_of(j * tk, tk), tk)
        return jnp.concatenate([k_ref[rows, h * D_NOPE:(h + 1) * D_NOPE], k_ref[rows, N_HEADS * D_NOPE:]],
                               axis=1)

    def chunk_visible(n_q):
        k_chunk = lax.broadcasted_iota(jnp.int32, (tk, n_q), 0) // CHUNK
        q_chunk = lax.broadcasted_iota(jnp.int32, (tk, n_q), 1) // CHUNK
        return k_chunk <= q_chunk

    def scores(h, j, buf, first_diag=False):
        s = _dot(keys(h, j), qt_ref[0, h * D_QK:(h + 1) * D_QK, :])
        if first_diag:
            s = jnp.where(chunk_visible(tq), s, NEG_INF)
        s_ref[buf, h] = s
        mx_ref[buf, h] = column_max(s)

    def scores_hi(h, j, buf):
        s = _dot(keys(h, j), qt_ref[0, h * D_QK:(h + 1) * D_QK, hi])
        s = jnp.where(chunk_visible(tk), s, NEG_INF)
        s_ref[buf, h, :, hi] = s
        mx_ref[buf, h, :, hi] = column_max(s)

    def values(h, j, buf, cols=slice(None)):
        m_old = m_ref[h, :, cols]
        m_new = jnp.maximum(m_old, mx_ref[buf, h, :, cols])
        n_q = m_new.shape[1]
        alpha = jnp.exp2(m_old - m_new)
        s = s_ref[buf, h, :, cols]
        p = jnp.exp2(s.reshape(tk // SUBLANES, SUBLANES, n_q) - m_new[None]).reshape(tk, n_q)
        m_ref[h, :, cols] = m_new
        acc = acc_ref[h, :, cols].reshape(D_VX // SUBLANES, SUBLANES, n_q) * alpha[None]
        acc_ref[h, :, cols] = acc.reshape(D_VX, n_q) + _dot(vt_ref[0, j, h], p.astype(bf16))

    def first_values(h, j, buf):
        m_new = mx_ref[buf, h]
        p = jnp.exp2(s_ref[buf, h].reshape(tk // SUBLANES, SUBLANES, tq) - m_new[None]).reshape(tk, tq)
        m_ref[h] = m_new
        acc_ref[h] = _dot(vt_ref[0, j, h], p.astype(bf16))

    def both(score_fn, value_fn):
        for h in range(N_HEADS):
            score_fn(h)
            value_fn(h)

    d0, d1 = 2 * g, 2 * g + 1
    for h in range(N_HEADS):
        scores(h, d0, 0, first_diag=True)
    both(lambda h: scores_hi(h, d1, 1), lambda h: first_values(h, d0, 0))

    @pl.when(g == 0)
    def _():
        for h in range(N_HEADS):
            values(h, d1, 1, hi)

    @pl.when(g > 0)
    def _():
        both(lambda h: scores(h, 0, 0), lambda h: values(h, d1, 1, hi))

        def step_pair(u, carry):
            both(lambda h: scores(h, 2 * u + 1, 1), lambda h: values(h, 2 * u, 0))
            both(lambda h: scores(h, 2 * u + 2, 0), lambda h: values(h, 2 * u + 1, 1))
            return carry

        lax.fori_loop(0, g - 1, step_pair, 0)
        both(lambda h: scores(h, d0 - 1, 1), lambda h: values(h, d0 - 2, 0))
        for h in range(N_HEADS):
            values(h, d0 - 1, 1)

    for h in range(N_HEADS):
        hs = slice(h * D_V, (h + 1) * D_V)
        o = (acc_ref[h, 0:D_V, :] * (1.0 / acc_ref[h, D_V:D_V + 1, :])).T
        o_ref[:, hs] = (o * gm_ref[:, hs].astype(f32)).astype(bf16)


def _attn_prompt(qt, k, vt, gm, *, n_seq, seq_len, tk):
    tq = 2 * tk
    nq = seq_len // tq
    body = functools.partial(_attn_prompt_body, tk=tk)
    return pl.pallas_call(
        body,
        grid=(n_seq, nq),
        in_specs=[pl.BlockSpec((1, N_HEADS * D_QK, tq), lambda b, i: (b, 0, i)),
                  pl.BlockSpec((seq_len, D_K), lambda b, i: (b, 0)),
                  pl.BlockSpec((1, seq_len // tk, N_HEADS, D_VX, tk), lambda b, i: (b, 0, 0, 0, 0)),
                  pl.BlockSpec((tq, D_MLA), lambda b, i: (b * nq + i, 0))],
        out_specs=pl.BlockSpec((tq, D_MLA), lambda b, i: (b * nq + i, 0)),
        out_shape=jax.ShapeDtypeStruct((n_seq * seq_len, D_MLA), bf16),
        scratch_shapes=[pltpu.VMEM((N_HEADS, SUBLANES, tq), f32), pltpu.VMEM((N_HEADS, D_VX, tq), f32),
                        pltpu.VMEM((2, N_HEADS, tk, tq), f32),
                        pltpu.VMEM((2, N_HEADS, SUBLANES, tq), f32)],
        compiler_params=_cparams(2),
        name="attn_prompt",
    )(qt, k, vt, gm)


def _attn_sample_body(cckv_ref, ckr_t_ref, nckv_ref, nkr_ref, qa_ref, qr_ref, gm_ref, wukt_ref, wuv_ref,
                      o_ref, c_ref, ktail_ref, s_ref, olat_ref, *, past, t_new, chunks, n_b):
    n_tail = c_ref.shape[1] - past
    rows = N_HEADS * t_new
    ktail_ref[:, 0:t_new, :] = nkr_ref[...].astype(bf16)
    ktail_ref[:, t_new:, :] = jnp.zeros((n_b, n_tail - t_new, D_ROPE), bf16)

    def chunk_scores(b, start, size):
        cached = start < past
        if start == 0:
            c_ref[b % 2, past:past + t_new, :] = nckv_ref[b].astype(bf16)
            c_ref[b % 2, past + t_new:, :] = jnp.zeros((n_tail - t_new, KV_RANK), bf16)
        qa = qa_ref[:, b * t_new:(b + 1) * t_new, :].reshape(rows, KV_RANK)
        qr = qr_ref[:, b * t_new:(b + 1) * t_new, :].reshape(rows, D_ROPE)
        if cached:
            c32 = cckv_ref[b, start:start + size, :]
            c_ref[b % 2, start:start + size, :] = c32.astype(bf16)
            c_t = c32.T.astype(bf16)
            k_t = _dot(wukt_ref[...], c_t)
            s_nope = _dot(qa, c_t)
            s_rope = _dot(qr, ckr_t_ref[b, :, start:start + size].astype(bf16))
        else:
            c = c_ref[b % 2, start:start + size, :]
            k_t = _dot_nt(wukt_ref[...], c)
            s_nope = _dot_nt(qa, c)
            s_rope = _dot_nt(qr, ktail_ref[b])
        ssq = jnp.sum((k_t * k_t).reshape(N_HEADS, D_NOPE, size), axis=1)
        r = lax.rsqrt(ssq * (1.0 / D_NOPE) + EPS)
        for h in range(N_HEADS):
            hs = slice(h * t_new, (h + 1) * t_new)
            s = (s_nope[hs] * r[h:h + 1, :] + s_rope[hs]) * ATTN_SCALE
            if not cached:
                key = lax.broadcasted_iota(jnp.int32, (t_new, size), 1)
                s = jnp.where(key < t_new, s, NEG_INF)
            s_ref[b % 2, hs, start:start + size] = s

    def chunk_values(b, start, size, state):
        m, l, acc = state
        s = s_ref[b % 2, :, start:start + size]
        m_new = jnp.maximum(m, jnp.max(s, axis=-1, keepdims=True))
        alpha = jnp.exp(m - m_new)
        p = jnp.exp(s - m_new)
        l = alpha * l + jnp.sum(p, axis=-1, keepdims=True)
        acc = alpha * acc + _dot(p.astype(bf16), c_ref[b % 2, start:start + size, :])
        return m_new, l, acc

    def finish(b, state):
        _, l, acc = state
        olat_ref[b] = (acc / l).astype(bf16)

    def value_up_projection():
        for h in range(N_HEADS):
            o_lat = olat_ref[:, h * t_new:(h + 1) * t_new, :].reshape(n_b * t_new, KV_RANK)
            hs = slice(h * D_V, (h + 1) * D_V)
            o_ref[:, hs] = (_dot(o_lat, wuv_ref[h]) * gm_ref[:, hs].astype(f32)).astype(bf16)

    items = [(b,) + ch for b in range(n_b) for ch in chunks]
    init = (jnp.full((rows, 1), NEG_INF, f32), jnp.zeros((rows, 1), f32),
            jnp.zeros((rows, KV_RANK), f32))
    state = init
    chunk_scores(*items[0])
    for nxt, cur in zip(items[1:] + [None], items):
        if nxt is not None:
            chunk_scores(*nxt)
        state = chunk_values(*cur, state)
        if nxt is None or nxt[0] != cur[0]:
            finish(cur[0], state)
            state = init
    value_up_projection()


def _attn_sample(cache_ckv, cache_krope_t, ckv_new, krope_new, q_abs, q_rope, gm, wts, *, t_new, n_b):
    n_seq, past, _ = cache_ckv.shape
    chunk = 512
    chunks = tuple((s, min(chunk, past - s)) for s in range(0, past, chunk)) + ((past, LANES),)
    s_pad = past + LANES
    body = functools.partial(_attn_sample_body, past=past, t_new=t_new, chunks=chunks, n_b=n_b)
    return pl.pallas_call(
        body,
        grid=(n_seq // n_b,),
        in_specs=[pl.BlockSpec((n_b, past, KV_RANK), lambda b: (b, 0, 0)),
                  pl.BlockSpec((n_b, D_ROPE, past), lambda b: (b, 0, 0)),
                  pl.BlockSpec((n_b, t_new, KV_RANK), lambda b: (b, 0, 0)),
                  pl.BlockSpec((n_b, t_new, D_ROPE), lambda b: (b, 0, 0)),
                  pl.BlockSpec((N_HEADS, n_b * t_new, KV_RANK), lambda b: (0, b, 0)),
                  pl.BlockSpec((N_HEADS, n_b * t_new, D_ROPE), lambda b: (0, b, 0)),
                  pl.BlockSpec((n_b * t_new, D_MLA), lambda b: (b, 0)),
                  _resident((N_HEADS * D_NOPE, KV_RANK)),
                  _resident((N_HEADS, KV_RANK, D_V))],
        out_specs=pl.BlockSpec((n_b * t_new, D_MLA), lambda b: (b, 0)),
        out_shape=jax.ShapeDtypeStruct((n_seq * t_new, D_MLA), bf16),
        scratch_shapes=[pltpu.VMEM((2, s_pad, KV_RANK), bf16),
                        pltpu.VMEM((n_b, s_pad - past, D_ROPE), bf16),
                        pltpu.VMEM((2, N_HEADS * t_new, s_pad), f32),
                        pltpu.VMEM((n_b, N_HEADS * t_new, KV_RANK), bf16)],
        compiler_params=_cparams(1),
        name="attn_sample",
    )(cache_ckv, cache_krope_t, ckv_new, krope_new, q_abs, q_rope, gm,
      wts["w_uk_t"].reshape(N_HEADS * D_NOPE, KV_RANK), wts["w_uv3"])


def _out_proj_body(x_ref, pa_ref, ma_ref, p_ref, wo_ref, png_ref, wg_ref, bg_ref, wple_ref, y_ref):
    h = x_ref[...] + _dot(pa_ref[...], wo_ref[0:D_POOL, :]) + _dot(ma_ref[...], wo_ref[D_POOL:, :])
    hn = (h * _rms(h, D_MODEL) * png_ref[...]).astype(bf16)
    gate = jax.nn.sigmoid(_dot(hn, wg_ref[...]) + bg_ref[...])
    y_ref[...] = h + gate * _dot(p_ref[...].astype(bf16), wple_ref[...])


def _out_proj(x2d, pool_act, mla_act, p2d, wts, *, tm):
    n_tok = x2d.shape[0]
    row = lambda i: (i, 0)
    return pl.pallas_call(
        _out_proj_body,
        grid=(n_tok // tm,),
        in_specs=[pl.BlockSpec((tm, D_MODEL), row), pl.BlockSpec((tm, D_POOL), row),
                  pl.BlockSpec((tm, D_MLA), row), pl.BlockSpec((tm, D_PLE), row),
                  _resident((D_MODEL, D_MODEL)), _resident((1, D_MODEL)),
                  _resident((D_MODEL, D_MODEL)), _resident((1, D_MODEL)),
                  _resident((D_PLE, D_MODEL))],
        out_specs=pl.BlockSpec((tm, D_MODEL), row),
        out_shape=jax.ShapeDtypeStruct((n_tok, D_MODEL), f32),
        compiler_params=_cparams(1),
        name="out_proj",
    )(x2d, pool_act, mla_act, p2d, wts["w_out"], wts["ple_norm_g"], wts["w_ple_gate"],
      wts["b_ple_gate"], wts["w_ple"])


def _rope_tables(pos0, t, reps=1):
    pos = (pos0 + np.arange(t)).astype(np.float64)
    inv = ROPE_THETA ** (-(np.arange(0, D_ROPE, 2, dtype=np.float64) / D_ROPE))
    ang = pos[:, None] * inv[None, :]
    cos, sin = np.cos(ang), np.sin(ang)
    zero = np.zeros((t, LANES - D_ROPE))
    rows = (np.concatenate([cos, cos, zero], axis=-1), np.concatenate([-sin, sin, zero], axis=-1),
            pos[:, None])
    as_f32 = lambda a: jnp.asarray(a.astype(np.float32))
    return tuple(as_f32(np.tile(a, (reps, 1))) for a in rows) + ((as_f32(cos.T), as_f32(sin.T)),)


def _prep_weights(norm_g, w_in, q_norm_g, w_uq, kv_norm_g, w_ukv, q_nope_g, q_rope_g, k_nope_g,
                  k_rope_g, w_pool, pool_scale, w_out, ple_norm_g, w_ple_gate, b_ple_gate, w_ple):
    w_uq_r = jnp.pad(w_uq.reshape(Q_RANK, N_HEADS, D_NOPE + D_ROPE),
                     ((0, 0), (0, 0), (0, D_QK - D_NOPE - D_ROPE))).reshape(Q_RANK, N_HEADS * D_QK)
    w_ukv3 = w_ukv.reshape(KV_RANK, N_HEADS, D_NOPE + D_V)
    w_uk3, w_uv3 = w_ukv3[..., :D_NOPE], w_ukv3[..., D_NOPE:]
    pad_rope = lambda g: jnp.pad(g, (0, LANES - D_ROPE))[None, :]
    return {
        "norm_g": norm_g[None, :], "w_in_t_f32": w_in.T, "q_norm_g": q_norm_g[None, :],
        "kv_norm_g": kv_norm_g[None, :], "k_rope_g": pad_rope(k_rope_g),
        "w_pool": w_pool.astype(bf16), "pool_scale": pool_scale[None, :],
        "w_uq": w_uq_r.astype(bf16), "q_nope_g": q_nope_g[None, :], "q_rope_g": pad_rope(q_rope_g),
        "w_uq_t": w_uq.T.astype(bf16),
        "q_nope_g_col": q_nope_g[:, None], "q_rope_g_col": q_rope_g[:, None],
        "k_nope_g": k_nope_g[None, :],
        "w_uk": w_uk3.reshape(KV_RANK, N_HEADS * D_NOPE).astype(bf16),
        "w_uv_t": w_uv3.reshape(KV_RANK, D_MLA).T.astype(bf16),
        "w_uk_t": jnp.transpose(w_uk3, (1, 2, 0)).astype(bf16),
        "w_uv3": jnp.transpose(w_uv3, (1, 0, 2)).astype(bf16),
        "w_out_f32": w_out, "w_ple_gate_f32": w_ple_gate,
        "ple_norm_g": ple_norm_g[None, :], "b_ple_gate": b_ple_gate[None, :],
        "w_ple": w_ple.astype(bf16),
    }


def _layer_prompt(x, p, wts):
    n_seq, seq_len, _ = x.shape
    x2d = x.reshape(n_seq * seq_len, D_MODEL)
    cos2, sin2, pos, rope_t = _rope_tables(0, seq_len)
    hist = jnp.zeros((1, HALO, D_POOL), f32)
    pact, gm, ckv, krope, npool, qt, k, vt, w_out16, w_gate16, w_in_t16 = _in_proj(
        x2d, hist, cos2, sin2, pos, wts, nseg=1, seg_len=PROMPT_TILE, n_seq=n_seq, carry=True,
        rope_t=rope_t, kv_block=ATTN_BLOCK)
    wts.update(w_out=w_out16, w_ple_gate=w_gate16, w_in_t=w_in_t16)
    mact = _attn_prompt(qt, k, vt, gm, n_seq=n_seq, seq_len=seq_len, tk=ATTN_BLOCK)
    y = _out_proj(x2d, pact, mact, p.reshape(n_seq * seq_len, D_PLE), wts, tm=OUT_TILE)
    return (y.reshape(x.shape), ckv.reshape(n_seq, seq_len, KV_RANK),
            krope.reshape(n_seq, seq_len, D_ROPE), npool)


def _layer_sample(x, p, state_pool, cache_ckv, cache_krope, wts):
    n_seq, t_new, _ = x.shape
    past = cache_ckv.shape[1]
    n_tok = n_seq * t_new
    x2d = x.reshape(n_tok, D_MODEL)
    cos2, sin2, pos, _ = _rope_tables(past, t_new, reps=n_seq)
    hist = jnp.transpose(state_pool, (1, 0, 2))
    pact, gm, ckv, krope, npool, q_abs, q_rope = _in_proj(
        x2d, hist, cos2, sin2, pos, wts, nseg=SAMPLE_SEGS_PER_TILE, seg_len=t_new, n_seq=n_seq,
        carry=False)
    ckv3 = ckv.reshape(n_seq, t_new, KV_RANK)
    krope3 = krope.reshape(n_seq, t_new, D_ROPE)
    cache_krope_t = jnp.transpose(cache_krope, (0, 2, 1))
    mact = _attn_sample(cache_ckv, cache_krope_t, ckv3, krope3, q_abs, q_rope, gm, wts, t_new=t_new,
                        n_b=SAMPLE_BATCH_PER_STEP)
    y = _out_proj(x2d, pact, mact, p.reshape(n_tok, D_PLE), wts, tm=SAMPLE_OUT_TILE)
    return y.reshape(x.shape), ckv3, krope3, jnp.transpose(npool, (1, 0, 2))


def kernel(x_prompt, x_sample, cache_ckv, cache_krope, state_pool, p_prompt, p_sample, norm_g, w_in,
           q_norm_g, w_uq, kv_norm_g, w_ukv, q_nope_g, q_rope_g, k_nope_g, k_rope_g, w_pool,
           pool_scale, w_out, ple_norm_g, w_ple_gate, b_ple_gate, w_ple):
    depth = norm_g.shape[0]
    layer_w = (norm_g, w_in, q_norm_g, w_uq, kv_norm_g, w_ukv, q_nope_g, q_rope_g, k_nope_g, k_rope_g,
               w_pool, pool_scale, w_out, ple_norm_g, w_ple_gate, b_ple_gate, w_ple)
    yp, ys = x_prompt, x_sample
    outs = [[] for _ in range(6)]
    for i in range(depth):
        wts = _prep_weights(*(w[i] for w in layer_w))
        yp, c1, k1, s1 = _layer_prompt(yp, p_prompt[i], wts)
        ys, c2, k2, s2 = _layer_sample(ys, p_sample[i], state_pool[i], cache_ckv[i], cache_krope[i], wts)
        for lst, val in zip(outs, (c1, k1, s1, c2, k2, s2)):
            lst.append(val)
    return (yp, ys) + tuple(jnp.stack(o) for o in outs)
```

```python
import functools

import jax
import jax.numpy as jnp
import numpy as np
from jax import lax
from jax.experimental import pallas as pl
from jax.experimental.pallas import tpu as pltpu

D_MODEL = 2048
CHUNK = 64
D_POOL = 1024
POOL_WINDOWS = (2, 4, 8, 16)
POOL_GROUP = 256
POOL_HIST = 15
HALO = 16
N_HEADS = 8
D_NOPE = 128
D_ROPE = 64
D_V = 128
D_VX = D_V + 16
D_MLA = N_HEADS * D_V
Q_RANK = 512
KV_RANK = 256
D_PLE = 256
D_QK = 256
D_K = N_HEADS * D_NOPE + (D_QK - D_NOPE)
ROPE_THETA = 10000.0
EPS = 1e-6
ATTN_SCALE = (D_NOPE + D_ROPE) ** -0.5
NEG_INF = -1e30
EXP2_SCALE = ATTN_SCALE * float(np.log2(np.e))
LANES = 128
SUBLANES = 8

C_U, C_GP, C_Q, C_KV, C_KR, C_GM = 0, 1024, 2048, 2560, 2816, 2880
D_IN = 3904

VMEM_LIMIT = 56 * 1024 * 1024
PROMPT_TILE = 256
W_CHUNK = 512
ATTN_BLOCK = 256
OUT_TILE = 512
SAMPLE_OUT_TILE = 256
SAMPLE_BATCH_PER_STEP = 4
SAMPLE_SEGS_PER_TILE = 16

f32 = jnp.float32
bf16 = jnp.bfloat16


def _cparams(n_axes):
    return pltpu.CompilerParams(dimension_semantics=("arbitrary",) * n_axes,
                                vmem_limit_bytes=VMEM_LIMIT)


def _resident(shape):
    nd = len(shape)
    return pl.BlockSpec(shape, lambda *_: (0,) * nd, pipeline_mode=pl.Buffered(1))


def _rms(x, n):
    return lax.rsqrt(jnp.sum(x * x, axis=-1, keepdims=True) * (1.0 / n) + EPS)


def _rms_cols(x_t, n):
    return lax.rsqrt(jnp.sum(x_t * x_t, axis=0, keepdims=True) * (1.0 / n) + EPS)


def _dot(a, b):
    return jnp.dot(a, b, preferred_element_type=f32)


def _dot_nt(a, b):
    return lax.dot_general(a, b, (((1,), (1,)), ((), ())), preferred_element_type=f32)


def _rope128(x, cos2, sin2):
    lane = lax.broadcasted_iota(jnp.int32, x.shape, 1)
    swapped = jnp.where(lane < D_ROPE // 2, pltpu.roll(x, LANES - D_ROPE // 2, 1),
                        pltpu.roll(x, D_ROPE // 2, 1))
    return x * cos2 + swapped * sin2


def _queries_t(ql, wqt_ref, cos_ref, sin_ref, qng_ref, qrg_ref, qt_ref):
    q_t = _dot_nt(wqt_ref[...], ql)
    tm = q_t.shape[1]
    cos, sin = cos_ref[...], sin_ref[...]
    half = D_ROPE // 2
    for h in range(N_HEADS):
        r0 = h * D_QK
        s0 = h * (D_NOPE + D_ROPE)
        qn = q_t[s0:s0 + D_NOPE]
        qn = qn * (_rms_cols(qn, D_NOPE) * EXP2_SCALE) * qng_ref[...]
        qt_ref[0, r0:r0 + D_NOPE, :] = qn.astype(bf16)
        qr = q_t[s0 + D_NOPE:s0 + D_NOPE + D_ROPE]
        qr = qr * (_rms_cols(qr, D_ROPE) * EXP2_SCALE) * qrg_ref[...]
        x1, x2 = qr[:half], qr[half:]
        qt_ref[0, r0 + D_NOPE:r0 + D_NOPE + half, :] = (x1 * cos - x2 * sin).astype(bf16)
        qt_ref[0, r0 + D_NOPE + half:r0 + D_NOPE + D_ROPE, :] = (x2 * cos + x1 * sin).astype(bf16)
        qt_ref[0, r0 + D_NOPE + D_ROPE:r0 + D_QK, :] = jnp.zeros((D_QK - D_NOPE - D_ROPE, tm), bf16)


def _queries_absorbed(ql, wq_ref, cos_ref, sin_ref, qng_ref, qrg_ref, kng_ref, wukt_ref, qa_ref, qr_ref):
    q = _dot(ql, wq_ref[...])
    cos2, sin2 = cos_ref[...], sin_ref[...]
    for h in range(N_HEADS):
        qn = q[:, h * D_QK:h * D_QK + D_NOPE]
        qn = qn * _rms(qn, D_NOPE) * qng_ref[...]
        qa_ref[h] = _dot((qn * kng_ref[...]).astype(bf16), wukt_ref[h]).astype(bf16)
        qr = q[:, h * D_QK + D_NOPE:(h + 1) * D_QK]
        qr = _rope128(qr * _rms(qr, D_ROPE) * qrg_ref[...], cos2, sin2)
        qr_ref[h] = qr[:, :D_ROPE].astype(bf16)


def _keys_values(c, kr128, wk_ref, wvt_ref, kng_ref, k_ref, vt_ref):
    tm = c.shape[0]
    k = _dot(c, wk_ref[...])
    for h in range(N_HEADS):
        kn = k[:, h * D_NOPE:(h + 1) * D_NOPE]
        k_ref[:, h * D_NOPE:(h + 1) * D_NOPE] = (kn * _rms(kn, D_NOPE) * kng_ref[...]).astype(bf16)
    k_ref[:, N_HEADS * D_NOPE:] = kr128.astype(bf16)
    v_t = _dot_nt(wvt_ref[...], c).astype(bf16)
    n_blk, tk = vt_ref.shape[1], vt_ref.shape[4]
    row = lax.broadcasted_iota(jnp.int32, (D_VX - D_V, tk), 0)
    ones_row = jnp.where(row == 0, 1.0, 0.0).astype(bf16)
    for j in range(n_blk):
        for h in range(N_HEADS):
            vt_ref[0, j, h, 0:D_V, :] = v_t[h * D_V:(h + 1) * D_V, j * tk:(j + 1) * tk]
            vt_ref[0, j, h, D_V:D_VX, :] = ones_row


def _in_proj_body(*refs, n_conv, **static):
    if not static["fuse_qkv"]:
        _in_proj_tile(pl.program_id(0), refs, **static)
        return
    i = pl.program_id(0)
    w32_ref, w16_ref, wt_ref = refs[6], refs[-3], refs[-1]

    def convert(rows):
        chunk = w32_ref[0:rows, :].astype(bf16)
        w16_ref[0:rows, :] = chunk
        wt_ref[pl.ds(pl.multiple_of(i * W_CHUNK, W_CHUNK), rows), :] = chunk

    @pl.when(i == 0)
    def _():
        ucat_ref = refs[-2]
        ucat_ref[...] = jnp.zeros(ucat_ref.shape, f32)

    last_rows = D_IN - (n_conv - 1) * W_CHUNK
    pl.when(i < n_conv - 1)(lambda: convert(W_CHUNK))
    pl.when(i == n_conv - 1)(lambda: convert(last_rows))

    @pl.when(i >= n_conv)
    def _():
        _in_proj_tile(i - n_conv, refs[:6] + (wt_ref,) + refs[7:-3] + refs[-2:-1], **static)


def _in_proj_tile(i, refs, *, nseg, seg_len, tiles_per_seq, carry, fuse_qkv):
    (x_ref, hist_ref, cos_ref, sin_ref, pos_ref, ng_ref, wt_ref, qg_ref, kvg_ref, krg_ref, wp_ref,
     ps_ref) = refs[:12]
    if fuse_qkv:
        (cost_ref, sint_ref, wqt_ref, qng_ref, qrg_ref, wk_ref, wvt_ref, kng_ref, wo32_ref,
         wg32_ref) = refs[12:22]
        (pact_ref, gm_ref, ckv_ref, kr_ref, npool_ref, qt_ref, k_ref, vt_ref, wo16_ref, wg16_ref,
         ucat_ref) = refs[22:]
    else:
        wq_ref, qng_ref, qrg_ref, kng_ref, wukt_ref = refs[12:17]
        pact_ref, gm_ref, ckv_ref, kr_ref, npool_ref, qa_ref, qr_ref, ucat_ref = refs[17:]
    tm = nseg * seg_len

    x = x_ref[...]
    xn = (x * _rms(x, D_MODEL) * ng_ref[...]).astype(bf16)

    u = _dot_nt(xn, wt_ref[C_U:C_U + D_POOL, :])
    if carry:
        halo = ucat_ref[:, 0:HALO, :]
        ucat_ref[:, 0:HALO, :] = jnp.where(i % tiles_per_seq == 0, jnp.zeros_like(halo), halo)
    else:
        ucat_ref[:, 0:HALO - POOL_HIST, :] = jnp.zeros((nseg, HALO - POOL_HIST, D_POOL), f32)
        for t in range(POOL_HIST):
            ucat_ref[:, HALO - POOL_HIST + t, :] = hist_ref[t]
    ucat_ref[:, HALO:HALO + seg_len, :] = u.reshape(nseg, seg_len, D_POOL)

    pos = pos_ref[...]
    gp = _dot_nt(xn, wt_ref[C_GP:C_GP + D_POOL, :])
    gate = gp * jax.nn.sigmoid(gp) * ps_ref[...]

    def pool_group(g):
        w = POOL_WINDOWS[g]
        sl = slice(g * POOL_GROUP, (g + 1) * POOL_GROUP)
        acc = ucat_ref[:, :, sl]
        shift = 1
        while shift < w:
            acc = acc + pltpu.roll(acc, shift, 1)
            shift *= 2
        acc = acc[:, HALO:, :].reshape(tm, POOL_GROUP)
        inv_cnt = 1.0 / jnp.minimum(pos + 1.0, float(w))
        d = (acc * inv_cnt - u[:, sl]).astype(bf16)
        mixed = _dot(d, wp_ref[g])
        pact_ref[:, sl] = (mixed * gate[:, sl]).astype(bf16)

    cq = _dot_nt(xn, wt_ref[C_Q:C_Q + Q_RANK, :])
    pool_group(0)
    pool_group(1)
    ql = (cq * _rms(cq, Q_RANK) * qg_ref[...]).astype(bf16)
    if fuse_qkv:
        _queries_t(ql, wqt_ref, cost_ref, sint_ref, qng_ref, qrg_ref, qt_ref)
    else:
        _queries_absorbed(ql, wq_ref, cos_ref, sin_ref, qng_ref, qrg_ref, kng_ref, wukt_ref, qa_ref, qr_ref)

    ckv = _dot_nt(xn, wt_ref[C_KV:C_KV + KV_RANK, :])
    pool_group(2)
    ckv = ckv * _rms(ckv, KV_RANK) * kvg_ref[...]
    ckv_ref[...] = ckv

    kr = _dot_nt(xn, wt_ref[C_KR:C_KR + LANES, :])
    gm = _dot_nt(xn, wt_ref[C_GM:C_GM + D_MLA, :])
    pool_group(3)
    if carry:
        npool_ref[...] = ucat_ref[:, seg_len + 1:seg_len + HALO, :]
        ucat_ref[:, 0:HALO, :] = ucat_ref[:, seg_len:seg_len + HALO, :]
    else:
        for t in range(POOL_HIST):
            npool_ref[t] = ucat_ref[:, seg_len + 1 + t, :]
    lane = lax.broadcasted_iota(jnp.int32, (tm, LANES), 1)
    kr = jnp.where(lane < D_ROPE, kr, 0.0)
    kr = _rope128(kr * _rms(kr, D_ROPE) * krg_ref[...], cos_ref[...], sin_ref[...])
    kr_ref[...] = kr[:, :D_ROPE]
    gm_ref[...] = (gm * jax.nn.sigmoid(gm)).astype(bf16)
    if fuse_qkv:
        _keys_values(ckv.astype(bf16), kr, wk_ref, wvt_ref, kng_ref, k_ref, vt_ref)
        wo16_ref[...] = wo32_ref[...].astype(bf16)
        wg16_ref[...] = wg32_ref[...].astype(bf16)


def _in_proj(x2d, hist, cos2, sin2, pos, wts, *, nseg, seg_len, n_seq, carry, rope_t=None, kv_block=None):
    n_tok = x2d.shape[0]
    tm = nseg * seg_len
    n_tiles = n_tok // tm
    tiles_per_seq = n_tiles // n_seq if carry else 1
    tab_tiles = cos2.shape[0] // tm
    fuse_qkv = rope_t is not None
    n_conv = pl.cdiv(D_IN, W_CHUNK) if fuse_qkv else 0
    tile = (lambda i: jnp.maximum(i - n_conv, 0)) if n_conv else (lambda i: i)
    row = lambda i: (tile(i), 0)
    tab = lambda i: (tile(i) % tab_tiles, 0)
    if carry:
        hist_spec = pl.BlockSpec((1, HALO, D_POOL), lambda i: (0, 0, 0))
        npool_spec = pl.BlockSpec((1, POOL_HIST, D_POOL), lambda i: (tile(i) // tiles_per_seq, 0, 0))
    else:
        hist_spec = pl.BlockSpec((POOL_HIST, nseg, D_POOL), lambda i: (0, tile(i), 0))
        npool_spec = pl.BlockSpec((POOL_HIST, nseg, D_POOL), lambda i: (0, tile(i), 0))
    x_spec = pl.BlockSpec((tm, D_MODEL), row)
    if fuse_qkv:
        w_chunk = lambda i: (jnp.minimum(i, n_conv - 1), 0)
        w_spec, w_arg = pl.BlockSpec((W_CHUNK, D_MODEL), w_chunk), wts["w_in_t_f32"]
    else:
        w_spec, w_arg = _resident((D_IN, D_MODEL)), wts["w_in_t"]
    in_specs = [
        x_spec, hist_spec, pl.BlockSpec((tm, LANES), tab),
        pl.BlockSpec((tm, LANES), tab), pl.BlockSpec((tm, 1), tab), _resident((1, D_MODEL)),
        w_spec, _resident((1, Q_RANK)), _resident((1, KV_RANK)),
        _resident((1, LANES)), _resident((4, POOL_GROUP, POOL_GROUP)), _resident((1, D_POOL)),
    ]
    args = [x2d, hist, cos2, sin2, pos, wts["norm_g"], w_arg, wts["q_norm_g"],
            wts["kv_norm_g"], wts["k_rope_g"], wts["w_pool"], wts["pool_scale"]]
    out_specs = [pl.BlockSpec((tm, D_POOL), row), pl.BlockSpec((tm, D_MLA), row),
                 pl.BlockSpec((tm, KV_RANK), row), pl.BlockSpec((tm, D_ROPE), row), npool_spec]
    out_shape = [jax.ShapeDtypeStruct((n_tok, D_POOL), bf16), jax.ShapeDtypeStruct((n_tok, D_MLA), bf16),
                 jax.ShapeDtypeStruct((n_tok, KV_RANK), f32), jax.ShapeDtypeStruct((n_tok, D_ROPE), f32),
                 jax.ShapeDtypeStruct((n_seq, POOL_HIST, D_POOL) if carry else (POOL_HIST, n_seq, D_POOL), f32)]
    scratch = [pltpu.VMEM((nseg, HALO + seg_len, D_POOL), f32)]
    if fuse_qkv:
        seq_tab = lambda i: (0, tile(i) % tiles_per_seq)
        seq_blk = lambda i: (tile(i) // tiles_per_seq, 0, tile(i) % tiles_per_seq)
        in_specs += [pl.BlockSpec((D_ROPE // 2, tm), seq_tab), pl.BlockSpec((D_ROPE // 2, tm), seq_tab),
                     _resident((N_HEADS * (D_NOPE + D_ROPE), Q_RANK)), _resident((D_NOPE, 1)),
                     _resident((D_ROPE, 1)),
                     _resident((KV_RANK, N_HEADS * D_NOPE)), _resident((D_MLA, KV_RANK)),
                     _resident((1, D_NOPE)),
                     pl.BlockSpec((D_MODEL // n_tiles, D_MODEL), row),
                     pl.BlockSpec((D_MODEL // n_tiles, D_MODEL), row)]
        args += [rope_t[0], rope_t[1], wts["w_uq_t"], wts["q_nope_g_col"], wts["q_rope_g_col"],
                 wts["w_uk"], wts["w_uv_t"], wts["k_nope_g"], wts["w_out_f32"], wts["w_ple_gate_f32"]]
        out_specs += [
            pl.BlockSpec((1, N_HEADS * D_QK, tm), seq_blk),
            pl.BlockSpec((tm, D_K), row),
            pl.BlockSpec((1, tm // kv_block, N_HEADS, D_VX, kv_block), lambda i: seq_blk(i)[::2] + (0, 0, 0)),
            pl.BlockSpec((D_MODEL // n_tiles, D_MODEL), row),
            pl.BlockSpec((D_MODEL // n_tiles, D_MODEL), row),
            pl.BlockSpec((W_CHUNK, D_MODEL), w_chunk)]
        out_shape += [jax.ShapeDtypeStruct((n_seq, N_HEADS * D_QK, tiles_per_seq * tm), bf16),
                      jax.ShapeDtypeStruct((n_tok, D_K), bf16),
                      jax.ShapeDtypeStruct((n_seq, tiles_per_seq * tm // kv_block, N_HEADS, D_VX, kv_block),
                                           bf16),
                      jax.ShapeDtypeStruct((D_MODEL, D_MODEL), bf16),
                      jax.ShapeDtypeStruct((D_MODEL, D_MODEL), bf16),
                      jax.ShapeDtypeStruct((D_IN, D_MODEL), bf16)]
        scratch += [pltpu.VMEM((n_conv * W_CHUNK, D_MODEL), bf16)]
    else:
        in_specs += [_resident((Q_RANK, N_HEADS * D_QK)), _resident((1, D_NOPE)), _resident((1, LANES)),
                     _resident((1, D_NOPE)), _resident((N_HEADS, D_NOPE, KV_RANK))]
        args += [wts["w_uq"], wts["q_nope_g"], wts["q_rope_g"], wts["k_nope_g"], wts["w_uk_t"]]
        out_specs += [pl.BlockSpec((N_HEADS, tm, KV_RANK), lambda i: (0, i, 0)),
                      pl.BlockSpec((N_HEADS, tm, D_ROPE), lambda i: (0, i, 0))]
        out_shape += [jax.ShapeDtypeStruct((N_HEADS, n_tok, KV_RANK), bf16),
                      jax.ShapeDtypeStruct((N_HEADS, n_tok, D_ROPE), bf16)]
    body = functools.partial(_in_proj_body, n_conv=n_conv, nseg=nseg, seg_len=seg_len,
                             tiles_per_seq=tiles_per_seq, carry=carry, fuse_qkv=fuse_qkv)
    return pl.pallas_call(
        body,
        grid=(n_conv + n_tiles,),
        in_specs=in_specs,
        out_specs=out_specs,
        out_shape=out_shape,
        scratch_shapes=scratch,
        compiler_params=_cparams(1),
        name="in_proj",
    )(*args)


def _attn_prompt_body(qt_ref, k_ref, vt_ref, gm_ref, o_ref, m_ref, acc_ref, s_ref, mx_ref, *, tk):
    g = pl.program_id(1)
    tq = 2 * tk
    hi = slice(tk, tq)

    def column_max(s):
        x = jnp.max(s.reshape(s.shape[0] // SUBLANES, SUBLANES, s.shape[1]), axis=0)
        for shift in (4, 2, 1):
            x = jnp.maximum(x, pltpu.roll(x, shift, 0))
        return x

    def keys(h, j):
        rows = pl.ds(pl.multiple_of(j * tk, tk), tk)
        return jnp.concatenate([k_ref[rows, h * D_NOPE:(h + 1) * D_NOPE], k_ref[rows, N_HEADS * D_NOPE:]],
                               axis=1)

    def chunk_visible(n_q):
        k_chunk = lax.broadcasted_iota(jnp.int32, (tk, n_q), 0) // CHUNK
        q_chunk = lax.broadcasted_iota(jnp.int32, (tk, n_q), 1) // CHUNK
        return k_chunk <= q_chunk

    def scores(h, j, buf, first_diag=False):
        s = _dot(keys(h, j), qt_ref[0, h * D_QK:(h + 1) * D_QK, :])
        if first_diag:
            s = jnp.where(chunk_visible(tq), s, NEG_INF)
        s_ref[buf, h] = s
        mx_ref[buf, h] = column_max(s)

    def scores_hi(h, j, buf):
        s = _dot(keys(h, j), qt_ref[0, h * D_QK:(h + 1) * D_QK, hi])
        s = jnp.where(chunk_visible(tk), s, NEG_INF)
        s_ref[buf, h, :, hi] = s
        mx_ref[buf, h, :, hi] = column_max(s)

    def values(h, j, buf, cols=slice(None)):
        m_old = m_ref[h, :, cols]
        m_new = jnp.maximum(m_old, mx_ref[buf, h, :, cols])
        n_q = m_new.shape[1]
        alpha = jnp.exp2(m_old - m_new)
        s = s_ref[buf, h, :, cols]
        p = jnp.exp2(s.reshape(tk // SUBLANES, SUBLANES, n_q) - m_new[None]).reshape(tk, n_q)
        m_ref[h, :, cols] = m_new
        acc = acc_ref[h, :, cols].reshape(D_VX // SUBLANES, SUBLANES, n_q) * alpha[None]
        acc_ref[h, :, cols] = acc.reshape(D_VX, n_q) + _dot(vt_ref[0, j, h], p.astype(bf16))

    def first_values(h, j, buf):
        m_new = mx_ref[buf, h]
        p = jnp.exp2(s_ref[buf, h].reshape(tk // SUBLANES, SUBLANES, tq) - m_new[None]).reshape(tk, tq)
        m_ref[h] = m_new
        acc_ref[h] = _dot(vt_ref[0, j, h], p.astype(bf16))

    def both(score_fn, value_fn):
        for h in range(N_HEADS):
            score_fn(h)
            value_fn(h)

    d0, d1 = 2 * g, 2 * g + 1
    for h in range(N_HEADS):
        scores(h, d0, 0, first_diag=True)
    both(lambda h: scores_hi(h, d1, 1), lambda h: first_values(h, d0, 0))

    @pl.when(g == 0)
    def _():
        for h in range(N_HEADS):
            values(h, d1, 1, hi)

    @pl.when(g > 0)
    def _():
        both(lambda h: scores(h, 0, 0), lambda h: values(h, d1, 1, hi))

        def step_pair(u, carry):
            both(lambda h: scores(h, 2 * u + 1, 1), lambda h: values(h, 2 * u, 0))
            both(lambda h: scores(h, 2 * u + 2, 0), lambda h: values(h, 2 * u + 1, 1))
            return carry

        lax.fori_loop(0, g - 1, step_pair, 0)
        both(lambda h: scores(h, d0 - 1, 1), lambda h: values(h, d0 - 2, 0))
        for h in range(N_HEADS):
            values(h, d0 - 1, 1)

    for h in range(N_HEADS):
        hs = slice(h * D_V, (h + 1) * D_V)
        o = (acc_ref[h, 0:D_V, :] * (1.0 / acc_ref[h, D_V:D_V + 1, :])).T
        o_ref[:, hs] = (o * gm_ref[:, hs].astype(f32)).astype(bf16)


def _attn_prompt(qt, k, vt, gm, *, n_seq, seq_len, tk):
    tq = 2 * tk
    nq = seq_len // tq
    body = functools.partial(_attn_prompt_body, tk=tk)
    return pl.pallas_call(
        body,
        grid=(n_seq, nq),
        in_specs=[pl.BlockSpec((1, N_HEADS * D_QK, tq), lambda b, i: (b, 0, i)),
                  pl.BlockSpec((seq_len, D_K), lambda b, i: (b, 0)),
                  pl.BlockSpec((1, seq_len // tk, N_HEADS, D_VX, tk), lambda b, i: (b, 0, 0, 0, 0)),
                  pl.BlockSpec((tq, D_MLA), lambda b, i: (b * nq + i, 0))],
        out_specs=pl.BlockSpec((tq, D_MLA), lambda b, i: (b * nq + i, 0)),
        out_shape=jax.ShapeDtypeStruct((n_seq * seq_len, D_MLA), bf16),
        scratch_shapes=[pltpu.VMEM((N_HEADS, SUBLANES, tq), f32), pltpu.VMEM((N_HEADS, D_VX, tq), f32),
                        pltpu.VMEM((2, N_HEADS, tk, tq), f32),
                        pltpu.VMEM((2, N_HEADS, SUBLANES, tq), f32)],
        compiler_params=_cparams(2),
        name="attn_prompt",
    )(qt, k, vt, gm)


def _attn_sample_body(cckv_ref, ckr_t_ref, nckv_ref, nkr_ref, qa_ref, qr_ref, gm_ref, wukt_ref, wuv_ref,
                      o_ref, c_ref, ktail_ref, s_ref, olat_ref, *, past, t_new, chunks, n_b):
    n_tail = c_ref.shape[1] - past
    rows = N_HEADS * t_new
    c_ref[:, past:past + t_new, :] = nckv_ref[...].astype(bf16)
    c_ref[:, past + t_new:, :] = jnp.zeros((n_b, n_tail - t_new, KV_RANK), bf16)
    ktail_ref[:, 0:t_new, :] = nkr_ref[...].astype(bf16)
    ktail_ref[:, t_new:, :] = jnp.zeros((n_b, n_tail - t_new, D_ROPE), bf16)

    def chunk_scores(b, start, size):
        cached = start < past
        qa = qa_ref[:, b * t_new:(b + 1) * t_new, :].reshape(rows, KV_RANK)
        qr = qr_ref[:, b * t_new:(b + 1) * t_new, :].reshape(rows, D_ROPE)
        if cached:
            c32 = cckv_ref[b, start:start + size, :]
            c_ref[b, start:start + size, :] = c32.astype(bf16)
            c_t = c32.T.astype(bf16)
            k_t = _dot(wukt_ref[...], c_t)
            s_nope = _dot(qa, c_t)
            s_rope = _dot(qr, ckr_t_ref[b, :, start:start + size].astype(bf16))
        else:
            c = c_ref[b, start:start + size, :]
            k_t = _dot_nt(wukt_ref[...], c)
            s_nope = _dot_nt(qa, c)
            s_rope = _dot_nt(qr, ktail_ref[b])
        ssq = jnp.sum((k_t * k_t).reshape(N_HEADS, D_NOPE, size), axis=1)
        r = lax.rsqrt(ssq * (1.0 / D_NOPE) + EPS)
        for h in range(N_HEADS):
            hs = slice(h * t_new, (h + 1) * t_new)
            s = (s_nope[hs] * r[h:h + 1, :] + s_rope[hs]) * ATTN_SCALE
            if not cached:
                key = lax.broadcasted_iota(jnp.int32, (t_new, size), 1)
                s = jnp.where(key < t_new, s, NEG_INF)
            s_ref[b, hs, start:start + size] = s

    def chunk_values(b, start, size, state):
        m, l, acc = state
        s = s_ref[b, :, start:start + size]
        m_new = jnp.maximum(m, jnp.max(s, axis=-1, keepdims=True))
        alpha = jnp.exp(m - m_new)
        p = jnp.exp(s - m_new)
        l = alpha * l + jnp.sum(p, axis=-1, keepdims=True)
        acc = alpha * acc + _dot(p.astype(bf16), c_ref[b, start:start + size, :])
        return m_new, l, acc

    def finish(b, state):
        _, l, acc = state
        olat_ref[b] = (acc / l).astype(bf16)

    def value_up_projection():
        for h in range(N_HEADS):
            o_lat = olat_ref[:, h * t_new:(h + 1) * t_new, :].reshape(n_b * t_new, KV_RANK)
            hs = slice(h * D_V, (h + 1) * D_V)
            o_ref[:, hs] = (_dot(o_lat, wuv_ref[h]) * gm_ref[:, hs].astype(f32)).astype(bf16)

    items = [(b,) + ch for b in range(n_b) for ch in chunks]
    init = (jnp.full((rows, 1), NEG_INF, f32), jnp.zeros((rows, 1), f32),
            jnp.zeros((rows, KV_RANK), f32))
    state = init
    chunk_scores(*items[0])
    for nxt, cur in zip(items[1:] + [None], items):
        if nxt is not None:
            chunk_scores(*nxt)
        state = chunk_values(*cur, state)
        if nxt is None or nxt[0] != cur[0]:
            finish(cur[0], state)
            state = init
    value_up_projection()


def _attn_sample(cache_ckv, cache_krope_t, ckv_new, krope_new, q_abs, q_rope, gm, wts, *, t_new, n_b):
    n_seq, past, _ = cache_ckv.shape
    chunk = 512
    chunks = tuple((s, min(chunk, past - s)) for s in range(0, past, chunk)) + ((past, LANES),)
    s_pad = past + LANES
    body = functools.partial(_attn_sample_body, past=past, t_new=t_new, chunks=chunks, n_b=n_b)
    return pl.pallas_call(
        body,
        grid=(n_seq // n_b,),
        in_specs=[pl.BlockSpec((n_b, past, KV_RANK), lambda b: (b, 0, 0)),
                  pl.BlockSpec((n_b, D_ROPE, past), lambda b: (b, 0, 0)),
                  pl.BlockSpec((n_b, t_new, KV_RANK), lambda b: (b, 0, 0)),
                  pl.BlockSpec((n_b, t_new, D_ROPE), lambda b: (b, 0, 0)),
                  pl.BlockSpec((N_HEADS, n_b * t_new, KV_RANK), lambda b: (0, b, 0)),
                  pl.BlockSpec((N_HEADS, n_b * t_new, D_ROPE), lambda b: (0, b, 0)),
                  pl.BlockSpec((n_b * t_new, D_MLA), lambda b: (b, 0)),
                  _resident((N_HEADS * D_NOPE, KV_RANK)),
                  _resident((N_HEADS, KV_RANK, D_V))],
        out_specs=pl.BlockSpec((n_b * t_new, D_MLA), lambda b: (b, 0)),
        out_shape=jax.ShapeDtypeStruct((n_seq * t_new, D_MLA), bf16),
        scratch_shapes=[pltpu.VMEM((n_b, s_pad, KV_RANK), bf16),
                        pltpu.VMEM((n_b, s_pad - past, D_ROPE), bf16),
                        pltpu.VMEM((n_b, N_HEADS * t_new, s_pad), f32),
                        pltpu.VMEM((n_b, N_HEADS * t_new, KV_RANK), bf16)],
        compiler_params=_cparams(1),
        name="attn_sample",
    )(cache_ckv, cache_krope_t, ckv_new, krope_new, q_abs, q_rope, gm,
      wts["w_uk_t"].reshape(N_HEADS * D_NOPE, KV_RANK), wts["w_uv3"])


def _out_proj_body(x_ref, pa_ref, ma_ref, p_ref, wo_ref, png_ref, wg_ref, bg_ref, wple_ref, y_ref):
    h = x_ref[...] + _dot(pa_ref[...], wo_ref[0:D_POOL, :]) + _dot(ma_ref[...], wo_ref[D_POOL:, :])
    hn = (h * _rms(h, D_MODEL) * png_ref[...]).astype(bf16)
    gate = jax.nn.sigmoid(_dot(hn, wg_ref[...]) + bg_ref[...])
    y_ref[...] = h + gate * _dot(p_ref[...].astype(bf16), wple_ref[...])


def _out_proj(x2d, pool_act, mla_act, p2d, wts, *, tm):
    n_tok = x2d.shape[0]
    row = lambda i: (i, 0)
    return pl.pallas_call(
        _out_proj_body,
        grid=(n_tok // tm,),
        in_specs=[pl.BlockSpec((tm, D_MODEL), row), pl.BlockSpec((tm, D_POOL), row),
                  pl.BlockSpec((tm, D_MLA), row), pl.BlockSpec((tm, D_PLE), row),
                  _resident((D_MODEL, D_MODEL)), _resident((1, D_MODEL)),
                  _resident((D_MODEL, D_MODEL)), _resident((1, D_MODEL)),
                  _resident((D_PLE, D_MODEL))],
        out_specs=pl.BlockSpec((tm, D_MODEL), row),
        out_shape=jax.ShapeDtypeStruct((n_tok, D_MODEL), f32),
        compiler_params=_cparams(1),
        name="out_proj",
    )(x2d, pool_act, mla_act, p2d, wts["w_out"], wts["ple_norm_g"], wts["w_ple_gate"],
      wts["b_ple_gate"], wts["w_ple"])


def _rope_tables(pos0, t, reps=1):
    pos = (pos0 + np.arange(t)).astype(np.float64)
    inv = ROPE_THETA ** (-(np.arange(0, D_ROPE, 2, dtype=np.float64) / D_ROPE))
    ang = pos[:, None] * inv[None, :]
    cos, sin = np.cos(ang), np.sin(ang)
    zero = np.zeros((t, LANES - D_ROPE))
    rows = (np.concatenate([cos, cos, zero], axis=-1), np.concatenate([-sin, sin, zero], axis=-1),
            pos[:, None])
    as_f32 = lambda a: jnp.asarray(a.astype(np.float32))
    return tuple(as_f32(np.tile(a, (reps, 1))) for a in rows) + ((as_f32(cos.T), as_f32(sin.T)),)


def _prep_weights(norm_g, w_in, q_norm_g, w_uq, kv_norm_g, w_ukv, q_nope_g, q_rope_g, k_nope_g,
                  k_rope_g, w_pool, pool_scale, w_out, ple_norm_g, w_ple_gate, b_ple_gate, w_ple):
    w_uq_r = jnp.pad(w_uq.reshape(Q_RANK, N_HEADS, D_NOPE + D_ROPE),
                     ((0, 0), (0, 0), (0, D_QK - D_NOPE - D_ROPE))).reshape(Q_RANK, N_HEADS * D_QK)
    w_ukv3 = w_ukv.reshape(KV_RANK, N_HEADS, D_NOPE + D_V)
    w_uk3, w_uv3 = w_ukv3[..., :D_NOPE], w_ukv3[..., D_NOPE:]
    pad_rope = lambda g: jnp.pad(g, (0, LANES - D_ROPE))[None, :]
    return {
        "norm_g": norm_g[None, :], "w_in_t_f32": w_in.T, "q_norm_g": q_norm_g[None, :],
        "kv_norm_g": kv_norm_g[None, :], "k_rope_g": pad_rope(k_rope_g),
        "w_pool": w_pool.astype(bf16), "pool_scale": pool_scale[None, :],
        "w_uq": w_uq_r.astype(bf16), "q_nope_g": q_nope_g[None, :], "q_rope_g": pad_rope(q_rope_g),
        "w_uq_t": w_uq.T.astype(bf16),
        "q_nope_g_col": q_nope_g[:, None], "q_rope_g_col": q_rope_g[:, None],
        "k_nope_g": k_nope_g[None, :],
        "w_uk": w_uk3.reshape(KV_RANK, N_HEADS * D_NOPE).astype(bf16),
        "w_uv_t": w_uv3.reshape(KV_RANK, D_MLA).T.astype(bf16),
        "w_uk_t": jnp.transpose(w_uk3, (1, 2, 0)).astype(bf16),
        "w_uv3": jnp.transpose(w_uv3, (1, 0, 2)).astype(bf16),
        "w_out_f32": w_out, "w_ple_gate_f32": w_ple_gate,
        "ple_norm_g": ple_norm_g[None, :], "b_ple_gate": b_ple_gate[None, :],
        "w_ple": w_ple.astype(bf16),
    }


def _layer_prompt(x, p, wts):
    n_seq, seq_len, _ = x.shape
    x2d = x.reshape(n_seq * seq_len, D_MODEL)
    cos2, sin2, pos, rope_t = _rope_tables(0, seq_len)
    hist = jnp.zeros((1, HALO, D_POOL), f32)
    pact, gm, ckv, krope, npool, qt, k, vt, w_out16, w_gate16, w_in_t16 = _in_proj(
        x2d, hist, cos2, sin2, pos, wts, nseg=1, seg_len=PROMPT_TILE, n_seq=n_seq, carry=True,
        rope_t=rope_t, kv_block=ATTN_BLOCK)
    wts.update(w_out=w_out16, w_ple_gate=w_gate16, w_in_t=w_in_t16)
    mact = _attn_prompt(qt, k, vt, gm, n_seq=n_seq, seq_len=seq_len, tk=ATTN_BLOCK)
    y = _out_proj(x2d, pact, mact, p.reshape(n_seq * seq_len, D_PLE), wts, tm=OUT_TILE)
    return (y.reshape(x.shape), ckv.reshape(n_seq, seq_len, KV_RANK),
            krope.reshape(n_seq, seq_len, D_ROPE), npool)


def _layer_sample(x, p, state_pool, cache_ckv, cache_krope, wts):
    n_seq, t_new, _ = x.shape
    past = cache_ckv.shape[1]
    n_tok = n_seq * t_new
    x2d = x.reshape(n_tok, D_MODEL)
    cos2, sin2, pos, _ = _rope_tables(past, t_new, reps=n_seq)
    hist = jnp.transpose(state_pool, (1, 0, 2))
    pact, gm, ckv, krope, npool, q_abs, q_rope = _in_proj(
        x2d, hist, cos2, sin2, pos, wts, nseg=SAMPLE_SEGS_PER_TILE, seg_len=t_new, n_seq=n_seq,
        carry=False)
    ckv3 = ckv.reshape(n_seq, t_new, KV_RANK)
    krope3 = krope.reshape(n_seq, t_new, D_ROPE)
    cache_krope_t = jnp.transpose(cache_krope, (0, 2, 1))
    mact = _attn_sample(cache_ckv, cache_krope_t, ckv3, krope3, q_abs, q_rope, gm, wts, t_new=t_new,
                        n_b=SAMPLE_BATCH_PER_STEP)
    y = _out_proj(x2d, pact, mact, p.reshape(n_tok, D_PLE), wts, tm=SAMPLE_OUT_TILE)
    return y.reshape(x.shape), ckv3, krope3, jnp.transpose(npool, (1, 0, 2))


def kernel(x_prompt, x_sample, cache_ckv, cache_krope, state_pool, p_prompt, p_sample, norm_g, w_in,
           q_norm_g, w_uq, kv_norm_g, w_ukv, q_nope_g, q_rope_g, k_nope_g, k_rope_g, w_pool,
           pool_scale, w_out, ple_norm_g, w_ple_gate, b_ple_gate, w_ple):
    depth = norm_g.shape[0]
    layer_w = (norm_g, w_in, q_norm_g, w_uq, kv_norm_g, w_ukv, q_nope_g, q_rope_g, k_nope_g, k_rope_g,
               w_pool, pool_scale, w_out, ple_norm_g, w_ple_gate, b_ple_gate, w_ple)
    yp, ys = x_prompt, x_sample
    outs = [[] for _ in range(6)]
    for i in range(depth):
        wts = _prep_weights(*(w[i] for w in layer_w))
        yp, c1, k1, s1 = _layer_prompt(yp, p_prompt[i], wts)
        ys, c2, k2, s2 = _layer_sample(ys, p_sample[i], state_pool[i], cache_ckv[i], cache_krope[i], wts)
        for lst, val in zip(outs, (c1, k1, s1, c2, k2, s2)):
            lst.append(val)
    return (yp, ys) + tuple(jnp.stack(o) for o in outs)
```

```python
import functools

import jax
import jax.numpy as jnp
import numpy as np
from jax import lax
from jax.experimental import pallas as pl
from jax.experimental.pallas import tpu as pltpu

D_MODEL = 2048
CHUNK = 64
D_POOL = 1024
POOL_WINDOWS = (2, 4, 8, 16)
POOL_GROUP = 256
POOL_HIST = 15
HALO = 16
N_HEADS = 8
D_NOPE = 128
D_ROPE = 64
D_V = 128
D_VX = D_V + 16
D_MLA = N_HEADS * D_V
Q_RANK = 512
KV_RANK = 256
D_PLE = 256
D_QK = 256
D_K = N_HEADS * D_NOPE + (D_QK - D_NOPE)
ROPE_THETA = 10000.0
EPS = 1e-6
ATTN_SCALE = (D_NOPE + D_ROPE) ** -0.5
NEG_INF = -1e30
EXP2_SCALE = ATTN_SCALE * float(np.log2(np.e))
LANES = 128
SUBLANES = 8

C_U, C_GP, C_Q, C_KV, C_KR, C_GM = 0, 1024, 2048, 2560, 2816, 2880
D_IN = 3904

VMEM_LIMIT = 56 * 1024 * 1024
PROMPT_TILE = 256
W_CHUNK = 512
ATTN_BLOCK = 256
OUT_TILE = 512
SAMPLE_OUT_TILE = 256
SAMPLE_BATCH_PER_STEP = 4
SAMPLE_SEGS_PER_TILE = 16

f32 = jnp.float32
bf16 = jnp.bfloat16


def _cparams(n_axes):
    return pltpu.CompilerParams(dimension_semantics=("arbitrary",) * n_axes,
                                vmem_limit_bytes=VMEM_LIMIT)


def _resident(shape):
    nd = len(shape)
    return pl.BlockSpec(shape, lambda *_: (0,) * nd, pipeline_mode=pl.Buffered(1))


def _rms(x, n):
    return lax.rsqrt(jnp.sum(x * x, axis=-1, keepdims=True) * (1.0 / n) + EPS)


def _rms_cols(x_t, n):
    return lax.rsqrt(jnp.sum(x_t * x_t, axis=0, keepdims=True) * (1.0 / n) + EPS)


def _dot(a, b):
    return jnp.dot(a, b, preferred_element_type=f32)


def _dot_nt(a, b):
    return lax.dot_general(a, b, (((1,), (1,)), ((), ())), preferred_element_type=f32)


def _rope128(x, cos2, sin2):
    lane = lax.broadcasted_iota(jnp.int32, x.shape, 1)
    swapped = jnp.where(lane < D_ROPE // 2, pltpu.roll(x, LANES - D_ROPE // 2, 1),
                        pltpu.roll(x, D_ROPE // 2, 1))
    return x * cos2 + swapped * sin2


def _queries_t(ql, wqt_ref, cos_ref, sin_ref, qng_ref, qrg_ref, qt_ref):
    q_t = _dot_nt(wqt_ref[...], ql)
    tm = q_t.shape[1]
    cos, sin = cos_ref[...], sin_ref[...]
    half = D_ROPE // 2
    for h in range(N_HEADS):
        r0 = h * D_QK
        s0 = h * (D_NOPE + D_ROPE)
        qn = q_t[s0:s0 + D_NOPE]
        qn = qn * (_rms_cols(qn, D_NOPE) * EXP2_SCALE) * qng_ref[...]
        qt_ref[0, r0:r0 + D_NOPE, :] = qn.astype(bf16)
        qr = q_t[s0 + D_NOPE:s0 + D_NOPE + D_ROPE]
        qr = qr * (_rms_cols(qr, D_ROPE) * EXP2_SCALE) * qrg_ref[...]
        x1, x2 = qr[:half], qr[half:]
        qt_ref[0, r0 + D_NOPE:r0 + D_NOPE + half, :] = (x1 * cos - x2 * sin).astype(bf16)
        qt_ref[0, r0 + D_NOPE + half:r0 + D_NOPE + D_ROPE, :] = (x2 * cos + x1 * sin).astype(bf16)
        qt_ref[0, r0 + D_NOPE + D_ROPE:r0 + D_QK, :] = jnp.zeros((D_QK - D_NOPE - D_ROPE, tm), bf16)


def _queries_absorbed(ql, wq_ref, cos_ref, sin_ref, qng_ref, qrg_ref, kng_ref, wukt_ref, qa_ref, qr_ref):
    q = _dot(ql, wq_ref[...])
    cos2, sin2 = cos_ref[...], sin_ref[...]
    for h in range(N_HEADS):
        qn = q[:, h * D_QK:h * D_QK + D_NOPE]
        qn = qn * _rms(qn, D_NOPE) * qng_ref[...]
        qa_ref[h] = _dot((qn * kng_ref[...]).astype(bf16), wukt_ref[h]).astype(bf16)
        qr = q[:, h * D_QK + D_NOPE:(h + 1) * D_QK]
        qr = _rope128(qr * _rms(qr, D_ROPE) * qrg_ref[...], cos2, sin2)
        qr_ref[h] = qr[:, :D_ROPE].astype(bf16)


def _keys_values(c, kr128, wk_ref, wvt_ref, kng_ref, k_ref, vt_ref):
    tm = c.shape[0]
    k = _dot(c, wk_ref[...])
    for h in range(N_HEADS):
        kn = k[:, h * D_NOPE:(h + 1) * D_NOPE]
        k_ref[:, h * D_NOPE:(h + 1) * D_NOPE] = (kn * _rms(kn, D_NOPE) * kng_ref[...]).astype(bf16)
    k_ref[:, N_HEADS * D_NOPE:] = kr128.astype(bf16)
    v_t = _dot_nt(wvt_ref[...], c).astype(bf16)
    n_blk, tk = vt_ref.shape[1], vt_ref.shape[4]
    row = lax.broadcasted_iota(jnp.int32, (D_VX - D_V, tk), 0)
    ones_row = jnp.where(row == 0, 1.0, 0.0).astype(bf16)
    for j in range(n_blk):
        for h in range(N_HEADS):
            vt_ref[0, j, h, 0:D_V, :] = v_t[h * D_V:(h + 1) * D_V, j * tk:(j + 1) * tk]
            vt_ref[0, j, h, D_V:D_VX, :] = ones_row


def _in_proj_body(*refs, n_conv, **static):
    if not static["fuse_qkv"]:
        _in_proj_tile(pl.program_id(0), refs, **static)
        return
    i = pl.program_id(0)
    w32_ref, w16_ref, wt_ref = refs[6], refs[-3], refs[-1]

    def convert(rows):
        chunk = w32_ref[0:rows, :].astype(bf16)
        w16_ref[0:rows, :] = chunk
        wt_ref[pl.ds(pl.multiple_of(i * W_CHUNK, W_CHUNK), rows), :] = chunk

    @pl.when(i == 0)
    def _():
        ucat_ref = refs[-2]
        ucat_ref[...] = jnp.zeros(ucat_ref.shape, f32)

    last_rows = D_IN - (n_conv - 1) * W_CHUNK
    pl.when(i < n_conv - 1)(lambda: convert(W_CHUNK))
    pl.when(i == n_conv - 1)(lambda: convert(last_rows))

    @pl.when(i >= n_conv)
    def _():
        _in_proj_tile(i - n_conv, refs[:6] + (wt_ref,) + refs[7:-3] + refs[-2:-1], **static)


def _in_proj_tile(i, refs, *, nseg, seg_len, tiles_per_seq, carry, fuse_qkv):
    (x_ref, hist_ref, cos_ref, sin_ref, pos_ref, ng_ref, wt_ref, qg_ref, kvg_ref, krg_ref, wp_ref,
     ps_ref) = refs[:12]
    if fuse_qkv:
        (cost_ref, sint_ref, wqt_ref, qng_ref, qrg_ref, wk_ref, wvt_ref, kng_ref, wo32_ref,
         wg32_ref) = refs[12:22]
        (pact_ref, gm_ref, ckv_ref, kr_ref, npool_ref, qt_ref, k_ref, vt_ref, wo16_ref, wg16_ref,
         ucat_ref) = refs[22:]
    else:
        wq_ref, qng_ref, qrg_ref, kng_ref, wukt_ref = refs[12:17]
        pact_ref, gm_ref, ckv_ref, kr_ref, npool_ref, qa_ref, qr_ref, ucat_ref = refs[17:]
    tm = nseg * seg_len

    x = x_ref[...]
    xn = (x * _rms(x, D_MODEL) * ng_ref[...]).astype(bf16)

    u = _dot_nt(xn, wt_ref[C_U:C_U + D_POOL, :])
    if carry:
        halo = ucat_ref[:, 0:HALO, :]
        ucat_ref[:, 0:HALO, :] = jnp.where(i % tiles_per_seq == 0, jnp.zeros_like(halo), halo)
    else:
        ucat_ref[:, 0:HALO - POOL_HIST, :] = jnp.zeros((nseg, HALO - POOL_HIST, D_POOL), f32)
        for t in range(POOL_HIST):
            ucat_ref[:, HALO - POOL_HIST + t, :] = hist_ref[t]
    ucat_ref[:, HALO:HALO + seg_len, :] = u.reshape(nseg, seg_len, D_POOL)

    pos = pos_ref[...]
    gp = _dot_nt(xn, wt_ref[C_GP:C_GP + D_POOL, :])
    gate = gp * jax.nn.sigmoid(gp) * ps_ref[...]

    def pool_group(g):
        w = POOL_WINDOWS[g]
        sl = slice(g * POOL_GROUP, (g + 1) * POOL_GROUP)
        acc = ucat_ref[:, :, sl]
        shift = 1
        while shift < w:
            acc = acc + pltpu.roll(acc, shift, 1)
            shift *= 2
        acc = acc[:, HALO:, :].reshape(tm, POOL_GROUP)
        inv_cnt = 1.0 / jnp.minimum(pos + 1.0, float(w))
        d = (acc * inv_cnt - u[:, sl]).astype(bf16)
        mixed = _dot(d, wp_ref[g])
        pact_ref[:, sl] = (mixed * gate[:, sl]).astype(bf16)

    cq = _dot_nt(xn, wt_ref[C_Q:C_Q + Q_RANK, :])
    pool_group(0)
    pool_group(1)
    ql = (cq * _rms(cq, Q_RANK) * qg_ref[...]).astype(bf16)
    if fuse_qkv:
        _queries_t(ql, wqt_ref, cost_ref, sint_ref, qng_ref, qrg_ref, qt_ref)
    else:
        _queries_absorbed(ql, wq_ref, cos_ref, sin_ref, qng_ref, qrg_ref, kng_ref, wukt_ref, qa_ref, qr_ref)

    ckv = _dot_nt(xn, wt_ref[C_KV:C_KV + KV_RANK, :])
    pool_group(2)
    ckv = ckv * _rms(ckv, KV_RANK) * kvg_ref[...]
    ckv_ref[...] = ckv

    kr = _dot_nt(xn, wt_ref[C_KR:C_KR + LANES, :])
    gm = _dot_nt(xn, wt_ref[C_GM:C_GM + D_MLA, :])
    pool_group(3)
    if carry:
        npool_ref[...] = ucat_ref[:, seg_len + 1:seg_len + HALO, :]
        ucat_ref[:, 0:HALO, :] = ucat_ref[:, seg_len:seg_len + HALO, :]
    else:
        for t in range(POOL_HIST):
            npool_ref[t] = ucat_ref[:, seg_len + 1 + t, :]
    lane = lax.broadcasted_iota(jnp.int32, (tm, LANES), 1)
    kr = jnp.where(lane < D_ROPE, kr, 0.0)
    kr = _rope128(kr * _rms(kr, D_ROPE) * krg_ref[...], cos_ref[...], sin_ref[...])
    kr_ref[...] = kr[:, :D_ROPE]
    gm_ref[...] = (gm * jax.nn.sigmoid(gm)).astype(bf16)
    if fuse_qkv:
        _keys_values(ckv.astype(bf16), kr, wk_ref, wvt_ref, kng_ref, k_ref, vt_ref)
        wo16_ref[...] = wo32_ref[...].astype(bf16)
        wg16_ref[...] = wg32_ref[...].astype(bf16)


def _in_proj(x2d, hist, cos2, sin2, pos, wts, *, nseg, seg_len, n_seq, carry, rope_t=None, kv_block=None):
    n_tok = x2d.shape[0]
    tm = nseg * seg_len
    n_tiles = n_tok // tm
    tiles_per_seq = n_tiles // n_seq if carry else 1
    tab_tiles = cos2.shape[0] // tm
    fuse_qkv = rope_t is not None
    n_conv = pl.cdiv(D_IN, W_CHUNK) if fuse_qkv else 0
    tile = (lambda i: jnp.maximum(i - n_conv, 0)) if n_conv else (lambda i: i)
    row = lambda i: (tile(i), 0)
    tab = lambda i: (tile(i) % tab_tiles, 0)
    if carry:
        hist_spec = pl.BlockSpec((1, HALO, D_POOL), lambda i: (0, 0, 0))
        npool_spec = pl.BlockSpec((1, POOL_HIST, D_POOL), lambda i: (tile(i) // tiles_per_seq, 0, 0))
    else:
        hist_spec = pl.BlockSpec((POOL_HIST, nseg, D_POOL), lambda i: (0, tile(i), 0))
        npool_spec = pl.BlockSpec((POOL_HIST, nseg, D_POOL), lambda i: (0, tile(i), 0))
    x_spec = pl.BlockSpec((tm, D_MODEL), row)
    if fuse_qkv:
        w_chunk = lambda i: (jnp.minimum(i, n_conv - 1), 0)
        w_spec, w_arg = pl.BlockSpec((W_CHUNK, D_MODEL), w_chunk), wts["w_in_t_f32"]
    else:
        w_spec, w_arg = _resident((D_IN, D_MODEL)), wts["w_in_t"]
    in_specs = [
        x_spec, hist_spec, pl.BlockSpec((tm, LANES), tab),
        pl.BlockSpec((tm, LANES), tab), pl.BlockSpec((tm, 1), tab), _resident((1, D_MODEL)),
        w_spec, _resident((1, Q_RANK)), _resident((1, KV_RANK)),
        _resident((1, LANES)), _resident((4, POOL_GROUP, POOL_GROUP)), _resident((1, D_POOL)),
    ]
    args = [x2d, hist, cos2, sin2, pos, wts["norm_g"], w_arg, wts["q_norm_g"],
            wts["kv_norm_g"], wts["k_rope_g"], wts["w_pool"], wts["pool_scale"]]
    out_specs = [pl.BlockSpec((tm, D_POOL), row), pl.BlockSpec((tm, D_MLA), row),
                 pl.BlockSpec((tm, KV_RANK), row), pl.BlockSpec((tm, D_ROPE), row), npool_spec]
    out_shape = [jax.ShapeDtypeStruct((n_tok, D_POOL), bf16), jax.ShapeDtypeStruct((n_tok, D_MLA), bf16),
                 jax.ShapeDtypeStruct((n_tok, KV_RANK), f32), jax.ShapeDtypeStruct((n_tok, D_ROPE), f32),
                 jax.ShapeDtypeStruct((n_seq, POOL_HIST, D_POOL) if carry else (POOL_HIST, n_seq, D_POOL), f32)]
    scratch = [pltpu.VMEM((nseg, HALO + seg_len, D_POOL), f32)]
    if fuse_qkv:
        seq_tab = lambda i: (0, tile(i) % tiles_per_seq)
        seq_blk = lambda i: (tile(i) // tiles_per_seq, 0, tile(i) % tiles_per_seq)
        in_specs += [pl.BlockSpec((D_ROPE // 2, tm), seq_tab), pl.BlockSpec((D_ROPE // 2, tm), seq_tab),
                     _resident((N_HEADS * (D_NOPE + D_ROPE), Q_RANK)), _resident((D_NOPE, 1)),
                     _resident((D_ROPE, 1)),
                     _resident((KV_RANK, N_HEADS * D_NOPE)), _resident((D_MLA, KV_RANK)),
                     _resident((1, D_NOPE)),
                     pl.BlockSpec((D_MODEL // n_tiles, D_MODEL), row),
                     pl.BlockSpec((D_MODEL // n_tiles, D_MODEL), row)]
        args += [rope_t[0], rope_t[1], wts["w_uq_t"], wts["q_nope_g_col"], wts["q_rope_g_col"],
                 wts["w_uk"], wts["w_uv_t"], wts["k_nope_g"], wts["w_out_f32"], wts["w_ple_gate_f32"]]
        out_specs += [
            pl.BlockSpec((1, N_HEADS * D_QK, tm), seq_blk),
            pl.BlockSpec((tm, D_K), row),
            pl.BlockSpec((1, tm // kv_block, N_HEADS, D_VX, kv_block), lambda i: seq_blk(i)[::2] + (0, 0, 0)),
            pl.BlockSpec((D_MODEL // n_tiles, D_MODEL), row),
            pl.BlockSpec((D_MODEL // n_tiles, D_MODEL), row),
            pl.BlockSpec((W_CHUNK, D_MODEL), w_chunk)]
        out_shape += [jax.ShapeDtypeStruct((n_seq, N_HEADS * D_QK, tiles_per_seq * tm), bf16),
                      jax.ShapeDtypeStruct((n_tok, D_K), bf16),
                      jax.ShapeDtypeStruct((n_seq, tiles_per_seq * tm // kv_block, N_HEADS, D_VX, kv_block),
                                           bf16),
                      jax.ShapeDtypeStruct((D_MODEL, D_MODEL), bf16),
                      jax.ShapeDtypeStruct((D_MODEL, D_MODEL), bf16),
                      jax.ShapeDtypeStruct((D_IN, D_MODEL), bf16)]
        scratch += [pltpu.VMEM((n_conv * W_CHUNK, D_MODEL), bf16)]
    else:
        in_specs += [_resident((Q_RANK, N_HEADS * D_QK)), _resident((1, D_NOPE)), _resident((1, LANES)),
                     _resident((1, D_NOPE)), _resident((N_HEADS, D_NOPE, KV_RANK))]
        args += [wts["w_uq"], wts["q_nope_g"], wts["q_rope_g"], wts["k_nope_g"], wts["w_uk_t"]]
        out_specs += [pl.BlockSpec((N_HEADS, tm, KV_RANK), lambda i: (0, i, 0)),
                      pl.BlockSpec((N_HEADS, tm, D_ROPE), lambda i: (0, i, 0))]
        out_shape += [jax.ShapeDtypeStruct((N_HEADS, n_tok, KV_RANK), bf16),
                      jax.ShapeDtypeStruct((N_HEADS, n_tok, D_ROPE), bf16)]
    body = functools.partial(_in_proj_body, n_conv=n_conv, nseg=nseg, seg_len=seg_len,
                             tiles_per_seq=tiles_per_seq, carry=carry, fuse_qkv=fuse_qkv)
    return pl.pallas_call(
        body,
        grid=(n_conv + n_tiles,),
        in_specs=in_specs,
        out_specs=out_specs,
        out_shape=out_shape,
        scratch_shapes=scratch,
        compiler_params=_cparams(1),
        name="in_proj",
    )(*args)


def _attn_prompt_body(qt_ref, k_ref, vt_ref, gm_ref, o_ref, m_ref, acc_ref, s_ref, mx_ref, *, tk):
    g = pl.program_id(1)
    tq = 2 * tk
    hi = slice(tk, tq)

    def column_max(s):
        x = jnp.max(s.reshape(s.shape[0] // SUBLANES, SUBLANES, s.shape[1]), axis=0)
        for shift in (4, 2, 1):
            x = jnp.maximum(x, pltpu.roll(x, shift, 0))
        return x

    def keys(h, j):
        rows = pl.ds(pl.multiple_of(j * tk, tk), tk)
        return jnp.concatenate([k_ref[rows, h * D_NOPE:(h + 1) * D_NOPE], k_ref[rows, N_HEADS * D_NOPE:]],
                               axis=1)

    def chunk_visible(n_q):
        k_chunk = lax.broadcasted_iota(jnp.int32, (tk, n_q), 0) // CHUNK
        q_chunk = lax.broadcasted_iota(jnp.int32, (tk, n_q), 1) // CHUNK
        return k_chunk <= q_chunk

    def scores(h, j, buf, first_diag=False):
        s = _dot(keys(h, j), qt_ref[0, h * D_QK:(h + 1) * D_QK, :])
        if first_diag:
            s = jnp.where(chunk_visible(tq), s, NEG_INF)
        s_ref[buf, h] = s
        mx_ref[buf, h] = column_max(s)

    def scores_hi(h, j, buf):
        s = _dot(keys(h, j), qt_ref[0, h * D_QK:(h + 1) * D_QK, hi])
        s = jnp.where(chunk_visible(tk), s, NEG_INF)
        s_ref[buf, h, :, hi] = s
        mx_ref[buf, h, :, hi] = column_max(s)

    def values(h, j, buf, cols=slice(None)):
        m_old = m_ref[h, :, cols]
        m_new = jnp.maximum(m_old, mx_ref[buf, h, :, cols])
        n_q = m_new.shape[1]
        alpha = jnp.exp2(m_old - m_new)
        s = s_ref[buf, h, :, cols]
        p = jnp.exp2(s.reshape(tk // SUBLANES, SUBLANES, n_q) - m_new[None]).reshape(tk, n_q)
        m_ref[h, :, cols] = m_new
        acc = acc_ref[h, :, cols].reshape(D_VX // SUBLANES, SUBLANES, n_q) * alpha[None]
        acc_ref[h, :, cols] = acc.reshape(D_VX, n_q) + _dot(vt_ref[0, j, h], p.astype(bf16))

    def first_values(h, j, buf):
        m_new = mx_ref[buf, h]
        p = jnp.exp2(s_ref[buf, h].reshape(tk // SUBLANES, SUBLANES, tq) - m_new[None]).reshape(tk, tq)
        m_ref[h] = m_new
        acc_ref[h] = _dot(vt_ref[0, j, h], p.astype(bf16))

    def both(score_fn, value_fn):
        for h in range(N_HEADS):
            score_fn(h)
            value_fn(h)

    d0, d1 = 2 * g, 2 * g + 1

    def start():
        for h in range(N_HEADS):
            scores(h, d0, 0, first_diag=True)
        both(lambda h: scores_hi(h, d1, 1), lambda h: first_values(h, d0, 0))

    def finish(h):
        hs = slice(h * D_V, (h + 1) * D_V)
        o = (acc_ref[h, 0:D_V, :] * (1.0 / acc_ref[h, D_V:D_V + 1, :])).T
        o_ref[:, hs] = (o * gm_ref[:, hs].astype(f32)).astype(bf16)

    @pl.when(g == 0)
    def _():
        start()
        both(lambda h: values(h, d1, 1, hi), finish)

    @pl.when(g > 0)
    def _():
        start()
        both(lambda h: scores(h, 0, 0), lambda h: values(h, d1, 1, hi))

        def step_pair(u, carry):
            both(lambda h: scores(h, 2 * u + 1, 1), lambda h: values(h, 2 * u, 0))
            both(lambda h: scores(h, 2 * u + 2, 0), lambda h: values(h, 2 * u + 1, 1))
            return carry

        lax.fori_loop(0, g - 1, step_pair, 0)
        both(lambda h: scores(h, d0 - 1, 1), lambda h: values(h, d0 - 2, 0))
        both(lambda h: values(h, d0 - 1, 1), finish)


def _attn_prompt(qt, k, vt, gm, *, n_seq, seq_len, tk):
    tq = 2 * tk
    nq = seq_len // tq
    body = functools.partial(_attn_prompt_body, tk=tk)
    return pl.pallas_call(
        body,
        grid=(n_seq, nq),
        in_specs=[pl.BlockSpec((1, N_HEADS * D_QK, tq), lambda b, i: (b, 0, i)),
                  pl.BlockSpec((seq_len, D_K), lambda b, i: (b, 0)),
                  pl.BlockSpec((1, seq_len // tk, N_HEADS, D_VX, tk), lambda b, i: (b, 0, 0, 0, 0)),
                  pl.BlockSpec((tq, D_MLA), lambda b, i: (b * nq + i, 0))],
        out_specs=pl.BlockSpec((tq, D_MLA), lambda b, i: (b * nq + i, 0)),
        out_shape=jax.ShapeDtypeStruct((n_seq * seq_len, D_MLA), bf16),
        scratch_shapes=[pltpu.VMEM((N_HEADS, SUBLANES, tq), f32), pltpu.VMEM((N_HEADS, D_VX, tq), f32),
                        pltpu.VMEM((2, N_HEADS, tk, tq), f32),
                        pltpu.VMEM((2, N_HEADS, SUBLANES, tq), f32)],
        compiler_params=_cparams(2),
        name="attn_prompt",
    )(qt, k, vt, gm)


def _attn_sample_body(cckv_ref, ckr_t_ref, nckv_ref, nkr_ref, qa_ref, qr_ref, gm_ref, wukt_ref, wuv_ref,
                      o_ref, c_ref, ktail_ref, s_ref, olat_ref, *, past, t_new, chunks, n_b):
    n_tail = c_ref.shape[1] - past
    rows = N_HEADS * t_new
    c_ref[:, past:past + t_new, :] = nckv_ref[...].astype(bf16)
    c_ref[:, past + t_new:, :] = jnp.zeros((n_b, n_tail - t_new, KV_RANK), bf16)
    ktail_ref[:, 0:t_new, :] = nkr_ref[...].astype(bf16)
    ktail_ref[:, t_new:, :] = jnp.zeros((n_b, n_tail - t_new, D_ROPE), bf16)

    def chunk_scores(b, start, size):
        cached = start < past
        qa = qa_ref[:, b * t_new:(b + 1) * t_new, :].reshape(rows, KV_RANK)
        qr = qr_ref[:, b * t_new:(b + 1) * t_new, :].reshape(rows, D_ROPE)
        if cached:
            c32 = cckv_ref[b, start:start + size, :]
            c_ref[b, start:start + size, :] = c32.astype(bf16)
            c_t = c32.T.astype(bf16)
            k_t = _dot(wukt_ref[...], c_t)
            s_nope = _dot(qa, c_t)
            s_rope = _dot(qr, ckr_t_ref[b, :, start:start + size].astype(bf16))
        else:
            c = c_ref[b, start:start + size, :]
            k_t = _dot_nt(wukt_ref[...], c)
            s_nope = _dot_nt(qa, c)
            s_rope = _dot_nt(qr, ktail_ref[b])
        ssq = jnp.sum((k_t * k_t).reshape(N_HEADS, D_NOPE, size), axis=1)
        r = lax.rsqrt(ssq * (1.0 / D_NOPE) + EPS)
        for h in range(N_HEADS):
            hs = slice(h * t_new, (h + 1) * t_new)
            s = (s_nope[hs] * r[h:h + 1, :] + s_rope[hs]) * ATTN_SCALE
            if not cached:
                key = lax.broadcasted_iota(jnp.int32, (t_new, size), 1)
                s = jnp.where(key < t_new, s, NEG_INF)
            s_ref[b, hs, start:start + size] = s

    def chunk_values(b, start, size, state):
        m, l, acc = state
        s = s_ref[b, :, start:start + size]
        m_new = jnp.maximum(m, jnp.max(s, axis=-1, keepdims=True))
        alpha = jnp.exp(m - m_new)
        p = jnp.exp(s - m_new)
        l = alpha * l + jnp.sum(p, axis=-1, keepdims=True)
        acc = alpha * acc + _dot(p.astype(bf16), c_ref[b, start:start + size, :])
        return m_new, l, acc

    def finish(b, state):
        _, l, acc = state
        olat_ref[b] = (acc / l).astype(bf16)

    def value_up_projection():
        for h in range(N_HEADS):
            o_lat = olat_ref[:, h * t_new:(h + 1) * t_new, :].reshape(n_b * t_new, KV_RANK)
            hs = slice(h * D_V, (h + 1) * D_V)
            o_ref[:, hs] = (_dot(o_lat, wuv_ref[h]) * gm_ref[:, hs].astype(f32)).astype(bf16)

    items = [(b,) + ch for b in range(n_b) for ch in chunks]
    init = (jnp.full((rows, 1), NEG_INF, f32), jnp.zeros((rows, 1), f32),
            jnp.zeros((rows, KV_RANK), f32))
    state = init
    chunk_scores(*items[0])
    for nxt, cur in zip(items[1:] + [None], items):
        if nxt is not None:
            chunk_scores(*nxt)
        state = chunk_values(*cur, state)
        if nxt is None or nxt[0] != cur[0]:
            finish(cur[0], state)
            state = init
    value_up_projection()


def _attn_sample(cache_ckv, cache_krope_t, ckv_new, krope_new, q_abs, q_rope, gm, wts, *, t_new, n_b):
    n_seq, past, _ = cache_ckv.shape
    chunk = 512
    chunks = tuple((s, min(chunk, past - s)) for s in range(0, past, chunk)) + ((past, LANES),)
    s_pad = past + LANES
    body = functools.partial(_attn_sample_body, past=past, t_new=t_new, chunks=chunks, n_b=n_b)
    return pl.pallas_call(
        body,
        grid=(n_seq // n_b,),
        in_specs=[pl.BlockSpec((n_b, past, KV_RANK), lambda b: (b, 0, 0)),
                  pl.BlockSpec((n_b, D_ROPE, past), lambda b: (b, 0, 0)),
                  pl.BlockSpec((n_b, t_new, KV_RANK), lambda b: (b, 0, 0)),
                  pl.BlockSpec((n_b, t_new, D_ROPE), lambda b: (b, 0, 0)),
                  pl.BlockSpec((N_HEADS, n_b * t_new, KV_RANK), lambda b: (0, b, 0)),
                  pl.BlockSpec((N_HEADS, n_b * t_new, D_ROPE), lambda b: (0, b, 0)),
                  pl.BlockSpec((n_b * t_new, D_MLA), lambda b: (b, 0)),
                  _resident((N_HEADS * D_NOPE, KV_RANK)),
                  _resident((N_HEADS, KV_RANK, D_V))],
        out_specs=pl.BlockSpec((n_b * t_new, D_MLA), lambda b: (b, 0)),
        out_shape=jax.ShapeDtypeStruct((n_seq * t_new, D_MLA), bf16),
        scratch_shapes=[pltpu.VMEM((n_b, s_pad, KV_RANK), bf16),
                        pltpu.VMEM((n_b, s_pad - past, D_ROPE), bf16),
                        pltpu.VMEM((n_b, N_HEADS * t_new, s_pad), f32),
                        pltpu.VMEM((n_b, N_HEADS * t_new, KV_RANK), bf16)],
        compiler_params=_cparams(1),
        name="attn_sample",
    )(cache_ckv, cache_krope_t, ckv_new, krope_new, q_abs, q_rope, gm,
      wts["w_uk_t"].reshape(N_HEADS * D_NOPE, KV_RANK), wts["w_uv3"])


def _out_proj_body(x_ref, pa_ref, ma_ref, p_ref, wo_ref, png_ref, wg_ref, bg_ref, wple_ref, y_ref):
    h = x_ref[...] + _dot(pa_ref[...], wo_ref[0:D_POOL, :]) + _dot(ma_ref[...], wo_ref[D_POOL:, :])
    hn = (h * _rms(h, D_MODEL) * png_ref[...]).astype(bf16)
    gate = jax.nn.sigmoid(_dot(hn, wg_ref[...]) + bg_ref[...])
    y_ref[...] = h + gate * _dot(p_ref[...].astype(bf16), wple_ref[...])


def _out_proj(x2d, pool_act, mla_act, p2d, wts, *, tm):
    n_tok = x2d.shape[0]
    row = lambda i: (i, 0)
    return pl.pallas_call(
        _out_proj_body,
        grid=(n_tok // tm,),
        in_specs=[pl.BlockSpec((tm, D_MODEL), row), pl.BlockSpec((tm, D_POOL), row),
                  pl.BlockSpec((tm, D_MLA), row), pl.BlockSpec((tm, D_PLE), row),
                  _resident((D_MODEL, D_MODEL)), _resident((1, D_MODEL)),
                  _resident((D_MODEL, D_MODEL)), _resident((1, D_MODEL)),
                  _resident((D_PLE, D_MODEL))],
        out_specs=pl.BlockSpec((tm, D_MODEL), row),
        out_shape=jax.ShapeDtypeStruct((n_tok, D_MODEL), f32),
        compiler_params=_cparams(1),
        name="out_proj",
    )(x2d, pool_act, mla_act, p2d, wts["w_out"], wts["ple_norm_g"], wts["w_ple_gate"],
      wts["b_ple_gate"], wts["w_ple"])


def _rope_tables(pos0, t, reps=1):
    pos = (pos0 + np.arange(t)).astype(np.float64)
    inv = ROPE_THETA ** (-(np.arange(0, D_ROPE, 2, dtype=np.float64) / D_ROPE))
    ang = pos[:, None] * inv[None, :]
    cos, sin = np.cos(ang), np.sin(ang)
    zero = np.zeros((t, LANES - D_ROPE))
    rows = (np.concatenate([cos, cos, zero], axis=-1), np.concatenate([-sin, sin, zero], axis=-1),
            pos[:, None])
    as_f32 = lambda a: jnp.asarray(a.astype(np.float32))
    return tuple(as_f32(np.tile(a, (reps, 1))) for a in rows) + ((as_f32(cos.T), as_f32(sin.T)),)


def _prep_weights(norm_g, w_in, q_norm_g, w_uq, kv_norm_g, w_ukv, q_nope_g, q_rope_g, k_nope_g,
                  k_rope_g, w_pool, pool_scale, w_out, ple_norm_g, w_ple_gate, b_ple_gate, w_ple):
    w_uq_r = jnp.pad(w_uq.reshape(Q_RANK, N_HEADS, D_NOPE + D_ROPE),
                     ((0, 0), (0, 0), (0, D_QK - D_NOPE - D_ROPE))).reshape(Q_RANK, N_HEADS * D_QK)
    w_ukv3 = w_ukv.reshape(KV_RANK, N_HEADS, D_NOPE + D_V)
    w_uk3, w_uv3 = w_ukv3[..., :D_NOPE], w_ukv3[..., D_NOPE:]
    pad_rope = lambda g: jnp.pad(g, (0, LANES - D_ROPE))[None, :]
    return {
        "norm_g": norm_g[None, :], "w_in_t_f32": w_in.T, "q_norm_g": q_norm_g[None, :],
        "kv_norm_g": kv_norm_g[None, :], "k_rope_g": pad_rope(k_rope_g),
        "w_pool": w_pool.astype(bf16), "pool_scale": pool_scale[None, :],
        "w_uq": w_uq_r.astype(bf16), "q_nope_g": q_nope_g[None, :], "q_rope_g": pad_rope(q_rope_g),
        "w_uq_t": w_uq.T.astype(bf16),
        "q_nope_g_col": q_nope_g[:, None], "q_rope_g_col": q_rope_g[:, None],
        "k_nope_g": k_nope_g[None, :],
        "w_uk": w_uk3.reshape(KV_RANK, N_HEADS * D_NOPE).astype(bf16),
        "w_uv_t": w_uv3.reshape(KV_RANK, D_MLA).T.astype(bf16),
        "w_uk_t": jnp.transpose(w_uk3, (1, 2, 0)).astype(bf16),
        "w_uv3": jnp.transpose(w_uv3, (1, 0, 2)).astype(bf16),
        "w_out_f32": w_out, "w_ple_gate_f32": w_ple_gate,
        "ple_norm_g": ple_norm_g[None, :], "b_ple_gate": b_ple_gate[None, :],
        "w_ple": w_ple.astype(bf16),
    }


def _layer_prompt(x, p, wts):
    n_seq, seq_len, _ = x.shape
    x2d = x.reshape(n_seq * seq_len, D_MODEL)
    cos2, sin2, pos, rope_t = _rope_tables(0, seq_len)
    hist = jnp.zeros((1, HALO, D_POOL), f32)
    pact, gm, ckv, krope, npool, qt, k, vt, w_out16, w_gate16, w_in_t16 = _in_proj(
        x2d, hist, cos2, sin2, pos, wts, nseg=1, seg_len=PROMPT_TILE, n_seq=n_seq, carry=True,
        rope_t=rope_t, kv_block=ATTN_BLOCK)
    wts.update(w_out=w_out16, w_ple_gate=w_gate16, w_in_t=w_in_t16)
    mact = _attn_prompt(qt, k, vt, gm, n_seq=n_seq, seq_len=seq_len, tk=ATTN_BLOCK)
    y = _out_proj(x2d, pact, mact, p.reshape(n_seq * seq_len, D_PLE), wts, tm=OUT_TILE)
    return (y.reshape(x.shape), ckv.reshape(n_seq, seq_len, KV_RANK),
            krope.reshape(n_seq, seq_len, D_ROPE), npool)


def _layer_sample(x, p, state_pool, cache_ckv, cache_krope, wts):
    n_seq, t_new, _ = x.shape
    past = cache_ckv.shape[1]
    n_tok = n_seq * t_new
    x2d = x.reshape(n_tok, D_MODEL)
    cos2, sin2, pos, _ = _rope_tables(past, t_new, reps=n_seq)
    hist = jnp.transpose(state_pool, (1, 0, 2))
    pact, gm, ckv, krope, npool, q_abs, q_rope = _in_proj(
        x2d, hist, cos2, sin2, pos, wts, nseg=SAMPLE_SEGS_PER_TILE, seg_len=t_new, n_seq=n_seq,
        carry=False)
    ckv3 = ckv.reshape(n_seq, t_new, KV_RANK)
    krope3 = krope.reshape(n_seq, t_new, D_ROPE)
    cache_krope_t = jnp.transpose(cache_krope, (0, 2, 1))
    mact = _attn_sample(cache_ckv, cache_krope_t, ckv3, krope3, q_abs, q_rope, gm, wts, t_new=t_new,
                        n_b=SAMPLE_BATCH_PER_STEP)
    y = _out_proj(x2d, pact, mact, p.reshape(n_tok, D_PLE), wts, tm=SAMPLE_OUT_TILE)
    return y.reshape(x.shape), ckv3, krope3, jnp.transpose(npool, (1, 0, 2))


def kernel(x_prompt, x_sample, cache_ckv, cache_krope, state_pool, p_prompt, p_sample, norm_g, w_in,
           q_norm_g, w_uq, kv_norm_g, w_ukv, q_nope_g, q_rope_g, k_nope_g, k_rope_g, w_pool,
           pool_scale, w_out, ple_norm_g, w_ple_gate, b_ple_gate, w_ple):
    depth = norm_g.shape[0]
    layer_w = (norm_g, w_in, q_norm_g, w_uq, kv_norm_g, w_ukv, q_nope_g, q_rope_g, k_nope_g, k_rope_g,
               w_pool, pool_scale, w_out, ple_norm_g, w_ple_gate, b_ple_gate, w_ple)
    yp, ys = x_prompt, x_sample
    outs = [[] for _ in range(6)]
    for i in range(depth):
        wts = _prep_weights(*(w[i] for w in layer_w))
        yp, c1, k1, s1 = _layer_prompt(yp, p_prompt[i], wts)
        ys, c2, k2, s2 = _layer_sample(ys, p_sample[i], state_pool[i], cache_ckv[i], cache_krope[i], wts)
        for lst, val in zip(outs, (c1, k1, s1, c2, k2, s2)):
            lst.append(val)
    return (yp, ys) + tuple(jnp.stack(o) for o in outs)
```

```python
import functools

import jax
import jax.numpy as jnp
import numpy as np
from jax import lax
from jax.experimental import pallas as pl
from jax.experimental.pallas import tpu as pltpu

D_MODEL = 2048
CHUNK = 64
D_POOL = 1024
POOL_WINDOWS = (2, 4, 8, 16)
POOL_GROUP = 256
POOL_HIST = 15
HALO = 16
N_HEADS = 8
D_NOPE = 128
D_ROPE = 64
D_V = 128
D_VX = D_V + 16
D_MLA = N_HEADS * D_V
Q_RANK = 512
KV_RANK = 256
D_PLE = 256
D_QK = 256
D_K = N_HEADS * D_NOPE + (D_QK - D_NOPE)
ROPE_THETA = 10000.0
EPS = 1e-6
ATTN_SCALE = (D_NOPE + D_ROPE) ** -0.5
NEG_INF = -1e30
EXP2_SCALE = ATTN_SCALE * float(np.log2(np.e))
LANES = 128
SUBLANES = 8

C_U, C_GP, C_Q, C_KV, C_KR, C_GM = 0, 1024, 2048, 2560, 2816, 2880
D_IN = 3904

VMEM_LIMIT = 56 * 1024 * 1024
PROMPT_TILE = 256
W_CHUNK = 512
ATTN_BLOCK = 256
OUT_TILE = 512
SAMPLE_OUT_TILE = 256
SAMPLE_BATCH_PER_STEP = 4
SAMPLE_SEGS_PER_TILE = 32

f32 = jnp.float32
bf16 = jnp.bfloat16


def _cparams(n_axes):
    return pltpu.CompilerParams(dimension_semantics=("arbitrary",) * n_axes,
                                vmem_limit_bytes=VMEM_LIMIT)


def _resident(shape):
    nd = len(shape)
    return pl.BlockSpec(shape, lambda *_: (0,) * nd, pipeline_mode=pl.Buffered(1))


def _rms(x, n):
    return lax.rsqrt(jnp.sum(x * x, axis=-1, keepdims=True) * (1.0 / n) + EPS)


def _rms_cols(x_t, n):
    return lax.rsqrt(jnp.sum(x_t * x_t, axis=0, keepdims=True) * (1.0 / n) + EPS)


def _dot(a, b):
    return jnp.dot(a, b, preferred_element_type=f32)


def _dot_nt(a, b):
    return lax.dot_general(a, b, (((1,), (1,)), ((), ())), preferred_element_type=f32)


def _rope128(x, cos2, sin2):
    lane = lax.broadcasted_iota(jnp.int32, x.shape, 1)
    swapped = jnp.where(lane < D_ROPE // 2, pltpu.roll(x, LANES - D_ROPE // 2, 1),
                        pltpu.roll(x, D_ROPE // 2, 1))
    return x * cos2 + swapped * sin2


def _queries_t(ql, wqt_ref, cos_ref, sin_ref, qng_ref, qrg_ref, qt_ref):
    q_t = _dot_nt(wqt_ref[...], ql)
    tm = q_t.shape[1]
    cos, sin = cos_ref[...], sin_ref[...]
    half = D_ROPE // 2
    for h in range(N_HEADS):
        r0 = h * D_QK
        s0 = h * (D_NOPE + D_ROPE)
        qn = q_t[s0:s0 + D_NOPE]
        qn = qn * (_rms_cols(qn, D_NOPE) * EXP2_SCALE) * qng_ref[...]
        qt_ref[0, r0:r0 + D_NOPE, :] = qn.astype(bf16)
        qr = q_t[s0 + D_NOPE:s0 + D_NOPE + D_ROPE]
        qr = qr * (_rms_cols(qr, D_ROPE) * EXP2_SCALE) * qrg_ref[...]
        x1, x2 = qr[:half], qr[half:]
        qt_ref[0, r0 + D_NOPE:r0 + D_NOPE + half, :] = (x1 * cos - x2 * sin).astype(bf16)
        qt_ref[0, r0 + D_NOPE + half:r0 + D_NOPE + D_ROPE, :] = (x2 * cos + x1 * sin).astype(bf16)
        qt_ref[0, r0 + D_NOPE + D_ROPE:r0 + D_QK, :] = jnp.zeros((D_QK - D_NOPE - D_ROPE, tm), bf16)


def _queries_absorbed(ql, wq_ref, cos_ref, sin_ref, qng_ref, qrg_ref, kng_ref, wukt_ref, qa_ref, qr_ref):
    q = _dot(ql, wq_ref[...])
    cos2, sin2 = cos_ref[...], sin_ref[...]
    for h in range(N_HEADS):
        qn = q[:, h * D_QK:h * D_QK + D_NOPE]
        qn = qn * _rms(qn, D_NOPE) * qng_ref[...]
        qa_ref[h] = _dot((qn * kng_ref[...]).astype(bf16), wukt_ref[h]).astype(bf16)
        qr = q[:, h * D_QK + D_NOPE:(h + 1) * D_QK]
        qr = _rope128(qr * _rms(qr, D_ROPE) * qrg_ref[...], cos2, sin2)
        qr_ref[h] = qr[:, :D_ROPE].astype(bf16)


def _keys_values(c, kr128, wk_ref, wvt_ref, kng_ref, k_ref, vt_ref):
    tm = c.shape[0]
    k = _dot(c, wk_ref[...])
    for h in range(N_HEADS):
        kn = k[:, h * D_NOPE:(h + 1) * D_NOPE]
        k_ref[:, h * D_NOPE:(h + 1) * D_NOPE] = (kn * _rms(kn, D_NOPE) * kng_ref[...]).astype(bf16)
    k_ref[:, N_HEADS * D_NOPE:] = kr128.astype(bf16)
    v_t = _dot_nt(wvt_ref[...], c).astype(bf16)
    n_blk, tk = vt_ref.shape[1], vt_ref.shape[4]
    row = lax.broadcasted_iota(jnp.int32, (D_VX - D_V, tk), 0)
    ones_row = jnp.where(row == 0, 1.0, 0.0).astype(bf16)
    for j in range(n_blk):
        for h in range(N_HEADS):
            vt_ref[0, j, h, 0:D_V, :] = v_t[h * D_V:(h + 1) * D_V, j * tk:(j + 1) * tk]
            vt_ref[0, j, h, D_V:D_VX, :] = ones_row


def _in_proj_body(*refs, n_conv, **static):
    if not static["fuse_qkv"]:
        _in_proj_tile(pl.program_id(0), refs, **static)
        return
    i = pl.program_id(0)
    w32_ref, w16_ref, wt_ref = refs[6], refs[-3], refs[-1]

    def convert(rows):
        chunk = w32_ref[0:rows, :].astype(bf16)
        w16_ref[0:rows, :] = chunk
        wt_ref[pl.ds(pl.multiple_of(i * W_CHUNK, W_CHUNK), rows), :] = chunk

    @pl.when(i == 0)
    def _():
        ucat_ref = refs[-2]
        ucat_ref[...] = jnp.zeros(ucat_ref.shape, f32)

    last_rows = D_IN - (n_conv - 1) * W_CHUNK
    pl.when(i < n_conv - 1)(lambda: convert(W_CHUNK))
    pl.when(i == n_conv - 1)(lambda: convert(last_rows))

    @pl.when(i >= n_conv)
    def _():
        _in_proj_tile(i - n_conv, refs[:6] + (wt_ref,) + refs[7:-3] + refs[-2:-1], **static)


def _in_proj_tile(i, refs, *, nseg, seg_len, tiles_per_seq, carry, fuse_qkv):
    (x_ref, hist_ref, cos_ref, sin_ref, pos_ref, ng_ref, wt_ref, qg_ref, kvg_ref, krg_ref, wp_ref,
     ps_ref) = refs[:12]
    if fuse_qkv:
        (cost_ref, sint_ref, wqt_ref, qng_ref, qrg_ref, wk_ref, wvt_ref, kng_ref, wo32_ref,
         wg32_ref) = refs[12:22]
        (pact_ref, gm_ref, ckv_ref, kr_ref, npool_ref, qt_ref, k_ref, vt_ref, wo16_ref, wg16_ref,
         ucat_ref) = refs[22:]
    else:
        wq_ref, qng_ref, qrg_ref, kng_ref, wukt_ref = refs[12:17]
        pact_ref, gm_ref, ckv_ref, kr_ref, npool_ref, qa_ref, qr_ref, ucat_ref = refs[17:]
    tm = nseg * seg_len

    x = x_ref[...]
    xn = (x * _rms(x, D_MODEL) * ng_ref[...]).astype(bf16)

    u = _dot_nt(xn, wt_ref[C_U:C_U + D_POOL, :])
    if carry:
        halo = ucat_ref[:, 0:HALO, :]
        ucat_ref[:, 0:HALO, :] = jnp.where(i % tiles_per_seq == 0, jnp.zeros_like(halo), halo)
    else:
        ucat_ref[:, 0:HALO - POOL_HIST, :] = jnp.zeros((nseg, HALO - POOL_HIST, D_POOL), f32)
        for t in range(POOL_HIST):
            ucat_ref[:, HALO - POOL_HIST + t, :] = hist_ref[t]
    ucat_ref[:, HALO:HALO + seg_len, :] = u.reshape(nseg, seg_len, D_POOL)

    pos = pos_ref[...]
    gp = _dot_nt(xn, wt_ref[C_GP:C_GP + D_POOL, :])
    gate = gp * jax.nn.sigmoid(gp) * ps_ref[...]

    def pool_group(g):
        w = POOL_WINDOWS[g]
        sl = slice(g * POOL_GROUP, (g + 1) * POOL_GROUP)
        acc = ucat_ref[:, :, sl]
        shift = 1
        while shift < w:
            acc = acc + pltpu.roll(acc, shift, 1)
            shift *= 2
        acc = acc[:, HALO:, :].reshape(tm, POOL_GROUP)
        inv_cnt = 1.0 / jnp.minimum(pos + 1.0, float(w))
        d = (acc * inv_cnt - u[:, sl]).astype(bf16)
        mixed = _dot(d, wp_ref[g])
        pact_ref[:, sl] = (mixed * gate[:, sl]).astype(bf16)

    cq = _dot_nt(xn, wt_ref[C_Q:C_Q + Q_RANK, :])
    pool_group(0)
    pool_group(1)
    ql = (cq * _rms(cq, Q_RANK) * qg_ref[...]).astype(bf16)
    if fuse_qkv:
        _queries_t(ql, wqt_ref, cost_ref, sint_ref, qng_ref, qrg_ref, qt_ref)
    else:
        _queries_absorbed(ql, wq_ref, cos_ref, sin_ref, qng_ref, qrg_ref, kng_ref, wukt_ref, qa_ref, qr_ref)

    ckv = _dot_nt(xn, wt_ref[C_KV:C_KV + KV_RANK, :])
    pool_group(2)
    ckv = ckv * _rms(ckv, KV_RANK) * kvg_ref[...]
    ckv_ref[...] = ckv

    kr = _dot_nt(xn, wt_ref[C_KR:C_KR + LANES, :])
    gm = _dot_nt(xn, wt_ref[C_GM:C_GM + D_MLA, :])
    pool_group(3)
    if carry:
        npool_ref[...] = ucat_ref[:, seg_len + 1:seg_len + HALO, :]
        ucat_ref[:, 0:HALO, :] = ucat_ref[:, seg_len:seg_len + HALO, :]
    else:
        for t in range(POOL_HIST):
            npool_ref[t] = ucat_ref[:, seg_len + 1 + t, :]
    lane = lax.broadcasted_iota(jnp.int32, (tm, LANES), 1)
    kr = jnp.where(lane < D_ROPE, kr, 0.0)
    kr = _rope128(kr * _rms(kr, D_ROPE) * krg_ref[...], cos_ref[...], sin_ref[...])
    kr_ref[...] = kr[:, :D_ROPE]
    gm_ref[...] = (gm * jax.nn.sigmoid(gm)).astype(bf16)
    if fuse_qkv:
        _keys_values(ckv.astype(bf16), kr, wk_ref, wvt_ref, kng_ref, k_ref, vt_ref)
        wo16_ref[...] = wo32_ref[...].astype(bf16)
        wg16_ref[...] = wg32_ref[...].astype(bf16)


def _in_proj(x2d, hist, cos2, sin2, pos, wts, *, nseg, seg_len, n_seq, carry, rope_t=None, kv_block=None):
    n_tok = x2d.shape[0]
    tm = nseg * seg_len
    n_tiles = n_tok // tm
    tiles_per_seq = n_tiles // n_seq if carry else 1
    tab_tiles = cos2.shape[0] // tm
    fuse_qkv = rope_t is not None
    n_conv = pl.cdiv(D_IN, W_CHUNK) if fuse_qkv else 0
    tile = (lambda i: jnp.maximum(i - n_conv, 0)) if n_conv else (lambda i: i)
    row = lambda i: (tile(i), 0)
    tab = lambda i: (tile(i) % tab_tiles, 0)
    if carry:
        hist_spec = pl.BlockSpec((1, HALO, D_POOL), lambda i: (0, 0, 0))
        npool_spec = pl.BlockSpec((1, POOL_HIST, D_POOL), lambda i: (tile(i) // tiles_per_seq, 0, 0))
    else:
        hist_spec = pl.BlockSpec((POOL_HIST, nseg, D_POOL), lambda i: (0, tile(i), 0))
        npool_spec = pl.BlockSpec((POOL_HIST, nseg, D_POOL), lambda i: (0, tile(i), 0))
    x_spec = pl.BlockSpec((tm, D_MODEL), row)
    if fuse_qkv:
        w_chunk = lambda i: (jnp.minimum(i, n_conv - 1), 0)
        w_spec, w_arg = pl.BlockSpec((W_CHUNK, D_MODEL), w_chunk), wts["w_in_t_f32"]
    else:
        w_spec, w_arg = _resident((D_IN, D_MODEL)), wts["w_in_t"]
    in_specs = [
        x_spec, hist_spec, pl.BlockSpec((tm, LANES), tab),
        pl.BlockSpec((tm, LANES), tab), pl.BlockSpec((tm, 1), tab), _resident((1, D_MODEL)),
        w_spec, _resident((1, Q_RANK)), _resident((1, KV_RANK)),
        _resident((1, LANES)), _resident((4, POOL_GROUP, POOL_GROUP)), _resident((1, D_POOL)),
    ]
    args = [x2d, hist, cos2, sin2, pos, wts["norm_g"], w_arg, wts["q_norm_g"],
            wts["kv_norm_g"], wts["k_rope_g"], wts["w_pool"], wts["pool_scale"]]
    out_specs = [pl.BlockSpec((tm, D_POOL), row), pl.BlockSpec((tm, D_MLA), row),
                 pl.BlockSpec((tm, KV_RANK), row), pl.BlockSpec((tm, D_ROPE), row), npool_spec]
    out_shape = [jax.ShapeDtypeStruct((n_tok, D_POOL), bf16), jax.ShapeDtypeStruct((n_tok, D_MLA), bf16),
                 jax.ShapeDtypeStruct((n_tok, KV_RANK), f32), jax.ShapeDtypeStruct((n_tok, D_ROPE), f32),
                 jax.ShapeDtypeStruct((n_seq, POOL_HIST, D_POOL) if carry else (POOL_HIST, n_seq, D_POOL), f32)]
    scratch = [pltpu.VMEM((nseg, HALO + seg_len, D_POOL), f32)]
    if fuse_qkv:
        seq_tab = lambda i: (0, tile(i) % tiles_per_seq)
        seq_blk = lambda i: (tile(i) // tiles_per_seq, 0, tile(i) % tiles_per_seq)
        in_specs += [pl.BlockSpec((D_ROPE // 2, tm), seq_tab), pl.BlockSpec((D_ROPE // 2, tm), seq_tab),
                     _resident((N_HEADS * (D_NOPE + D_ROPE), Q_RANK)), _resident((D_NOPE, 1)),
                     _resident((D_ROPE, 1)),
                     _resident((KV_RANK, N_HEADS * D_NOPE)), _resident((D_MLA, KV_RANK)),
                     _resident((1, D_NOPE)),
                     pl.BlockSpec((D_MODEL // n_tiles, D_MODEL), row),
                     pl.BlockSpec((D_MODEL // n_tiles, D_MODEL), row)]
        args += [rope_t[0], rope_t[1], wts["w_uq_t"], wts["q_nope_g_col"], wts["q_rope_g_col"],
                 wts["w_uk"], wts["w_uv_t"], wts["k_nope_g"], wts["w_out_f32"], wts["w_ple_gate_f32"]]
        out_specs += [
            pl.BlockSpec((1, N_HEADS * D_QK, tm), seq_blk),
            pl.BlockSpec((tm, D_K), row),
            pl.BlockSpec((1, tm // kv_block, N_HEADS, D_VX, kv_block), lambda i: seq_blk(i)[::2] + (0, 0, 0)),
            pl.BlockSpec((D_MODEL // n_tiles, D_MODEL), row),
            pl.BlockSpec((D_MODEL // n_tiles, D_MODEL), row),
            pl.BlockSpec((W_CHUNK, D_MODEL), w_chunk)]
        out_shape += [jax.ShapeDtypeStruct((n_seq, N_HEADS * D_QK, tiles_per_seq * tm), bf16),
                      jax.ShapeDtypeStruct((n_tok, D_K), bf16),
                      jax.ShapeDtypeStruct((n_seq, tiles_per_seq * tm // kv_block, N_HEADS, D_VX, kv_block),
                                           bf16),
                      jax.ShapeDtypeStruct((D_MODEL, D_MODEL), bf16),
                      jax.ShapeDtypeStruct((D_MODEL, D_MODEL), bf16),
                      jax.ShapeDtypeStruct((D_IN, D_MODEL), bf16)]
        scratch += [pltpu.VMEM((n_conv * W_CHUNK, D_MODEL), bf16)]
    else:
        in_specs += [_resident((Q_RANK, N_HEADS * D_QK)), _resident((1, D_NOPE)), _resident((1, LANES)),
                     _resident((1, D_NOPE)), _resident((N_HEADS, D_NOPE, KV_RANK))]
        args += [wts["w_uq"], wts["q_nope_g"], wts["q_rope_g"], wts["k_nope_g"], wts["w_uk_t"]]
        out_specs += [pl.BlockSpec((N_HEADS, tm, KV_RANK), lambda i: (0, i, 0)),
                      pl.BlockSpec((N_HEADS, tm, D_ROPE), lambda i: (0, i, 0))]
        out_shape += [jax.ShapeDtypeStruct((N_HEADS, n_tok, KV_RANK), bf16),
                      jax.ShapeDtypeStruct((N_HEADS, n_tok, D_ROPE), bf16)]
    body = functools.partial(_in_proj_body, n_conv=n_conv, nseg=nseg, seg_len=seg_len,
                             tiles_per_seq=tiles_per_seq, carry=carry, fuse_qkv=fuse_qkv)
    return pl.pallas_call(
        body,
        grid=(n_conv + n_tiles,),
        in_specs=in_specs,
        out_specs=out_specs,
        out_shape=out_shape,
        scratch_shapes=scratch,
        compiler_params=_cparams(1),
        name="in_proj",
    )(*args)


def _attn_prompt_body(qt_ref, k_ref, vt_ref, gm_ref, o_ref, m_ref, acc_ref, s_ref, mx_ref, *, tk):
    g = pl.program_id(1)
    tq = 2 * tk
    hi = slice(tk, tq)

    def column_max(s):
        x = jnp.max(s.reshape(s.shape[0] // SUBLANES, SUBLANES, s.shape[1]), axis=0)
        for shift in (4, 2, 1):
            x = jnp.maximum(x, pltpu.roll(x, shift, 0))
        return x

    def keys(h, j):
        rows = pl.ds(pl.multiple_of(j * tk, tk), tk)
        return jnp.concatenate([k_ref[rows, h * D_NOPE:(h + 1) * D_NOPE], k_ref[rows, N_HEADS * D_NOPE:]],
                               axis=1)

    def chunk_visible(n_q):
        k_chunk = lax.broadcasted_iota(jnp.int32, (tk, n_q), 0) // CHUNK
        q_chunk = lax.broadcasted_iota(jnp.int32, (tk, n_q), 1) // CHUNK
        return k_chunk <= q_chunk

    def scores(h, j, buf, first_diag=False):
        s = _dot(keys(h, j), qt_ref[0, h * D_QK:(h + 1) * D_QK, :])
        if first_diag:
            s = jnp.where(chunk_visible(tq), s, NEG_INF)
        s_ref[buf, h] = s
        mx_ref[buf, h] = column_max(s)

    def scores_hi(h, j, buf):
        s = _dot(keys(h, j), qt_ref[0, h * D_QK:(h + 1) * D_QK, hi])
        s = jnp.where(chunk_visible(tk), s, NEG_INF)
        s_ref[buf, h, :, hi] = s
        mx_ref[buf, h, :, hi] = column_max(s)

    def values(h, j, buf, cols=slice(None)):
        m_old = m_ref[h, :, cols]
        m_new = jnp.maximum(m_old, mx_ref[buf, h, :, cols])
        n_q = m_new.shape[1]
        alpha = jnp.exp2(m_old - m_new)
        s = s_ref[buf, h, :, cols]
        p = jnp.exp2(s.reshape(tk // SUBLANES, SUBLANES, n_q) - m_new[None]).reshape(tk, n_q)
        m_ref[h, :, cols] = m_new
        acc = acc_ref[h, :, cols].reshape(D_VX // SUBLANES, SUBLANES, n_q) * alpha[None]
        acc_ref[h, :, cols] = acc.reshape(D_VX, n_q) + _dot(vt_ref[0, j, h], p.astype(bf16))

    def first_values(h, j, buf):
        m_new = mx_ref[buf, h]
        p = jnp.exp2(s_ref[buf, h].reshape(tk // SUBLANES, SUBLANES, tq) - m_new[None]).reshape(tk, tq)
        m_ref[h] = m_new
        acc_ref[h] = _dot(vt_ref[0, j, h], p.astype(bf16))

    def both(score_fn, value_fn):
        for h in range(N_HEADS):
            score_fn(h)
            value_fn(h)

    d0, d1 = 2 * g, 2 * g + 1

    def start():
        for h in range(N_HEADS):
            scores(h, d0, 0, first_diag=True)
        both(lambda h: scores_hi(h, d1, 1), lambda h: first_values(h, d0, 0))

    def finish(h):
        hs = slice(h * D_V, (h + 1) * D_V)
        o = (acc_ref[h, 0:D_V, :] * (1.0 / acc_ref[h, D_V:D_V + 1, :])).T
        o_ref[:, hs] = (o * gm_ref[:, hs].astype(f32)).astype(bf16)

    @pl.when(g == 0)
    def _():
        start()
        both(lambda h: values(h, d1, 1, hi), finish)

    @pl.when(g > 0)
    def _():
        start()
        both(lambda h: scores(h, 0, 0), lambda h: values(h, d1, 1, hi))

        def step_pair(u, carry):
            both(lambda h: scores(h, 2 * u + 1, 1), lambda h: values(h, 2 * u, 0))
            both(lambda h: scores(h, 2 * u + 2, 0), lambda h: values(h, 2 * u + 1, 1))
            return carry

        lax.fori_loop(0, g - 1, step_pair, 0)
        both(lambda h: scores(h, d0 - 1, 1), lambda h: values(h, d0 - 2, 0))
        both(lambda h: values(h, d0 - 1, 1), finish)


def _attn_prompt(qt, k, vt, gm, *, n_seq, seq_len, tk):
    tq = 2 * tk
    nq = seq_len // tq
    body = functools.partial(_attn_prompt_body, tk=tk)
    return pl.pallas_call(
        body,
        grid=(n_seq, nq),
        in_specs=[pl.BlockSpec((1, N_HEADS * D_QK, tq), lambda b, i: (b, 0, i)),
                  pl.BlockSpec((seq_len, D_K), lambda b, i: (b, 0)),
                  pl.BlockSpec((1, seq_len // tk, N_HEADS, D_VX, tk), lambda b, i: (b, 0, 0, 0, 0)),
                  pl.BlockSpec((tq, D_MLA), lambda b, i: (b * nq + i, 0))],
        out_specs=pl.BlockSpec((tq, D_MLA), lambda b, i: (b * nq + i, 0)),
        out_shape=jax.ShapeDtypeStruct((n_seq * seq_len, D_MLA), bf16),
        scratch_shapes=[pltpu.VMEM((N_HEADS, SUBLANES, tq), f32), pltpu.VMEM((N_HEADS, D_VX, tq), f32),
                        pltpu.VMEM((2, N_HEADS, tk, tq), f32),
                        pltpu.VMEM((2, N_HEADS, SUBLANES, tq), f32)],
        compiler_params=_cparams(2),
        name="attn_prompt",
    )(qt, k, vt, gm)


def _attn_sample_body(cckv_ref, ckr_t_ref, nckv_ref, nkr_ref, qa_ref, qr_ref, gm_ref, wukt_ref, wuv_ref,
                      o_ref, c_ref, ktail_ref, s_ref, olat_ref, *, past, t_new, chunks, n_b):
    n_tail = c_ref.shape[1] - past
    rows = N_HEADS * t_new
    c_ref[:, past:past + t_new, :] = nckv_ref[...].astype(bf16)
    c_ref[:, past + t_new:, :] = jnp.zeros((n_b, n_tail - t_new, KV_RANK), bf16)
    ktail_ref[:, 0:t_new, :] = nkr_ref[...].astype(bf16)
    ktail_ref[:, t_new:, :] = jnp.zeros((n_b, n_tail - t_new, D_ROPE), bf16)

    def chunk_scores(b, start, size):
        cached = start < past
        qa = qa_ref[:, b * t_new:(b + 1) * t_new, :].reshape(rows, KV_RANK)
        qr = qr_ref[:, b * t_new:(b + 1) * t_new, :].reshape(rows, D_ROPE)
        if cached:
            c32 = cckv_ref[b, start:start + size, :]
            c_ref[b, start:start + size, :] = c32.astype(bf16)
            c_t = c32.T.astype(bf16)
            k_t = _dot(wukt_ref[...], c_t)
            s_nope = _dot(qa, c_t)
            s_rope = _dot(qr, ckr_t_ref[b, :, start:start + size].astype(bf16))
        else:
            c = c_ref[b, start:start + size, :]
            k_t = _dot_nt(wukt_ref[...], c)
            s_nope = _dot_nt(qa, c)
            s_rope = _dot_nt(qr, ktail_ref[b])
        ssq = jnp.sum((k_t * k_t).reshape(N_HEADS, D_NOPE, size), axis=1)
        r = lax.rsqrt(ssq * (1.0 / D_NOPE) + EPS)
        for h in range(N_HEADS):
            hs = slice(h * t_new, (h + 1) * t_new)
            s = (s_nope[hs] * r[h:h + 1, :] + s_rope[hs]) * ATTN_SCALE
            if not cached:
                key = lax.broadcasted_iota(jnp.int32, (t_new, size), 1)
                s = jnp.where(key < t_new, s, NEG_INF)
            s_ref[b, hs, start:start + size] = s

    def chunk_values(b, start, size, state):
        m, l, acc = state
        s = s_ref[b, :, start:start + size]
        m_new = jnp.maximum(m, jnp.max(s, axis=-1, keepdims=True))
        alpha = jnp.exp(m - m_new)
        p = jnp.exp(s - m_new)
        l = alpha * l + jnp.sum(p, axis=-1, keepdims=True)
        acc = alpha * acc + _dot(p.astype(bf16), c_ref[b, start:start + size, :])
        return m_new, l, acc

    def finish(b, state):
        _, l, acc = state
        olat_ref[b] = (acc / l).astype(bf16)

    def value_up_projection():
        for h in range(N_HEADS):
            o_lat = olat_ref[:, h * t_new:(h + 1) * t_new, :].reshape(n_b * t_new, KV_RANK)
            hs = slice(h * D_V, (h + 1) * D_V)
            o_ref[:, hs] = (_dot(o_lat, wuv_ref[h]) * gm_ref[:, hs].astype(f32)).astype(bf16)

    items = [(b,) + ch for b in range(n_b) for ch in chunks]
    init = (jnp.full((rows, 1), NEG_INF, f32), jnp.zeros((rows, 1), f32),
            jnp.zeros((rows, KV_RANK), f32))
    state = init
    chunk_scores(*items[0])
    for nxt, cur in zip(items[1:] + [None], items):
        if nxt is not None:
            chunk_scores(*nxt)
        state = chunk_values(*cur, state)
        if nxt is None or nxt[0] != cur[0]:
            finish(cur[0], state)
            state = init
    value_up_projection()


def _attn_sample(cache_ckv, cache_krope_t, ckv_new, krope_new, q_abs, q_rope, gm, wts, *, t_new, n_b):
    n_seq, past, _ = cache_ckv.shape
    chunk = 512
    chunks = tuple((s, min(chunk, past - s)) for s in range(0, past, chunk)) + ((past, LANES),)
    s_pad = past + LANES
    body = functools.partial(_attn_sample_body, past=past, t_new=t_new, chunks=chunks, n_b=n_b)
    return pl.pallas_call(
        body,
        grid=(n_seq // n_b,),
        in_specs=[pl.BlockSpec((n_b, past, KV_RANK), lambda b: (b, 0, 0)),
                  pl.BlockSpec((n_b, D_ROPE, past), lambda b: (b, 0, 0)),
                  pl.BlockSpec((n_b, t_new, KV_RANK), lambda b: (b, 0, 0)),
                  pl.BlockSpec((n_b, t_new, D_ROPE), lambda b: (b, 0, 0)),
                  pl.BlockSpec((N_HEADS, n_b * t_new, KV_RANK), lambda b: (0, b, 0)),
                  pl.BlockSpec((N_HEADS, n_b * t_new, D_ROPE), lambda b: (0, b, 0)),
                  pl.BlockSpec((n_b * t_new, D_MLA), lambda b: (b, 0)),
                  _resident((N_HEADS * D_NOPE, KV_RANK)),
                  _resident((N_HEADS, KV_RANK, D_V))],
        out_specs=pl.BlockSpec((n_b * t_new, D_MLA), lambda b: (b, 0)),
        out_shape=jax.ShapeDtypeStruct((n_seq * t_new, D_MLA), bf16),
        scratch_shapes=[pltpu.VMEM((n_b, s_pad, KV_RANK), bf16),
                        pltpu.VMEM((n_b, s_pad - past, D_ROPE), bf16),
                        pltpu.VMEM((n_b, N_HEADS * t_new, s_pad), f32),
                        pltpu.VMEM((n_b, N_HEADS * t_new, KV_RANK), bf16)],
        compiler_params=_cparams(1),
        name="attn_sample",
    )(cache_ckv, cache_krope_t, ckv_new, krope_new, q_abs, q_rope, gm,
      wts["w_uk_t"].reshape(N_HEADS * D_NOPE, KV_RANK), wts["w_uv3"])


def _out_proj_body(x_ref, pa_ref, ma_ref, p_ref, wo_ref, png_ref, wg_ref, bg_ref, wple_ref, y_ref):
    h = x_ref[...] + _dot(pa_ref[...], wo_ref[0:D_POOL, :]) + _dot(ma_ref[...], wo_ref[D_POOL:, :])
    hn = (h * _rms(h, D_MODEL) * png_ref[...]).astype(bf16)
    gate = jax.nn.sigmoid(_dot(hn, wg_ref[...]) + bg_ref[...])
    y_ref[...] = h + gate * _dot(p_ref[...].astype(bf16), wple_ref[...])


def _out_proj(x2d, pool_act, mla_act, p2d, wts, *, tm):
    n_tok = x2d.shape[0]
    row = lambda i: (i, 0)
    return pl.pallas_call(
        _out_proj_body,
        grid=(n_tok // tm,),
        in_specs=[pl.BlockSpec((tm, D_MODEL), row), pl.BlockSpec((tm, D_POOL), row),
                  pl.BlockSpec((tm, D_MLA), row), pl.BlockSpec((tm, D_PLE), row),
                  _resident((D_MODEL, D_MODEL)), _resident((1, D_MODEL)),
                  _resident((D_MODEL, D_MODEL)), _resident((1, D_MODEL)),
                  _resident((D_PLE, D_MODEL))],
        out_specs=pl.BlockSpec((tm, D_MODEL), row),
        out_shape=jax.ShapeDtypeStruct((n_tok, D_MODEL), f32),
        compiler_params=_cparams(1),
        name="out_proj",
    )(x2d, pool_act, mla_act, p2d, wts["w_out"], wts["ple_norm_g"], wts["w_ple_gate"],
      wts["b_ple_gate"], wts["w_ple"])


def _rope_tables(pos0, t, reps=1):
    pos = (pos0 + np.arange(t)).astype(np.float64)
    inv = ROPE_THETA ** (-(np.arange(0, D_ROPE, 2, dtype=np.float64) / D_ROPE))
    ang = pos[:, None] * inv[None, :]
    cos, sin = np.cos(ang), np.sin(ang)
    zero = np.zeros((t, LANES - D_ROPE))
    rows = (np.concatenate([cos, cos, zero], axis=-1), np.concatenate([-sin, sin, zero], axis=-1),
            pos[:, None])
    as_f32 = lambda a: jnp.asarray(a.astype(np.float32))
    return tuple(as_f32(np.tile(a, (reps, 1))) for a in rows) + ((as_f32(cos.T), as_f32(sin.T)),)


def _prep_weights(norm_g, w_in, q_norm_g, w_uq, kv_norm_g, w_ukv, q_nope_g, q_rope_g, k_nope_g,
                  k_rope_g, w_pool, pool_scale, w_out, ple_norm_g, w_ple_gate, b_ple_gate, w_ple):
    w_uq_r = jnp.pad(w_uq.reshape(Q_RANK, N_HEADS, D_NOPE + D_ROPE),
                     ((0, 0), (0, 0), (0, D_QK - D_NOPE - D_ROPE))).reshape(Q_RANK, N_HEADS * D_QK)
    w_ukv3 = w_ukv.reshape(KV_RANK, N_HEADS, D_NOPE + D_V)
    w_uk3, w_uv3 = w_ukv3[..., :D_NOPE], w_ukv3[..., D_NOPE:]
    pad_rope = lambda g: jnp.pad(g, (0, LANES - D_ROPE))[None, :]
    return {
        "norm_g": norm_g[None, :], "w_in_t_f32": w_in.T, "q_norm_g": q_norm_g[None, :],
        "kv_norm_g": kv_norm_g[None, :], "k_rope_g": pad_rope(k_rope_g),
        "w_pool": w_pool.astype(bf16), "pool_scale": pool_scale[None, :],
        "w_uq": w_uq_r.astype(bf16), "q_nope_g": q_nope_g[None, :], "q_rope_g": pad_rope(q_rope_g),
        "w_uq_t": w_uq.T.astype(bf16),
        "q_nope_g_col": q_nope_g[:, None], "q_rope_g_col": q_rope_g[:, None],
        "k_nope_g": k_nope_g[None, :],
        "w_uk": w_uk3.reshape(KV_RANK, N_HEADS * D_NOPE).astype(bf16),
        "w_uv_t": w_uv3.reshape(KV_RANK, D_MLA).T.astype(bf16),
        "w_uk_t": jnp.transpose(w_uk3, (1, 2, 0)).astype(bf16),
        "w_uv3": jnp.transpose(w_uv3, (1, 0, 2)).astype(bf16),
        "w_out_f32": w_out, "w_ple_gate_f32": w_ple_gate,
        "ple_norm_g": ple_norm_g[None, :], "b_ple_gate": b_ple_gate[None, :],
        "w_ple": w_ple.astype(bf16),
    }


def _layer_prompt(x, p, wts):
    n_seq, seq_len, _ = x.shape
    x2d = x.reshape(n_seq * seq_len, D_MODEL)
    cos2, sin2, pos, rope_t = _rope_tables(0, seq_len)
    hist = jnp.zeros((1, HALO, D_POOL), f32)
    pact, gm, ckv, krope, npool, qt, k, vt, w_out16, w_gate16, w_in_t16 = _in_proj(
        x2d, hist, cos2, sin2, pos, wts, nseg=1, seg_len=PROMPT_TILE, n_seq=n_seq, carry=True,
        rope_t=rope_t, kv_block=ATTN_BLOCK)
    wts.update(w_out=w_out16, w_ple_gate=w_gate16, w_in_t=w_in_t16)
    mact = _attn_prompt(qt, k, vt, gm, n_seq=n_seq, seq_len=seq_len, tk=ATTN_BLOCK)
    y = _out_proj(x2d, pact, mact, p.reshape(n_seq * seq_len, D_PLE), wts, tm=OUT_TILE)
    return (y.reshape(x.shape), ckv.reshape(n_seq, seq_len, KV_RANK),
            krope.reshape(n_seq, seq_len, D_ROPE), npool)


def _layer_sample(x, p, state_pool, cache_ckv, cache_krope, wts):
    n_seq, t_new, _ = x.shape
    past = cache_ckv.shape[1]
    n_tok = n_seq * t_new
    x2d = x.reshape(n_tok, D_MODEL)
    cos2, sin2, pos, _ = _rope_tables(past, t_new, reps=n_seq)
    hist = jnp.transpose(state_pool, (1, 0, 2))
    pact, gm, ckv, krope, npool, q_abs, q_rope = _in_proj(
        x2d, hist, cos2, sin2, pos, wts, nseg=SAMPLE_SEGS_PER_TILE, seg_len=t_new, n_seq=n_seq,
        carry=False)
    ckv3 = ckv.reshape(n_seq, t_new, KV_RANK)
    krope3 = krope.reshape(n_seq, t_new, D_ROPE)
    cache_krope_t = jnp.transpose(cache_krope, (0, 2, 1))
    mact = _attn_sample(cache_ckv, cache_krope_t, ckv3, krope3, q_abs, q_rope, gm, wts, t_new=t_new,
                        n_b=SAMPLE_BATCH_PER_STEP)
    y = _out_proj(x2d, pact, mact, p.reshape(n_tok, D_PLE), wts, tm=SAMPLE_OUT_TILE)
    return y.reshape(x.shape), ckv3, krope3, jnp.transpose(npool, (1, 0, 2))


def kernel(x_prompt, x_sample, cache_ckv, cache_krope, state_pool, p_prompt, p_sample, norm_g, w_in,
           q_norm_g, w_uq, kv_norm_g, w_ukv, q_nope_g, q_rope_g, k_nope_g, k_rope_g, w_pool,
           pool_scale, w_out, ple_norm_g, w_ple_gate, b_ple_gate, w_ple):
    depth = norm_g.shape[0]
    layer_w = (norm_g, w_in, q_norm_g, w_uq, kv_norm_g, w_ukv, q_nope_g, q_rope_g, k_nope_g, k_rope_g,
               w_pool, pool_scale, w_out, ple_norm_g, w_ple_gate, b_ple_gate, w_ple)
    yp, ys = x_prompt, x_sample
    outs = [[] for _ in range(6)]
    for i in range(depth):
        wts = _prep_weights(*(w[i] for w in layer_w))
        yp, c1, k1, s1 = _layer_prompt(yp, p_prompt[i], wts)
        ys, c2, k2, s2 = _layer_sample(ys, p_sample[i], state_pool[i], cache_ckv[i], cache_krope[i], wts)
        for lst, val in zip(outs, (c1, k1, s1, c2, k2, s2)):
            lst.append(val)
    return (yp, ys) + tuple(jnp.stack(o) for o in outs)
```

```python
import functools

import jax
import jax.numpy as jnp
import numpy as np
from jax import lax
from jax.experimental import pallas as pl
from jax.experimental.pallas import tpu as pltpu

D_MODEL = 2048
CHUNK = 64
D_POOL = 1024
POOL_WINDOWS = (2, 4, 8, 16)
POOL_GROUP = 256
POOL_HIST = 15
HALO = 16
N_HEADS = 8
D_NOPE = 128
D_ROPE = 64
D_V = 128
D_VX = D_V + 16
D_MLA = N_HEADS * D_V
Q_RANK = 512
KV_RANK = 256
D_PLE = 256
D_QK = 256
D_K = N_HEADS * D_NOPE + (D_QK - D_NOPE)
ROPE_THETA = 10000.0
EPS = 1e-6
ATTN_SCALE = (D_NOPE + D_ROPE) ** -0.5
NEG_INF = -1e30
EXP2_SCALE = ATTN_SCALE * float(np.log2(np.e))
LANES = 128
SUBLANES = 8

C_U, C_GP, C_Q, C_KV, C_KR, C_GM = 0, 1024, 2048, 2560, 2816, 2880
D_IN = 3904

VMEM_LIMIT = 56 * 1024 * 1024
PROMPT_TILE = 256
W_CHUNK = 512
ATTN_BLOCK = 256
OUT_TILE = 512
SAMPLE_OUT_TILE = 256
SAMPLE_BATCH_PER_STEP = 4
SAMPLE_SEGS_PER_TILE = 32

f32 = jnp.float32
bf16 = jnp.bfloat16


def _cparams(n_axes):
    return pltpu.CompilerParams(dimension_semantics=("arbitrary",) * n_axes,
                                vmem_limit_bytes=VMEM_LIMIT)


def _resident(shape):
    nd = len(shape)
    return pl.BlockSpec(shape, lambda *_: (0,) * nd, pipeline_mode=pl.Buffered(1))


def _rms(x, n):
    return lax.rsqrt(jnp.sum(x * x, axis=-1, keepdims=True) * (1.0 / n) + EPS)


def _rms_cols(x_t, n):
    return lax.rsqrt(jnp.sum(x_t * x_t, axis=0, keepdims=True) * (1.0 / n) + EPS)


def _dot(a, b):
    return jnp.dot(a, b, preferred_element_type=f32)


def _dot_nt(a, b):
    return lax.dot_general(a, b, (((1,), (1,)), ((), ())), preferred_element_type=f32)


def _rope128(x, cos2, sin2):
    lane = lax.broadcasted_iota(jnp.int32, x.shape, 1)
    swapped = jnp.where(lane < D_ROPE // 2, pltpu.roll(x, LANES - D_ROPE // 2, 1),
                        pltpu.roll(x, D_ROPE // 2, 1))
    return x * cos2 + swapped * sin2


def _queries_t(ql, wqt_ref, cos_ref, sin_ref, qng_ref, qrg_ref, qt_ref):
    q_t = _dot_nt(wqt_ref[...], ql)
    tm = q_t.shape[1]
    cos, sin = cos_ref[...], sin_ref[...]
    half = D_ROPE // 2
    for h in range(N_HEADS):
        r0 = h * D_QK
        s0 = h * (D_NOPE + D_ROPE)
        qn = q_t[s0:s0 + D_NOPE]
        qn = qn * (_rms_cols(qn, D_NOPE) * EXP2_SCALE) * qng_ref[...]
        qt_ref[0, r0:r0 + D_NOPE, :] = qn.astype(bf16)
        qr = q_t[s0 + D_NOPE:s0 + D_NOPE + D_ROPE]
        qr = qr * (_rms_cols(qr, D_ROPE) * EXP2_SCALE) * qrg_ref[...]
        x1, x2 = qr[:half], qr[half:]
        qt_ref[0, r0 + D_NOPE:r0 + D_NOPE + half, :] = (x1 * cos - x2 * sin).astype(bf16)
        qt_ref[0, r0 + D_NOPE + half:r0 + D_NOPE + D_ROPE, :] = (x2 * cos + x1 * sin).astype(bf16)
        qt_ref[0, r0 + D_NOPE + D_ROPE:r0 + D_QK, :] = jnp.zeros((D_QK - D_NOPE - D_ROPE, tm), bf16)


def _queries_absorbed(ql, wq_ref, cos_ref, sin_ref, qng_ref, qrg_ref, kng_ref, wukt_ref, qa_ref, qr_ref):
    q = _dot(ql, wq_ref[...])
    cos2, sin2 = cos_ref[...], sin_ref[...]
    for h in range(N_HEADS):
        qn = q[:, h * D_QK:h * D_QK + D_NOPE]
        qn = qn * _rms(qn, D_NOPE) * qng_ref[...]
        qa_ref[h] = _dot((qn * kng_ref[...]).astype(bf16), wukt_ref[h]).astype(bf16)
        qr = q[:, h * D_QK + D_NOPE:(h + 1) * D_QK]
        qr = _rope128(qr * _rms(qr, D_ROPE) * qrg_ref[...], cos2, sin2)
        qr_ref[h] = qr[:, :D_ROPE].astype(bf16)


def _keys_values(c, kr128, wk_ref, wvt_ref, kng_ref, k_ref, vt_ref):
    tm = c.shape[0]
    k = _dot(c, wk_ref[...])
    for h in range(N_HEADS):
        kn = k[:, h * D_NOPE:(h + 1) * D_NOPE]
        k_ref[:, h * D_NOPE:(h + 1) * D_NOPE] = (kn * _rms(kn, D_NOPE) * kng_ref[...]).astype(bf16)
    k_ref[:, N_HEADS * D_NOPE:] = kr128.astype(bf16)
    v_t = _dot_nt(wvt_ref[...], c).astype(bf16)
    n_blk, tk = vt_ref.shape[1], vt_ref.shape[4]
    row = lax.broadcasted_iota(jnp.int32, (D_VX - D_V, tk), 0)
    ones_row = jnp.where(row == 0, 1.0, 0.0).astype(bf16)
    for j in range(n_blk):
        for h in range(N_HEADS):
            vt_ref[0, j, h, 0:D_V, :] = v_t[h * D_V:(h + 1) * D_V, j * tk:(j + 1) * tk]
            vt_ref[0, j, h, D_V:D_VX, :] = ones_row


def _in_proj_body(*refs, n_conv, **static):
    if not static["fuse_qkv"]:
        _in_proj_tile(pl.program_id(0), refs, **static)
        return
    i = pl.program_id(0)
    w32_ref, w16_ref, wt_ref = refs[6], refs[-3], refs[-1]

    def convert(rows):
        chunk = w32_ref[0:rows, :].astype(bf16)
        w16_ref[0:rows, :] = chunk
        wt_ref[pl.ds(pl.multiple_of(i * W_CHUNK, W_CHUNK), rows), :] = chunk

    @pl.when(i == 0)
    def _():
        ucat_ref = refs[-2]
        ucat_ref[...] = jnp.zeros(ucat_ref.shape, f32)

    last_rows = D_IN - (n_conv - 1) * W_CHUNK
    pl.when(i < n_conv - 1)(lambda: convert(W_CHUNK))
    pl.when(i == n_conv - 1)(lambda: convert(last_rows))

    @pl.when(i >= n_conv)
    def _():
        _in_proj_tile(i - n_conv, refs[:6] + (wt_ref,) + refs[7:-3] + refs[-2:-1], **static)


def _in_proj_tile(i, refs, *, nseg, seg_len, tiles_per_seq, carry, fuse_qkv):
    (x_ref, hist_ref, cos_ref, sin_ref, pos_ref, ng_ref, wt_ref, qg_ref, kvg_ref, krg_ref, wp_ref,
     ps_ref) = refs[:12]
    if fuse_qkv:
        (cost_ref, sint_ref, wqt_ref, qng_ref, qrg_ref, wk_ref, wvt_ref, kng_ref, wo32_ref,
         wg32_ref) = refs[12:22]
        (pact_ref, gm_ref, ckv_ref, kr_ref, npool_ref, qt_ref, k_ref, vt_ref, wo16_ref, wg16_ref,
         ucat_ref) = refs[22:]
    else:
        wq_ref, qng_ref, qrg_ref, kng_ref, wukt_ref = refs[12:17]
        pact_ref, gm_ref, ckv_ref, kr_ref, npool_ref, qa_ref, qr_ref, ucat_ref = refs[17:]
    tm = nseg * seg_len

    x = x_ref[...]
    xn = (x * _rms(x, D_MODEL) * ng_ref[...]).astype(bf16)

    u = _dot_nt(xn, wt_ref[C_U:C_U + D_POOL, :])
    if carry:
        halo = ucat_ref[:, 0:HALO, :]
        ucat_ref[:, 0:HALO, :] = jnp.where(i % tiles_per_seq == 0, jnp.zeros_like(halo), halo)
    else:
        ucat_ref[:, 0:HALO - POOL_HIST, :] = jnp.zeros((nseg, HALO - POOL_HIST, D_POOL), f32)
        for t in range(POOL_HIST):
            ucat_ref[:, HALO - POOL_HIST + t, :] = hist_ref[t]
    ucat_ref[:, HALO:HALO + seg_len, :] = u.reshape(nseg, seg_len, D_POOL)

    pos = pos_ref[...]
    gp = _dot_nt(xn, wt_ref[C_GP:C_GP + D_POOL, :])
    gate = gp * jax.nn.sigmoid(gp) * ps_ref[...]

    def pool_group(g):
        w = POOL_WINDOWS[g]
        sl = slice(g * POOL_GROUP, (g + 1) * POOL_GROUP)
        acc = ucat_ref[:, :, sl]
        shift = 1
        while shift < w:
            acc = acc + pltpu.roll(acc, shift, 1)
            shift *= 2
        acc = acc[:, HALO:, :].reshape(tm, POOL_GROUP)
        inv_cnt = 1.0 / jnp.minimum(pos + 1.0, float(w))
        d = (acc * inv_cnt - u[:, sl]).astype(bf16)
        mixed = _dot(d, wp_ref[g])
        pact_ref[:, sl] = (mixed * gate[:, sl]).astype(bf16)

    cq = _dot_nt(xn, wt_ref[C_Q:C_Q + Q_RANK, :])
    pool_group(0)
    pool_group(1)
    ql = (cq * _rms(cq, Q_RANK) * qg_ref[...]).astype(bf16)
    if fuse_qkv:
        _queries_t(ql, wqt_ref, cost_ref, sint_ref, qng_ref, qrg_ref, qt_ref)
    else:
        _queries_absorbed(ql, wq_ref, cos_ref, sin_ref, qng_ref, qrg_ref, kng_ref, wukt_ref, qa_ref, qr_ref)

    ckv = _dot_nt(xn, wt_ref[C_KV:C_KV + KV_RANK, :])
    pool_group(2)
    ckv = ckv * _rms(ckv, KV_RANK) * kvg_ref[...]
    ckv_ref[...] = ckv

    kr = _dot_nt(xn, wt_ref[C_KR:C_KR + LANES, :])
    gm = _dot_nt(xn, wt_ref[C_GM:C_GM + D_MLA, :])
    pool_group(3)
    if carry:
        npool_ref[...] = ucat_ref[:, seg_len + 1:seg_len + HALO, :]
        ucat_ref[:, 0:HALO, :] = ucat_ref[:, seg_len:seg_len + HALO, :]
    else:
        for t in range(POOL_HIST):
            npool_ref[t] = ucat_ref[:, seg_len + 1 + t, :]
    lane = lax.broadcasted_iota(jnp.int32, (tm, LANES), 1)
    kr = jnp.where(lane < D_ROPE, kr, 0.0)
    kr = _rope128(kr * _rms(kr, D_ROPE) * krg_ref[...], cos_ref[...], sin_ref[...])
    kr_ref[...] = kr[:, :D_ROPE]
    gm_ref[...] = (gm * jax.nn.sigmoid(gm)).astype(bf16)
    if fuse_qkv:
        _keys_values(ckv.astype(bf16), kr, wk_ref, wvt_ref, kng_ref, k_ref, vt_ref)
        wo16_ref[...] = wo32_ref[...].astype(bf16)
        wg16_ref[...] = wg32_ref[...].astype(bf16)


def _in_proj(x2d, hist, cos2, sin2, pos, wts, *, nseg, seg_len, n_seq, carry, rope_t=None, kv_block=None):
    n_tok = x2d.shape[0]
    tm = nseg * seg_len
    n_tiles = n_tok // tm
    tiles_per_seq = n_tiles // n_seq if carry else 1
    tab_tiles = cos2.shape[0] // tm
    fuse_qkv = rope_t is not None
    n_conv = pl.cdiv(D_IN, W_CHUNK) if fuse_qkv else 0
    tile = (lambda i: jnp.maximum(i - n_conv, 0)) if n_conv else (lambda i: i)
    row = lambda i: (tile(i), 0)
    tab = lambda i: (tile(i) % tab_tiles, 0)
    if carry:
        hist_spec = pl.BlockSpec((1, HALO, D_POOL), lambda i: (0, 0, 0))
        npool_spec = pl.BlockSpec((1, POOL_HIST, D_POOL), lambda i: (tile(i) // tiles_per_seq, 0, 0))
    else:
        hist_spec = pl.BlockSpec((POOL_HIST, nseg, D_POOL), lambda i: (0, tile(i), 0))
        npool_spec = pl.BlockSpec((POOL_HIST, nseg, D_POOL), lambda i: (0, tile(i), 0))
    x_spec = pl.BlockSpec((tm, D_MODEL), row)
    if fuse_qkv:
        w_chunk = lambda i: (jnp.minimum(i, n_conv - 1), 0)
        w_spec, w_arg = pl.BlockSpec((W_CHUNK, D_MODEL), w_chunk), wts["w_in_t_f32"]
    else:
        w_spec, w_arg = _resident((D_IN, D_MODEL)), wts["w_in_t"]
    in_specs = [
        x_spec, hist_spec, pl.BlockSpec((tm, LANES), tab),
        pl.BlockSpec((tm, LANES), tab), pl.BlockSpec((tm, 1), tab), _resident((1, D_MODEL)),
        w_spec, _resident((1, Q_RANK)), _resident((1, KV_RANK)),
        _resident((1, LANES)), _resident((4, POOL_GROUP, POOL_GROUP)), _resident((1, D_POOL)),
    ]
    args = [x2d, hist, cos2, sin2, pos, wts["norm_g"], w_arg, wts["q_norm_g"],
            wts["kv_norm_g"], wts["k_rope_g"], wts["w_pool"], wts["pool_scale"]]
    out_specs = [pl.BlockSpec((tm, D_POOL), row), pl.BlockSpec((tm, D_MLA), row),
                 pl.BlockSpec((tm, KV_RANK), row), pl.BlockSpec((tm, D_ROPE), row), npool_spec]
    out_shape = [jax.ShapeDtypeStruct((n_tok, D_POOL), bf16), jax.ShapeDtypeStruct((n_tok, D_MLA), bf16),
                 jax.ShapeDtypeStruct((n_tok, KV_RANK), f32), jax.ShapeDtypeStruct((n_tok, D_ROPE), f32),
                 jax.ShapeDtypeStruct((n_seq, POOL_HIST, D_POOL) if carry else (POOL_HIST, n_seq, D_POOL), f32)]
    scratch = [pltpu.VMEM((nseg, HALO + seg_len, D_POOL), f32)]
    if fuse_qkv:
        seq_tab = lambda i: (0, tile(i) % tiles_per_seq)
        seq_blk = lambda i: (tile(i) // tiles_per_seq, 0, tile(i) % tiles_per_seq)
        in_specs += [pl.BlockSpec((D_ROPE // 2, tm), seq_tab), pl.BlockSpec((D_ROPE // 2, tm), seq_tab),
                     _resident((N_HEADS * (D_NOPE + D_ROPE), Q_RANK)), _resident((D_NOPE, 1)),
                     _resident((D_ROPE, 1)),
                     _resident((KV_RANK, N_HEADS * D_NOPE)), _resident((D_MLA, KV_RANK)),
                     _resident((1, D_NOPE)),
                     pl.BlockSpec((D_MODEL // n_tiles, D_MODEL), row),
                     pl.BlockSpec((D_MODEL // n_tiles, D_MODEL), row)]
        args += [rope_t[0], rope_t[1], wts["w_uq_t"], wts["q_nope_g_col"], wts["q_rope_g_col"],
                 wts["w_uk"], wts["w_uv_t"], wts["k_nope_g"], wts["w_out_f32"], wts["w_ple_gate_f32"]]
        out_specs += [
            pl.BlockSpec((1, N_HEADS * D_QK, tm), seq_blk),
            pl.BlockSpec((tm, D_K), row),
            pl.BlockSpec((1, tm // kv_block, N_HEADS, D_VX, kv_block), lambda i: seq_blk(i)[::2] + (0, 0, 0)),
            pl.BlockSpec((D_MODEL // n_tiles, D_MODEL), row),
            pl.BlockSpec((D_MODEL // n_tiles, D_MODEL), row),
            pl.BlockSpec((W_CHUNK, D_MODEL), w_chunk)]
        out_shape += [jax.ShapeDtypeStruct((n_seq, N_HEADS * D_QK, tiles_per_seq * tm), bf16),
                      jax.ShapeDtypeStruct((n_tok, D_K), bf16),
                      jax.ShapeDtypeStruct((n_seq, tiles_per_seq * tm // kv_block, N_HEADS, D_VX, kv_block),
                                           bf16),
                      jax.ShapeDtypeStruct((D_MODEL, D_MODEL), bf16),
                      jax.ShapeDtypeStruct((D_MODEL, D_MODEL), bf16),
                      jax.ShapeDtypeStruct((D_IN, D_MODEL), bf16)]
        scratch += [pltpu.VMEM((n_conv * W_CHUNK, D_MODEL), bf16)]
    else:
        in_specs += [_resident((Q_RANK, N_HEADS * D_QK)), _resident((1, D_NOPE)), _resident((1, LANES)),
                     _resident((1, D_NOPE)), _resident((N_HEADS, D_NOPE, KV_RANK))]
        args += [wts["w_uq"], wts["q_nope_g"], wts["q_rope_g"], wts["k_nope_g"], wts["w_uk_t"]]
        out_specs += [pl.BlockSpec((N_HEADS, tm, KV_RANK), lambda i: (0, i, 0)),
                      pl.BlockSpec((N_HEADS, tm, D_ROPE), lambda i: (0, i, 0))]
        out_shape += [jax.ShapeDtypeStruct((N_HEADS, n_tok, KV_RANK), bf16),
                      jax.ShapeDtypeStruct((N_HEADS, n_tok, D_ROPE), bf16)]
    body = functools.partial(_in_proj_body, n_conv=n_conv, nseg=nseg, seg_len=seg_len,
                             tiles_per_seq=tiles_per_seq, carry=carry, fuse_qkv=fuse_qkv)
    return pl.pallas_call(
        body,
        grid=(n_conv + n_tiles,),
        in_specs=in_specs,
        out_specs=out_specs,
        out_shape=out_shape,
        scratch_shapes=scratch,
        compiler_params=_cparams(1),
        name="in_proj",
    )(*args)


def _attn_prompt_body(qt_ref, k_ref, vt_ref, gm_ref, o_ref, m_ref, acc_ref, s_ref, mx_ref, *, tk, n_pairs):
    g = pl.program_id(1)
    tq = 2 * tk
    hi = slice(tk, tq)

    def column_max(s):
        x = jnp.max(s.reshape(s.shape[0] // SUBLANES, SUBLANES, s.shape[1]), axis=0)
        for shift in (4, 2, 1):
            x = jnp.maximum(x, pltpu.roll(x, shift, 0))
        return x

    def keys(h, j):
        rows = pl.ds(j * tk, tk)
        return jnp.concatenate([k_ref[rows, h * D_NOPE:(h + 1) * D_NOPE], k_ref[rows, N_HEADS * D_NOPE:]],
                               axis=1)

    def chunk_visible(n_q):
        k_chunk = lax.broadcasted_iota(jnp.int32, (tk, n_q), 0) // CHUNK
        q_chunk = lax.broadcasted_iota(jnp.int32, (tk, n_q), 1) // CHUNK
        return k_chunk <= q_chunk

    def scores(h, j, buf, first_diag=False):
        s = _dot(keys(h, j), qt_ref[0, h * D_QK:(h + 1) * D_QK, :])
        if first_diag:
            s = jnp.where(chunk_visible(tq), s, NEG_INF)
        s_ref[buf, h] = s
        mx_ref[buf, h] = column_max(s)

    def scores_hi(h, j, buf):
        s = _dot(keys(h, j), qt_ref[0, h * D_QK:(h + 1) * D_QK, hi])
        s = jnp.where(chunk_visible(tk), s, NEG_INF)
        s_ref[buf, h, :, hi] = s
        mx_ref[buf, h, :, hi] = column_max(s)

    def values(h, j, buf, cols=slice(None)):
        m_old = m_ref[h, :, cols]
        m_new = jnp.maximum(m_old, mx_ref[buf, h, :, cols])
        n_q = m_new.shape[1]
        alpha = jnp.exp2(m_old - m_new)
        s = s_ref[buf, h, :, cols]
        p = jnp.exp2(s.reshape(tk // SUBLANES, SUBLANES, n_q) - m_new[None]).reshape(tk, n_q)
        m_ref[h, :, cols] = m_new
        acc = acc_ref[h, :, cols].reshape(D_VX // SUBLANES, SUBLANES, n_q) * alpha[None]
        acc_ref[h, :, cols] = acc.reshape(D_VX, n_q) + _dot(vt_ref[0, j, h], p.astype(bf16))

    def first_values(h, j, buf):
        m_new = mx_ref[buf, h]
        p = jnp.exp2(s_ref[buf, h].reshape(tk // SUBLANES, SUBLANES, tq) - m_new[None]).reshape(tk, tq)
        m_ref[h] = m_new
        acc_ref[h] = _dot(vt_ref[0, j, h], p.astype(bf16))

    def both(score_fn, value_fn):
        for h in range(N_HEADS):
            score_fn(h)
            value_fn(h)

    def start(d0, d1):
        for h in range(N_HEADS):
            scores(h, d0, 0, first_diag=True)
        both(lambda h: scores_hi(h, d1, 1), lambda h: first_values(h, d0, 0))

    def finish(h):
        hs = slice(h * D_V, (h + 1) * D_V)
        o = (acc_ref[h, 0:D_V, :] * (1.0 / acc_ref[h, D_V:D_V + 1, :])).T
        o_ref[:, hs] = (o * gm_ref[:, hs].astype(f32)).astype(bf16)

    def pair_body(gs):
        d0, d1 = 2 * gs, 2 * gs + 1
        start(d0, d1)
        if gs == 0:
            both(lambda h: values(h, d1, 1, hi), finish)
            return
        both(lambda h: scores(h, 0, 0), lambda h: values(h, d1, 1, hi))
        for j in range(1, d0):
            both(lambda h: scores(h, j, j % 2), lambda h: values(h, j - 1, (j - 1) % 2))
        both(lambda h: values(h, d0 - 1, (d0 - 1) % 2), finish)

    for gs in range(n_pairs):
        pl.when(g == gs)(functools.partial(pair_body, gs))


def _attn_prompt(qt, k, vt, gm, *, n_seq, seq_len, tk):
    tq = 2 * tk
    nq = seq_len // tq
    body = functools.partial(_attn_prompt_body, tk=tk, n_pairs=nq)
    return pl.pallas_call(
        body,
        grid=(n_seq, nq),
        in_specs=[pl.BlockSpec((1, N_HEADS * D_QK, tq), lambda b, i: (b, 0, i)),
                  pl.BlockSpec((seq_len, D_K), lambda b, i: (b, 0)),
                  pl.BlockSpec((1, seq_len // tk, N_HEADS, D_VX, tk), lambda b, i: (b, 0, 0, 0, 0)),
                  pl.BlockSpec((tq, D_MLA), lambda b, i: (b * nq + i, 0))],
        out_specs=pl.BlockSpec((tq, D_MLA), lambda b, i: (b * nq + i, 0)),
        out_shape=jax.ShapeDtypeStruct((n_seq * seq_len, D_MLA), bf16),
        scratch_shapes=[pltpu.VMEM((N_HEADS, SUBLANES, tq), f32), pltpu.VMEM((N_HEADS, D_VX, tq), f32),
                        pltpu.VMEM((2, N_HEADS, tk, tq), f32),
                        pltpu.VMEM((2, N_HEADS, SUBLANES, tq), f32)],
        compiler_params=_cparams(2),
        name="attn_prompt",
    )(qt, k, vt, gm)


def _attn_sample_body(cckv_ref, ckr_t_ref, nckv_ref, nkr_ref, qa_ref, qr_ref, gm_ref, wukt_ref, wuv_ref,
                      o_ref, c_ref, ktail_ref, s_ref, olat_ref, *, past, t_new, chunks, n_b):
    n_tail = c_ref.shape[1] - past
    rows = N_HEADS * t_new
    c_ref[:, past:past + t_new, :] = nckv_ref[...].astype(bf16)
    c_ref[:, past + t_new:, :] = jnp.zeros((n_b, n_tail - t_new, KV_RANK), bf16)
    ktail_ref[:, 0:t_new, :] = nkr_ref[...].astype(bf16)
    ktail_ref[:, t_new:, :] = jnp.zeros((n_b, n_tail - t_new, D_ROPE), bf16)

    def chunk_scores(b, start, size):
        cached = start < past
        qa = qa_ref[:, b * t_new:(b + 1) * t_new, :].reshape(rows, KV_RANK)
        qr = qr_ref[:, b * t_new:(b + 1) * t_new, :].reshape(rows, D_ROPE)
        if cached:
            c32 = cckv_ref[b, start:start + size, :]
            c_ref[b, start:start + size, :] = c32.astype(bf16)
            c_t = c32.T.astype(bf16)
            k_t = _dot(wukt_ref[...], c_t)
            s_nope = _dot(qa, c_t)
            s_rope = _dot(qr, ckr_t_ref[b, :, start:start + size].astype(bf16))
        else:
            c = c_ref[b, start:start + size, :]
            k_t = _dot_nt(wukt_ref[...], c)
            s_nope = _dot_nt(qa, c)
            s_rope = _dot_nt(qr, ktail_ref[b])
        ssq = jnp.sum((k_t * k_t).reshape(N_HEADS, D_NOPE, size), axis=1)
        r = lax.rsqrt(ssq * (1.0 / D_NOPE) + EPS)
        for h in range(N_HEADS):
            hs = slice(h * t_new, (h + 1) * t_new)
            s = (s_nope[hs] * r[h:h + 1, :] + s_rope[hs]) * ATTN_SCALE
            if not cached:
                key = lax.broadcasted_iota(jnp.int32, (t_new, size), 1)
                s = jnp.where(key < t_new, s, NEG_INF)
            s_ref[b, hs, start:start + size] = s

    def chunk_values(b, start, size, state):
        m, l, acc = state
        s = s_ref[b, :, start:start + size]
        m_new = jnp.maximum(m, jnp.max(s, axis=-1, keepdims=True))
        alpha = jnp.exp(m - m_new)
        p = jnp.exp(s - m_new)
        l = alpha * l + jnp.sum(p, axis=-1, keepdims=True)
        acc = alpha * acc + _dot(p.astype(bf16), c_ref[b, start:start + size, :])
        return m_new, l, acc

    def finish(b, state):
        _, l, acc = state
        olat_ref[b] = (acc / l).astype(bf16)

    def value_up_projection():
        for h in range(N_HEADS):
            o_lat = olat_ref[:, h * t_new:(h + 1) * t_new, :].reshape(n_b * t_new, KV_RANK)
            hs = slice(h * D_V, (h + 1) * D_V)
            o_ref[:, hs] = (_dot(o_lat, wuv_ref[h]) * gm_ref[:, hs].astype(f32)).astype(bf16)

    items = [(b,) + ch for b in range(n_b) for ch in chunks]
    init = (jnp.full((rows, 1), NEG_INF, f32), jnp.zeros((rows, 1), f32),
            jnp.zeros((rows, KV_RANK), f32))
    state = init
    chunk_scores(*items[0])
    for nxt, cur in zip(items[1:] + [None], items):
        if nxt is not None:
            chunk_scores(*nxt)
        state = chunk_values(*cur, state)
        if nxt is None or nxt[0] != cur[0]:
            finish(cur[0], state)
            state = init
    value_up_projection()


def _attn_sample(cache_ckv, cache_krope_t, ckv_new, krope_new, q_abs, q_rope, gm, wts, *, t_new, n_b):
    n_seq, past, _ = cache_ckv.shape
    chunk = 512
    chunks = tuple((s, min(chunk, past - s)) for s in range(0, past, chunk)) + ((past, LANES),)
    s_pad = past + LANES
    body = functools.partial(_attn_sample_body, past=past, t_new=t_new, chunks=chunks, n_b=n_b)
    return pl.pallas_call(
        body,
        grid=(n_seq // n_b,),
        in_specs=[pl.BlockSpec((n_b, past, KV_RANK), lambda b: (b, 0, 0)),
                  pl.BlockSpec((n_b, D_ROPE, past), lambda b: (b, 0, 0)),
                  pl.BlockSpec((n_b, t_new, KV_RANK), lambda b: (b, 0, 0)),
                  pl.BlockSpec((n_b, t_new, D_ROPE), lambda b: (b, 0, 0)),
                  pl.BlockSpec((N_HEADS, n_b * t_new, KV_RANK), lambda b: (0, b, 0)),
                  pl.BlockSpec((N_HEADS, n_b * t_new, D_ROPE), lambda b: (0, b, 0)),
                  pl.BlockSpec((n_b * t_new, D_MLA), lambda b: (b, 0)),
                  _resident((N_HEADS * D_NOPE, KV_RANK)),
                  _resident((N_HEADS, KV_RANK, D_V))],
        out_specs=pl.BlockSpec((n_b * t_new, D_MLA), lambda b: (b, 0)),
        out_shape=jax.ShapeDtypeStruct((n_seq * t_new, D_MLA), bf16),
        scratch_shapes=[pltpu.VMEM((n_b, s_pad, KV_RANK), bf16),
                        pltpu.VMEM((n_b, s_pad - past, D_ROPE), bf16),
                        pltpu.VMEM((n_b, N_HEADS * t_new, s_pad), f32),
                        pltpu.VMEM((n_b, N_HEADS * t_new, KV_RANK), bf16)],
        compiler_params=_cparams(1),
        name="attn_sample",
    )(cache_ckv, cache_krope_t, ckv_new, krope_new, q_abs, q_rope, gm,
      wts["w_uk_t"].reshape(N_HEADS * D_NOPE, KV_RANK), wts["w_uv3"])


def _out_proj_body(x_ref, pa_ref, ma_ref, p_ref, wo_ref, png_ref, wg_ref, bg_ref, wple_ref, y_ref):
    h = x_ref[...] + _dot(pa_ref[...], wo_ref[0:D_POOL, :]) + _dot(ma_ref[...], wo_ref[D_POOL:, :])
    hn = (h * _rms(h, D_MODEL) * png_ref[...]).astype(bf16)
    gate = jax.nn.sigmoid(_dot(hn, wg_ref[...]) + bg_ref[...])
    y_ref[...] = h + gate * _dot(p_ref[...].astype(bf16), wple_ref[...])


def _out_proj(x2d, pool_act, mla_act, p2d, wts, *, tm):
    n_tok = x2d.shape[0]
    row = lambda i: (i, 0)
    return pl.pallas_call(
        _out_proj_body,
        grid=(n_tok // tm,),
        in_specs=[pl.BlockSpec((tm, D_MODEL), row), pl.BlockSpec((tm, D_POOL), row),
                  pl.BlockSpec((tm, D_MLA), row), pl.BlockSpec((tm, D_PLE), row),
                  _resident((D_MODEL, D_MODEL)), _resident((1, D_MODEL)),
                  _resident((D_MODEL, D_MODEL)), _resident((1, D_MODEL)),
                  _resident((D_PLE, D_MODEL))],
        out_specs=pl.BlockSpec((tm, D_MODEL), row),
        out_shape=jax.ShapeDtypeStruct((n_tok, D_MODEL), f32),
        compiler_params=_cparams(1),
        name="out_proj",
    )(x2d, pool_act, mla_act, p2d, wts["w_out"], wts["ple_norm_g"], wts["w_ple_gate"],
      wts["b_ple_gate"], wts["w_ple"])


def _rope_tables(pos0, t, reps=1):
    pos = (pos0 + np.arange(t)).astype(np.float64)
    inv = ROPE_THETA ** (-(np.arange(0, D_ROPE, 2, dtype=np.float64) / D_ROPE))
    ang = pos[:, None] * inv[None, :]
    cos, sin = np.cos(ang), np.sin(ang)
    zero = np.zeros((t, LANES - D_ROPE))
    rows = (np.concatenate([cos, cos, zero], axis=-1), np.concatenate([-sin, sin, zero], axis=-1),
            pos[:, None])
    as_f32 = lambda a: jnp.asarray(a.astype(np.float32))
    return tuple(as_f32(np.tile(a, (reps, 1))) for a in rows) + ((as_f32(cos.T), as_f32(sin.T)),)


def _prep_weights(norm_g, w_in, q_norm_g, w_uq, kv_norm_g, w_ukv, q_nope_g, q_rope_g, k_nope_g,
                  k_rope_g, w_pool, pool_scale, w_out, ple_norm_g, w_ple_gate, b_ple_gate, w_ple):
    w_uq_r = jnp.pad(w_uq.reshape(Q_RANK, N_HEADS, D_NOPE + D_ROPE),
                     ((0, 0), (0, 0), (0, D_QK - D_NOPE - D_ROPE))).reshape(Q_RANK, N_HEADS * D_QK)
    w_ukv3 = w_ukv.reshape(KV_RANK, N_HEADS, D_NOPE + D_V)
    w_uk3, w_uv3 = w_ukv3[..., :D_NOPE], w_ukv3[..., D_NOPE:]
    pad_rope = lambda g: jnp.pad(g, (0, LANES - D_ROPE))[None, :]
    return {
        "norm_g": norm_g[None, :], "w_in_t_f32": w_in.T, "q_norm_g": q_norm_g[None, :],
        "kv_norm_g": kv_norm_g[None, :], "k_rope_g": pad_rope(k_rope_g),
        "w_pool": w_pool.astype(bf16), "pool_scale": pool_scale[None, :],
        "w_uq": w_uq_r.astype(bf16), "q_nope_g": q_nope_g[None, :], "q_rope_g": pad_rope(q_rope_g),
        "w_uq_t": w_uq.T.astype(bf16),
        "q_nope_g_col": q_nope_g[:, None], "q_rope_g_col": q_rope_g[:, None],
        "k_nope_g": k_nope_g[None, :],
        "w_uk": w_uk3.reshape(KV_RANK, N_HEADS * D_NOPE).astype(bf16),
        "w_uv_t": w_uv3.reshape(KV_RANK, D_MLA).T.astype(bf16),
        "w_uk_t": jnp.transpose(w_uk3, (1, 2, 0)).astype(bf16),
        "w_uv3": jnp.transpose(w_uv3, (1, 0, 2)).astype(bf16),
        "w_out_f32": w_out, "w_ple_gate_f32": w_ple_gate,
        "ple_norm_g": ple_norm_g[None, :], "b_ple_gate": b_ple_gate[None, :],
        "w_ple": w_ple.astype(bf16),
    }


def _layer_prompt(x, p, wts):
    n_seq, seq_len, _ = x.shape
    x2d = x.reshape(n_seq * seq_len, D_MODEL)
    cos2, sin2, pos, rope_t = _rope_tables(0, seq_len)
    hist = jnp.zeros((1, HALO, D_POOL), f32)
    pact, gm, ckv, krope, npool, qt, k, vt, w_out16, w_gate16, w_in_t16 = _in_proj(
        x2d, hist, cos2, sin2, pos, wts, nseg=1, seg_len=PROMPT_TILE, n_seq=n_seq, carry=True,
        rope_t=rope_t, kv_block=ATTN_BLOCK)
    wts.update(w_out=w_out16, w_ple_gate=w_gate16, w_in_t=w_in_t16)
    mact = _attn_prompt(qt, k, vt, gm, n_seq=n_seq, seq_len=seq_len, tk=ATTN_BLOCK)
    y = _out_proj(x2d, pact, mact, p.reshape(n_seq * seq_len, D_PLE), wts, tm=OUT_TILE)
    return (y.reshape(x.shape), ckv.reshape(n_seq, seq_len, KV_RANK),
            krope.reshape(n_seq, seq_len, D_ROPE), npool)


def _layer_sample(x, p, state_pool, cache_ckv, cache_krope, wts):
    n_seq, t_new, _ = x.shape
    past = cache_ckv.shape[1]
    n_tok = n_seq * t_new
    x2d = x.reshape(n_tok, D_MODEL)
    cos2, sin2, pos, _ = _rope_tables(past, t_new, reps=n_seq)
    hist = jnp.transpose(state_pool, (1, 0, 2))
    pact, gm, ckv, krope, npool, q_abs, q_rope = _in_proj(
        x2d, hist, cos2, sin2, pos, wts, nseg=SAMPLE_SEGS_PER_TILE, seg_len=t_new, n_seq=n_seq,
        carry=False)
    ckv3 = ckv.reshape(n_seq, t_new, KV_RANK)
    krope3 = krope.reshape(n_seq, t_new, D_ROPE)
    cache_krope_t = jnp.transpose(cache_krope, (0, 2, 1))
    mact = _attn_sample(cache_ckv, cache_krope_t, ckv3, krope3, q_abs, q_rope, gm, wts, t_new=t_new,
                        n_b=SAMPLE_BATCH_PER_STEP)
    y = _out_proj(x2d, pact, mact, p.reshape(n_tok, D_PLE), wts, tm=SAMPLE_OUT_TILE)
    return y.reshape(x.shape), ckv3, krope3, jnp.transpose(npool, (1, 0, 2))


def kernel(x_prompt, x_sample, cache_ckv, cache_krope, state_pool, p_prompt, p_sample, norm_g, w_in,
           q_norm_g, w_uq, kv_norm_g, w_ukv, q_nope_g, q_rope_g, k_nope_g, k_rope_g, w_pool,
           pool_scale, w_out, ple_norm_g, w_ple_gate, b_ple_gate, w_ple):
    depth = norm_g.shape[0]
    layer_w = (norm_g, w_in, q_norm_g, w_uq, kv_norm_g, w_ukv, q_nope_g, q_rope_g, k_nope_g, k_rope_g,
               w_pool, pool_scale, w_out, ple_norm_g, w_ple_gate, b_ple_gate, w_ple)
    yp, ys = x_prompt, x_sample
    outs = [[] for _ in range(6)]
    for i in range(depth):
        wts = _prep_weights(*(w[i] for w in layer_w))
        yp, c1, k1, s1 = _layer_prompt(yp, p_prompt[i], wts)
        ys, c2, k2, s2 = _layer_sample(ys, p_sample[i], state_pool[i], cache_ckv[i], cache_krope[i], wts)
        for lst, val in zip(outs, (c1, k1, s1, c2, k2, s2)):
            lst.append(val)
    return (yp, ys) + tuple(jnp.stack(o) for o in outs)
```
